```python
import jax, jax.numpy as jnp
from jax import lax
import numpy as np


D_MODEL = 1024
BATCH = 8
SEQ = 8192
DEPTH = 1

HEAD_DIM = 64
BLOCK = 128
DIL_PATTERNS = ((128, 1), (512, 4), (2048, 16))
N_DIL_GROUPS = 3
DIL_HEADS_PER_GROUP = 8
DIL_HEADS = N_DIL_GROUPS * DIL_HEADS_PER_GROUP
DIL_WIDTH = DIL_HEADS * HEAD_DIM
DIL_MERGED = DIL_HEADS_PER_GROUP * HEAD_DIM
SWA_WINDOW = 128
SWA_Q_HEADS = 16
SWA_KV_HEADS = 2
SWA_Q_WIDTH = SWA_Q_HEADS * HEAD_DIM
SWA_KV_WIDTH = SWA_KV_HEADS * HEAD_DIM
IN_WIDTHS = (DIL_WIDTH, DIL_WIDTH, DIL_WIDTH, SWA_Q_WIDTH, SWA_KV_WIDTH, SWA_KV_WIDTH, D_MODEL, D_MODEL)
IN_WIDTH = sum(IN_WIDTHS)
MOE_GROUPS = 4
EXPERTS_PER_GROUP = 8
N_EXPERTS = MOE_GROUPS * EXPERTS_PER_GROUP
TOP_K = 2
D_EXPERT = 512
RMS_EPS = 1e-6

kernel_name = 'hybrid_dilated_swa_sink_hiermoe_block'


def rmsnorm(x, gain):
    x32 = x.astype(jnp.float32)
    y = x32 * lax.rsqrt(jnp.mean(x32 * x32, axis=-1, keepdims=True) + RMS_EPS)
    return (y * gain.astype(jnp.float32)).astype(x.dtype)


def alibi_slopes(n):
    return (2.0 ** (-8.0 * np.arange(1, n + 1) / n)).astype(np.float32)


def split_cols(proj, widths):
    offs = [0]
    for w in widths:
        offs.append(offs[-1] + w)
    return [proj[..., offs[i]:offs[i + 1]] for i in range(len(widths))]


def banded_attention(q, k, v, slopes, max_back, unit, sinks=None):
    b, s, l, hq, hd = q.shape
    hkv = k.shape[3]
    rep = hq // hkv
    nb = -(-l // BLOCK)
    lp = nb * BLOCK
    pad = [(0, 0), (0, 0), (0, lp - l), (0, 0), (0, 0)]
    qb = jnp.pad(q, pad).reshape(b, s, nb, BLOCK, hkv, rep, hd)
    kb = jnp.pad(k, pad).reshape(b, s, nb, BLOCK, hkv, hd)
    vb = jnp.pad(v, pad).reshape(b, s, nb, BLOCK, hkv, hd)

    def with_prev(z):
        prev = jnp.pad(z[:, :, :-1], [(0, 0), (0, 0), (1, 0), (0, 0), (0, 0), (0, 0)])
        return jnp.concatenate([prev, z], axis=3)

    kk, vv = with_prev(kb), with_prev(vb)
    scores = jnp.einsum('bsnqgrd,bsnkgd->bsngrqk', qb, kk,
                        preferred_element_type=jnp.float32) * (hd ** -0.5)
    a = np.arange(BLOCK)[:, None]
    c = np.arange(2 * BLOCK)[None, :]
    delta = a + BLOCK - c
    band = (delta >= 0) & (delta <= max_back)
    valid = band[None] & ((np.arange(nb)[:, None, None] > 0) | (c >= BLOCK)[None])
    bias = (-slopes.reshape(hkv, rep)[:, :, None, None] * (delta * unit)[None, None]).astype(np.float32)
    scores = jnp.where(valid[None, None, :, None, None], scores + bias, -jnp.inf)
    m = scores.max(axis=-1)
    if sinks is not None:
        sink = sinks.astype(jnp.float32).reshape(hkv, rep)[:, :, None]
        m = jnp.maximum(m, sink)
    p = jnp.exp(scores - m[..., None])
    denom = p.sum(axis=-1)
    if sinks is not None:
        denom = denom + jnp.exp(sink - m)
    out = jnp.einsum('bsngrqk,bsnkgd->bsnqgrd', p, vv.astype(jnp.float32))
    denom_t = jnp.moveaxis(denom, -1, 3)
    out = (out / denom_t[..., None]).reshape(b, s, lp, hq, hd)[:, :, :l]
    lse = (jnp.moveaxis(m, -1, 3) + jnp.log(denom_t)).reshape(b, s, lp, hq)[:, :, :l]
    return out.astype(q.dtype), lse


def dilated_attention(q, k, v):
    b, t = q.shape[:2]
    slopes = alibi_slopes(DIL_HEADS)
    outs, lses = [], []
    for gi, (window, dil) in enumerate(DIL_PATTERNS):
        lo, hi = gi * DIL_HEADS_PER_GROUP, (gi + 1) * DIL_HEADS_PER_GROUP
        l = t // dil

        def to_streams(z):
            return z[:, :, lo:hi].reshape(b, l, dil, DIL_HEADS_PER_GROUP, HEAD_DIM).transpose(0, 2, 1, 3, 4)

        o, lse = banded_attention(to_streams(q), to_streams(k), to_streams(v),
                                  slopes[lo:hi], window // dil, dil)
        outs.append(o.transpose(0, 2, 1, 3, 4).reshape(b, t, DIL_HEADS_PER_GROUP, HEAD_DIM))
        lses.append(lse.transpose(0, 2, 1, 3).reshape(b, t, DIL_HEADS_PER_GROUP))
    alpha = jax.nn.softmax(jnp.stack(lses, axis=2), axis=2)
    y = jnp.einsum('btgh,btghd->bthd', alpha.astype(q.dtype), jnp.stack(outs, axis=2))
    return y.reshape(b, t, DIL_MERGED)


def swa_sink_attention(q, k, v, sinks):
    b, t = q.shape[:2]
    out, _ = banded_attention(q[:, None], k[:, None], v[:, None], alibi_slopes(SWA_Q_HEADS),
                              SWA_WINDOW - 1, 1, sinks=sinks)
    return out[:, 0].reshape(b, t, SWA_Q_WIDTH)


def hier_moe(h, w_group, b_group, w_router, b_router, w_e_gate, w_e_up, w_e_down):
    b, t, d = h.shape
    hf = h.reshape(b * t, d)
    g_logits = (hf @ w_group).astype(jnp.float32) + b_group.astype(jnp.float32)
    g_probs = jax.nn.softmax(g_logits, axis=-1)
    g_top = jnp.argmax(g_logits, axis=-1)
    g_w = jnp.take_along_axis(g_probs, g_top[:, None], axis=1)
    e_logits = ((hf @ w_router).astype(jnp.float32) + b_router.astype(jnp.float32)).reshape(
        -1, MOE_GROUPS, EXPERTS_PER_GROUP)
    e_logits = jnp.take_along_axis(e_logits, g_top[:, None, None], axis=1)[:, 0]
    e_w, e_idx = lax.top_k(jax.nn.softmax(e_logits, axis=-1), TOP_K)
    e_w = e_w / e_w.sum(axis=-1, keepdims=True)
    ids = g_top[:, None] * EXPERTS_PER_GROUP + e_idx
    gates = jnp.einsum('nk,nke->ne', g_w * e_w, jax.nn.one_hot(ids, N_EXPERTS, dtype=jnp.float32))
    out = jnp.zeros_like(hf)
    for e in range(N_EXPERTS):
        a = jax.nn.silu(hf @ w_e_gate[e]) * (hf @ w_e_up[e])
        out = out + gates[:, e:e + 1].astype(hf.dtype) * (a @ w_e_down[e])
    return out.reshape(b, t, d)


def setup_inputs(seed: int = 0) -> dict:
    key = jax.random.key(seed)
    ks = jax.random.split(key, 20)
    f32 = jnp.float32

    def nrm(k, shape, fan_in):
        return jax.random.normal(k, shape, f32) * (fan_in ** -0.5)

    return {
        'x': jax.random.normal(ks[0], (BATCH, SEQ, D_MODEL), f32),
        'g_mix': 1.0 + 0.05 * jax.random.normal(ks[1], (DEPTH, D_MODEL), f32),
        'w_in': nrm(ks[2], (DEPTH, D_MODEL, IN_WIDTH), D_MODEL),
        'sinks': jax.random.normal(ks[3], (DEPTH, SWA_Q_HEADS), f32),
        'w_br_dil': nrm(ks[4], (DEPTH, DIL_MERGED, D_MODEL), DIL_MERGED),
        'w_br_swa': nrm(ks[5], (DEPTH, SWA_Q_WIDTH, D_MODEL), SWA_Q_WIDTH),
        'w_out': nrm(ks[6], (DEPTH, D_MODEL, D_MODEL), D_MODEL),
        'g_ffn': 1.0 + 0.05 * jax.random.normal(ks[7], (DEPTH, D_MODEL), f32),
        'w_group': nrm(ks[8], (DEPTH, D_MODEL, MOE_GROUPS), D_MODEL),
        'b_group': 0.01 * jax.random.normal(ks[9], (DEPTH, MOE_GROUPS), f32),
        'w_router': nrm(ks[10], (DEPTH, D_MODEL, N_EXPERTS), D_MODEL),
        'b_router': 0.01 * jax.random.normal(ks[11], (DEPTH, N_EXPERTS), f32),
        'w_e_gate': nrm(ks[12], (DEPTH, N_EXPERTS, D_MODEL, D_EXPERT), D_MODEL),
        'w_e_up': nrm(ks[13], (DEPTH, N_EXPERTS, D_MODEL, D_EXPERT), D_MODEL),
        'w_e_down': nrm(ks[14], (DEPTH, N_EXPERTS, D_EXPERT, D_MODEL), D_EXPERT),
        'g_final': 1.0 + 0.05 * jax.random.normal(ks[15], (D_MODEL,), f32),
    }


def reference(x, g_mix, w_in, sinks, w_br_dil, w_br_swa, w_out, g_ffn, w_group, b_group,
              w_router, b_router, w_e_gate, w_e_up, w_e_down, g_final):
    b, t, _ = x.shape
    for l in range(DEPTH):
        h = rmsnorm(x, g_mix[l])
        proj = h @ w_in[l]
        q_d, k_d, v_d, q_s, k_s, v_s, gate_d, gate_s = split_cols(proj, IN_WIDTHS)
        y_dil = dilated_attention(q_d.reshape(b, t, DIL_HEADS, HEAD_DIM),
                                  k_d.reshape(b, t, DIL_HEADS, HEAD_DIM),
                                  v_d.reshape(b, t, DIL_HEADS, HEAD_DIM))
        y_swa = swa_sink_attention(q_s.reshape(b, t, SWA_Q_HEADS, HEAD_DIM),
                                   k_s.reshape(b, t, SWA_KV_HEADS, HEAD_DIM),
                                   v_s.reshape(b, t, SWA_KV_HEADS, HEAD_DIM), sinks[l])
        mixed = (jax.nn.sigmoid(gate_d) * (y_dil @ w_br_dil[l])
                 + jax.nn.sigmoid(gate_s) * (y_swa @ w_br_swa[l]))
        x = x + mixed @ w_out[l]
        x = x + hier_moe(rmsnorm(x, g_ffn[l]), w_group[l], b_group[l], w_router[l], b_router[l],
                         w_e_gate[l], w_e_up[l], w_e_down[l])
    return rmsnorm(x, g_final)
```

```python
import functools

import numpy as np
import jax
import jax.numpy as jnp
from jax import lax
from jax.experimental import pallas as pl
from jax.experimental.pallas import tpu as pltpu

F32 = jnp.float32
BF16 = jnp.bfloat16

HEAD_DIM = 64
BAND = 128
DIL_PATTERNS = ((128, 1), (512, 4), (2048, 16))
DIL_HEADS_PER_GROUP = 8
DIL_HEADS = 24
GROUP_W = DIL_HEADS_PER_GROUP * HEAD_DIM
QKV_W = 3 * GROUP_W
SWA_WINDOW = 128
SWA_Q_HEADS = 16
SWA_KV_HEADS = 2
SWA_Q_W = SWA_Q_HEADS * HEAD_DIM
SWA_W = SWA_Q_W + 4 * 128
MOE_GROUPS = 4
EXPERTS_PER_GROUP = 8
N_EXPERTS = 32
D_EXPERT = 512
RMS_EPS = 1e-6
LANES = 128
NEG_INF = float("-inf")
ROW_TILE = 8

VMEM_LIMIT = 56 * 1024 * 1024


def _alibi_slopes(n):
    return (2.0 ** (-8.0 * np.arange(1, n + 1) / n)).astype(np.float32)


def _band_bias(slopes, max_back, unit):
    a = np.arange(BAND)[:, None]
    c = np.arange(2 * BAND)[None, :]
    delta = a + BAND - c
    band = (delta >= 0) & (delta <= max_back)
    pen = (-slopes[:, None, None] * (delta * unit)[None]).astype(np.float32)
    full = np.where(band[None], pen, -np.inf).astype(np.float32)
    first = np.where((band & (c >= BAND))[None], pen, -np.inf).astype(np.float32)
    return np.stack([full, first], axis=0)


def _inproj_kernel(x_ref, g_ref, w_ref, qkv0_ref, swa_ref, gates_ref, qkv4_ref, qkv16_ref, hs_ref, *, tm):
    x = x_ref[...]
    ms = jnp.mean(x * x, axis=-1, keepdims=True)
    h = x * lax.rsqrt(ms + RMS_EPS) * g_ref[...]
    nchunk = h.shape[1] // LANES
    for c in range(nchunk):
        hs_ref[c] = h[:, c * LANES:(c + 1) * LANES]
    hb = h.astype(BF16)

    def proj(lhs, c0, width):
        return jnp.dot(lhs, w_ref[:, c0:c0 + width], preferred_element_type=F32).astype(BF16)

    col = 0
    for out_ref, width in ((qkv0_ref, QKV_W), (swa_ref, SWA_W), (gates_ref, 2048)):
        for c0 in range(0, width, 512):
            out_ref[:, c0:c0 + 512] = proj(hb, col + c0, 512)
        col += width
    for out_ref, d in ((qkv4_ref, 4), (qkv16_ref, 16)):
        rows = tm // d
        hp = jnp.concatenate(
            [jnp.concatenate([hs_ref[c, pl.ds(r, rows, stride=d), :] for c in range(nchunk)], axis=1)
             for r in range(d)], axis=0).astype(BF16)
        for c0 in range(0, QKV_W, 512):
            res = proj(hp, col + c0, 512)
            for r in range(d):
                out_ref[r, :, c0:c0 + 512] = res[r * rows:(r + 1) * rows]
        col += QKV_W


def _prep_w_in(w_in):
    scale = HEAD_DIM ** -0.5
    o_qd, o_kd, o_vd, o_qs, o_ks, o_vs, o_g = 0, 1536, 3072, 4608, 5632, 5760, 5888

    def grp(g):
        return [w_in[:, o_qd + g * 512:o_qd + (g + 1) * 512] * scale,
                w_in[:, o_kd + g * 512:o_kd + (g + 1) * 512],
                w_in[:, o_vd + g * 512:o_vd + (g + 1) * 512]]

    k0, k1 = w_in[:, o_ks:o_ks + 64], w_in[:, o_ks + 64:o_ks + 128]
    v0, v1 = w_in[:, o_vs:o_vs + 64], w_in[:, o_vs + 64:o_vs + 128]
    cols = grp(0) + [w_in[:, o_qs:o_qs + 1024] * scale, k0, k0, k1, k1, v0, v0, v1, v1,
                     w_in[:, o_g:o_g + 2048]] + grp(1) + grp(2)
    return jnp.concatenate(cols, axis=1).astype(BF16)


def _inproj(x, g_mix, w_in, *, tm):
    b, t, dm = x.shape
    wp = _prep_w_in(w_in)
    nw = wp.shape[1]
    nt = t // tm
    out_shape = (
        jax.ShapeDtypeStruct((b, t, QKV_W), BF16),
        jax.ShapeDtypeStruct((b, t, SWA_W), BF16),
        jax.ShapeDtypeStruct((b, t, 2048), BF16),
        jax.ShapeDtypeStruct((b, 4, t // 4, QKV_W), BF16),
        jax.ShapeDtypeStruct((b, 16, t // 16, QKV_W), BF16),
    )
    return pl.pallas_call(
        functools.partial(_inproj_kernel, tm=tm),
        out_shape=out_shape,
        grid=(b, nt),
        in_specs=[
            pl.BlockSpec((None, tm, dm), lambda bi, i: (bi, i, 0)),
            pl.BlockSpec((1, dm), lambda bi, i: (0, 0)),
            pl.BlockSpec((dm, nw), lambda bi, i: (0, 0), pipeline_mode=pl.Buffered(1)),
        ],
        out_specs=(
            pl.BlockSpec((None, tm, QKV_W), lambda bi, i: (bi, i, 0)),
            pl.BlockSpec((None, tm, SWA_W), lambda bi, i: (bi, i, 0)),
            pl.BlockSpec((None, tm, 2048), lambda bi, i: (bi, i, 0)),
            pl.BlockSpec((None, 4, tm // 4, QKV_W), lambda bi, i: (bi, 0, i, 0)),
            pl.BlockSpec((None, 16, tm // 16, QKV_W), lambda bi, i: (bi, 0, i, 0)),
        ),
        scratch_shapes=[pltpu.VMEM((dm // LANES, tm, LANES), F32)],
        compiler_params=pltpu.CompilerParams(
            dimension_semantics=("arbitrary", "arbitrary"), vmem_limit_bytes=VMEM_LIMIT),
        name="inproj",
    )(x, g_mix.reshape(1, dm), wp)


def _softmax_pv(s, vv, sink=None):
    m = jnp.max(s, axis=-1, keepdims=True)
    if sink is not None:
        m = jnp.maximum(m, sink)
    p = jnp.exp(s - m)
    l = jnp.sum(p, axis=-1, keepdims=True)
    if sink is not None:
        l = l + jnp.exp(sink - m)
    o = jnp.dot(p.astype(BF16), vv, preferred_element_type=F32)
    return o * (1.0 / l), m + jnp.log(l)


def _dil_attn_kernel(q_ref, kc_ref, vc_ref, kp_ref, vp_ref, bias_ref, o_ref, lse_ref, kbuf, vbuf, *, tq):
    i = pl.program_id(2)
    kbuf[0:BAND] = kp_ref[...]
    kbuf[BAND:BAND + tq] = kc_ref[...]
    vbuf[0:BAND] = vp_ref[...]
    vbuf[BAND:BAND + tq] = vc_ref[...]
    lo = lax.broadcasted_iota(jnp.int32, (BAND, LANES), 1) < HEAD_DIM
    zero = jnp.zeros((BAND, LANES), BF16)
    for qb in range(tq // BAND):
        sel = jnp.where(i == 0, 1, 0) if qb == 0 else 0
        rq = slice(qb * BAND, (qb + 1) * BAND)
        rk = slice(qb * BAND, qb * BAND + 2 * BAND)
        for pr in range(GROUP_W // LANES):
            cl = slice(pr * LANES, (pr + 1) * LANES)
            q2 = q_ref[rq, cl]
            kk = kbuf[rk, cl]
            vv = vbuf[rk, cl]
            res = []
            for hh, qm in enumerate((jnp.where(lo, q2, zero), jnp.where(lo, zero, q2))):
                s = lax.dot_general(qm, kk, (((1,), (1,)), ((), ())), preferred_element_type=F32)
                res.append(_softmax_pv(s + bias_ref[sel, 2 * pr + hh], vv))
            o_ref[rq, cl] = jnp.where(lo, res[0][0], res[1][0]).astype(o_ref.dtype)
            lse_ref[rq, cl] = jnp.where(lo, res[0][1], res[1][1])


def _dil_attention(qkv, gi, *, tq):
    b, d, l, _ = qkv.shape
    window, dil = DIL_PATTERNS[gi]
    assert dil == d
    tq = min(tq, l)
    nq = tq // BAND
    slopes = _alibi_slopes(DIL_HEADS)[gi * 8:(gi + 1) * 8]
    bias = jnp.asarray(_band_bias(slopes, window // dil, dil))

    def cur(c):
        return pl.BlockSpec((None, None, tq, GROUP_W), lambda bi, r, i: (bi, r, i, c))

    def prev(c):
        return pl.BlockSpec((None, None, BAND, GROUP_W), lambda bi, r, i: (bi, r, jnp.maximum(i * nq - 1, 0), c))

    return pl.pallas_call(
        functools.partial(_dil_attn_kernel, tq=tq),
        out_shape=(jax.ShapeDtypeStruct((b, d, l, GROUP_W), BF16),
                   jax.ShapeDtypeStruct((b, d, l, GROUP_W), F32)),
        grid=(b, d, l // tq),
        in_specs=[cur(0), cur(1), cur(2), prev(1), prev(2),
                  pl.BlockSpec((2, 8, BAND, 2 * BAND), lambda bi, r, i: (0, 0, 0, 0))],
        out_specs=(cur(0), cur(0)),
        scratch_shapes=[pltpu.VMEM((BAND + tq, GROUP_W), BF16), pltpu.VMEM((BAND + tq, GROUP_W), BF16)],
        compiler_params=pltpu.CompilerParams(
            dimension_semantics=("arbitrary", "arbitrary", "arbitrary"), vmem_limit_bytes=VMEM_LIMIT),
        name=f"dil_attn_{gi}",
    )(qkv, qkv, qkv, qkv, qkv, bias)


def _swa_attn_kernel(sink_ref, qlo_ref, qhi_ref, kvc_ref, kvp_ref, bias_ref, o_ref, kvbuf, *, tq):
    i = pl.program_id(1)
    kvbuf[0:BAND] = kvp_ref[...]
    kvbuf[BAND:BAND + tq] = kvc_ref[...]
    lo = lax.broadcasted_iota(jnp.int32, (BAND, LANES), 1) < HEAD_DIM
    zero = jnp.zeros((BAND, LANES), BF16)
    for qb in range(tq // BAND):
        sel = jnp.where(i == 0, 1, 0) if qb == 0 else 0
        rq = slice(qb * BAND, (qb + 1) * BAND)
        rk = slice(qb * BAND, qb * BAND + 2 * BAND)
        for pr in range(SWA_Q_HEADS // 2):
            g = pr // 4
            q_ref = qlo_ref if pr < 4 else qhi_ref
            q2 = q_ref[rq, (pr % 4) * LANES:(pr % 4 + 1) * LANES]
            kk = kvbuf[rk, g * LANES:(g + 1) * LANES]
            vv = kvbuf[rk, (2 + g) * LANES:(3 + g) * LANES]
            res = []
            for hh, qm in enumerate((jnp.where(lo, q2, zero), jnp.where(lo, zero, q2))):
                h = 2 * pr + hh
                s = lax.dot_general(qm, kk, (((1,), (1,)), ((), ())), preferred_element_type=F32)
                res.append(_softmax_pv(s + bias_ref[sel, h], vv, sink=sink_ref[h])[0])
            o_ref[rq, pr * LANES:(pr + 1) * LANES] = jnp.where(lo, res[0], res[1]).astype(o_ref.dtype)


def _swa_attention(swa, sinks, *, tq):
    b, t, _ = swa.shape
    tq = min(tq, t)
    nq = tq // BAND
    bias = jnp.asarray(_band_bias(_alibi_slopes(SWA_Q_HEADS), SWA_WINDOW - 1, 1))
    grid_spec = pltpu.PrefetchScalarGridSpec(
        num_scalar_prefetch=1,
        grid=(b, t // tq),
        in_specs=[
            pl.BlockSpec((None, tq, 512), lambda bi, i, s: (bi, i, 0)),
            pl.BlockSpec((None, tq, 512), lambda bi, i, s: (bi, i, 1)),
            pl.BlockSpec((None, tq, 512), lambda bi, i, s: (bi, i, 2)),
            pl.BlockSpec((None, BAND, 512), lambda bi, i, s: (bi, jnp.maximum(i * nq - 1, 0), 2)),
            pl.BlockSpec((2, SWA_Q_HEADS, BAND, 2 * BAND), lambda bi, i, s: (0, 0, 0, 0)),
        ],
        out_specs=pl.BlockSpec((None, tq, SWA_Q_W), lambda bi, i, s: (bi, i, 0)),
        scratch_shapes=[pltpu.VMEM((BAND + tq, 512), BF16)],
    )
    return pl.pallas_call(
        functools.partial(_swa_attn_kernel, tq=tq),
        out_shape=jax.ShapeDtypeStruct((b, t, SWA_Q_W), BF16),
        grid_spec=grid_spec,
        compiler_params=pltpu.CompilerParams(
            dimension_semantics=("arbitrary", "arbitrary"), vmem_limit_bytes=VMEM_LIMIT),
        name="swa_attn",
    )(sinks.astype(F32), swa, swa, swa, swa, bias)


ROUTE_E1, ROUTE_E2, ROUTE_W1, ROUTE_W2, ROUTE_R1, ROUTE_R2 = range(6)
ROUTER_LANE0 = MOE_GROUPS


def _route(logits, carry_ref, tm):
    lane = lax.broadcasted_iota(jnp.int32, (tm, LANES), 1)
    lanef = lane.astype(F32)

    def first_argmax(v):
        m = jnp.max(v, axis=-1, keepdims=True)
        return m, jnp.min(jnp.where(v == m, lanef, float(LANES)), axis=-1, keepdims=True)

    gl = jnp.where(lane < MOE_GROUPS, logits, NEG_INF)
    gmax, gidx = first_argmax(gl)
    g_w = 1.0 / jnp.sum(jnp.exp(gl - gmax), axis=-1, keepdims=True)
    e_lane = lane - ROUTER_LANE0
    lane_group = (e_lane >> 3).astype(F32)
    in_group = (e_lane >= 0) & (e_lane < N_EXPERTS) & (lane_group == gidx)
    el = jnp.where(in_group, logits, NEG_INF)
    m1, i1 = first_argmax(el)
    m2, i2 = first_argmax(jnp.where(lanef == i1, NEG_INF, el))
    tt = jnp.exp(m2 - m1)
    w1 = g_w / (1.0 + tt)
    w2 = g_w * tt / (1.0 + tt)
    e1 = i1 - float(ROUTER_LANE0)
    e2 = i2 - float(ROUTER_LANE0)
    oh1 = jnp.where(lanef == e1, 1.0, 0.0)
    oh2 = jnp.where(lanef == e2, 1.0, 0.0)
    oh = oh1 + oh2
    row = lax.broadcasted_iota(jnp.int32, (tm, tm), 0)
    colr = lax.broadcasted_iota(jnp.int32, (tm, tm), 1)
    ltri = jnp.where(colr < row, 1.0, 0.0).astype(BF16)
    before = jnp.dot(ltri, oh.astype(BF16), preferred_element_type=F32) + carry_ref[0:1, :]
    r1 = jnp.sum(oh1 * before, axis=-1, keepdims=True)
    r2 = jnp.sum(oh2 * before, axis=-1, keepdims=True)
    carry_ref[...] = carry_ref[...] + jnp.sum(oh, axis=0, keepdims=True)
    rec = jnp.zeros((tm, LANES), F32)
    for ln, val in ((ROUTE_E1, e1), (ROUTE_E2, e2), (ROUTE_W1, w1), (ROUTE_W2, w2),
                    (ROUTE_R1, r1), (ROUTE_R2, r2)):
        rec = jnp.where(lane == ln, val, rec)
    return rec


def _merge_kernel(x_ref, o0_ref, l0_ref, o1_ref, l1_ref, o2_ref, l2_ref, ys_ref, gates_ref,
                  wbd_ref, wbs_ref, wo_ref, gffn_ref, wrt_ref, brt_ref,
                  x1_ref, h2_ref, route_ref, cnt_ref,
                  uo1, ul1, uo2, ul2, carry_ref, *, tm):
    first = (pl.program_id(0) == 0) & (pl.program_id(1) == 0)

    @pl.when(first)
    def _():
        carry_ref[...] = jnp.zeros_like(carry_ref)

    nch = GROUP_W // LANES
    for src_o, src_l, dst_o, dst_l, d in ((o1_ref, l1_ref, uo1, ul1, 4), (o2_ref, l2_ref, uo2, ul2, 16)):
        rows = tm // d
        for r in range(d):
            for c in range(nch):
                cl = slice(c * LANES, (c + 1) * LANES)
                dst_o[c, pl.ds(r, rows, stride=d), :] = src_o[r, :, cl].astype(F32)
                dst_l[c, pl.ds(r, rows, stride=d), :] = src_l[r, :, cl]
    ych = []
    for c in range(nch):
        cl = slice(c * LANES, (c + 1) * LANES)
        la, lb, lc = l0_ref[:, cl], ul1[c], ul2[c]
        mx = jnp.maximum(jnp.maximum(la, lb), lc)
        ea, eb, ec = jnp.exp(la - mx), jnp.exp(lb - mx), jnp.exp(lc - mx)
        num = ea * o0_ref[:, cl].astype(F32) + eb * uo1[c] + ec * uo2[c]
        ych.append((num * (1.0 / (ea + eb + ec))).astype(BF16))
    y = jnp.concatenate(ych, axis=1)
    a = jnp.dot(y, wbd_ref[...], preferred_element_type=F32)
    bsw = jnp.dot(ys_ref[...], wbs_ref[...], preferred_element_type=F32)
    gd = gates_ref[:, 0:1024].astype(F32)
    gs = gates_ref[:, 1024:2048].astype(F32)
    mixed = (jax.nn.sigmoid(gd) * a + jax.nn.sigmoid(gs) * bsw).astype(BF16)
    x1 = x_ref[...] + jnp.dot(mixed, wo_ref[...], preferred_element_type=F32)
    x1_ref[...] = x1
    ms = jnp.mean(x1 * x1, axis=-1, keepdims=True)
    h2 = x1 * lax.rsqrt(ms + RMS_EPS) * gffn_ref[...]
    for c in range(h2.shape[1] // LANES):
        h2_ref[pl.ds(c, tm, stride=ROW_TILE), :] = h2[:, c * LANES:(c + 1) * LANES]
    logits = jnp.dot(h2, wrt_ref[...], preferred_element_type=F32, precision=lax.Precision.HIGHEST) + brt_ref[...]
    route_ref[...] = _route(logits, carry_ref, tm)
    cnt_ref[...] = carry_ref[...]


def _merge_route(x, o0, l0, o1, l1, o2, l2, ys, gates, w_br_dil, w_br_swa, w_out, g_ffn,
                 w_group, b_group, w_router, b_router, *, tm):
    b, t, dm = x.shape
    nt = t // tm
    wrt = jnp.zeros((dm, LANES), F32).at[:, :MOE_GROUPS].set(w_group).at[
        :, ROUTER_LANE0:ROUTER_LANE0 + N_EXPERTS].set(w_router)
    brt = jnp.zeros((1, LANES), F32).at[0, :MOE_GROUPS].set(b_group).at[
        0, ROUTER_LANE0:ROUTER_LANE0 + N_EXPERTS].set(b_router)

    def tok(width):
        return pl.BlockSpec((None, tm, width), lambda bi, i: (bi, i, 0))

    def stream(d):
        return pl.BlockSpec((None, d, tm // d, GROUP_W), lambda bi, i: (bi, 0, i, 0))

    def const(shape):
        return pl.BlockSpec(shape, lambda bi, i: (0,) * len(shape), pipeline_mode=pl.Buffered(1))

    nch = GROUP_W // LANES
    return pl.pallas_call(
        functools.partial(_merge_kernel, tm=tm),
        out_shape=(jax.ShapeDtypeStruct((b, t, dm), F32), jax.ShapeDtypeStruct((b, t * ROW_TILE, LANES), F32),
                   jax.ShapeDtypeStruct((b, t, LANES), F32), jax.ShapeDtypeStruct((8, LANES), F32)),
        grid=(b, nt),
        in_specs=[tok(dm), tok(GROUP_W), tok(GROUP_W), stream(4), stream(4), stream(16), stream(16),
                  tok(SWA_Q_W), tok(2048),
                  const((GROUP_W, dm)), const((SWA_Q_W, dm)), const((dm, dm)), const((1, dm)),
                  const((dm, LANES)), const((1, LANES))],
        out_specs=(tok(dm), pl.BlockSpec((None, tm * ROW_TILE, LANES), lambda bi, i: (bi, i, 0)), tok(LANES),
                   pl.BlockSpec((8, LANES), lambda bi, i: (0, 0))),
        scratch_shapes=[pltpu.VMEM((nch, tm, LANES), F32) for _ in range(4)] + [pltpu.VMEM((8, LANES), F32)],
        compiler_params=pltpu.CompilerParams(
            dimension_semantics=("arbitrary", "arbitrary"), vmem_limit_bytes=VMEM_LIMIT),
        name="merge_route",
    )(x, o0, l0, o1, l1, o2, l2, ys, gates, w_br_dil.astype(BF16), w_br_swa.astype(BF16), w_out.astype(BF16),
      g_ffn.reshape(1, dm), wrt, brt)


def _dispatch_kernel(pos_ref, zoff_ref, nt_ref, h_ref, xs_ref, zbuf, sem, zsem, *, td, tme, nt_max):
    step = pl.program_id(0)

    def _zero_copy(e):
        off = pl.multiple_of(zoff_ref[e] * ROW_TILE, tme * ROW_TILE)
        return pltpu.make_async_copy(zbuf, xs_ref.at[pl.ds(off, tme * ROW_TILE)], zsem)

    def _row_copy(src_row, dst_row):
        return pltpu.make_async_copy(h_ref.at[pl.ds(pl.multiple_of(src_row * ROW_TILE, ROW_TILE), ROW_TILE)],
                                     xs_ref.at[pl.ds(pl.multiple_of(dst_row * ROW_TILE, ROW_TILE), ROW_TILE)], sem)

    def _tail_copy(tile):
        off = pl.multiple_of(tile * (tme * ROW_TILE), tme * ROW_TILE)
        return pltpu.make_async_copy(zbuf, xs_ref.at[pl.ds(off, tme * ROW_TILE)], zsem)

    @pl.when(step == 0)
    def _():
        zbuf[...] = jnp.zeros_like(zbuf)
        for e in range(N_EXPERTS):
            _zero_copy(e).start()
        for e in range(N_EXPERTS):
            _zero_copy(e).wait()

        def tail(tile, carry):
            _tail_copy(tile).start()
            _tail_copy(tile).wait()
            return carry

        lax.fori_loop(nt_ref[0], nt_max, tail, 0)

    base = step * td

    def issue(j, carry):
        for k in range(2):
            _row_copy(base + j, pos_ref[0, 0, 2 * j + k]).start()
        return carry

    lax.fori_loop(0, td, issue, 0)

    def drain(j, carry):
        for k in range(2):
            _row_copy(base, 0).wait()
        return carry

    lax.fori_loop(0, td, drain, 0)


def _dispatch(h2, pos, zoff, ntiles, nslot, *, td, tme):
    n = h2.shape[0] // ROW_TILE
    nb = n // td
    grid_spec = pltpu.PrefetchScalarGridSpec(
        num_scalar_prefetch=0,
        grid=(nb,),
        in_specs=[
            pl.BlockSpec((1, 1, 2 * td), lambda i: (i, 0, 0), memory_space=pltpu.SMEM),
            pl.BlockSpec(memory_space=pltpu.SMEM),
            pl.BlockSpec(memory_space=pltpu.SMEM),
            pl.BlockSpec(memory_space=pl.ANY),
        ],
        out_specs=pl.BlockSpec(memory_space=pl.ANY),
        scratch_shapes=[pltpu.VMEM((tme * ROW_TILE, LANES), F32), pltpu.SemaphoreType.DMA(()),
                        pltpu.SemaphoreType.DMA(())],
    )
    return pl.pallas_call(
        functools.partial(_dispatch_kernel, td=td, tme=tme, nt_max=nslot // tme),
        out_shape=jax.ShapeDtypeStruct((nslot * ROW_TILE, LANES), F32),
        grid_spec=grid_spec,
        compiler_params=pltpu.CompilerParams(dimension_semantics=("arbitrary",), has_side_effects=True),
        name="dispatch",
    )(pos.reshape(nb, 1, 2 * td), zoff, ntiles, h2)


def _expert_kernel(te_ref, tb_ref, nt_ref, x_ref, wg_ref, wu_ref, wd_ref, y_ref, *, tme):
    @pl.when(pl.program_id(0) < nt_ref[0])
    def _():
        xb = jnp.concatenate([x_ref[pl.ds(c, tme, stride=ROW_TILE), :] for c in range(ROW_TILE)],
                             axis=1).astype(BF16)
        g = jnp.dot(xb, wg_ref[...], preferred_element_type=F32)
        u = jnp.dot(xb, wu_ref[...], preferred_element_type=F32)
        a = (g * jax.nn.sigmoid(g) * u).astype(BF16)
        y = jnp.dot(a, wd_ref[...], preferred_element_type=F32)
        for c in range(ROW_TILE):
            y_ref[pl.ds(c, tme, stride=ROW_TILE), :] = y[:, c * LANES:(c + 1) * LANES]

    @pl.when(pl.program_id(0) >= nt_ref[0])
    def _():
        y_ref[...] = jnp.zeros_like(y_ref)


def _experts(xs, tile_expert, tile_block, ntiles, w_e_gate, w_e_up, w_e_down, *, tme):
    nslot = xs.shape[0] // ROW_TILE
    dm = ROW_TILE * LANES
    nt = nslot // tme
    grid_spec = pltpu.PrefetchScalarGridSpec(
        num_scalar_prefetch=3,
        grid=(nt,),
        in_specs=[
            pl.BlockSpec((tme * ROW_TILE, LANES), lambda i, te, tb, n: (tb[i], 0)),
            pl.BlockSpec((None, dm, D_EXPERT), lambda i, te, tb, n: (te[i], 0, 0)),
            pl.BlockSpec((None, dm, D_EXPERT), lambda i, te, tb, n: (te[i], 0, 0)),
            pl.BlockSpec((None, D_EXPERT, dm), lambda i, te, tb, n: (te[i], 0, 0)),
        ],
        out_specs=pl.BlockSpec((tme * ROW_TILE, LANES), lambda i, te, tb, n: (i, 0)),
    )
    return pl.pallas_call(
        functools.partial(_expert_kernel, tme=tme),
        out_shape=jax.ShapeDtypeStruct((nslot * ROW_TILE, LANES), F32),
        grid_spec=grid_spec,
        compiler_params=pltpu.CompilerParams(dimension_semantics=("arbitrary",), vmem_limit_bytes=VMEM_LIMIT),
        name="experts",
    )(tile_expert, tile_block, ntiles, xs, w_e_gate.astype(BF16), w_e_up.astype(BF16), w_e_down.astype(BF16))


def _combine_kernel(pos_ref, ys_ref, x1_ref, route_ref, gfin_ref, out_ref, ybuf, sem, *, tc):
    def _row_copy(src_row, k, dst_row):
        return pltpu.make_async_copy(ys_ref.at[pl.ds(pl.multiple_of(src_row * ROW_TILE, ROW_TILE), ROW_TILE)],
                                     ybuf.at[k, pl.ds(pl.multiple_of(dst_row * ROW_TILE, ROW_TILE), ROW_TILE)], sem)

    def issue(j, carry):
        for k in range(2):
            _row_copy(pos_ref[0, 0, 2 * j + k], k, j).start()
        return carry

    lax.fori_loop(0, tc, issue, 0)

    def drain(j, carry):
        for k in range(2):
            _row_copy(0, k, 0).wait()
        return carry

    lax.fori_loop(0, tc, drain, 0)
    rec = route_ref[...]
    w1 = rec[:, ROUTE_W1:ROUTE_W1 + 1]
    w2 = rec[:, ROUTE_W2:ROUTE_W2 + 1]
    y1 = jnp.concatenate([ybuf[0, pl.ds(c, tc, stride=ROW_TILE), :] for c in range(ROW_TILE)], axis=1)
    y2 = jnp.concatenate([ybuf[1, pl.ds(c, tc, stride=ROW_TILE), :] for c in range(ROW_TILE)], axis=1)
    z = x1_ref[...] + w1 * y1 + w2 * y2
    ms = jnp.mean(z * z, axis=-1, keepdims=True)
    out_ref[...] = z * lax.rsqrt(ms + RMS_EPS) * gfin_ref[...]


def _combine(ys, pos, x1, route, g_final, *, tc):
    n, dm = x1.shape
    nb = n // tc
    grid_spec = pltpu.PrefetchScalarGridSpec(
        num_scalar_prefetch=0,
        grid=(nb,),
        in_specs=[
            pl.BlockSpec((1, 1, 2 * tc), lambda i: (i, 0, 0), memory_space=pltpu.SMEM),
            pl.BlockSpec(memory_space=pl.ANY),
            pl.BlockSpec((tc, dm), lambda i: (i, 0)),
            pl.BlockSpec((tc, LANES), lambda i: (i, 0)),
            pl.BlockSpec((1, dm), lambda i: (0, 0)),
        ],
        out_specs=pl.BlockSpec((tc, dm), lambda i: (i, 0)),
        scratch_shapes=[pltpu.VMEM((2, tc * ROW_TILE, LANES), F32), pltpu.SemaphoreType.DMA(())],
    )
    return pl.pallas_call(
        functools.partial(_combine_kernel, tc=tc),
        out_shape=jax.ShapeDtypeStruct((n, dm), F32),
        grid_spec=grid_spec,
        compiler_params=pltpu.CompilerParams(dimension_semantics=("arbitrary",), vmem_limit_bytes=VMEM_LIMIT),
        name="combine",
    )(pos.reshape(nb, 1, 2 * tc), ys, x1, route, g_final.reshape(1, dm))


def _slot_layout(route, counts, n, tme):
    cnt = counts[0, :N_EXPERTS].astype(jnp.int32)
    tiles_per = (cnt + tme - 1) // tme
    tile_end = jnp.cumsum(tiles_per)
    seg_base = (tile_end - tiles_per) * tme
    ntiles = tile_end[-1]
    nt_max = (2 * n) // tme + N_EXPERTS
    tid = jnp.arange(nt_max, dtype=jnp.int32)
    live = jnp.minimum(tid, ntiles - 1)
    tile_expert = jnp.searchsorted(tile_end, live, side="right").astype(jnp.int32)
    e12 = route[:, ROUTE_E1:ROUTE_E2 + 1].astype(jnp.int32)
    r12 = route[:, ROUTE_R1:ROUTE_R2 + 1].astype(jnp.int32)
    pos = jnp.take(seg_base, e12) + r12
    zoff = jnp.maximum(tile_end - 1, 0) * tme
    return pos, zoff.astype(jnp.int32), tile_expert, live.astype(jnp.int32), ntiles.reshape(1).astype(jnp.int32)


def kernel(x, g_mix, w_in, sinks, w_br_dil, w_br_swa, w_out, g_ffn, w_group, b_group, w_router, b_router,
           w_e_gate, w_e_up, w_e_down, g_final):
    b, t, dm = x.shape
    n = b * t
    tme = 256
    qkv0, swa, gates, qkv4, qkv16 = _inproj(x, g_mix[0], w_in[0], tm=512)
    o0, l0 = _dil_attention(qkv0.reshape(b, 1, t, QKV_W), 0, tq=512)
    o1, l1 = _dil_attention(qkv4, 1, tq=512)
    o2, l2 = _dil_attention(qkv16, 2, tq=512)
    ys = _swa_attention(swa, sinks[0], tq=512)
    x1, h2, route, counts = _merge_route(
        x, o0.reshape(b, t, GROUP_W), l0.reshape(b, t, GROUP_W), o1, l1, o2, l2, ys, gates,
        w_br_dil[0], w_br_swa[0], w_out[0], g_ffn[0], w_group[0], b_group[0], w_router[0], b_router[0], tm=512)
    route = route.reshape(n, LANES)
    pos, zoff, tile_expert, tile_block, ntiles = _slot_layout(route, counts, n, tme)
    nslot = 2 * n + N_EXPERTS * tme
    xs = _dispatch(h2.reshape(n * ROW_TILE, LANES), pos, zoff, ntiles, nslot, td=512, tme=tme)
    yslots = _experts(xs, tile_expert, tile_block, ntiles, w_e_gate[0], w_e_up[0], w_e_down[0], tme=tme)
    out = _combine(yslots, pos, x1.reshape(n, dm), route, g_final, tc=256)
    return out.reshape(b, t, dm)
```

```python
import functools

import numpy as np
import jax
import jax.numpy as jnp
from jax import lax
from jax.experimental import pallas as pl
from jax.experimental.pallas import tpu as pltpu

F32 = jnp.float32
BF16 = jnp.bfloat16

HEAD_DIM = 64
BAND = 128
DIL_PATTERNS = ((128, 1), (512, 4), (2048, 16))
DIL_HEADS_PER_GROUP = 8
DIL_HEADS = 24
GROUP_W = DIL_HEADS_PER_GROUP * HEAD_DIM
QKV_W = 3 * GROUP_W
SWA_WINDOW = 128
SWA_Q_HEADS = 16
SWA_KV_HEADS = 2
SWA_Q_W = SWA_Q_HEADS * HEAD_DIM
SWA_W = SWA_Q_W + 4 * 128
MOE_GROUPS = 4
EXPERTS_PER_GROUP = 8
N_EXPERTS = 32
D_EXPERT = 512
RMS_EPS = 1e-6
LANES = 128
NEG_INF = float("-inf")
ROW_TILE = 8

VMEM_LIMIT = 56 * 1024 * 1024


def _alibi_slopes(n):
    return (2.0 ** (-8.0 * np.arange(1, n + 1) / n)).astype(np.float32)


def _band_bias(slopes, max_back, unit):
    a = np.arange(BAND)[:, None]
    c = np.arange(2 * BAND)[None, :]
    delta = a + BAND - c
    band = (delta >= 0) & (delta <= max_back)
    pen = (-slopes[:, None, None] * (delta * unit)[None]).astype(np.float32)
    full = np.where(band[None], pen, -np.inf).astype(np.float32)
    first = np.where((band & (c >= BAND))[None], pen, -np.inf).astype(np.float32)
    return np.stack([full, first], axis=0)


def _inproj_kernel(x_ref, g_ref, w_ref, qkv0_ref, swa_ref, gates_ref, qkv4_ref, qkv16_ref, hs_ref, *, tm):
    x = x_ref[...]
    ms = jnp.mean(x * x, axis=-1, keepdims=True)
    h = x * lax.rsqrt(ms + RMS_EPS) * g_ref[...]
    nchunk = h.shape[1] // LANES
    for c in range(nchunk):
        hs_ref[c] = h[:, c * LANES:(c + 1) * LANES]
    hb = h.astype(BF16)

    def proj(lhs, c0, width):
        return jnp.dot(lhs, w_ref[:, c0:c0 + width], preferred_element_type=F32).astype(BF16)

    col = 0
    for out_ref, width in ((qkv0_ref, QKV_W), (swa_ref, SWA_W), (gates_ref, 2048)):
        for c0 in range(0, width, 512):
            out_ref[:, c0:c0 + 512] = proj(hb, col + c0, 512)
        col += width
    for out_ref, d in ((qkv4_ref, 4), (qkv16_ref, 16)):
        rows = tm // d
        hp = jnp.concatenate(
            [jnp.concatenate([hs_ref[c, pl.ds(r, rows, stride=d), :] for c in range(nchunk)], axis=1)
             for r in range(d)], axis=0).astype(BF16)
        for c0 in range(0, QKV_W, 512):
            res = proj(hp, col + c0, 512)
            for r in range(d):
                out_ref[r, :, c0:c0 + 512] = res[r * rows:(r + 1) * rows]
        col += QKV_W


def _prep_w_in(w_in):
    scale = HEAD_DIM ** -0.5
    o_qd, o_kd, o_vd, o_qs, o_ks, o_vs, o_g = 0, 1536, 3072, 4608, 5632, 5760, 5888

    def grp(g):
        return [w_in[:, o_qd + g * 512:o_qd + (g + 1) * 512] * scale,
                w_in[:, o_kd + g * 512:o_kd + (g + 1) * 512],
                w_in[:, o_vd + g * 512:o_vd + (g + 1) * 512]]

    k0, k1 = w_in[:, o_ks:o_ks + 64], w_in[:, o_ks + 64:o_ks + 128]
    v0, v1 = w_in[:, o_vs:o_vs + 64], w_in[:, o_vs + 64:o_vs + 128]
    cols = grp(0) + [w_in[:, o_qs:o_qs + 1024] * scale, k0, k0, k1, k1, v0, v0, v1, v1,
                     w_in[:, o_g:o_g + 2048]] + grp(1) + grp(2)
    return jnp.concatenate(cols, axis=1).astype(BF16)


def _inproj(x, g_mix, w_in, *, tm):
    b, t, dm = x.shape
    wp = _prep_w_in(w_in)
    nw = wp.shape[1]
    nt = t // tm
    out_shape = (
        jax.ShapeDtypeStruct((b, t, QKV_W), BF16),
        jax.ShapeDtypeStruct((b, t, SWA_W), BF16),
        jax.ShapeDtypeStruct((b, t, 2048), BF16),
        jax.ShapeDtypeStruct((b, 4, t // 4, QKV_W), BF16),
        jax.ShapeDtypeStruct((b, 16, t // 16, QKV_W), BF16),
    )
    return pl.pallas_call(
        functools.partial(_inproj_kernel, tm=tm),
        out_shape=out_shape,
        grid=(b, nt),
        in_specs=[
            pl.BlockSpec((None, tm, dm), lambda bi, i: (bi, i, 0)),
            pl.BlockSpec((1, dm), lambda bi, i: (0, 0)),
            pl.BlockSpec((dm, nw), lambda bi, i: (0, 0), pipeline_mode=pl.Buffered(1)),
        ],
        out_specs=(
            pl.BlockSpec((None, tm, QKV_W), lambda bi, i: (bi, i, 0)),
            pl.BlockSpec((None, tm, SWA_W), lambda bi, i: (bi, i, 0)),
            pl.BlockSpec((None, tm, 2048), lambda bi, i: (bi, i, 0)),
            pl.BlockSpec((None, 4, tm // 4, QKV_W), lambda bi, i: (bi, 0, i, 0)),
            pl.BlockSpec((None, 16, tm // 16, QKV_W), lambda bi, i: (bi, 0, i, 0)),
        ),
        scratch_shapes=[pltpu.VMEM((dm // LANES, tm, LANES), F32)],
        compiler_params=pltpu.CompilerParams(
            dimension_semantics=("arbitrary", "arbitrary"), vmem_limit_bytes=VMEM_LIMIT),
        name="inproj",
    )(x, g_mix.reshape(1, dm), wp)


def _softmax_pv(s, vv, sink=None):
    m = jnp.max(s, axis=-1, keepdims=True)
    if sink is not None:
        m = jnp.maximum(m, sink)
    p = jnp.exp(s - m)
    l = jnp.sum(p, axis=-1, keepdims=True)
    if sink is not None:
        l = l + jnp.exp(sink - m)
    o = jnp.dot(p.astype(BF16), vv, preferred_element_type=F32)
    return o * (1.0 / l), m + jnp.log(l)


def _dil_attn_kernel(q_ref, kc_ref, vc_ref, kp_ref, vp_ref, bias_ref, o_ref, lse_ref, kbuf, vbuf, *, tq):
    i = pl.program_id(2)
    kbuf[0:BAND] = kp_ref[...]
    kbuf[BAND:BAND + tq] = kc_ref[...]
    vbuf[0:BAND] = vp_ref[...]
    vbuf[BAND:BAND + tq] = vc_ref[...]
    lo = lax.broadcasted_iota(jnp.int32, (BAND, LANES), 1) < HEAD_DIM
    zero = jnp.zeros((BAND, LANES), BF16)
    for qb in range(tq // BAND):
        sel = jnp.where(i == 0, 1, 0) if qb == 0 else 0
        rq = slice(qb * BAND, (qb + 1) * BAND)
        rk = slice(qb * BAND, qb * BAND + 2 * BAND)
        for pr in range(GROUP_W // LANES):
            cl = slice(pr * LANES, (pr + 1) * LANES)
            q2 = q_ref[rq, cl]
            kk = kbuf[rk, cl]
            vv = vbuf[rk, cl]
            res = []
            for hh, qm in enumerate((jnp.where(lo, q2, zero), jnp.where(lo, zero, q2))):
                s = lax.dot_general(qm, kk, (((1,), (1,)), ((), ())), preferred_element_type=F32)
                res.append(_softmax_pv(s + bias_ref[sel, 2 * pr + hh], vv))
            o_ref[rq, cl] = jnp.where(lo, res[0][0], res[1][0]).astype(o_ref.dtype)
            lse_ref[rq, cl] = jnp.where(lo, res[0][1], res[1][1])


def _dil_attention(qkv, gi, *, tq):
    b, d, l, _ = qkv.shape
    window, dil = DIL_PATTERNS[gi]
    assert dil == d
    tq = min(tq, l)
    nq = tq // BAND
    slopes = _alibi_slopes(DIL_HEADS)[gi * 8:(gi + 1) * 8]
    bias = jnp.asarray(_band_bias(slopes, window // dil, dil))

    def cur(c):
        return pl.BlockSpec((None, None, tq, GROUP_W), lambda bi, r, i: (bi, r, i, c))

    def prev(c):
        return pl.BlockSpec((None, None, BAND, GROUP_W), lambda bi, r, i: (bi, r, jnp.maximum(i * nq - 1, 0), c))

    return pl.pallas_call(
        functools.partial(_dil_attn_kernel, tq=tq),
        out_shape=(jax.ShapeDtypeStruct((b, d, l, GROUP_W), BF16),
                   jax.ShapeDtypeStruct((b, d, l, GROUP_W), F32)),
        grid=(b, d, l // tq),
        in_specs=[cur(0), cur(1), cur(2), prev(1), prev(2),
                  pl.BlockSpec((2, 8, BAND, 2 * BAND), lambda bi, r, i: (0, 0, 0, 0))],
        out_specs=(cur(0), cur(0)),
        scratch_shapes=[pltpu.VMEM((BAND + tq, GROUP_W), BF16), pltpu.VMEM((BAND + tq, GROUP_W), BF16)],
        compiler_params=pltpu.CompilerParams(
            dimension_semantics=("arbitrary", "arbitrary", "arbitrary"), vmem_limit_bytes=VMEM_LIMIT),
        name=f"dil_attn_{gi}",
    )(qkv, qkv, qkv, qkv, qkv, bias)


def _swa_attn_kernel(sink_ref, qlo_ref, qhi_ref, kvc_ref, kvp_ref, bias_ref, o_ref, kvbuf, *, tq):
    i = pl.program_id(1)
    kvbuf[0:BAND] = kvp_ref[...]
    kvbuf[BAND:BAND + tq] = kvc_ref[...]
    lo = lax.broadcasted_iota(jnp.int32, (BAND, LANES), 1) < HEAD_DIM
    zero = jnp.zeros((BAND, LANES), BF16)
    for qb in range(tq // BAND):
        sel = jnp.where(i == 0, 1, 0) if qb == 0 else 0
        rq = slice(qb * BAND, (qb + 1) * BAND)
        rk = slice(qb * BAND, qb * BAND + 2 * BAND)
        for pr in range(SWA_Q_HEADS // 2):
            g = pr // 4
            q_ref = qlo_ref if pr < 4 else qhi_ref
            q2 = q_ref[rq, (pr % 4) * LANES:(pr % 4 + 1) * LANES]
            kk = kvbuf[rk, g * LANES:(g + 1) * LANES]
            vv = kvbuf[rk, (2 + g) * LANES:(3 + g) * LANES]
            res = []
            for hh, qm in enumerate((jnp.where(lo, q2, zero), jnp.where(lo, zero, q2))):
                h = 2 * pr + hh
                s = lax.dot_general(qm, kk, (((1,), (1,)), ((), ())), preferred_element_type=F32)
                res.append(_softmax_pv(s + bias_ref[sel, h], vv, sink=sink_ref[h])[0])
            o_ref[rq, pr * LANES:(pr + 1) * LANES] = jnp.where(lo, res[0], res[1]).astype(o_ref.dtype)


def _swa_attention(swa, sinks, *, tq):
    b, t, _ = swa.shape
    tq = min(tq, t)
    nq = tq // BAND
    bias = jnp.asarray(_band_bias(_alibi_slopes(SWA_Q_HEADS), SWA_WINDOW - 1, 1))
    grid_spec = pltpu.PrefetchScalarGridSpec(
        num_scalar_prefetch=1,
        grid=(b, t // tq),
        in_specs=[
            pl.BlockSpec((None, tq, 512), lambda bi, i, s: (bi, i, 0)),
            pl.BlockSpec((None, tq, 512), lambda bi, i, s: (bi, i, 1)),
            pl.BlockSpec((None, tq, 512), lambda bi, i, s: (bi, i, 2)),
            pl.BlockSpec((None, BAND, 512), lambda bi, i, s: (bi, jnp.maximum(i * nq - 1, 0), 2)),
            pl.BlockSpec((2, SWA_Q_HEADS, BAND, 2 * BAND), lambda bi, i, s: (0, 0, 0, 0)),
        ],
        out_specs=pl.BlockSpec((None, tq, SWA_Q_W), lambda bi, i, s: (bi, i, 0)),
        scratch_shapes=[pltpu.VMEM((BAND + tq, 512), BF16)],
    )
    return pl.pallas_call(
        functools.partial(_swa_attn_kernel, tq=tq),
        out_shape=jax.ShapeDtypeStruct((b, t, SWA_Q_W), BF16),
        grid_spec=grid_spec,
        compiler_params=pltpu.CompilerParams(
            dimension_semantics=("arbitrary", "arbitrary"), vmem_limit_bytes=VMEM_LIMIT),
        name="swa_attn",
    )(sinks.astype(F32), swa, swa, swa, swa, bias)


ROUTE_E1, ROUTE_E2, ROUTE_W1, ROUTE_W2, ROUTE_R1, ROUTE_R2 = range(6)
ROUTER_LANE0 = MOE_GROUPS


def _route(logits, carry_ref, tm):
    lane = lax.broadcasted_iota(jnp.int32, (tm, LANES), 1)
    lanef = lane.astype(F32)

    def first_argmax(v):
        m = jnp.max(v, axis=-1, keepdims=True)
        return m, jnp.min(jnp.where(v == m, lanef, float(LANES)), axis=-1, keepdims=True)

    gl = jnp.where(lane < MOE_GROUPS, logits, NEG_INF)
    gmax, gidx = first_argmax(gl)
    g_w = 1.0 / jnp.sum(jnp.exp(gl - gmax), axis=-1, keepdims=True)
    e_lane = lane - ROUTER_LANE0
    lane_group = (e_lane >> 3).astype(F32)
    in_group = (e_lane >= 0) & (e_lane < N_EXPERTS) & (lane_group == gidx)
    el = jnp.where(in_group, logits, NEG_INF)
    m1, i1 = first_argmax(el)
    m2, i2 = first_argmax(jnp.where(lanef == i1, NEG_INF, el))
    tt = jnp.exp(m2 - m1)
    w1 = g_w / (1.0 + tt)
    w2 = g_w * tt / (1.0 + tt)
    e1 = i1 - float(ROUTER_LANE0)
    e2 = i2 - float(ROUTER_LANE0)
    oh1 = jnp.where(lanef == e1, 1.0, 0.0)
    oh2 = jnp.where(lanef == e2, 1.0, 0.0)
    oh = oh1 + oh2
    row = lax.broadcasted_iota(jnp.int32, (tm, tm), 0)
    colr = lax.broadcasted_iota(jnp.int32, (tm, tm), 1)
    ltri = jnp.where(colr < row, 1.0, 0.0).astype(BF16)
    before = jnp.dot(ltri, oh.astype(BF16), preferred_element_type=F32) + carry_ref[0:1, :]
    r1 = jnp.sum(oh1 * before, axis=-1, keepdims=True)
    r2 = jnp.sum(oh2 * before, axis=-1, keepdims=True)
    carry_ref[...] = carry_ref[...] + jnp.sum(oh, axis=0, keepdims=True)
    rec = jnp.zeros((tm, LANES), F32)
    for ln, val in ((ROUTE_E1, e1), (ROUTE_E2, e2), (ROUTE_W1, w1), (ROUTE_W2, w2),
                    (ROUTE_R1, r1), (ROUTE_R2, r2)):
        rec = jnp.where(lane == ln, val, rec)
    return rec


def _merge_kernel(x_ref, o0_ref, l0_ref, o1_ref, l1_ref, o2_ref, l2_ref, ys_ref, gates_ref,
                  wbd_ref, wbs_ref, wo_ref, gffn_ref, wrt_ref, brt_ref,
                  x1_ref, h2_ref, route_ref, cnt_ref,
                  uo1, ul1, uo2, ul2, carry_ref, *, tm):
    first = (pl.program_id(0) == 0) & (pl.program_id(1) == 0)

    @pl.when(first)
    def _():
        carry_ref[...] = jnp.zeros_like(carry_ref)

    nch = GROUP_W // LANES
    for src_o, src_l, dst_o, dst_l, d in ((o1_ref, l1_ref, uo1, ul1, 4), (o2_ref, l2_ref, uo2, ul2, 16)):
        rows = tm // d
        for r in range(d):
            for c in range(nch):
                cl = slice(c * LANES, (c + 1) * LANES)
                dst_o[c, pl.ds(r, rows, stride=d), :] = src_o[r, :, cl].astype(F32)
                dst_l[c, pl.ds(r, rows, stride=d), :] = src_l[r, :, cl]
    ych = []
    for c in range(nch):
        cl = slice(c * LANES, (c + 1) * LANES)
        la, lb, lc = l0_ref[:, cl], ul1[c], ul2[c]
        mx = jnp.maximum(jnp.maximum(la, lb), lc)
        ea, eb, ec = jnp.exp(la - mx), jnp.exp(lb - mx), jnp.exp(lc - mx)
        num = ea * o0_ref[:, cl].astype(F32) + eb * uo1[c] + ec * uo2[c]
        ych.append((num * (1.0 / (ea + eb + ec))).astype(BF16))
    y = jnp.concatenate(ych, axis=1)
    a = jnp.dot(y, wbd_ref[...], preferred_element_type=F32)
    bsw = jnp.dot(ys_ref[...], wbs_ref[...], preferred_element_type=F32)
    gd = gates_ref[:, 0:1024].astype(F32)
    gs = gates_ref[:, 1024:2048].astype(F32)
    mixed = (jax.nn.sigmoid(gd) * a + jax.nn.sigmoid(gs) * bsw).astype(BF16)
    x1 = x_ref[...] + jnp.dot(mixed, wo_ref[...], preferred_element_type=F32)
    x1_ref[...] = x1
    ms = jnp.mean(x1 * x1, axis=-1, keepdims=True)
    h2 = x1 * lax.rsqrt(ms + RMS_EPS) * gffn_ref[...]
    for c in range(h2.shape[1] // LANES):
        h2_ref[pl.ds(c, tm, stride=ROW_TILE), :] = h2[:, c * LANES:(c + 1) * LANES]
    h_hi = h2.astype(BF16)
    h_lo = (h2 - h_hi.astype(F32)).astype(BF16)
    hw = jnp.dot(h_hi, wrt_ref[...], preferred_element_type=F32)
    logits = (hw[:, :LANES] + hw[:, LANES:]
              + jnp.dot(h_lo, wrt_ref[:, :LANES], preferred_element_type=F32) + brt_ref[...])
    route_ref[...] = _route(logits, carry_ref, tm)
    cnt_ref[...] = carry_ref[...]


def _merge_route(x, o0, l0, o1, l1, o2, l2, ys, gates, w_br_dil, w_br_swa, w_out, g_ffn,
                 w_group, b_group, w_router, b_router, *, tm):
    b, t, dm = x.shape
    nt = t // tm
    wrt = jnp.zeros((dm, LANES), F32).at[:, :MOE_GROUPS].set(w_group).at[
        :, ROUTER_LANE0:ROUTER_LANE0 + N_EXPERTS].set(w_router)
    brt = jnp.zeros((1, LANES), F32).at[0, :MOE_GROUPS].set(b_group).at[
        0, ROUTER_LANE0:ROUTER_LANE0 + N_EXPERTS].set(b_router)
    wrt_hi = wrt.astype(BF16)
    wrt_lo = (wrt - wrt_hi.astype(F32)).astype(BF16)
    wrt = jnp.concatenate([wrt_hi, wrt_lo], axis=1)

    def tok(width):
        return pl.BlockSpec((None, tm, width), lambda bi, i: (bi, i, 0))

    def stream(d):
        return pl.BlockSpec((None, d, tm // d, GROUP_W), lambda bi, i: (bi, 0, i, 0))

    def const(shape):
        return pl.BlockSpec(shape, lambda bi, i: (0,) * len(shape), pipeline_mode=pl.Buffered(1))

    nch = GROUP_W // LANES
    return pl.pallas_call(
        functools.partial(_merge_kernel, tm=tm),
        out_shape=(jax.ShapeDtypeStruct((b, t, dm), F32), jax.ShapeDtypeStruct((b, t * ROW_TILE, LANES), F32),
                   jax.ShapeDtypeStruct((b, t, LANES), F32), jax.ShapeDtypeStruct((8, LANES), F32)),
        grid=(b, nt),
        in_specs=[tok(dm), tok(GROUP_W), tok(GROUP_W), stream(4), stream(4), stream(16), stream(16),
                  tok(SWA_Q_W), tok(2048),
                  const((GROUP_W, dm)), const((SWA_Q_W, dm)), const((dm, dm)), const((1, dm)),
                  const((dm, 2 * LANES)), const((1, LANES))],
        out_specs=(tok(dm), pl.BlockSpec((None, tm * ROW_TILE, LANES), lambda bi, i: (bi, i, 0)), tok(LANES),
                   pl.BlockSpec((8, LANES), lambda bi, i: (0, 0))),
        scratch_shapes=[pltpu.VMEM((nch, tm, LANES), F32) for _ in range(4)] + [pltpu.VMEM((8, LANES), F32)],
        compiler_params=pltpu.CompilerParams(
            dimension_semantics=("arbitrary", "arbitrary"), vmem_limit_bytes=VMEM_LIMIT),
        name="merge_route",
    )(x, o0, l0, o1, l1, o2, l2, ys, gates, w_br_dil.astype(BF16), w_br_swa.astype(BF16), w_out.astype(BF16),
      g_ffn.reshape(1, dm), wrt, brt)


def _dispatch_kernel(pos_ref, zoff_ref, nt_ref, h_ref, xs_ref, zbuf, sem, zsem, *, td, tme, nt_max):
    step = pl.program_id(0)

    def _zero_copy(e):
        off = pl.multiple_of(zoff_ref[e] * ROW_TILE, tme * ROW_TILE)
        return pltpu.make_async_copy(zbuf, xs_ref.at[pl.ds(off, tme * ROW_TILE)], zsem)

    def _row_copy(src_row, dst_row):
        return pltpu.make_async_copy(h_ref.at[pl.ds(pl.multiple_of(src_row * ROW_TILE, ROW_TILE), ROW_TILE)],
                                     xs_ref.at[pl.ds(pl.multiple_of(dst_row * ROW_TILE, ROW_TILE), ROW_TILE)], sem)

    def _wait_rows():
        pltpu.make_async_copy(h_ref, xs_ref.at[pl.ds(0, td * ROW_TILE)], sem).wait()

    def _tail_copy(tile):
        off = pl.multiple_of(tile * (tme * ROW_TILE), tme * ROW_TILE)
        return pltpu.make_async_copy(zbuf, xs_ref.at[pl.ds(off, tme * ROW_TILE)], zsem)

    @pl.when(step == 0)
    def _():
        zbuf[...] = jnp.zeros_like(zbuf)
        for e in range(N_EXPERTS):
            _zero_copy(e).start()
        for e in range(N_EXPERTS):
            _zero_copy(e).wait()

        def tail(tile, carry):
            _tail_copy(tile).start()
            _tail_copy(tile).wait()
            return carry

        lax.fori_loop(nt_ref[0], nt_max, tail, 0)

    def issue(j, carry):
        for k in range(2):
            _row_copy(j, pos_ref[0, 0, 2 * j + k]).start()
        return carry

    lax.fori_loop(0, td, issue, 0, unroll=8)
    _wait_rows()
    _wait_rows()


def _dispatch(h2, pos, zoff, ntiles, nslot, *, td, tme):
    n = h2.shape[0] // ROW_TILE
    nb = n // td
    grid_spec = pltpu.PrefetchScalarGridSpec(
        num_scalar_prefetch=0,
        grid=(nb,),
        in_specs=[
            pl.BlockSpec((1, 1, 2 * td), lambda i: (i, 0, 0), memory_space=pltpu.SMEM),
            pl.BlockSpec(memory_space=pltpu.SMEM),
            pl.BlockSpec(memory_space=pltpu.SMEM),
            pl.BlockSpec((td * ROW_TILE, LANES), lambda i: (i, 0)),
        ],
        out_specs=pl.BlockSpec(memory_space=pl.ANY),
        scratch_shapes=[pltpu.VMEM((tme * ROW_TILE, LANES), F32), pltpu.SemaphoreType.DMA(()),
                        pltpu.SemaphoreType.DMA(())],
    )
    return pl.pallas_call(
        functools.partial(_dispatch_kernel, td=td, tme=tme, nt_max=nslot // tme),
        out_shape=jax.ShapeDtypeStruct((nslot * ROW_TILE, LANES), F32),
        grid_spec=grid_spec,
        compiler_params=pltpu.CompilerParams(dimension_semantics=("arbitrary",), has_side_effects=True),
        name="dispatch",
    )(pos.reshape(nb, 1, 2 * td), zoff, ntiles, h2)


def _expert_kernel(te_ref, tb_ref, nt_ref, x_ref, wg_ref, wu_ref, wd_ref, y_ref, *, tme):
    @pl.when(pl.program_id(0) < nt_ref[0])
    def _():
        xb = jnp.concatenate([x_ref[pl.ds(c, tme, stride=ROW_TILE), :] for c in range(ROW_TILE)],
                             axis=1).astype(BF16)
        g = jnp.dot(xb, wg_ref[...], preferred_element_type=F32)
        u = jnp.dot(xb, wu_ref[...], preferred_element_type=F32)
        a = (g * jax.nn.sigmoid(g) * u).astype(BF16)
        y = jnp.dot(a, wd_ref[...], preferred_element_type=F32)
        for c in range(ROW_TILE):
            y_ref[pl.ds(c, tme, stride=ROW_TILE), :] = y[:, c * LANES:(c + 1) * LANES]

    @pl.when(pl.program_id(0) >= nt_ref[0])
    def _():
        y_ref[...] = jnp.zeros_like(y_ref)


def _experts(xs, tile_expert, tile_block, ntiles, w_e_gate, w_e_up, w_e_down, *, tme):
    nslot = xs.shape[0] // ROW_TILE
    dm = ROW_TILE * LANES
    nt = nslot // tme
    grid_spec = pltpu.PrefetchScalarGridSpec(
        num_scalar_prefetch=3,
        grid=(nt,),
        in_specs=[
            pl.BlockSpec((tme * ROW_TILE, LANES), lambda i, te, tb, n: (tb[i], 0)),
            pl.BlockSpec((None, dm, D_EXPERT), lambda i, te, tb, n: (te[i], 0, 0)),
            pl.BlockSpec((None, dm, D_EXPERT), lambda i, te, tb, n: (te[i], 0, 0)),
            pl.BlockSpec((None, D_EXPERT, dm), lambda i, te, tb, n: (te[i], 0, 0)),
        ],
        out_specs=pl.BlockSpec((tme * ROW_TILE, LANES), lambda i, te, tb, n: (i, 0)),
    )
    return pl.pallas_call(
        functools.partial(_expert_kernel, tme=tme),
        out_shape=jax.ShapeDtypeStruct((nslot * ROW_TILE, LANES), F32),
        grid_spec=grid_spec,
        compiler_params=pltpu.CompilerParams(dimension_semantics=("arbitrary",), vmem_limit_bytes=VMEM_LIMIT),
        name="experts",
    )(tile_expert, tile_block, ntiles, xs, w_e_gate.astype(BF16), w_e_up.astype(BF16), w_e_down.astype(BF16))


def _combine_kernel(pos_ref, ys_ref, x1_ref, route_ref, gfin_ref, out_ref, ybuf, sem, *, tc):
    def _row_copy(src_row, k, dst_row):
        return pltpu.make_async_copy(ys_ref.at[pl.ds(pl.multiple_of(src_row * ROW_TILE, ROW_TILE), ROW_TILE)],
                                     ybuf.at[k, pl.ds(pl.multiple_of(dst_row * ROW_TILE, ROW_TILE), ROW_TILE)], sem)

    def issue(j, carry):
        for k in range(2):
            _row_copy(pos_ref[0, 0, 2 * j + k], k, j).start()
        return carry

    lax.fori_loop(0, tc, issue, 0, unroll=8)
    for k in range(2):
        pltpu.make_async_copy(ys_ref.at[pl.ds(0, tc * ROW_TILE)], ybuf.at[k], sem).wait()
    rec = route_ref[...]
    w1 = rec[:, ROUTE_W1:ROUTE_W1 + 1]
    w2 = rec[:, ROUTE_W2:ROUTE_W2 + 1]
    y1 = jnp.concatenate([ybuf[0, pl.ds(c, tc, stride=ROW_TILE), :] for c in range(ROW_TILE)], axis=1)
    y2 = jnp.concatenate([ybuf[1, pl.ds(c, tc, stride=ROW_TILE), :] for c in range(ROW_TILE)], axis=1)
    z = x1_ref[...] + w1 * y1 + w2 * y2
    ms = jnp.mean(z * z, axis=-1, keepdims=True)
    out_ref[...] = z * lax.rsqrt(ms + RMS_EPS) * gfin_ref[...]


def _combine(ys, pos, x1, route, g_final, *, tc):
    n, dm = x1.shape
    nb = n // tc
    grid_spec = pltpu.PrefetchScalarGridSpec(
        num_scalar_prefetch=0,
        grid=(nb,),
        in_specs=[
            pl.BlockSpec((1, 1, 2 * tc), lambda i: (i, 0, 0), memory_space=pltpu.SMEM),
            pl.BlockSpec(memory_space=pl.ANY),
            pl.BlockSpec((tc, dm), lambda i: (i, 0)),
            pl.BlockSpec((tc, LANES), lambda i: (i, 0)),
            pl.BlockSpec((1, dm), lambda i: (0, 0)),
        ],
        out_specs=pl.BlockSpec((tc, dm), lambda i: (i, 0)),
        scratch_shapes=[pltpu.VMEM((2, tc * ROW_TILE, LANES), F32), pltpu.SemaphoreType.DMA(())],
    )
    return pl.pallas_call(
        functools.partial(_combine_kernel, tc=tc),
        out_shape=jax.ShapeDtypeStruct((n, dm), F32),
        grid_spec=grid_spec,
        compiler_params=pltpu.CompilerParams(dimension_semantics=("arbitrary",), vmem_limit_bytes=VMEM_LIMIT),
        name="combine",
    )(pos.reshape(nb, 1, 2 * tc), ys, x1, route, g_final.reshape(1, dm))


def _slot_layout(route, counts, n, tme):
    cnt = counts[0, :N_EXPERTS].astype(jnp.int32)
    tiles_per = (cnt + tme - 1) // tme
    tile_end = jnp.cumsum(tiles_per)
    seg_base = (tile_end - tiles_per) * tme
    ntiles = tile_end[-1]
    nt_max = (2 * n) // tme + N_EXPERTS
    tid = jnp.arange(nt_max, dtype=jnp.int32)
    live = jnp.minimum(tid, ntiles - 1)
    tile_expert = jnp.sum((tile_end[None, :] <= live[:, None]).astype(jnp.int32), axis=1)
    e12 = route[:, ROUTE_E1:ROUTE_E2 + 1].astype(jnp.int32)
    r12 = route[:, ROUTE_R1:ROUTE_R2 + 1].astype(jnp.int32)
    pos = jnp.take(seg_base, e12) + r12
    zoff = jnp.maximum(tile_end - 1, 0) * tme
    return pos, zoff.astype(jnp.int32), tile_expert, live.astype(jnp.int32), ntiles.reshape(1).astype(jnp.int32)


def kernel(x, g_mix, w_in, sinks, w_br_dil, w_br_swa, w_out, g_ffn, w_group, b_group, w_router, b_router,
           w_e_gate, w_e_up, w_e_down, g_final):
    b, t, dm = x.shape
    n = b * t
    tme = 512
    qkv0, swa, gates, qkv4, qkv16 = _inproj(x, g_mix[0], w_in[0], tm=512)
    o0, l0 = _dil_attention(qkv0.reshape(b, 1, t, QKV_W), 0, tq=512)
    o1, l1 = _dil_attention(qkv4, 1, tq=512)
    o2, l2 = _dil_attention(qkv16, 2, tq=512)
    ys = _swa_attention(swa, sinks[0], tq=512)
    x1, h2, route, counts = _merge_route(
        x, o0.reshape(b, t, GROUP_W), l0.reshape(b, t, GROUP_W), o1, l1, o2, l2, ys, gates,
        w_br_dil[0], w_br_swa[0], w_out[0], g_ffn[0], w_group[0], b_group[0], w_router[0], b_router[0], tm=512)
    route = route.reshape(n, LANES)
    pos, zoff, tile_expert, tile_block, ntiles = _slot_layout(route, counts, n, tme)
    nslot = 2 * n + N_EXPERTS * tme
    xs = _dispatch(h2.reshape(n * ROW_TILE, LANES), pos, zoff, ntiles, nslot, td=512, tme=tme)
    yslots = _experts(xs, tile_expert, tile_block, ntiles, w_e_gate[0], w_e_up[0], w_e_down[0], tme=tme)
    out = _combine(yslots, pos, x1.reshape(n, dm), route, g_final, tc=256)
    return out.reshape(b, t, dm)
```

```python
import functools

import numpy as np
import jax
import jax.numpy as jnp
from jax import lax
from jax.experimental import pallas as pl
from jax.experimental.pallas import tpu as pltpu

F32 = jnp.float32
BF16 = jnp.bfloat16

HEAD_DIM = 64
BAND = 128
DIL_PATTERNS = ((128, 1), (512, 4), (2048, 16))
DIL_HEADS_PER_GROUP = 8
DIL_HEADS = 24
GROUP_W = DIL_HEADS_PER_GROUP * HEAD_DIM
QKV_W = 3 * GROUP_W
SWA_WINDOW = 128
SWA_Q_HEADS = 16
SWA_KV_HEADS = 2
SWA_Q_W = SWA_Q_HEADS * HEAD_DIM
SWA_W = SWA_Q_W + 4 * 128
MOE_GROUPS = 4
EXPERTS_PER_GROUP = 8
N_EXPERTS = 32
D_EXPERT = 512
RMS_EPS = 1e-6
LANES = 128
NEG_INF = float("-inf")
LOG2E = 1.4426950408889634
LN2 = 0.6931471805599453
ROW_TILE = 8

VMEM_LIMIT = 56 * 1024 * 1024


def _alibi_slopes(n):
    return (2.0 ** (-8.0 * np.arange(1, n + 1) / n)).astype(np.float32)


def _band_bias(slopes, max_back, unit):
    a = np.arange(BAND)[:, None]
    c = np.arange(2 * BAND)[None, :]
    delta = a + BAND - c
    band = (delta >= 0) & (delta <= max_back)
    pen = (-slopes[:, None, None] * (delta * unit)[None] * LOG2E).astype(np.float32)
    full = np.where(band[None], pen, -np.inf).astype(np.float32)
    first = np.where((band & (c >= BAND))[None], pen, -np.inf).astype(np.float32)
    return np.stack([full, first], axis=0)


def _inproj_kernel(x_ref, g_ref, w_ref, qkv0_ref, swa_ref, gates_ref, qkv4_ref, qkv16_ref, hs_ref, *, tm):
    x = x_ref[...]
    ms = jnp.mean(x * x, axis=-1, keepdims=True)
    h = x * lax.rsqrt(ms + RMS_EPS) * g_ref[...]
    nchunk = h.shape[1] // LANES
    for c in range(nchunk):
        hs_ref[c] = h[:, c * LANES:(c + 1) * LANES]
    hb = h.astype(BF16)

    def proj(lhs, c0, width):
        return jnp.dot(lhs, w_ref[:, c0:c0 + width], preferred_element_type=F32)

    for j in range(3):
        qkv0_ref[:, j * GROUP_W:(j + 1) * GROUP_W] = proj(hb, COL_QKV_D[j], GROUP_W).astype(BF16)
    for c0 in range(0, SWA_Q_W, 512):
        swa_ref[:, c0:c0 + 512] = proj(hb, COL_Q_S + c0, 512).astype(BF16)
    kv = proj(hb, COL_KV_S, 4 * HEAD_DIM)
    part = [kv[:, j * HEAD_DIM:(j + 1) * HEAD_DIM] for j in range(4)]
    swa_ref[:, SWA_Q_W:SWA_W] = jnp.concatenate(
        [part[0], part[0], part[1], part[1], part[2], part[2], part[3], part[3]], axis=1).astype(BF16)
    for c0 in range(0, 2048, 512):
        gates_ref[:, c0:c0 + 512] = proj(hb, COL_GATES + c0, 512).astype(BF16)
    for gi, out_ref, d in ((1, qkv4_ref, 4), (2, qkv16_ref, 16)):
        rows = tm // d
        hp = jnp.concatenate(
            [jnp.concatenate([hs_ref[c, pl.ds(r, rows, stride=d), :] for c in range(nchunk)], axis=1)
             for r in range(d)], axis=0).astype(BF16)
        for j in range(3):
            res = proj(hp, COL_QKV_D[j] + gi * GROUP_W, GROUP_W).astype(BF16)
            for r in range(d):
                out_ref[r, :, j * GROUP_W:(j + 1) * GROUP_W] = res[r * rows:(r + 1) * rows]


COL_QKV_D = (0, 1536, 3072)
COL_Q_S = 4608
COL_KV_S = 5632
COL_GATES = 5888
IN_WIDTH = 7936
Q_SCALE = HEAD_DIM ** -0.5 * LOG2E


def _prep_w_in(w_in):
    col = np.arange(IN_WIDTH)
    is_q = (col < COL_QKV_D[1]) | ((col >= COL_Q_S) & (col < COL_KV_S))
    colscale = np.where(is_q, Q_SCALE, 1.0).astype(np.float32)
    return (w_in * colscale[None, :]).astype(BF16)


def _inproj(x, g_mix, w_in, *, tm):
    b, t, dm = x.shape
    wp = _prep_w_in(w_in)
    nw = wp.shape[1]
    nt = t // tm
    out_shape = (
        jax.ShapeDtypeStruct((b, t, QKV_W), BF16),
        jax.ShapeDtypeStruct((b, t, SWA_W), BF16),
        jax.ShapeDtypeStruct((b, t, 2048), BF16),
        jax.ShapeDtypeStruct((b, 4, t // 4, QKV_W), BF16),
        jax.ShapeDtypeStruct((b, 16, t // 16, QKV_W), BF16),
    )
    return pl.pallas_call(
        functools.partial(_inproj_kernel, tm=tm),
        out_shape=out_shape,
        grid=(b, nt),
        in_specs=[
            pl.BlockSpec((None, tm, dm), lambda bi, i: (bi, i, 0)),
            pl.BlockSpec((1, dm), lambda bi, i: (0, 0)),
            pl.BlockSpec((dm, nw), lambda bi, i: (0, 0), pipeline_mode=pl.Buffered(1)),
        ],
        out_specs=(
            pl.BlockSpec((None, tm, QKV_W), lambda bi, i: (bi, i, 0)),
            pl.BlockSpec((None, tm, SWA_W), lambda bi, i: (bi, i, 0)),
            pl.BlockSpec((None, tm, 2048), lambda bi, i: (bi, i, 0)),
            pl.BlockSpec((None, 4, tm // 4, QKV_W), lambda bi, i: (bi, 0, i, 0)),
            pl.BlockSpec((None, 16, tm // 16, QKV_W), lambda bi, i: (bi, 0, i, 0)),
        ),
        scratch_shapes=[pltpu.VMEM((dm // LANES, tm, LANES), F32)],
        compiler_params=pltpu.CompilerParams(
            dimension_semantics=("arbitrary", "arbitrary"), vmem_limit_bytes=VMEM_LIMIT),
        name="inproj",
    )(x, g_mix.reshape(1, dm), wp)


def _softmax_pv(s, vv, sink=None):
    m = jnp.max(s, axis=-1, keepdims=True)
    if sink is not None:
        m = jnp.maximum(m, sink)
    p = jnp.exp2(s - m)
    l = jnp.sum(p, axis=-1, keepdims=True)
    if sink is not None:
        l = l + jnp.exp2(sink - m)
    o = jnp.dot(p.astype(BF16), vv, preferred_element_type=F32)
    return o * (1.0 / l), (m + jnp.log2(l)) * LN2


def _dil_attn_kernel(q_ref, kc_ref, vc_ref, kp_ref, vp_ref, bias_ref, o_ref, lse_ref, kbuf, vbuf, *, tq):
    i = pl.program_id(2)
    kbuf[0:BAND] = kp_ref[...]
    kbuf[BAND:BAND + tq] = kc_ref[...]
    vbuf[0:BAND] = vp_ref[...]
    vbuf[BAND:BAND + tq] = vc_ref[...]
    lane = lax.broadcasted_iota(jnp.int32, (BAND, LANES), 1)
    lo = lane < HEAD_DIM
    zero = jnp.zeros((BAND, LANES), BF16)
    for qb in range(tq // BAND):
        sel = jnp.where(i == 0, 1, 0) if qb == 0 else 0
        rq = slice(qb * BAND, (qb + 1) * BAND)
        rk = slice(qb * BAND, qb * BAND + 2 * BAND)
        lse_parts = []
        for pr in range(GROUP_W // LANES):
            cl = slice(pr * LANES, (pr + 1) * LANES)
            q2 = q_ref[rq, cl]
            kk = kbuf[rk, cl]
            vv = vbuf[rk, cl]
            res = []
            for hh, qm in enumerate((jnp.where(lo, q2, zero), jnp.where(lo, zero, q2))):
                s = lax.dot_general(qm, kk, (((1,), (1,)), ((), ())), preferred_element_type=F32)
                res.append(_softmax_pv(s + bias_ref[sel, 2 * pr + hh], vv))
            o_ref[rq, cl] = jnp.where(lo, res[0][0], res[1][0]).astype(o_ref.dtype)
            lse_parts.append(jnp.where(lane == 2 * pr, res[0][1], jnp.where(lane == 2 * pr + 1, res[1][1], 0.0)))
        lse_ref[rq, :] = (lse_parts[0] + lse_parts[1]) + (lse_parts[2] + lse_parts[3])


def _dil_attention(qkv, gi, *, tq):
    b, d, l, _ = qkv.shape
    window, dil = DIL_PATTERNS[gi]
    assert dil == d
    tq = min(tq, l)
    nq = tq // BAND
    slopes = _alibi_slopes(DIL_HEADS)[gi * 8:(gi + 1) * 8]
    bias = jnp.asarray(_band_bias(slopes, window // dil, dil))

    def cur(c):
        return pl.BlockSpec((None, None, tq, GROUP_W), lambda bi, r, i: (bi, r, i, c))

    def prev(c):
        return pl.BlockSpec((None, None, BAND, GROUP_W), lambda bi, r, i: (bi, r, jnp.maximum(i * nq - 1, 0), c))

    return pl.pallas_call(
        functools.partial(_dil_attn_kernel, tq=tq),
        out_shape=(jax.ShapeDtypeStruct((b, d, l, GROUP_W), BF16),
                   jax.ShapeDtypeStruct((b, d, l, LANES), F32)),
        grid=(b, d, l // tq),
        in_specs=[cur(0), cur(1), cur(2), prev(1), prev(2),
                  pl.BlockSpec((2, 8, BAND, 2 * BAND), lambda bi, r, i: (0, 0, 0, 0))],
        out_specs=(cur(0), pl.BlockSpec((None, None, tq, LANES), lambda bi, r, i: (bi, r, i, 0))),
        scratch_shapes=[pltpu.VMEM((BAND + tq, GROUP_W), BF16), pltpu.VMEM((BAND + tq, GROUP_W), BF16)],
        compiler_params=pltpu.CompilerParams(
            dimension_semantics=("arbitrary", "arbitrary", "arbitrary"), vmem_limit_bytes=VMEM_LIMIT),
        name=f"dil_attn_{gi}",
    )(qkv, qkv, qkv, qkv, qkv, bias)


def _swa_attn_kernel(sink_ref, qlo_ref, qhi_ref, kvc_ref, kvp_ref, bias_ref, o_ref, kvbuf, *, tq):
    i = pl.program_id(1)
    kvbuf[0:BAND] = kvp_ref[...]
    kvbuf[BAND:BAND + tq] = kvc_ref[...]
    lo = lax.broadcasted_iota(jnp.int32, (BAND, LANES), 1) < HEAD_DIM
    zero = jnp.zeros((BAND, LANES), BF16)
    for qb in range(tq // BAND):
        sel = jnp.where(i == 0, 1, 0) if qb == 0 else 0
        rq = slice(qb * BAND, (qb + 1) * BAND)
        rk = slice(qb * BAND, qb * BAND + 2 * BAND)
        for pr in range(SWA_Q_HEADS // 2):
            g = pr // 4
            q_ref = qlo_ref if pr < 4 else qhi_ref
            q2 = q_ref[rq, (pr % 4) * LANES:(pr % 4 + 1) * LANES]
            kk = kvbuf[rk, g * LANES:(g + 1) * LANES]
            vv = kvbuf[rk, (2 + g) * LANES:(3 + g) * LANES]
            res = []
            for hh, qm in enumerate((jnp.where(lo, q2, zero), jnp.where(lo, zero, q2))):
                h = 2 * pr + hh
                s = lax.dot_general(qm, kk, (((1,), (1,)), ((), ())), preferred_element_type=F32)
                res.append(_softmax_pv(s + bias_ref[sel, h], vv, sink=sink_ref[h] * LOG2E)[0])
            o_ref[rq, pr * LANES:(pr + 1) * LANES] = jnp.where(lo, res[0], res[1]).astype(o_ref.dtype)


def _swa_attention(swa, sinks, *, tq):
    b, t, _ = swa.shape
    tq = min(tq, t)
    nq = tq // BAND
    bias = jnp.asarray(_band_bias(_alibi_slopes(SWA_Q_HEADS), SWA_WINDOW - 1, 1))
    grid_spec = pltpu.PrefetchScalarGridSpec(
        num_scalar_prefetch=1,
        grid=(b, t // tq),
        in_specs=[
            pl.BlockSpec((None, tq, 512), lambda bi, i, s: (bi, i, 0)),
            pl.BlockSpec((None, tq, 512), lambda bi, i, s: (bi, i, 1)),
            pl.BlockSpec((None, tq, 512), lambda bi, i, s: (bi, i, 2)),
            pl.BlockSpec((None, BAND, 512), lambda bi, i, s: (bi, jnp.maximum(i * nq - 1, 0), 2)),
            pl.BlockSpec((2, SWA_Q_HEADS, BAND, 2 * BAND), lambda bi, i, s: (0, 0, 0, 0)),
        ],
        out_specs=pl.BlockSpec((None, tq, SWA_Q_W), lambda bi, i, s: (bi, i, 0)),
        scratch_shapes=[pltpu.VMEM((BAND + tq, 512), BF16)],
    )
    return pl.pallas_call(
        functools.partial(_swa_attn_kernel, tq=tq),
        out_shape=jax.ShapeDtypeStruct((b, t, SWA_Q_W), BF16),
        grid_spec=grid_spec,
        compiler_params=pltpu.CompilerParams(
            dimension_semantics=("arbitrary", "arbitrary"), vmem_limit_bytes=VMEM_LIMIT),
        name="swa_attn",
    )(sinks.astype(F32), swa, swa, swa, swa, bias)


ROUTE_E1, ROUTE_E2, ROUTE_W1, ROUTE_W2, ROUTE_R1, ROUTE_R2 = range(6)
ROUTER_LANE0 = MOE_GROUPS


def _route(logits, carry_ref, tm):
    lane = lax.broadcasted_iota(jnp.int32, (tm, LANES), 1)
    lanef = lane.astype(F32)

    def first_argmax(v):
        m = jnp.max(v, axis=-1, keepdims=True)
        return m, jnp.min(jnp.where(v == m, lanef, float(LANES)), axis=-1, keepdims=True)

    gl = jnp.where(lane < MOE_GROUPS, logits, NEG_INF)
    gmax, gidx = first_argmax(gl)
    g_w = 1.0 / jnp.sum(jnp.exp(gl - gmax), axis=-1, keepdims=True)
    e_lane = lane - ROUTER_LANE0
    lane_group = (e_lane >> 3).astype(F32)
    in_group = (e_lane >= 0) & (e_lane < N_EXPERTS) & (lane_group == gidx)
    el = jnp.where(in_group, logits, NEG_INF)
    m1, i1 = first_argmax(el)
    m2, i2 = first_argmax(jnp.where(lanef == i1, NEG_INF, el))
    tt = jnp.exp(m2 - m1)
    w1 = g_w / (1.0 + tt)
    w2 = g_w * tt / (1.0 + tt)
    e1 = i1 - float(ROUTER_LANE0)
    e2 = i2 - float(ROUTER_LANE0)
    oh1 = jnp.where(lanef == e1, 1.0, 0.0)
    oh2 = jnp.where(lanef == e2, 1.0, 0.0)
    oh = oh1 + oh2
    row = lax.broadcasted_iota(jnp.int32, (tm, tm), 0)
    colr = lax.broadcasted_iota(jnp.int32, (tm, tm), 1)
    ltri = jnp.where(colr < row, 1.0, 0.0).astype(BF16)
    before = jnp.dot(ltri, oh.astype(BF16), preferred_element_type=F32) + carry_ref[0:1, :]
    r1 = jnp.sum(oh1 * before, axis=-1, keepdims=True)
    r2 = jnp.sum(oh2 * before, axis=-1, keepdims=True)
    carry_ref[...] = carry_ref[...] + jnp.sum(oh, axis=0, keepdims=True)
    rec = jnp.zeros((tm, LANES), F32)
    for ln, val in ((ROUTE_E1, e1), (ROUTE_E2, e2), (ROUTE_W1, w1), (ROUTE_W2, w2),
                    (ROUTE_R1, r1), (ROUTE_R2, r2)):
        rec = jnp.where(lane == ln, val, rec)
    return rec


def _merge_kernel(x_ref, o0_ref, l0_ref, o1_ref, l1_ref, o2_ref, l2_ref, ys_ref, gates_ref,
                  wbd_ref, wbs_ref, wo_ref, gffn_ref, wrt_ref, brt_ref, exp_ref,
                  x1_ref, h2_ref, route_ref, cnt_ref,
                  uo1, ul1, uo2, ul2, carry_ref, *, tm):
    first = (pl.program_id(0) == 0) & (pl.program_id(1) == 0)

    @pl.when(first)
    def _():
        carry_ref[...] = jnp.zeros_like(carry_ref)

    nch = GROUP_W // LANES
    for src_o, src_l, dst_o, dst_l, d in ((o1_ref, l1_ref, uo1, ul1, 4), (o2_ref, l2_ref, uo2, ul2, 16)):
        rows = tm // d
        for r in range(d):
            dst_l[pl.ds(r, rows, stride=d), :] = src_l[r]
            for c in range(nch):
                dst_o[c, pl.ds(r, rows, stride=d), :] = src_o[r, :, c * LANES:(c + 1) * LANES].astype(F32)
    la, lb, lc = l0_ref[...], ul1[...], ul2[...]
    mx = jnp.maximum(jnp.maximum(la, lb), lc)
    ea, eb, ec = jnp.exp(la - mx), jnp.exp(lb - mx), jnp.exp(lc - mx)
    inv = 1.0 / (ea + eb + ec)

    def per_head_to_lanes(w):
        hi = w.astype(BF16)
        lo = (w - hi.astype(F32)).astype(BF16)
        return jnp.dot(jnp.concatenate([hi, lo], axis=1), exp_ref[...], preferred_element_type=F32)

    wa, wb, wc = per_head_to_lanes(ea * inv), per_head_to_lanes(eb * inv), per_head_to_lanes(ec * inv)
    ych = []
    for c in range(nch):
        cl = slice(c * LANES, (c + 1) * LANES)
        ych.append((wa[:, cl] * o0_ref[:, cl].astype(F32) + wb[:, cl] * uo1[c] + wc[:, cl] * uo2[c]).astype(BF16))
    y = jnp.concatenate(ych, axis=1)
    a = jnp.dot(y, wbd_ref[...], preferred_element_type=F32)
    bsw = jnp.dot(ys_ref[...], wbs_ref[...], preferred_element_type=F32)
    def sig(v):
        return 0.5 * jnp.tanh(0.5 * v) + 0.5

    mixed = sig(gates_ref[:, 0:1024]) * a.astype(BF16) + sig(gates_ref[:, 1024:2048]) * bsw.astype(BF16)
    x1 = x_ref[...] + jnp.dot(mixed, wo_ref[...], preferred_element_type=F32)
    x1_ref[...] = x1
    ms = jnp.mean(x1 * x1, axis=-1, keepdims=True)
    h2 = x1 * lax.rsqrt(ms + RMS_EPS) * gffn_ref[...]
    for c in range(h2.shape[1] // LANES):
        h2_ref[pl.ds(c, tm, stride=ROW_TILE), :] = h2[:, c * LANES:(c + 1) * LANES]
    h_hi = h2.astype(BF16)
    h_lo = (h2 - h_hi.astype(F32)).astype(BF16)
    hw = jnp.dot(h_hi, wrt_ref[...], preferred_element_type=F32)
    logits = (hw[:, :LANES] + hw[:, LANES:]
              + jnp.dot(h_lo, wrt_ref[:, :LANES], preferred_element_type=F32) + brt_ref[...])
    route_ref[...] = _route(logits, carry_ref, tm)
    cnt_ref[...] = carry_ref[...]


def _merge_route(x, o0, l0, o1, l1, o2, l2, ys, gates, w_br_dil, w_br_swa, w_out, g_ffn,
                 w_group, b_group, w_router, b_router, *, tm):
    b, t, dm = x.shape
    nt = t // tm
    wrt = jnp.zeros((dm, LANES), F32).at[:, :MOE_GROUPS].set(w_group).at[
        :, ROUTER_LANE0:ROUTER_LANE0 + N_EXPERTS].set(w_router)
    brt = jnp.zeros((1, LANES), F32).at[0, :MOE_GROUPS].set(b_group).at[
        0, ROUTER_LANE0:ROUTER_LANE0 + N_EXPERTS].set(b_router)
    wrt_hi = wrt.astype(BF16)
    wrt_lo = (wrt - wrt_hi.astype(F32)).astype(BF16)
    wrt = jnp.concatenate([wrt_hi, wrt_lo], axis=1)

    def tok(width):
        return pl.BlockSpec((None, tm, width), lambda bi, i: (bi, i, 0))

    def stream(d, width):
        return pl.BlockSpec((None, d, tm // d, width), lambda bi, i: (bi, 0, i, 0))

    def const(shape):
        return pl.BlockSpec(shape, lambda bi, i: (0,) * len(shape), pipeline_mode=pl.Buffered(1))

    head_of_lane = np.arange(GROUP_W) // HEAD_DIM
    spread = (np.arange(LANES)[:, None] == head_of_lane[None, :]).astype(np.float32)
    expand = jnp.asarray(np.concatenate([spread, spread], axis=0), dtype=BF16)

    nch = GROUP_W // LANES
    return pl.pallas_call(
        functools.partial(_merge_kernel, tm=tm),
        out_shape=(jax.ShapeDtypeStruct((b, t, dm), F32), jax.ShapeDtypeStruct((b, t * ROW_TILE, LANES), F32),
                   jax.ShapeDtypeStruct((b, t, LANES), F32), jax.ShapeDtypeStruct((8, LANES), F32)),
        grid=(b, nt),
        in_specs=[tok(dm), tok(GROUP_W), tok(LANES), stream(4, GROUP_W), stream(4, LANES),
                  stream(16, GROUP_W), stream(16, LANES), tok(SWA_Q_W), tok(2048),
                  const((GROUP_W, dm)), const((SWA_Q_W, dm)), const((dm, dm)), const((1, dm)),
                  const((dm, 2 * LANES)), const((1, LANES)), const((2 * LANES, GROUP_W))],
        out_specs=(tok(dm), pl.BlockSpec((None, tm * ROW_TILE, LANES), lambda bi, i: (bi, i, 0)), tok(LANES),
                   pl.BlockSpec((8, LANES), lambda bi, i: (0, 0))),
        scratch_shapes=[pltpu.VMEM((nch, tm, LANES), F32), pltpu.VMEM((tm, LANES), F32),
                        pltpu.VMEM((nch, tm, LANES), F32), pltpu.VMEM((tm, LANES), F32),
                        pltpu.VMEM((8, LANES), F32)],
        compiler_params=pltpu.CompilerParams(
            dimension_semantics=("arbitrary", "arbitrary"), vmem_limit_bytes=VMEM_LIMIT),
        name="merge_route",
    )(x, o0, l0, o1, l1, o2, l2, ys, gates, w_br_dil.astype(BF16), w_br_swa.astype(BF16), w_out.astype(BF16),
      g_ffn.reshape(1, dm), wrt, brt, expand)


def _dispatch_kernel(pos_ref, zoff_ref, nt_ref, h_ref, xs_ref, zbuf, sem, zsem, *, td, tme, nt_max):
    step = pl.program_id(0)

    def _zero_copy(e):
        off = pl.multiple_of(zoff_ref[e] * ROW_TILE, tme * ROW_TILE)
        return pltpu.make_async_copy(zbuf, xs_ref.at[pl.ds(off, tme * ROW_TILE)], zsem)

    def _row_copy(src_row, dst_row):
        return pltpu.make_async_copy(h_ref.at[pl.ds(pl.multiple_of(src_row * ROW_TILE, ROW_TILE), ROW_TILE)],
                                     xs_ref.at[pl.ds(pl.multiple_of(dst_row * ROW_TILE, ROW_TILE), ROW_TILE)], sem)

    def _wait_rows():
        pltpu.make_async_copy(h_ref, xs_ref.at[pl.ds(0, td * ROW_TILE)], sem).wait()

    def _tail_copy(tile):
        off = pl.multiple_of(tile * (tme * ROW_TILE), tme * ROW_TILE)
        return pltpu.make_async_copy(zbuf, xs_ref.at[pl.ds(off, tme * ROW_TILE)], zsem)

    @pl.when(step == 0)
    def _():
        zbuf[...] = jnp.zeros_like(zbuf)
        for e in range(N_EXPERTS):
            _zero_copy(e).start()
        for e in range(N_EXPERTS):
            _zero_copy(e).wait()

        def tail(tile, carry):
            _tail_copy(tile).start()
            _tail_copy(tile).wait()
            return carry

        lax.fori_loop(nt_ref[0], nt_max, tail, 0)

    def issue(j, carry):
        for k in range(2):
            _row_copy(j, pos_ref[0, 0, 2 * j + k]).start()
        return carry

    lax.fori_loop(0, td, issue, 0, unroll=8)
    _wait_rows()
    _wait_rows()


def _dispatch(h2, pos, zoff, ntiles, nslot, *, td, tme):
    n = h2.shape[0] // ROW_TILE
    nb = n // td
    grid_spec = pltpu.PrefetchScalarGridSpec(
        num_scalar_prefetch=0,
        grid=(nb,),
        in_specs=[
            pl.BlockSpec((1, 1, 2 * td), lambda i: (i, 0, 0), memory_space=pltpu.SMEM),
            pl.BlockSpec(memory_space=pltpu.SMEM),
            pl.BlockSpec(memory_space=pltpu.SMEM),
            pl.BlockSpec((td * ROW_TILE, LANES), lambda i: (i, 0)),
        ],
        out_specs=pl.BlockSpec(memory_space=pl.ANY),
        scratch_shapes=[pltpu.VMEM((tme * ROW_TILE, LANES), F32), pltpu.SemaphoreType.DMA(()),
                        pltpu.SemaphoreType.DMA(())],
    )
    return pl.pallas_call(
        functools.partial(_dispatch_kernel, td=td, tme=tme, nt_max=nslot // tme),
        out_shape=jax.ShapeDtypeStruct((nslot * ROW_TILE, LANES), F32),
        grid_spec=grid_spec,
        compiler_params=pltpu.CompilerParams(dimension_semantics=("arbitrary",), has_side_effects=True),
        name="dispatch",
    )(pos.reshape(nb, 1, 2 * td), zoff, ntiles, h2)


def _expert_kernel(te_ref, tb_ref, nt_ref, x_ref, wg_ref, wu_ref, wd_ref, y_ref, wgb, wub, wdb, *, tme):
    i = pl.program_id(0)

    @pl.when((i == 0) | (te_ref[i] != te_ref[jnp.maximum(i - 1, 0)]))
    def _():
        wgb[...] = wg_ref[...].astype(BF16)
        wub[...] = wu_ref[...].astype(BF16)
        wdb[...] = wd_ref[...].astype(BF16)

    @pl.when(i < nt_ref[0])
    def _():
        xb = jnp.concatenate([x_ref[pl.ds(c, tme, stride=ROW_TILE), :] for c in range(ROW_TILE)],
                             axis=1).astype(BF16)
        g = jnp.dot(xb, wgb[...], preferred_element_type=F32)
        u = jnp.dot(xb, wub[...], preferred_element_type=F32)
        a = (g * jax.nn.sigmoid(g) * u).astype(BF16)
        y = jnp.dot(a, wdb[...], preferred_element_type=F32)
        for c in range(ROW_TILE):
            y_ref[pl.ds(c, tme, stride=ROW_TILE), :] = y[:, c * LANES:(c + 1) * LANES]

    @pl.when(pl.program_id(0) >= nt_ref[0])
    def _():
        y_ref[...] = jnp.zeros_like(y_ref)


def _experts(xs, tile_expert, tile_block, ntiles, w_e_gate, w_e_up, w_e_down, *, tme):
    nslot = xs.shape[0] // ROW_TILE
    dm = ROW_TILE * LANES
    nt = nslot // tme
    grid_spec = pltpu.PrefetchScalarGridSpec(
        num_scalar_prefetch=3,
        grid=(nt,),
        in_specs=[
            pl.BlockSpec((tme * ROW_TILE, LANES), lambda i, te, tb, n: (tb[i], 0)),
            pl.BlockSpec((None, dm, D_EXPERT), lambda i, te, tb, n: (te[i], 0, 0)),
            pl.BlockSpec((None, dm, D_EXPERT), lambda i, te, tb, n: (te[i], 0, 0)),
            pl.BlockSpec((None, D_EXPERT, dm), lambda i, te, tb, n: (te[i], 0, 0)),
        ],
        out_specs=pl.BlockSpec((tme * ROW_TILE, LANES), lambda i, te, tb, n: (i, 0)),
        scratch_shapes=[pltpu.VMEM((dm, D_EXPERT), BF16), pltpu.VMEM((dm, D_EXPERT), BF16),
                        pltpu.VMEM((D_EXPERT, dm), BF16)],
    )
    return pl.pallas_call(
        functools.partial(_expert_kernel, tme=tme),
        out_shape=jax.ShapeDtypeStruct((nslot * ROW_TILE, LANES), F32),
        grid_spec=grid_spec,
        compiler_params=pltpu.CompilerParams(dimension_semantics=("arbitrary",), vmem_limit_bytes=VMEM_LIMIT),
        name="experts",
    )(tile_expert, tile_block, ntiles, xs, w_e_gate, w_e_up, w_e_down)


def _combine_kernel(pos_ref, ys_ref, x1_ref, route_ref, gfin_ref, out_ref, ybuf, sem, *, tc):
    def _row_copy(src_row, k, dst_row):
        return pltpu.make_async_copy(ys_ref.at[pl.ds(pl.multiple_of(src_row * ROW_TILE, ROW_TILE), ROW_TILE)],
                                     ybuf.at[k, pl.ds(pl.multiple_of(dst_row * ROW_TILE, ROW_TILE), ROW_TILE)], sem)

    def issue(j, carry):
        for k in range(2):
            _row_copy(pos_ref[0, 0, 2 * j + k], k, j).start()
        return carry

    lax.fori_loop(0, tc, issue, 0, unroll=8)
    for k in range(2):
        pltpu.make_async_copy(ys_ref.at[pl.ds(0, tc * ROW_TILE)], ybuf.at[k], sem).wait()
    rec = route_ref[...]
    w1 = rec[:, ROUTE_W1:ROUTE_W1 + 1]
    w2 = rec[:, ROUTE_W2:ROUTE_W2 + 1]
    y1 = jnp.concatenate([ybuf[0, pl.ds(c, tc, stride=ROW_TILE), :] for c in range(ROW_TILE)], axis=1)
    y2 = jnp.concatenate([ybuf[1, pl.ds(c, tc, stride=ROW_TILE), :] for c in range(ROW_TILE)], axis=1)
    z = x1_ref[...] + w1 * y1 + w2 * y2
    ms = jnp.mean(z * z, axis=-1, keepdims=True)
    out_ref[...] = z * lax.rsqrt(ms + RMS_EPS) * gfin_ref[...]


def _combine(ys, pos, x1, route, g_final, *, tc):
    n, dm = x1.shape
    nb = n // tc
    grid_spec = pltpu.PrefetchScalarGridSpec(
        num_scalar_prefetch=0,
        grid=(nb,),
        in_specs=[
            pl.BlockSpec((1, 1, 2 * tc), lambda i: (i, 0, 0), memory_space=pltpu.SMEM),
            pl.BlockSpec(memory_space=pl.ANY),
            pl.BlockSpec((tc, dm), lambda i: (i, 0)),
            pl.BlockSpec((tc, LANES), lambda i: (i, 0)),
            pl.BlockSpec((1, dm), lambda i: (0, 0)),
        ],
        out_specs=pl.BlockSpec((tc, dm), lambda i: (i, 0)),
        scratch_shapes=[pltpu.VMEM((2, tc * ROW_TILE, LANES), F32), pltpu.SemaphoreType.DMA(())],
    )
    return pl.pallas_call(
        functools.partial(_combine_kernel, tc=tc),
        out_shape=jax.ShapeDtypeStruct((n, dm), F32),
        grid_spec=grid_spec,
        compiler_params=pltpu.CompilerParams(dimension_semantics=("arbitrary",), vmem_limit_bytes=VMEM_LIMIT),
        name="combine",
    )(pos.reshape(nb, 1, 2 * tc), ys, x1, route, g_final.reshape(1, dm))


def _slot_layout(route, counts, n, tme):
    cnt = counts[0, :N_EXPERTS].astype(jnp.int32)
    tiles_per = (cnt + tme - 1) // tme
    tile_end = jnp.cumsum(tiles_per)
    seg_base = (tile_end - tiles_per) * tme
    ntiles = tile_end[-1]
    nt_max = (2 * n) // tme + N_EXPERTS
    tid = jnp.arange(nt_max, dtype=jnp.int32)
    live = jnp.minimum(tid, ntiles - 1)
    tile_expert = jnp.sum((tile_end[None, :] <= live[:, None]).astype(jnp.int32), axis=1)
    e12 = route[:, ROUTE_E1:ROUTE_E2 + 1].astype(jnp.int32)
    r12 = route[:, ROUTE_R1:ROUTE_R2 + 1].astype(jnp.int32)
    pos = jnp.take(seg_base, e12) + r12
    zoff = jnp.maximum(tile_end - 1, 0) * tme
    return pos, zoff.astype(jnp.int32), tile_expert, live.astype(jnp.int32), ntiles.reshape(1).astype(jnp.int32)


def kernel(x, g_mix, w_in, sinks, w_br_dil, w_br_swa, w_out, g_ffn, w_group, b_group, w_router, b_router,
           w_e_gate, w_e_up, w_e_down, g_final):
    b, t, dm = x.shape
    n = b * t
    tme = 512
    qkv0, swa, gates, qkv4, qkv16 = _inproj(x, g_mix[0], w_in[0], tm=512)
    o0, l0 = _dil_attention(qkv0.reshape(b, 1, t, QKV_W), 0, tq=512)
    o1, l1 = _dil_attention(qkv4, 1, tq=512)
    o2, l2 = _dil_attention(qkv16, 2, tq=512)
    ys = _swa_attention(swa, sinks[0], tq=512)
    x1, h2, route, counts = _merge_route(
        x, o0.reshape(b, t, GROUP_W), l0.reshape(b, t, LANES), o1, l1, o2, l2, ys, gates,
        w_br_dil[0], w_br_swa[0], w_out[0], g_ffn[0], w_group[0], b_group[0], w_router[0], b_router[0], tm=512)
    route = route.reshape(n, LANES)
    pos, zoff, tile_expert, tile_block, ntiles = _slot_layout(route, counts, n, tme)
    nslot = 2 * n + N_EXPERTS * tme
    xs = _dispatch(h2.reshape(n * ROW_TILE, LANES), pos, zoff, ntiles, nslot, td=512, tme=tme)
    yslots = _experts(xs, tile_expert, tile_block, ntiles, w_e_gate[0], w_e_up[0], w_e_down[0], tme=tme)
    out = _combine(yslots, pos, x1.reshape(n, dm), route, g_final, tc=256)
    return out.reshape(b, t, dm)
```

```python
import functools

import numpy as np
import jax
import jax.numpy as jnp
from jax import lax
from jax.experimental import pallas as pl
from jax.experimental.pallas import tpu as pltpu

F32 = jnp.float32
BF16 = jnp.bfloat16

HEAD_DIM = 64
BAND = 128
DIL_PATTERNS = ((128, 1), (512, 4), (2048, 16))
DIL_HEADS_PER_GROUP = 8
DIL_HEADS = 24
GROUP_W = DIL_HEADS_PER_GROUP * HEAD_DIM
QKV_W = 3 * GROUP_W
SWA_WINDOW = 128
SWA_Q_HEADS = 16
SWA_KV_HEADS = 2
SWA_Q_W = SWA_Q_HEADS * HEAD_DIM
SWA_W = SWA_Q_W + 4 * 128
MOE_GROUPS = 4
EXPERTS_PER_GROUP = 8
N_EXPERTS = 32
D_EXPERT = 512
RMS_EPS = 1e-6
LANES = 128
NEG_INF = float("-inf")
LOG2E = 1.4426950408889634
LN2 = 0.6931471805599453
ROW_TILE = 8

VMEM_LIMIT = 56 * 1024 * 1024


def _alibi_slopes(n):
    return (2.0 ** (-8.0 * np.arange(1, n + 1) / n)).astype(np.float32)


def _band_bias(slopes, max_back, unit):
    a = np.arange(BAND)[:, None]
    c = np.arange(2 * BAND)[None, :]
    delta = a + BAND - c
    band = (delta >= 0) & (delta <= max_back)
    pen = (-slopes[:, None, None] * (delta * unit)[None] * LOG2E).astype(np.float32)
    full = np.where(band[None], pen, -np.inf).astype(np.float32)
    first = np.where((band & (c >= BAND))[None], pen, -np.inf).astype(np.float32)
    return np.stack([full, first], axis=0)


def _inproj_kernel(x_ref, g_ref, w_ref, qkv0_ref, swa_ref, gates_ref, qkv4_ref, qkv16_ref, hs_ref, *, tm):
    x = x_ref[...]
    ms = jnp.mean(x * x, axis=-1, keepdims=True)
    h = x * lax.rsqrt(ms + RMS_EPS) * g_ref[...]
    nchunk = h.shape[1] // LANES
    for c in range(nchunk):
        hs_ref[c] = h[:, c * LANES:(c + 1) * LANES]
    hb = h.astype(BF16)

    def proj(lhs, c0, width):
        return jnp.dot(lhs, w_ref[:, c0:c0 + width], preferred_element_type=F32)

    for j in range(3):
        qkv0_ref[:, j * GROUP_W:(j + 1) * GROUP_W] = proj(hb, COL_QKV_D[j], GROUP_W).astype(BF16)
    for c0 in range(0, SWA_Q_W, 512):
        swa_ref[:, c0:c0 + 512] = proj(hb, COL_Q_S + c0, 512).astype(BF16)
    kv = proj(hb, COL_KV_S, 4 * HEAD_DIM)
    part = [kv[:, j * HEAD_DIM:(j + 1) * HEAD_DIM] for j in range(4)]
    swa_ref[:, SWA_Q_W:SWA_W] = jnp.concatenate(
        [part[0], part[0], part[1], part[1], part[2], part[2], part[3], part[3]], axis=1).astype(BF16)
    for c0 in range(0, 2048, 512):
        gates_ref[:, c0:c0 + 512] = proj(hb, COL_GATES + c0, 512).astype(BF16)
    for gi, out_ref, d in ((1, qkv4_ref, 4), (2, qkv16_ref, 16)):
        rows = tm // d
        hp = jnp.concatenate(
            [jnp.concatenate([hs_ref[c, pl.ds(r, rows, stride=d), :] for c in range(nchunk)], axis=1)
             for r in range(d)], axis=0).astype(BF16)
        for j in range(3):
            res = proj(hp, COL_QKV_D[j] + gi * GROUP_W, GROUP_W).astype(BF16)
            for r in range(d):
                out_ref[r, :, j * GROUP_W:(j + 1) * GROUP_W] = res[r * rows:(r + 1) * rows]


COL_QKV_D = (0, 1536, 3072)
COL_Q_S = 4608
COL_KV_S = 5632
COL_GATES = 5888
IN_WIDTH = 7936
Q_SCALE = HEAD_DIM ** -0.5 * LOG2E


def _prep_w_in(w_in):
    col = np.arange(IN_WIDTH)
    is_q = (col < COL_QKV_D[1]) | ((col >= COL_Q_S) & (col < COL_KV_S))
    colscale = np.where(is_q, Q_SCALE, 1.0).astype(np.float32)
    return (w_in * colscale[None, :]).astype(BF16)


def _inproj(x, g_mix, w_in, *, tm):
    b, t, dm = x.shape
    wp = _prep_w_in(w_in)
    nw = wp.shape[1]
    nt = t // tm
    out_shape = (
        jax.ShapeDtypeStruct((b, t, QKV_W), BF16),
        jax.ShapeDtypeStruct((b, t, SWA_W), BF16),
        jax.ShapeDtypeStruct((b, t, 2048), BF16),
        jax.ShapeDtypeStruct((b, 4, t // 4, QKV_W), BF16),
        jax.ShapeDtypeStruct((b, 16, t // 16, QKV_W), BF16),
    )
    return pl.pallas_call(
        functools.partial(_inproj_kernel, tm=tm),
        out_shape=out_shape,
        grid=(b, nt),
        in_specs=[
            pl.BlockSpec((None, tm, dm), lambda bi, i: (bi, i, 0)),
            pl.BlockSpec((1, dm), lambda bi, i: (0, 0)),
            pl.BlockSpec((dm, nw), lambda bi, i: (0, 0), pipeline_mode=pl.Buffered(1)),
        ],
        out_specs=(
            pl.BlockSpec((None, tm, QKV_W), lambda bi, i: (bi, i, 0)),
            pl.BlockSpec((None, tm, SWA_W), lambda bi, i: (bi, i, 0)),
            pl.BlockSpec((None, tm, 2048), lambda bi, i: (bi, i, 0)),
            pl.BlockSpec((None, 4, tm // 4, QKV_W), lambda bi, i: (bi, 0, i, 0)),
            pl.BlockSpec((None, 16, tm // 16, QKV_W), lambda bi, i: (bi, 0, i, 0)),
        ),
        scratch_shapes=[pltpu.VMEM((dm // LANES, tm, LANES), F32)],
        compiler_params=pltpu.CompilerParams(
            dimension_semantics=("arbitrary", "arbitrary"), vmem_limit_bytes=VMEM_LIMIT),
        name="inproj",
    )(x, g_mix.reshape(1, dm), wp)


def _softmax_pv(s, vv, sink=None):
    m = jnp.max(s, axis=-1, keepdims=True)
    if sink is not None:
        m = jnp.maximum(m, sink)
    p = jnp.exp2(s - m)
    l = jnp.sum(p, axis=-1, keepdims=True)
    if sink is not None:
        l = l + jnp.exp2(sink - m)
    o = jnp.dot(p.astype(BF16), vv, preferred_element_type=F32)
    return o * (1.0 / l), (m + jnp.log2(l)) * LN2


def _dil_attn_kernel(q_ref, kc_ref, vc_ref, kp_ref, vp_ref, bias_ref, o_ref, lse_ref, kbuf, vbuf, *, tq):
    i = pl.program_id(2)
    kbuf[0:BAND] = kp_ref[...]
    kbuf[BAND:BAND + tq] = kc_ref[...]
    vbuf[0:BAND] = vp_ref[...]
    vbuf[BAND:BAND + tq] = vc_ref[...]
    lane = lax.broadcasted_iota(jnp.int32, (BAND, LANES), 1)
    lo = lane < HEAD_DIM
    zero = jnp.zeros((BAND, LANES), BF16)
    for qb in range(tq // BAND):
        sel = jnp.where(i == 0, 1, 0) if qb == 0 else 0
        rq = slice(qb * BAND, (qb + 1) * BAND)
        rk = slice(qb * BAND, qb * BAND + 2 * BAND)
        lse_parts = []
        for pr in range(GROUP_W // LANES):
            cl = slice(pr * LANES, (pr + 1) * LANES)
            q2 = q_ref[rq, cl]
            kk = kbuf[rk, cl]
            vv = vbuf[rk, cl]
            res = []
            for hh, qm in enumerate((jnp.where(lo, q2, zero), jnp.where(lo, zero, q2))):
                s = lax.dot_general(qm, kk, (((1,), (1,)), ((), ())), preferred_element_type=F32)
                res.append(_softmax_pv(s + bias_ref[sel, 2 * pr + hh], vv))
            o_ref[rq, cl] = jnp.where(lo, res[0][0], res[1][0]).astype(o_ref.dtype)
            lse_parts.append(jnp.where(lane == 2 * pr, res[0][1], jnp.where(lane == 2 * pr + 1, res[1][1], 0.0)))
        lse_ref[rq, :] = (lse_parts[0] + lse_parts[1]) + (lse_parts[2] + lse_parts[3])


def _dil_attention(qkv, gi, *, tq):
    b, d, l, _ = qkv.shape
    window, dil = DIL_PATTERNS[gi]
    assert dil == d
    tq = min(tq, l)
    nq = tq // BAND
    slopes = _alibi_slopes(DIL_HEADS)[gi * 8:(gi + 1) * 8]
    bias = jnp.asarray(_band_bias(slopes, window // dil, dil))

    def cur(c):
        return pl.BlockSpec((None, None, tq, GROUP_W), lambda bi, r, i: (bi, r, i, c))

    def prev(c):
        return pl.BlockSpec((None, None, BAND, GROUP_W), lambda bi, r, i: (bi, r, jnp.maximum(i * nq - 1, 0), c))

    return pl.pallas_call(
        functools.partial(_dil_attn_kernel, tq=tq),
        out_shape=(jax.ShapeDtypeStruct((b, d, l, GROUP_W), BF16),
                   jax.ShapeDtypeStruct((b, d, l, LANES), F32)),
        grid=(b, d, l // tq),
        in_specs=[cur(0), cur(1), cur(2), prev(1), prev(2),
                  pl.BlockSpec((2, 8, BAND, 2 * BAND), lambda bi, r, i: (0, 0, 0, 0))],
        out_specs=(cur(0), pl.BlockSpec((None, None, tq, LANES), lambda bi, r, i: (bi, r, i, 0))),
        scratch_shapes=[pltpu.VMEM((BAND + tq, GROUP_W), BF16), pltpu.VMEM((BAND + tq, GROUP_W), BF16)],
        compiler_params=pltpu.CompilerParams(
            dimension_semantics=("arbitrary", "arbitrary", "arbitrary"), vmem_limit_bytes=VMEM_LIMIT),
        name=f"dil_attn_{gi}",
    )(qkv, qkv, qkv, qkv, qkv, bias)


def _swa_attn_kernel(sink_ref, qlo_ref, qhi_ref, kvc_ref, kvp_ref, bias_ref, o_ref, kvbuf, *, tq):
    i = pl.program_id(1)
    kvbuf[0:BAND] = kvp_ref[...]
    kvbuf[BAND:BAND + tq] = kvc_ref[...]
    lo = lax.broadcasted_iota(jnp.int32, (BAND, LANES), 1) < HEAD_DIM
    zero = jnp.zeros((BAND, LANES), BF16)
    for qb in range(tq // BAND):
        sel = jnp.where(i == 0, 1, 0) if qb == 0 else 0
        rq = slice(qb * BAND, (qb + 1) * BAND)
        rk = slice(qb * BAND, qb * BAND + 2 * BAND)
        for pr in range(SWA_Q_HEADS // 2):
            g = pr // 4
            q_ref = qlo_ref if pr < 4 else qhi_ref
            q2 = q_ref[rq, (pr % 4) * LANES:(pr % 4 + 1) * LANES]
            kk = kvbuf[rk, g * LANES:(g + 1) * LANES]
            vv = kvbuf[rk, (2 + g) * LANES:(3 + g) * LANES]
            res = []
            for hh, qm in enumerate((jnp.where(lo, q2, zero), jnp.where(lo, zero, q2))):
                h = 2 * pr + hh
                s = lax.dot_general(qm, kk, (((1,), (1,)), ((), ())), preferred_element_type=F32)
                res.append(_softmax_pv(s + bias_ref[sel, h], vv, sink=sink_ref[h] * LOG2E)[0])
            o_ref[rq, pr * LANES:(pr + 1) * LANES] = jnp.where(lo, res[0], res[1]).astype(o_ref.dtype)


def _swa_attention(swa, sinks, *, tq):
    b, t, _ = swa.shape
    tq = min(tq, t)
    nq = tq // BAND
    bias = jnp.asarray(_band_bias(_alibi_slopes(SWA_Q_HEADS), SWA_WINDOW - 1, 1))
    grid_spec = pltpu.PrefetchScalarGridSpec(
        num_scalar_prefetch=1,
        grid=(b, t // tq),
        in_specs=[
            pl.BlockSpec((None, tq, 512), lambda bi, i, s: (bi, i, 0)),
            pl.BlockSpec((None, tq, 512), lambda bi, i, s: (bi, i, 1)),
            pl.BlockSpec((None, tq, 512), lambda bi, i, s: (bi, i, 2)),
            pl.BlockSpec((None, BAND, 512), lambda bi, i, s: (bi, jnp.maximum(i * nq - 1, 0), 2)),
            pl.BlockSpec((2, SWA_Q_HEADS, BAND, 2 * BAND), lambda bi, i, s: (0, 0, 0, 0)),
        ],
        out_specs=pl.BlockSpec((None, tq, SWA_Q_W), lambda bi, i, s: (bi, i, 0)),
        scratch_shapes=[pltpu.VMEM((BAND + tq, 512), BF16)],
    )
    return pl.pallas_call(
        functools.partial(_swa_attn_kernel, tq=tq),
        out_shape=jax.ShapeDtypeStruct((b, t, SWA_Q_W), BF16),
        grid_spec=grid_spec,
        compiler_params=pltpu.CompilerParams(
            dimension_semantics=("arbitrary", "arbitrary"), vmem_limit_bytes=VMEM_LIMIT),
        name="swa_attn",
    )(sinks.astype(F32), swa, swa, swa, swa, bias)


ROUTE_CLASS, ROUTE_RANK, ROUTE_WA, ROUTE_WB = range(4)
ROUTER_LANE0 = MOE_GROUPS
PAIRS_PER_GROUP = EXPERTS_PER_GROUP * (EXPERTS_PER_GROUP - 1) // 2
N_CLASSES = MOE_GROUPS * PAIRS_PER_GROUP
_PAIRS = [(a, b) for a in range(EXPERTS_PER_GROUP) for b in range(a + 1, EXPERTS_PER_GROUP)]
CLASS_EXPERT_A = np.array([g * EXPERTS_PER_GROUP + a for g in range(MOE_GROUPS) for a, _ in _PAIRS], np.int32)
CLASS_EXPERT_B = np.array([g * EXPERTS_PER_GROUP + b for g in range(MOE_GROUPS) for _, b in _PAIRS], np.int32)


def _route(logits, carry_ref, tm):
    lane = lax.broadcasted_iota(jnp.int32, (tm, LANES), 1)
    lanef = lane.astype(F32)

    def first_argmax(v):
        m = jnp.max(v, axis=-1, keepdims=True)
        return m, jnp.min(jnp.where(v == m, lanef, float(LANES)), axis=-1, keepdims=True)

    gl = jnp.where(lane < MOE_GROUPS, logits, NEG_INF)
    gmax, gidx = first_argmax(gl)
    g_w = 1.0 / jnp.sum(jnp.exp(gl - gmax), axis=-1, keepdims=True)
    e_lane = lane - ROUTER_LANE0
    lane_group = (e_lane >> 3).astype(F32)
    in_group = (e_lane >= 0) & (e_lane < N_EXPERTS) & (lane_group == gidx)
    el = jnp.where(in_group, logits, NEG_INF)
    m1, i1 = first_argmax(el)
    m2, i2 = first_argmax(jnp.where(lanef == i1, NEG_INF, el))
    tt = jnp.exp(m2 - m1)
    w1 = g_w / (1.0 + tt)
    w2 = g_w * tt / (1.0 + tt)
    first = float(ROUTER_LANE0) + float(EXPERTS_PER_GROUP) * gidx
    e1 = i1 - first
    e2 = i2 - first
    swap = e2 < e1
    ea = jnp.minimum(e1, e2)
    eb = jnp.maximum(e1, e2)
    wa = jnp.where(swap, w2, w1)
    wb = jnp.where(swap, w1, w2)
    pair = ea * (float(2 * EXPERTS_PER_GROUP - 1) - ea) * 0.5 + (eb - ea - 1.0)
    cls = float(PAIRS_PER_GROUP) * gidx + pair
    oh = jnp.where(lanef == cls, 1.0, 0.0)
    row = lax.broadcasted_iota(jnp.int32, (tm, tm), 0)
    colr = lax.broadcasted_iota(jnp.int32, (tm, tm), 1)
    ltri = jnp.where(colr < row, 1.0, 0.0).astype(BF16)
    before = jnp.dot(ltri, oh.astype(BF16), preferred_element_type=F32) + carry_ref[0:1, :]
    rank = jnp.sum(oh * before, axis=-1, keepdims=True)
    carry_ref[...] = carry_ref[...] + jnp.sum(oh, axis=0, keepdims=True)
    rec = jnp.zeros((tm, LANES), F32)
    for ln, val in ((ROUTE_CLASS, cls), (ROUTE_RANK, rank), (ROUTE_WA, wa), (ROUTE_WB, wb)):
        rec = jnp.where(lane == ln, val, rec)
    return rec


def _merge_kernel(x_ref, o0_ref, l0_ref, o1_ref, l1_ref, o2_ref, l2_ref, ys_ref, gates_ref,
                  wbd_ref, wbs_ref, wo_ref, gffn_ref, wrt_ref, brt_ref, exp_ref,
                  x1_ref, h2_ref, route_ref, cnt_ref,
                  uo1, ul1, uo2, ul2, carry_ref, *, tm):
    first = (pl.program_id(0) == 0) & (pl.program_id(1) == 0)

    @pl.when(first)
    def _():
        carry_ref[...] = jnp.zeros_like(carry_ref)

    nch = GROUP_W // LANES
    for src_o, src_l, dst_o, dst_l, d in ((o1_ref, l1_ref, uo1, ul1, 4), (o2_ref, l2_ref, uo2, ul2, 16)):
        rows = tm // d
        for r in range(d):
            dst_l[pl.ds(r, rows, stride=d), :] = src_l[r]
            for c in range(nch):
                dst_o[c, pl.ds(r, rows, stride=d), :] = src_o[r, :, c * LANES:(c + 1) * LANES].astype(F32)
    la, lb, lc = l0_ref[...], ul1[...], ul2[...]
    mx = jnp.maximum(jnp.maximum(la, lb), lc)
    ea, eb, ec = jnp.exp(la - mx), jnp.exp(lb - mx), jnp.exp(lc - mx)
    inv = 1.0 / (ea + eb + ec)

    def per_head_to_lanes(w):
        hi = w.astype(BF16)
        lo = (w - hi.astype(F32)).astype(BF16)
        return jnp.dot(jnp.concatenate([hi, lo], axis=1), exp_ref[...], preferred_element_type=F32)

    wa, wb, wc = per_head_to_lanes(ea * inv), per_head_to_lanes(eb * inv), per_head_to_lanes(ec * inv)
    ych = []
    for c in range(nch):
        cl = slice(c * LANES, (c + 1) * LANES)
        ych.append((wa[:, cl] * o0_ref[:, cl].astype(F32) + wb[:, cl] * uo1[c] + wc[:, cl] * uo2[c]).astype(BF16))
    y = jnp.concatenate(ych, axis=1)
    a = jnp.dot(y, wbd_ref[...], preferred_element_type=F32)
    bsw = jnp.dot(ys_ref[...], wbs_ref[...], preferred_element_type=F32)
    def sig(v):
        return 0.5 * jnp.tanh(0.5 * v) + 0.5

    mixed = sig(gates_ref[:, 0:1024]) * a.astype(BF16) + sig(gates_ref[:, 1024:2048]) * bsw.astype(BF16)
    x1 = x_ref[...] + jnp.dot(mixed, wo_ref[...], preferred_element_type=F32)
    x1_ref[...] = x1
    ms = jnp.mean(x1 * x1, axis=-1, keepdims=True)
    h2 = x1 * lax.rsqrt(ms + RMS_EPS) * gffn_ref[...]
    for c in range(h2.shape[1] // LANES):
        h2_ref[pl.ds(c, tm, stride=ROW_TILE), :] = h2[:, c * LANES:(c + 1) * LANES]
    h_hi = h2.astype(BF16)
    h_lo = (h2 - h_hi.astype(F32)).astype(BF16)
    hw = jnp.dot(h_hi, wrt_ref[...], preferred_element_type=F32)
    logits = (hw[:, :LANES] + hw[:, LANES:]
              + jnp.dot(h_lo, wrt_ref[:, :LANES], preferred_element_type=F32) + brt_ref[...])
    route_ref[...] = _route(logits, carry_ref, tm)
    cnt_ref[...] = carry_ref[...]


def _merge_route(x, o0, l0, o1, l1, o2, l2, ys, gates, w_br_dil, w_br_swa, w_out, g_ffn,
                 w_group, b_group, w_router, b_router, *, tm):
    b, t, dm = x.shape
    nt = t // tm
    wrt = jnp.zeros((dm, LANES), F32).at[:, :MOE_GROUPS].set(w_group).at[
        :, ROUTER_LANE0:ROUTER_LANE0 + N_EXPERTS].set(w_router)
    brt = jnp.zeros((1, LANES), F32).at[0, :MOE_GROUPS].set(b_group).at[
        0, ROUTER_LANE0:ROUTER_LANE0 + N_EXPERTS].set(b_router)
    wrt_hi = wrt.astype(BF16)
    wrt_lo = (wrt - wrt_hi.astype(F32)).astype(BF16)
    wrt = jnp.concatenate([wrt_hi, wrt_lo], axis=1)

    def tok(width):
        return pl.BlockSpec((None, tm, width), lambda bi, i: (bi, i, 0))

    def stream(d, width):
        return pl.BlockSpec((None, d, tm // d, width), lambda bi, i: (bi, 0, i, 0))

    def const(shape):
        return pl.BlockSpec(shape, lambda bi, i: (0,) * len(shape), pipeline_mode=pl.Buffered(1))

    head_of_lane = np.arange(GROUP_W) // HEAD_DIM
    spread = (np.arange(LANES)[:, None] == head_of_lane[None, :]).astype(np.float32)
    expand = jnp.asarray(np.concatenate([spread, spread], axis=0), dtype=BF16)

    nch = GROUP_W // LANES
    return pl.pallas_call(
        functools.partial(_merge_kernel, tm=tm),
        out_shape=(jax.ShapeDtypeStruct((b, t, dm), F32), jax.ShapeDtypeStruct((b, t * ROW_TILE, LANES), F32),
                   jax.ShapeDtypeStruct((b, t, LANES), F32), jax.ShapeDtypeStruct((8, LANES), F32)),
        grid=(b, nt),
        in_specs=[tok(dm), tok(GROUP_W), tok(LANES), stream(4, GROUP_W), stream(4, LANES),
                  stream(16, GROUP_W), stream(16, LANES), tok(SWA_Q_W), tok(2048),
                  const((GROUP_W, dm)), const((SWA_Q_W, dm)), const((dm, dm)), const((1, dm)),
                  const((dm, 2 * LANES)), const((1, LANES)), const((2 * LANES, GROUP_W))],
        out_specs=(tok(dm), pl.BlockSpec((None, tm * ROW_TILE, LANES), lambda bi, i: (bi, i, 0)), tok(LANES),
                   pl.BlockSpec((8, LANES), lambda bi, i: (0, 0))),
        scratch_shapes=[pltpu.VMEM((nch, tm, LANES), F32), pltpu.VMEM((tm, LANES), F32),
                        pltpu.VMEM((nch, tm, LANES), F32), pltpu.VMEM((tm, LANES), F32),
                        pltpu.VMEM((8, LANES), F32)],
        compiler_params=pltpu.CompilerParams(
            dimension_semantics=("arbitrary", "arbitrary"), vmem_limit_bytes=VMEM_LIMIT),
        name="merge_route",
    )(x, o0, l0, o1, l1, o2, l2, ys, gates, w_br_dil.astype(BF16), w_br_swa.astype(BF16), w_out.astype(BF16),
      g_ffn.reshape(1, dm), wrt, brt, expand)


def _dispatch_kernel(pos_ref, zoff_ref, nt_ref, h_ref, xs_ref, zbuf, sem, zsem, *, td, tme, nt_max):
    step = pl.program_id(0)

    def _zero_copy(e):
        off = pl.multiple_of(zoff_ref[e] * ROW_TILE, tme * ROW_TILE)
        return pltpu.make_async_copy(zbuf, xs_ref.at[pl.ds(off, tme * ROW_TILE)], zsem)

    def _row_copy(src_row, dst_row):
        return pltpu.make_async_copy(h_ref.at[pl.ds(pl.multiple_of(src_row * ROW_TILE, ROW_TILE), ROW_TILE)],
                                     xs_ref.at[pl.ds(pl.multiple_of(dst_row * ROW_TILE, ROW_TILE), ROW_TILE)], sem)

    def _wait_rows():
        pltpu.make_async_copy(h_ref, xs_ref.at[pl.ds(0, td * ROW_TILE)], sem).wait()

    def _tail_copy(tile):
        off = pl.multiple_of(tile * (tme * ROW_TILE), tme * ROW_TILE)
        return pltpu.make_async_copy(zbuf, xs_ref.at[pl.ds(off, tme * ROW_TILE)], zsem)

    @pl.when(step == 0)
    def _():
        zbuf[...] = jnp.zeros_like(zbuf)
        for e in range(N_CLASSES):
            _zero_copy(e).start()
        for e in range(N_CLASSES):
            _zero_copy(e).wait()

        def tail(tile, carry):
            _tail_copy(tile).start()
            _tail_copy(tile).wait()
            return carry

        lax.fori_loop(nt_ref[0], nt_max, tail, 0)

    def issue(j, carry):
        _row_copy(j, pos_ref[0, 0, j]).start()
        return carry

    lax.fori_loop(0, td, issue, 0, unroll=8)
    _wait_rows()


def _dispatch(h2, pos, zoff, ntiles, nslot, *, td, tme):
    n = h2.shape[0] // ROW_TILE
    nb = n // td
    grid_spec = pltpu.PrefetchScalarGridSpec(
        num_scalar_prefetch=0,
        grid=(nb,),
        in_specs=[
            pl.BlockSpec((1, 1, td), lambda i: (i, 0, 0), memory_space=pltpu.SMEM),
            pl.BlockSpec(memory_space=pltpu.SMEM),
            pl.BlockSpec(memory_space=pltpu.SMEM),
            pl.BlockSpec((td * ROW_TILE, LANES), lambda i: (i, 0)),
        ],
        out_specs=pl.BlockSpec(memory_space=pl.ANY),
        scratch_shapes=[pltpu.VMEM((tme * ROW_TILE, LANES), F32), pltpu.SemaphoreType.DMA(()),
                        pltpu.SemaphoreType.DMA(())],
    )
    return pl.pallas_call(
        functools.partial(_dispatch_kernel, td=td, tme=tme, nt_max=nslot // tme),
        out_shape=jax.ShapeDtypeStruct((nslot * ROW_TILE, LANES), F32),
        grid_spec=grid_spec,
        compiler_params=pltpu.CompilerParams(dimension_semantics=("arbitrary",), has_side_effects=True),
        name="dispatch",
    )(pos.reshape(nb, 1, td), zoff, ntiles, h2)


PAIR_ROWS = 2 * ROW_TILE


def _expert_kernel(ta_ref, tb_ref, tblk_ref, nt_ref, x_ref, wga_ref, wua_ref, wda_ref, wgb_ref, wub_ref, wdb_ref,
                   y_ref, *, tme):
    i = pl.program_id(0)

    @pl.when(i < nt_ref[0])
    def _():
        xb = jnp.concatenate([x_ref[pl.ds(c, tme, stride=ROW_TILE), :] for c in range(ROW_TILE)],
                             axis=1).astype(BF16)
        for half, (wg_ref, wu_ref, wd_ref) in enumerate(((wga_ref, wua_ref, wda_ref), (wgb_ref, wub_ref, wdb_ref))):
            g = jnp.dot(xb, wg_ref[...], preferred_element_type=F32)
            u = jnp.dot(xb, wu_ref[...], preferred_element_type=F32)
            a = (g * jax.nn.sigmoid(g) * u).astype(BF16)
            y = jnp.dot(a, wd_ref[...], preferred_element_type=F32)
            for c in range(ROW_TILE):
                y_ref[pl.ds(half * ROW_TILE + c, tme, stride=PAIR_ROWS), :] = y[:, c * LANES:(c + 1) * LANES]

    @pl.when(i >= nt_ref[0])
    def _():
        y_ref[...] = jnp.zeros_like(y_ref)


def _experts(xs, tile_a, tile_b, tile_block, ntiles, w_e_gate, w_e_up, w_e_down, *, tme):
    nslot = xs.shape[0] // ROW_TILE
    dm = ROW_TILE * LANES
    nt = nslot // tme
    wg, wu, wd = w_e_gate.astype(BF16), w_e_up.astype(BF16), w_e_down.astype(BF16)

    def w_in_spec(which):
        return pl.BlockSpec((None, dm, D_EXPERT), lambda i, ta, tb, tk, n: ((ta, tb)[which][i], 0, 0))

    def w_out_spec(which):
        return pl.BlockSpec((None, D_EXPERT, dm), lambda i, ta, tb, tk, n: ((ta, tb)[which][i], 0, 0))

    grid_spec = pltpu.PrefetchScalarGridSpec(
        num_scalar_prefetch=4,
        grid=(nt,),
        in_specs=[
            pl.BlockSpec((tme * ROW_TILE, LANES), lambda i, ta, tb, tk, n: (tk[i], 0)),
            w_in_spec(0), w_in_spec(0), w_out_spec(0), w_in_spec(1), w_in_spec(1), w_out_spec(1),
        ],
        out_specs=pl.BlockSpec((tme * PAIR_ROWS, LANES), lambda i, ta, tb, tk, n: (i, 0)),
    )
    return pl.pallas_call(
        functools.partial(_expert_kernel, tme=tme),
        out_shape=jax.ShapeDtypeStruct((nslot * PAIR_ROWS, LANES), F32),
        grid_spec=grid_spec,
        compiler_params=pltpu.CompilerParams(dimension_semantics=("arbitrary",), vmem_limit_bytes=VMEM_LIMIT),
        name="experts",
    )(tile_a, tile_b, tile_block, ntiles, xs, wg, wu, wd, wg, wu, wd)


def _combine_kernel(pos_ref, ys_ref, x1_ref, route_ref, gfin_ref, out_ref, ybuf, sem, *, tc):
    def _row_copy(src_row, dst_row):
        return pltpu.make_async_copy(ys_ref.at[pl.ds(pl.multiple_of(src_row * PAIR_ROWS, PAIR_ROWS), PAIR_ROWS)],
                                     ybuf.at[pl.ds(pl.multiple_of(dst_row * PAIR_ROWS, PAIR_ROWS), PAIR_ROWS)], sem)

    def issue(j, carry):
        _row_copy(pos_ref[0, 0, j], j).start()
        return carry

    lax.fori_loop(0, tc, issue, 0, unroll=8)
    pltpu.make_async_copy(ys_ref.at[pl.ds(0, tc * PAIR_ROWS)], ybuf, sem).wait()
    rec = route_ref[...]
    wa = rec[:, ROUTE_WA:ROUTE_WA + 1]
    wb = rec[:, ROUTE_WB:ROUTE_WB + 1]
    ya = jnp.concatenate([ybuf[pl.ds(c, tc, stride=PAIR_ROWS), :] for c in range(ROW_TILE)], axis=1)
    yb = jnp.concatenate([ybuf[pl.ds(ROW_TILE + c, tc, stride=PAIR_ROWS), :] for c in range(ROW_TILE)], axis=1)
    z = x1_ref[...] + wa * ya + wb * yb
    ms = jnp.mean(z * z, axis=-1, keepdims=True)
    out_ref[...] = z * lax.rsqrt(ms + RMS_EPS) * gfin_ref[...]


def _combine(ys, pos, x1, route, g_final, *, tc):
    n, dm = x1.shape
    nb = n // tc
    grid_spec = pltpu.PrefetchScalarGridSpec(
        num_scalar_prefetch=0,
        grid=(nb,),
        in_specs=[
            pl.BlockSpec((1, 1, tc), lambda i: (i, 0, 0), memory_space=pltpu.SMEM),
            pl.BlockSpec(memory_space=pl.ANY),
            pl.BlockSpec((tc, dm), lambda i: (i, 0)),
            pl.BlockSpec((tc, LANES), lambda i: (i, 0)),
            pl.BlockSpec((1, dm), lambda i: (0, 0)),
        ],
        out_specs=pl.BlockSpec((tc, dm), lambda i: (i, 0)),
        scratch_shapes=[pltpu.VMEM((tc * PAIR_ROWS, LANES), F32), pltpu.SemaphoreType.DMA(())],
    )
    return pl.pallas_call(
        functools.partial(_combine_kernel, tc=tc),
        out_shape=jax.ShapeDtypeStruct((n, dm), F32),
        grid_spec=grid_spec,
        compiler_params=pltpu.CompilerParams(dimension_semantics=("arbitrary",), vmem_limit_bytes=VMEM_LIMIT),
        name="combine",
    )(pos.reshape(nb, 1, tc), ys, x1, route, g_final.reshape(1, dm))


def _slot_layout(route, counts, n, tme):
    cnt = counts[0, :N_CLASSES].astype(jnp.int32)
    tiles_per = (cnt + tme - 1) // tme
    tile_end = jnp.cumsum(tiles_per)
    seg_base = (tile_end - tiles_per) * tme
    ntiles = tile_end[-1]
    nt_max = n // tme + N_CLASSES
    tid = jnp.arange(nt_max, dtype=jnp.int32)
    live = jnp.minimum(tid, ntiles - 1)
    tile_class = jnp.sum((tile_end[None, :] <= live[:, None]).astype(jnp.int32), axis=1)
    tile_a = jnp.take(jnp.asarray(CLASS_EXPERT_A), tile_class)
    tile_b = jnp.take(jnp.asarray(CLASS_EXPERT_B), tile_class)
    cls = route[:, ROUTE_CLASS].astype(jnp.int32)
    rank = route[:, ROUTE_RANK].astype(jnp.int32)
    base = jnp.sum(jnp.where(cls[:, None] == jnp.arange(N_CLASSES, dtype=jnp.int32)[None, :], seg_base[None, :], 0),
                   axis=1)
    pos = base + rank
    zoff = jnp.maximum(tile_end - 1, 0) * tme
    return (pos, zoff.astype(jnp.int32), tile_a.astype(jnp.int32), tile_b.astype(jnp.int32),
            live.astype(jnp.int32), ntiles.reshape(1).astype(jnp.int32))


def kernel(x, g_mix, w_in, sinks, w_br_dil, w_br_swa, w_out, g_ffn, w_group, b_group, w_router, b_router,
           w_e_gate, w_e_up, w_e_down, g_final):
    b, t, dm = x.shape
    n = b * t
    tme = 256
    qkv0, swa, gates, qkv4, qkv16 = _inproj(x, g_mix[0], w_in[0], tm=512)
    o0, l0 = _dil_attention(qkv0.reshape(b, 1, t, QKV_W), 0, tq=512)
    o1, l1 = _dil_attention(qkv4, 1, tq=512)
    o2, l2 = _dil_attention(qkv16, 2, tq=512)
    ys = _swa_attention(swa, sinks[0], tq=512)
    x1, h2, route, counts = _merge_route(
        x, o0.reshape(b, t, GROUP_W), l0.reshape(b, t, LANES), o1, l1, o2, l2, ys, gates,
        w_br_dil[0], w_br_swa[0], w_out[0], g_ffn[0], w_group[0], b_group[0], w_router[0], b_router[0], tm=512)
    route = route.reshape(n, LANES)
    pos, zoff, tile_a, tile_b, tile_block, ntiles = _slot_layout(route, counts, n, tme)
    nslot = n + N_CLASSES * tme
    xs = _dispatch(h2.reshape(n * ROW_TILE, LANES), pos, zoff, ntiles, nslot, td=512, tme=tme)
    yslots = _experts(xs, tile_a, tile_b, tile_block, ntiles, w_e_gate[0], w_e_up[0], w_e_down[0], tme=tme)
    out = _combine(yslots, pos, x1.reshape(n, dm), route, g_final, tc=256)
    return out.reshape(b, t, dm)
```

```python
import functools

import numpy as np
import jax
import jax.numpy as jnp
from jax import lax
from jax.experimental import pallas as pl
from jax.experimental.pallas import tpu as pltpu

F32 = jnp.float32
BF16 = jnp.bfloat16

HEAD_DIM = 64
BAND = 128
DIL_PATTERNS = ((128, 1), (512, 4), (2048, 16))
DIL_HEADS_PER_GROUP = 8
DIL_HEADS = 24
GROUP_W = DIL_HEADS_PER_GROUP * HEAD_DIM
QKV_W = 3 * GROUP_W
SWA_WINDOW = 128
SWA_Q_HEADS = 16
SWA_KV_HEADS = 2
SWA_Q_W = SWA_Q_HEADS * HEAD_DIM
SWA_W = SWA_Q_W + 4 * 128
MOE_GROUPS = 4
EXPERTS_PER_GROUP = 8
N_EXPERTS = 32
D_EXPERT = 512
RMS_EPS = 1e-6
LANES = 128
NEG_INF = float("-inf")
LOG2E = 1.4426950408889634
LN2 = 0.6931471805599453
ROW_TILE = 8

VMEM_LIMIT = 56 * 1024 * 1024


def _alibi_slopes(n):
    return (2.0 ** (-8.0 * np.arange(1, n + 1) / n)).astype(np.float32)


def _band_bias(slopes, max_back, unit):
    a = np.arange(BAND)[:, None]
    c = np.arange(2 * BAND)[None, :]
    delta = a + BAND - c
    band = (delta >= 0) & (delta <= max_back)
    pen = (-slopes[:, None, None] * (delta * unit)[None] * LOG2E).astype(np.float32)
    full = np.where(band[None], pen, -np.inf).astype(np.float32)
    first = np.where((band & (c >= BAND))[None], pen, -np.inf).astype(np.float32)
    return np.stack([full, first], axis=0)


def _inproj_kernel(x_ref, g_ref, w_ref, qkv0_ref, swa_ref, gates_ref, qkv4_ref, qkv16_ref, hs_ref, *, tm):
    x = x_ref[...]
    ms = jnp.mean(x * x, axis=-1, keepdims=True)
    h = x * lax.rsqrt(ms + RMS_EPS) * g_ref[...]
    nchunk = h.shape[1] // LANES
    for c in range(nchunk):
        hs_ref[c] = h[:, c * LANES:(c + 1) * LANES]
    hb = h.astype(BF16)

    def proj(lhs, c0, width):
        return jnp.dot(lhs, w_ref[:, c0:c0 + width], preferred_element_type=F32)

    for j in range(3):
        qkv0_ref[:, j * GROUP_W:(j + 1) * GROUP_W] = proj(hb, COL_QKV_D[j], GROUP_W).astype(BF16)
    for c0 in range(0, SWA_Q_W, 512):
        swa_ref[:, c0:c0 + 512] = proj(hb, COL_Q_S + c0, 512).astype(BF16)
    kv = proj(hb, COL_KV_S, 4 * HEAD_DIM)
    part = [kv[:, j * HEAD_DIM:(j + 1) * HEAD_DIM] for j in range(4)]
    swa_ref[:, SWA_Q_W:SWA_W] = jnp.concatenate(
        [part[0], part[0], part[1], part[1], part[2], part[2], part[3], part[3]], axis=1).astype(BF16)
    for c0 in range(0, 2048, 512):
        gates_ref[:, c0:c0 + 512] = proj(hb, COL_GATES + c0, 512).astype(BF16)
    for gi, out_ref, d in ((1, qkv4_ref, 4), (2, qkv16_ref, 16)):
        rows = tm // d
        hp = jnp.concatenate(
            [jnp.concatenate([hs_ref[c, pl.ds(r, rows, stride=d), :] for c in range(nchunk)], axis=1)
             for r in range(d)], axis=0).astype(BF16)
        for j in range(3):
            res = proj(hp, COL_QKV_D[j] + gi * GROUP_W, GROUP_W).astype(BF16)
            for r in range(d):
                out_ref[r, :, j * GROUP_W:(j + 1) * GROUP_W] = res[r * rows:(r + 1) * rows]


COL_QKV_D = (0, 1536, 3072)
COL_Q_S = 4608
COL_KV_S = 5632
COL_GATES = 5888
IN_WIDTH = 7936
Q_SCALE = HEAD_DIM ** -0.5 * LOG2E


def _prep_w_in(w_in):
    col = np.arange(IN_WIDTH)
    is_q = (col < COL_QKV_D[1]) | ((col >= COL_Q_S) & (col < COL_KV_S))
    colscale = np.where(is_q, Q_SCALE, 1.0).astype(np.float32)
    return (w_in * colscale[None, :]).astype(BF16)


def _inproj(x, g_mix, w_in, *, tm):
    b, t, dm = x.shape
    wp = _prep_w_in(w_in)
    nw = wp.shape[1]
    nt = t // tm
    out_shape = (
        jax.ShapeDtypeStruct((b, t, QKV_W), BF16),
        jax.ShapeDtypeStruct((b, t, SWA_W), BF16),
        jax.ShapeDtypeStruct((b, t, 2048), BF16),
        jax.ShapeDtypeStruct((b, 4, t // 4, QKV_W), BF16),
        jax.ShapeDtypeStruct((b, 16, t // 16, QKV_W), BF16),
    )
    return pl.pallas_call(
        functools.partial(_inproj_kernel, tm=tm),
        out_shape=out_shape,
        grid=(b, nt),
        in_specs=[
            pl.BlockSpec((None, tm, dm), lambda bi, i: (bi, i, 0)),
            pl.BlockSpec((1, dm), lambda bi, i: (0, 0)),
            pl.BlockSpec((dm, nw), lambda bi, i: (0, 0), pipeline_mode=pl.Buffered(1)),
        ],
        out_specs=(
            pl.BlockSpec((None, tm, QKV_W), lambda bi, i: (bi, i, 0)),
            pl.BlockSpec((None, tm, SWA_W), lambda bi, i: (bi, i, 0)),
            pl.BlockSpec((None, tm, 2048), lambda bi, i: (bi, i, 0)),
            pl.BlockSpec((None, 4, tm // 4, QKV_W), lambda bi, i: (bi, 0, i, 0)),
            pl.BlockSpec((None, 16, tm // 16, QKV_W), lambda bi, i: (bi, 0, i, 0)),
        ),
        scratch_shapes=[pltpu.VMEM((dm // LANES, tm, LANES), F32)],
        compiler_params=pltpu.CompilerParams(
            dimension_semantics=("arbitrary", "arbitrary"), vmem_limit_bytes=VMEM_LIMIT),
        name="inproj",
    )(x, g_mix.reshape(1, dm), wp)


def _softmax_pv(s, vv, sink=None):
    m = jnp.max(s, axis=-1, keepdims=True)
    if sink is not None:
        m = jnp.maximum(m, sink)
    p = jnp.exp2(s - m)
    l = jnp.sum(p, axis=-1, keepdims=True)
    if sink is not None:
        l = l + jnp.exp2(sink - m)
    o = jnp.dot(p.astype(BF16), vv, preferred_element_type=F32)
    return o * (1.0 / l), (m + jnp.log2(l)) * LN2


def _dil_attn_kernel(q_ref, kc_ref, vc_ref, kp_ref, vp_ref, bias_ref, o_ref, lse_ref, kbuf, vbuf, *, tq):
    i = pl.program_id(2)
    kbuf[0:BAND] = kp_ref[...]
    kbuf[BAND:BAND + tq] = kc_ref[...]
    vbuf[0:BAND] = vp_ref[...]
    vbuf[BAND:BAND + tq] = vc_ref[...]
    lane = lax.broadcasted_iota(jnp.int32, (BAND, LANES), 1)
    lo = lane < HEAD_DIM
    zero = jnp.zeros((BAND, LANES), BF16)
    for qb in range(tq // BAND):
        sel = jnp.where(i == 0, 1, 0) if qb == 0 else 0
        rq = slice(qb * BAND, (qb + 1) * BAND)
        rk = slice(qb * BAND, qb * BAND + 2 * BAND)
        lse_parts = []
        for pr in range(GROUP_W // LANES):
            cl = slice(pr * LANES, (pr + 1) * LANES)
            q2 = q_ref[rq, cl]
            kk = kbuf[rk, cl]
            vv = vbuf[rk, cl]
            res = []
            for hh, qm in enumerate((jnp.where(lo, q2, zero), jnp.where(lo, zero, q2))):
                s = lax.dot_general(qm, kk, (((1,), (1,)), ((), ())), preferred_element_type=F32)
                res.append(_softmax_pv(s + bias_ref[sel, 2 * pr + hh], vv))
            o_ref[rq, cl] = jnp.where(lo, res[0][0], res[1][0]).astype(o_ref.dtype)
            lse_parts.append(jnp.where(lane == 2 * pr, res[0][1], jnp.where(lane == 2 * pr + 1, res[1][1], 0.0)))
        lse_ref[rq, :] = (lse_parts[0] + lse_parts[1]) + (lse_parts[2] + lse_parts[3])


def _dil_attention(qkv, gi, *, tq):
    b, d, l, _ = qkv.shape
    window, dil = DIL_PATTERNS[gi]
    assert dil == d
    tq = min(tq, l)
    nq = tq // BAND
    slopes = _alibi_slopes(DIL_HEADS)[gi * 8:(gi + 1) * 8]
    bias = jnp.asarray(_band_bias(slopes, window // dil, dil))

    def cur(c):
        return pl.BlockSpec((None, None, tq, GROUP_W), lambda bi, r, i: (bi, r, i, c))

    def prev(c):
        return pl.BlockSpec((None, None, BAND, GROUP_W), lambda bi, r, i: (bi, r, jnp.maximum(i * nq - 1, 0), c))

    return pl.pallas_call(
        functools.partial(_dil_attn_kernel, tq=tq),
        out_shape=(jax.ShapeDtypeStruct((b, d, l, GROUP_W), BF16),
                   jax.ShapeDtypeStruct((b, d, l, LANES), F32)),
        grid=(b, d, l // tq),
        in_specs=[cur(0), cur(1), cur(2), prev(1), prev(2),
                  pl.BlockSpec((2, 8, BAND, 2 * BAND), lambda bi, r, i: (0, 0, 0, 0))],
        out_specs=(cur(0), pl.BlockSpec((None, None, tq, LANES), lambda bi, r, i: (bi, r, i, 0))),
        scratch_shapes=[pltpu.VMEM((BAND + tq, GROUP_W), BF16), pltpu.VMEM((BAND + tq, GROUP_W), BF16)],
        compiler_params=pltpu.CompilerParams(
            dimension_semantics=("arbitrary", "arbitrary", "arbitrary"), vmem_limit_bytes=VMEM_LIMIT),
        name=f"dil_attn_{gi}",
    )(qkv, qkv, qkv, qkv, qkv, bias)


def _swa_attn_kernel(sink_ref, qlo_ref, qhi_ref, kvc_ref, kvp_ref, bias_ref, o_ref, kvbuf, *, tq):
    i = pl.program_id(1)
    kvbuf[0:BAND] = kvp_ref[...]
    kvbuf[BAND:BAND + tq] = kvc_ref[...]
    lo = lax.broadcasted_iota(jnp.int32, (BAND, LANES), 1) < HEAD_DIM
    zero = jnp.zeros((BAND, LANES), BF16)
    for qb in range(tq // BAND):
        sel = jnp.where(i == 0, 1, 0) if qb == 0 else 0
        rq = slice(qb * BAND, (qb + 1) * BAND)
        rk = slice(qb * BAND, qb * BAND + 2 * BAND)
        for pr in range(SWA_Q_HEADS // 2):
            g = pr // 4
            q_ref = qlo_ref if pr < 4 else qhi_ref
            q2 = q_ref[rq, (pr % 4) * LANES:(pr % 4 + 1) * LANES]
            kk = kvbuf[rk, g * LANES:(g + 1) * LANES]
            vv = kvbuf[rk, (2 + g) * LANES:(3 + g) * LANES]
            res = []
            for hh, qm in enumerate((jnp.where(lo, q2, zero), jnp.where(lo, zero, q2))):
                h = 2 * pr + hh
                s = lax.dot_general(qm, kk, (((1,), (1,)), ((), ())), preferred_element_type=F32)
                res.append(_softmax_pv(s + bias_ref[sel, h], vv, sink=sink_ref[h] * LOG2E)[0])
            o_ref[rq, pr * LANES:(pr + 1) * LANES] = jnp.where(lo, res[0], res[1]).astype(o_ref.dtype)


def _swa_attention(swa, sinks, *, tq):
    b, t, _ = swa.shape
    tq = min(tq, t)
    nq = tq // BAND
    bias = jnp.asarray(_band_bias(_alibi_slopes(SWA_Q_HEADS), SWA_WINDOW - 1, 1))
    grid_spec = pltpu.PrefetchScalarGridSpec(
        num_scalar_prefetch=1,
        grid=(b, t // tq),
        in_specs=[
            pl.BlockSpec((None, tq, 512), lambda bi, i, s: (bi, i, 0)),
            pl.BlockSpec((None, tq, 512), lambda bi, i, s: (bi, i, 1)),
            pl.BlockSpec((None, tq, 512), lambda bi, i, s: (bi, i, 2)),
            pl.BlockSpec((None, BAND, 512), lambda bi, i, s: (bi, jnp.maximum(i * nq - 1, 0), 2)),
            pl.BlockSpec((2, SWA_Q_HEADS, BAND, 2 * BAND), lambda bi, i, s: (0, 0, 0, 0)),
        ],
        out_specs=pl.BlockSpec((None, tq, SWA_Q_W), lambda bi, i, s: (bi, i, 0)),
        scratch_shapes=[pltpu.VMEM((BAND + tq, 512), BF16)],
    )
    return pl.pallas_call(
        functools.partial(_swa_attn_kernel, tq=tq),
        out_shape=jax.ShapeDtypeStruct((b, t, SWA_Q_W), BF16),
        grid_spec=grid_spec,
        compiler_params=pltpu.CompilerParams(
            dimension_semantics=("arbitrary", "arbitrary"), vmem_limit_bytes=VMEM_LIMIT),
        name="swa_attn",
    )(sinks.astype(F32), swa, swa, swa, swa, bias)


ROUTE_CLASS, ROUTE_RANK, ROUTE_WA, ROUTE_WB = range(4)
ROUTER_LANE0 = MOE_GROUPS
PAIRS_PER_GROUP = EXPERTS_PER_GROUP * (EXPERTS_PER_GROUP - 1) // 2
N_CLASSES = MOE_GROUPS * PAIRS_PER_GROUP
_PAIRS = [(a, b) for a in range(EXPERTS_PER_GROUP) for b in range(a + 1, EXPERTS_PER_GROUP)]
CLASS_EXPERT_A = np.array([g * EXPERTS_PER_GROUP + a for g in range(MOE_GROUPS) for a, _ in _PAIRS], np.int32)
CLASS_EXPERT_B = np.array([g * EXPERTS_PER_GROUP + b for g in range(MOE_GROUPS) for _, b in _PAIRS], np.int32)


def _route(logits, carry_ref, tm):
    lane = lax.broadcasted_iota(jnp.int32, (tm, LANES), 1)
    lanef = lane.astype(F32)

    def first_argmax(v):
        m = jnp.max(v, axis=-1, keepdims=True)
        return m, jnp.min(jnp.where(v == m, lanef, float(LANES)), axis=-1, keepdims=True)

    gl = jnp.where(lane < MOE_GROUPS, logits, NEG_INF)
    gmax, gidx = first_argmax(gl)
    g_w = 1.0 / jnp.sum(jnp.exp(gl - gmax), axis=-1, keepdims=True)
    e_lane = lane - ROUTER_LANE0
    lane_group = (e_lane >> 3).astype(F32)
    in_group = (e_lane >= 0) & (e_lane < N_EXPERTS) & (lane_group == gidx)
    el = jnp.where(in_group, logits, NEG_INF)
    m1, i1 = first_argmax(el)
    m2, i2 = first_argmax(jnp.where(lanef == i1, NEG_INF, el))
    tt = jnp.exp(m2 - m1)
    w1 = g_w / (1.0 + tt)
    w2 = g_w * tt / (1.0 + tt)
    first = float(ROUTER_LANE0) + float(EXPERTS_PER_GROUP) * gidx
    e1 = i1 - first
    e2 = i2 - first
    swap = e2 < e1
    ea = jnp.minimum(e1, e2)
    eb = jnp.maximum(e1, e2)
    wa = jnp.where(swap, w2, w1)
    wb = jnp.where(swap, w1, w2)
    pair = ea * (float(2 * EXPERTS_PER_GROUP - 1) - ea) * 0.5 + (eb - ea - 1.0)
    cls = float(PAIRS_PER_GROUP) * gidx + pair
    oh = jnp.where(lanef == cls, 1.0, 0.0)
    row = lax.broadcasted_iota(jnp.int32, (tm, tm), 0)
    colr = lax.broadcasted_iota(jnp.int32, (tm, tm), 1)
    ltri = jnp.where(colr < row, 1.0, 0.0).astype(BF16)
    before = jnp.dot(ltri, oh.astype(BF16), preferred_element_type=F32) + carry_ref[0:1, :]
    rank = jnp.sum(oh * before, axis=-1, keepdims=True)
    carry_ref[...] = carry_ref[...] + jnp.sum(oh, axis=0, keepdims=True)
    rec = jnp.zeros((tm, LANES), F32)
    for ln, val in ((ROUTE_CLASS, cls), (ROUTE_RANK, rank), (ROUTE_WA, wa), (ROUTE_WB, wb)):
        rec = jnp.where(lane == ln, val, rec)
    return rec


def _merge_kernel(x_ref, o0_ref, l0_ref, o1_ref, l1_ref, o2_ref, l2_ref, ys_ref, gates_ref,
                  wbd_ref, wbs_ref, wo_ref, gffn_ref, wrt_ref, brt_ref, exp_ref,
                  x1_ref, h2_ref, route_ref, cnt_ref,
                  uo1, ul1, uo2, ul2, carry_ref, *, tm):
    first = (pl.program_id(0) == 0) & (pl.program_id(1) == 0)

    @pl.when(first)
    def _():
        carry_ref[...] = jnp.zeros_like(carry_ref)

    nch = GROUP_W // LANES
    for src_o, src_l, dst_o, dst_l, d in ((o1_ref, l1_ref, uo1, ul1, 4), (o2_ref, l2_ref, uo2, ul2, 16)):
        rows = tm // d
        for r in range(d):
            dst_l[pl.ds(r, rows, stride=d), :] = src_l[r]
            for c in range(nch):
                dst_o[c, pl.ds(r, rows, stride=d), :] = src_o[r, :, c * LANES:(c + 1) * LANES].astype(F32)
    la, lb, lc = l0_ref[...], ul1[...], ul2[...]
    mx = jnp.maximum(jnp.maximum(la, lb), lc)
    ea, eb, ec = jnp.exp(la - mx), jnp.exp(lb - mx), jnp.exp(lc - mx)
    inv = 1.0 / (ea + eb + ec)

    def per_head_to_lanes(w):
        hi = w.astype(BF16)
        lo = (w - hi.astype(F32)).astype(BF16)
        return jnp.dot(jnp.concatenate([hi, lo], axis=1), exp_ref[...], preferred_element_type=F32)

    wa, wb, wc = per_head_to_lanes(ea * inv), per_head_to_lanes(eb * inv), per_head_to_lanes(ec * inv)
    ych = []
    for c in range(nch):
        cl = slice(c * LANES, (c + 1) * LANES)
        ych.append((wa[:, cl] * o0_ref[:, cl].astype(F32) + wb[:, cl] * uo1[c] + wc[:, cl] * uo2[c]).astype(BF16))
    y = jnp.concatenate(ych, axis=1)
    a = jnp.dot(y, wbd_ref[...], preferred_element_type=F32)
    bsw = jnp.dot(ys_ref[...], wbs_ref[...], preferred_element_type=F32)
    def sig(v):
        return 0.5 * jnp.tanh(0.5 * v) + 0.5

    mixed = sig(gates_ref[:, 0:1024]) * a.astype(BF16) + sig(gates_ref[:, 1024:2048]) * bsw.astype(BF16)
    x1 = x_ref[...] + jnp.dot(mixed, wo_ref[...], preferred_element_type=F32)
    x1_ref[...] = x1
    ms = jnp.mean(x1 * x1, axis=-1, keepdims=True)
    h2 = x1 * lax.rsqrt(ms + RMS_EPS) * gffn_ref[...]
    for c in range(h2.shape[1] // LANES):
        h2_ref[pl.ds(c, tm, stride=ROW_TILE), :] = h2[:, c * LANES:(c + 1) * LANES]
    h_hi = h2.astype(BF16)
    h_lo = (h2 - h_hi.astype(F32)).astype(BF16)
    hw = jnp.dot(h_hi, wrt_ref[...], preferred_element_type=F32)
    logits = (hw[:, :LANES] + hw[:, LANES:]
              + jnp.dot(h_lo, wrt_ref[:, :LANES], preferred_element_type=F32) + brt_ref[...])
    route_ref[...] = _route(logits, carry_ref, tm)
    cnt_ref[...] = carry_ref[...]


def _merge_route(x, o0, l0, o1, l1, o2, l2, ys, gates, w_br_dil, w_br_swa, w_out, g_ffn,
                 w_group, b_group, w_router, b_router, *, tm):
    b, t, dm = x.shape
    nt = t // tm
    wrt = jnp.zeros((dm, LANES), F32).at[:, :MOE_GROUPS].set(w_group).at[
        :, ROUTER_LANE0:ROUTER_LANE0 + N_EXPERTS].set(w_router)
    brt = jnp.zeros((1, LANES), F32).at[0, :MOE_GROUPS].set(b_group).at[
        0, ROUTER_LANE0:ROUTER_LANE0 + N_EXPERTS].set(b_router)
    wrt_hi = wrt.astype(BF16)
    wrt_lo = (wrt - wrt_hi.astype(F32)).astype(BF16)
    wrt = jnp.concatenate([wrt_hi, wrt_lo], axis=1)

    def tok(width):
        return pl.BlockSpec((None, tm, width), lambda bi, i: (bi, i, 0))

    def stream(d, width):
        return pl.BlockSpec((None, d, tm // d, width), lambda bi, i: (bi, 0, i, 0))

    def const(shape):
        return pl.BlockSpec(shape, lambda bi, i: (0,) * len(shape), pipeline_mode=pl.Buffered(1))

    head_of_lane = np.arange(GROUP_W) // HEAD_DIM
    spread = (np.arange(LANES)[:, None] == head_of_lane[None, :]).astype(np.float32)
    expand = jnp.asarray(np.concatenate([spread, spread], axis=0), dtype=BF16)

    nch = GROUP_W // LANES
    return pl.pallas_call(
        functools.partial(_merge_kernel, tm=tm),
        out_shape=(jax.ShapeDtypeStruct((b, t, dm), F32), jax.ShapeDtypeStruct((b, t * ROW_TILE, LANES), F32),
                   jax.ShapeDtypeStruct((b, t, LANES), F32), jax.ShapeDtypeStruct((8, LANES), F32)),
        grid=(b, nt),
        in_specs=[tok(dm), tok(GROUP_W), tok(LANES), stream(4, GROUP_W), stream(4, LANES),
                  stream(16, GROUP_W), stream(16, LANES), tok(SWA_Q_W), tok(2048),
                  const((GROUP_W, dm)), const((SWA_Q_W, dm)), const((dm, dm)), const((1, dm)),
                  const((dm, 2 * LANES)), const((1, LANES)), const((2 * LANES, GROUP_W))],
        out_specs=(tok(dm), pl.BlockSpec((None, tm * ROW_TILE, LANES), lambda bi, i: (bi, i, 0)), tok(LANES),
                   pl.BlockSpec((8, LANES), lambda bi, i: (0, 0))),
        scratch_shapes=[pltpu.VMEM((nch, tm, LANES), F32), pltpu.VMEM((tm, LANES), F32),
                        pltpu.VMEM((nch, tm, LANES), F32), pltpu.VMEM((tm, LANES), F32),
                        pltpu.VMEM((8, LANES), F32)],
        compiler_params=pltpu.CompilerParams(
            dimension_semantics=("arbitrary", "arbitrary"), vmem_limit_bytes=VMEM_LIMIT),
        name="merge_route",
    )(x, o0, l0, o1, l1, o2, l2, ys, gates, w_br_dil.astype(BF16), w_br_swa.astype(BF16), w_out.astype(BF16),
      g_ffn.reshape(1, dm), wrt, brt, expand)


def _dispatch_kernel(pos_ref, zoff_ref, nt_ref, h_ref, xs_ref, zbuf, sem, zsem, *, td, tme, nt_max):
    step = pl.program_id(0)

    def _zero_copy(e):
        off = pl.multiple_of(zoff_ref[e] * ROW_TILE, tme * ROW_TILE)
        return pltpu.make_async_copy(zbuf, xs_ref.at[pl.ds(off, tme * ROW_TILE)], zsem)

    def _row_copy(src_row, dst_row):
        return pltpu.make_async_copy(h_ref.at[pl.ds(pl.multiple_of(src_row * ROW_TILE, ROW_TILE), ROW_TILE)],
                                     xs_ref.at[pl.ds(pl.multiple_of(dst_row * ROW_TILE, ROW_TILE), ROW_TILE)], sem)

    def _wait_rows():
        pltpu.make_async_copy(h_ref, xs_ref.at[pl.ds(0, td * ROW_TILE)], sem).wait()

    def _tail_copy(tile):
        off = pl.multiple_of(tile * (tme * ROW_TILE), tme * ROW_TILE)
        return pltpu.make_async_copy(zbuf, xs_ref.at[pl.ds(off, tme * ROW_TILE)], zsem)

    @pl.when(step == 0)
    def _():
        zbuf[...] = jnp.zeros_like(zbuf)
        for e in range(N_CLASSES):
            _zero_copy(e).start()
        for e in range(N_CLASSES):
            _zero_copy(e).wait()

        def tail(tile, carry):
            _tail_copy(tile).start()
            _tail_copy(tile).wait()
            return carry

        lax.fori_loop(nt_ref[0], nt_max, tail, 0)

    def issue(j, carry):
        _row_copy(j, pos_ref[0, 0, j]).start()
        return carry

    lax.fori_loop(0, td, issue, 0, unroll=8)
    _wait_rows()


def _dispatch(h2, pos, zoff, ntiles, nslot, *, td, tme):
    n = h2.shape[0] // ROW_TILE
    nb = n // td
    grid_spec = pltpu.PrefetchScalarGridSpec(
        num_scalar_prefetch=0,
        grid=(nb,),
        in_specs=[
            pl.BlockSpec((1, 1, td), lambda i: (i, 0, 0), memory_space=pltpu.SMEM),
            pl.BlockSpec(memory_space=pltpu.SMEM),
            pl.BlockSpec(memory_space=pltpu.SMEM),
            pl.BlockSpec((td * ROW_TILE, LANES), lambda i: (i, 0)),
        ],
        out_specs=pl.BlockSpec(memory_space=pl.ANY),
        scratch_shapes=[pltpu.VMEM((tme * ROW_TILE, LANES), F32), pltpu.SemaphoreType.DMA(()),
                        pltpu.SemaphoreType.DMA(())],
    )
    return pl.pallas_call(
        functools.partial(_dispatch_kernel, td=td, tme=tme, nt_max=nslot // tme),
        out_shape=jax.ShapeDtypeStruct((nslot * ROW_TILE, LANES), F32),
        grid_spec=grid_spec,
        compiler_params=pltpu.CompilerParams(dimension_semantics=("arbitrary",), has_side_effects=True),
        name="dispatch",
    )(pos.reshape(nb, 1, td), zoff, ntiles, h2)


PAIR_ROWS = 2 * ROW_TILE


def _expert_kernel(ta_ref, tb_ref, tblk_ref, nt_ref, x_ref, wga_ref, wua_ref, wda_ref, wgb_ref, wub_ref, wdb_ref,
                   y_ref, *, tme):
    i = pl.program_id(0)

    @pl.when(i < nt_ref[0])
    def _():
        xb = jnp.concatenate([x_ref[pl.ds(c, tme, stride=ROW_TILE), :] for c in range(ROW_TILE)],
                             axis=1).astype(BF16)
        nw = 2 * LANES
        for half, (wg_ref, wu_ref, wd_ref) in enumerate(((wga_ref, wua_ref, wda_ref), (wgb_ref, wub_ref, wdb_ref))):
            act = []
            for c0 in range(0, D_EXPERT, nw):
                g = jnp.dot(xb, wg_ref[:, c0:c0 + nw], preferred_element_type=F32)
                u = jnp.dot(xb, wu_ref[:, c0:c0 + nw], preferred_element_type=F32)
                act.append((g * jax.nn.sigmoid(g) * u).astype(BF16))
            a = jnp.concatenate(act, axis=1)
            for c0 in range(0, ROW_TILE * LANES, nw):
                y = jnp.dot(a, wd_ref[:, c0:c0 + nw], preferred_element_type=F32)
                for k in range(nw // LANES):
                    c = c0 // LANES + k
                    y_ref[pl.ds(half * ROW_TILE + c, tme, stride=PAIR_ROWS), :] = y[:, k * LANES:(k + 1) * LANES]

    @pl.when(i >= nt_ref[0])
    def _():
        y_ref[...] = jnp.zeros_like(y_ref)


def _experts(xs, tile_a, tile_b, tile_block, ntiles, w_e_gate, w_e_up, w_e_down, *, tme):
    nslot = xs.shape[0] // ROW_TILE
    dm = ROW_TILE * LANES
    nt = nslot // tme
    wg, wu, wd = w_e_gate.astype(BF16), w_e_up.astype(BF16), w_e_down.astype(BF16)

    def w_in_spec(which):
        return pl.BlockSpec((None, dm, D_EXPERT), lambda i, ta, tb, tk, n: ((ta, tb)[which][i], 0, 0))

    def w_out_spec(which):
        return pl.BlockSpec((None, D_EXPERT, dm), lambda i, ta, tb, tk, n: ((ta, tb)[which][i], 0, 0))

    grid_spec = pltpu.PrefetchScalarGridSpec(
        num_scalar_prefetch=4,
        grid=(nt,),
        in_specs=[
            pl.BlockSpec((tme * ROW_TILE, LANES), lambda i, ta, tb, tk, n: (tk[i], 0)),
            w_in_spec(0), w_in_spec(0), w_out_spec(0), w_in_spec(1), w_in_spec(1), w_out_spec(1),
        ],
        out_specs=pl.BlockSpec((tme * PAIR_ROWS, LANES), lambda i, ta, tb, tk, n: (i, 0)),
    )
    return pl.pallas_call(
        functools.partial(_expert_kernel, tme=tme),
        out_shape=jax.ShapeDtypeStruct((nslot * PAIR_ROWS, LANES), F32),
        grid_spec=grid_spec,
        compiler_params=pltpu.CompilerParams(dimension_semantics=("arbitrary",), vmem_limit_bytes=VMEM_LIMIT),
        name="experts",
    )(tile_a, tile_b, tile_block, ntiles, xs, wg, wu, wd, wg, wu, wd)


def _combine_kernel(pos_ref, posn_ref, ys_ref, x1_ref, route_ref, gfin_ref, out_ref, ybuf, sem, *, tc):
    i = pl.program_id(0)
    nb = pl.num_programs(0)
    slot = i % 2

    def gather(p_ref, s):
        def issue(j, carry):
            src = pl.multiple_of(p_ref[0, 0, j] * PAIR_ROWS, PAIR_ROWS)
            dst = pl.multiple_of(j * PAIR_ROWS, PAIR_ROWS)
            pltpu.make_async_copy(ys_ref.at[pl.ds(src, PAIR_ROWS)], ybuf.at[s, pl.ds(dst, PAIR_ROWS)],
                                  sem.at[s]).start()
            return carry

        lax.fori_loop(0, tc, issue, 0, unroll=8)

    @pl.when(i == 0)
    def _():
        gather(pos_ref, 0)

    @pl.when(i + 1 < nb)
    def _():
        gather(posn_ref, 1 - slot)

    pltpu.make_async_copy(ys_ref.at[pl.ds(0, tc * PAIR_ROWS)], ybuf.at[slot], sem.at[slot]).wait()
    rec = route_ref[...]
    wa = rec[:, ROUTE_WA:ROUTE_WA + 1]
    wb = rec[:, ROUTE_WB:ROUTE_WB + 1]
    ya = jnp.concatenate([ybuf[slot, pl.ds(c, tc, stride=PAIR_ROWS), :] for c in range(ROW_TILE)], axis=1)
    yb = jnp.concatenate([ybuf[slot, pl.ds(ROW_TILE + c, tc, stride=PAIR_ROWS), :] for c in range(ROW_TILE)],
                         axis=1)
    z = x1_ref[...] + wa * ya + wb * yb
    ms = jnp.mean(z * z, axis=-1, keepdims=True)
    out_ref[...] = z * lax.rsqrt(ms + RMS_EPS) * gfin_ref[...]


def _combine(ys, pos, x1, route, g_final, *, tc):
    n, dm = x1.shape
    nb = n // tc
    grid_spec = pltpu.PrefetchScalarGridSpec(
        num_scalar_prefetch=0,
        grid=(nb,),
        in_specs=[
            pl.BlockSpec((1, 1, tc), lambda i: (i, 0, 0), memory_space=pltpu.SMEM),
            pl.BlockSpec((1, 1, tc), lambda i: (jnp.minimum(i + 1, nb - 1), 0, 0), memory_space=pltpu.SMEM),
            pl.BlockSpec(memory_space=pl.ANY),
            pl.BlockSpec((tc, dm), lambda i: (i, 0)),
            pl.BlockSpec((tc, LANES), lambda i: (i, 0)),
            pl.BlockSpec((1, dm), lambda i: (0, 0)),
        ],
        out_specs=pl.BlockSpec((tc, dm), lambda i: (i, 0)),
        scratch_shapes=[pltpu.VMEM((2, tc * PAIR_ROWS, LANES), F32), pltpu.SemaphoreType.DMA((2,))],
    )
    return pl.pallas_call(
        functools.partial(_combine_kernel, tc=tc),
        out_shape=jax.ShapeDtypeStruct((n, dm), F32),
        grid_spec=grid_spec,
        compiler_params=pltpu.CompilerParams(dimension_semantics=("arbitrary",), vmem_limit_bytes=VMEM_LIMIT),
        name="combine",
    )(pos.reshape(nb, 1, tc), pos.reshape(nb, 1, tc), ys, x1, route, g_final.reshape(1, dm))


def _slot_layout(route, counts, n, tme):
    cnt = counts[0, :N_CLASSES].astype(jnp.int32)
    tiles_per = (cnt + tme - 1) // tme
    tile_end = jnp.cumsum(tiles_per)
    seg_base = (tile_end - tiles_per) * tme
    ntiles = tile_end[-1]
    nt_max = n // tme + N_CLASSES
    tid = jnp.arange(nt_max, dtype=jnp.int32)
    live = jnp.minimum(tid, ntiles - 1)
    tile_class = jnp.sum((tile_end[None, :] <= live[:, None]).astype(jnp.int32), axis=1)
    tile_a = jnp.take(jnp.asarray(CLASS_EXPERT_A), tile_class)
    tile_b = jnp.take(jnp.asarray(CLASS_EXPERT_B), tile_class)
    cls = route[:, ROUTE_CLASS].astype(jnp.int32)
    rank = route[:, ROUTE_RANK].astype(jnp.int32)
    base = jnp.sum(jnp.where(cls[:, None] == jnp.arange(N_CLASSES, dtype=jnp.int32)[None, :], seg_base[None, :], 0),
                   axis=1)
    pos = base + rank
    zoff = jnp.maximum(tile_end - 1, 0) * tme
    return (pos, zoff.astype(jnp.int32), tile_a.astype(jnp.int32), tile_b.astype(jnp.int32),
            live.astype(jnp.int32), ntiles.reshape(1).astype(jnp.int32))


def kernel(x, g_mix, w_in, sinks, w_br_dil, w_br_swa, w_out, g_ffn, w_group, b_group, w_router, b_router,
           w_e_gate, w_e_up, w_e_down, g_final):
    b, t, dm = x.shape
    n = b * t
    tme = 256
    qkv0, swa, gates, qkv4, qkv16 = _inproj(x, g_mix[0], w_in[0], tm=512)
    o0, l0 = _dil_attention(qkv0.reshape(b, 1, t, QKV_W), 0, tq=512)
    o1, l1 = _dil_attention(qkv4, 1, tq=512)
    o2, l2 = _dil_attention(qkv16, 2, tq=512)
    ys = _swa_attention(swa, sinks[0], tq=512)
    x1, h2, route, counts = _merge_route(
        x, o0.reshape(b, t, GROUP_W), l0.reshape(b, t, LANES), o1, l1, o2, l2, ys, gates,
        w_br_dil[0], w_br_swa[0], w_out[0], g_ffn[0], w_group[0], b_group[0], w_router[0], b_router[0], tm=512)
    route = route.reshape(n, LANES)
    pos, zoff, tile_a, tile_b, tile_block, ntiles = _slot_layout(route, counts, n, tme)
    nslot = n + N_CLASSES * tme
    xs = _dispatch(h2.reshape(n * ROW_TILE, LANES), pos, zoff, ntiles, nslot, td=2048, tme=tme)
    yslots = _experts(xs, tile_a, tile_b, tile_block, ntiles, w_e_gate[0], w_e_up[0], w_e_down[0], tme=tme)
    out = _combine(yslots, pos, x1.reshape(n, dm), route, g_final, tc=512)
    return out.reshape(b, t, dm)
```

```python
import functools

import numpy as np
import jax
import jax.numpy as jnp
from jax import lax
from jax.experimental import pallas as pl
from jax.experimental.pallas import tpu as pltpu

F32 = jnp.float32
BF16 = jnp.bfloat16

HEAD_DIM = 64
BAND = 128
DIL_PATTERNS = ((128, 1), (512, 4), (2048, 16))
DIL_HEADS_PER_GROUP = 8
DIL_HEADS = 24
GROUP_W = DIL_HEADS_PER_GROUP * HEAD_DIM
QKV_W = 3 * GROUP_W
SWA_WINDOW = 128
SWA_Q_HEADS = 16
SWA_KV_HEADS = 2
SWA_Q_W = SWA_Q_HEADS * HEAD_DIM
SWA_W = SWA_Q_W + 4 * 128
MOE_GROUPS = 4
EXPERTS_PER_GROUP = 8
N_EXPERTS = 32
D_EXPERT = 512
RMS_EPS = 1e-6
LANES = 128
NEG_INF = float("-inf")
LOG2E = 1.4426950408889634
LN2 = 0.6931471805599453
ROW_TILE = 8

VMEM_LIMIT = 56 * 1024 * 1024


def _alibi_slopes(n):
    return (2.0 ** (-8.0 * np.arange(1, n + 1) / n)).astype(np.float32)


def _band_bias(slopes, max_back, unit):
    a = np.arange(BAND)[:, None]
    c = np.arange(2 * BAND)[None, :]
    delta = a + BAND - c
    band = (delta >= 0) & (delta <= max_back)
    pen = (-slopes[:, None, None] * (delta * unit)[None] * LOG2E).astype(np.float32)
    full = np.where(band[None], pen, -np.inf).astype(np.float32)
    first = np.where((band & (c >= BAND))[None], pen, -np.inf).astype(np.float32)
    return np.stack([full, first], axis=0)


def _inproj_kernel(x_ref, g_ref, w_ref, qkv0_ref, swa_ref, gates_ref, qkv4_ref, qkv16_ref, hs_ref, *, tm):
    x = x_ref[...]
    ms = jnp.mean(x * x, axis=-1, keepdims=True)
    h = x * lax.rsqrt(ms + RMS_EPS) * g_ref[...]
    nchunk = h.shape[1] // LANES
    for c in range(nchunk):
        hs_ref[c] = h[:, c * LANES:(c + 1) * LANES]
    hb = h.astype(BF16)

    def proj(lhs, c0, width):
        return jnp.dot(lhs, w_ref[:, c0:c0 + width], preferred_element_type=F32)

    for j in range(3):
        qkv0_ref[:, j * GROUP_W:(j + 1) * GROUP_W] = proj(hb, COL_QKV_D[j], GROUP_W).astype(BF16)
    for c0 in range(0, SWA_Q_W, 512):
        swa_ref[:, c0:c0 + 512] = proj(hb, COL_Q_S + c0, 512).astype(BF16)
    kv = proj(hb, COL_KV_S, 4 * HEAD_DIM)
    part = [kv[:, j * HEAD_DIM:(j + 1) * HEAD_DIM] for j in range(4)]
    swa_ref[:, SWA_Q_W:SWA_W] = jnp.concatenate(
        [part[0], part[0], part[1], part[1], part[2], part[2], part[3], part[3]], axis=1).astype(BF16)
    for c0 in range(0, 2048, 512):
        gates_ref[:, c0:c0 + 512] = proj(hb, COL_GATES + c0, 512).astype(BF16)
    for gi, out_ref, d in ((1, qkv4_ref, 4), (2, qkv16_ref, 16)):
        rows = tm // d
        hp = jnp.concatenate(
            [jnp.concatenate([hs_ref[c, pl.ds(r, rows, stride=d), :] for c in range(nchunk)], axis=1)
             for r in range(d)], axis=0).astype(BF16)
        for j in range(3):
            res = proj(hp, COL_QKV_D[j] + gi * GROUP_W, GROUP_W).astype(BF16)
            for r in range(d):
                out_ref[r, :, j * GROUP_W:(j + 1) * GROUP_W] = res[r * rows:(r + 1) * rows]


COL_QKV_D = (0, 1536, 3072)
COL_Q_S = 4608
COL_KV_S = 5632
COL_GATES = 5888
IN_WIDTH = 7936
Q_SCALE = HEAD_DIM ** -0.5 * LOG2E


def _prep_w_in(w_in):
    col = np.arange(IN_WIDTH)
    is_q = (col < COL_QKV_D[1]) | ((col >= COL_Q_S) & (col < COL_KV_S))
    colscale = np.where(is_q, Q_SCALE, 1.0).astype(np.float32)
    return (w_in * colscale[None, :]).astype(BF16)


def _inproj(x, g_mix, w_in, *, tm):
    b, t, dm = x.shape
    wp = _prep_w_in(w_in)
    nw = wp.shape[1]
    nt = t // tm
    out_shape = (
        jax.ShapeDtypeStruct((b, t, QKV_W), BF16),
        jax.ShapeDtypeStruct((b, t, SWA_W), BF16),
        jax.ShapeDtypeStruct((b, t, 2048), BF16),
        jax.ShapeDtypeStruct((b, 4, t // 4, QKV_W), BF16),
        jax.ShapeDtypeStruct((b, 16, t // 16, QKV_W), BF16),
    )
    return pl.pallas_call(
        functools.partial(_inproj_kernel, tm=tm),
        out_shape=out_shape,
        grid=(b, nt),
        in_specs=[
            pl.BlockSpec((None, tm, dm), lambda bi, i: (bi, i, 0)),
            pl.BlockSpec((1, dm), lambda bi, i: (0, 0)),
            pl.BlockSpec((dm, nw), lambda bi, i: (0, 0), pipeline_mode=pl.Buffered(1)),
        ],
        out_specs=(
            pl.BlockSpec((None, tm, QKV_W), lambda bi, i: (bi, i, 0)),
            pl.BlockSpec((None, tm, SWA_W), lambda bi, i: (bi, i, 0)),
            pl.BlockSpec((None, tm, 2048), lambda bi, i: (bi, i, 0)),
            pl.BlockSpec((None, 4, tm // 4, QKV_W), lambda bi, i: (bi, 0, i, 0)),
            pl.BlockSpec((None, 16, tm // 16, QKV_W), lambda bi, i: (bi, 0, i, 0)),
        ),
        scratch_shapes=[pltpu.VMEM((dm // LANES, tm, LANES), F32)],
        compiler_params=pltpu.CompilerParams(
            dimension_semantics=("arbitrary", "arbitrary"), vmem_limit_bytes=VMEM_LIMIT),
        name="inproj",
    )(x, g_mix.reshape(1, dm), wp)


def _dil_attn_kernel(q_ref, kc_ref, vc_ref, kp_ref, vp_ref, bias_ref, o_ref, lse_ref, kbuf, vbuf, *, tq):
    i = pl.program_id(2)
    kbuf[0:BAND] = kp_ref[...]
    kbuf[BAND:BAND + tq] = kc_ref[...]
    vbuf[0:BAND] = vp_ref[...]
    vbuf[BAND:BAND + tq] = vc_ref[...]
    lane = lax.broadcasted_iota(jnp.int32, (BAND, LANES), 1)
    lo = lane < HEAD_DIM
    zero = jnp.zeros((BAND, LANES), BF16)
    for qb in range(tq // BAND):
        sel = jnp.where(i == 0, 1, 0) if qb == 0 else 0
        rq = slice(qb * BAND, (qb + 1) * BAND)
        rk = slice(qb * BAND, qb * BAND + 2 * BAND)
        lse_parts = []
        for pr in range(GROUP_W // LANES):
            cl = slice(pr * LANES, (pr + 1) * LANES)
            q2 = q_ref[rq, cl]
            kk = kbuf[rk, cl]
            vv = vbuf[rk, cl]
            s_all = lax.dot_general(jnp.concatenate([jnp.where(lo, q2, zero), jnp.where(lo, zero, q2)], axis=0), kk,
                                    (((1,), (1,)), ((), ())), preferred_element_type=F32)
            probs, stats = [], []
            for hh in range(2):
                s = s_all[hh * BAND:(hh + 1) * BAND] + bias_ref[sel, 2 * pr + hh]
                m = jnp.max(s, axis=-1, keepdims=True)
                p = jnp.exp2(s - m)
                l = jnp.sum(p, axis=-1, keepdims=True)
                probs.append(p.astype(BF16))
                stats.append((1.0 / l, (m + jnp.log2(l)) * LN2))
            o_all = jnp.dot(jnp.concatenate(probs, axis=0), vv, preferred_element_type=F32)
            res = [(o_all[hh * BAND:(hh + 1) * BAND] * stats[hh][0], stats[hh][1]) for hh in range(2)]
            o_ref[rq, cl] = jnp.where(lo, res[0][0], res[1][0]).astype(o_ref.dtype)
            lse_parts.append(jnp.where(lane == 2 * pr, res[0][1], jnp.where(lane == 2 * pr + 1, res[1][1], 0.0)))
        lse_ref[rq, :] = (lse_parts[0] + lse_parts[1]) + (lse_parts[2] + lse_parts[3])


def _dil_attention(qkv, gi, *, tq):
    b, d, l, _ = qkv.shape
    window, dil = DIL_PATTERNS[gi]
    assert dil == d
    tq = min(tq, l)
    nq = tq // BAND
    slopes = _alibi_slopes(DIL_HEADS)[gi * 8:(gi + 1) * 8]
    bias = jnp.asarray(_band_bias(slopes, window // dil, dil))

    def cur(c):
        return pl.BlockSpec((None, None, tq, GROUP_W), lambda bi, r, i: (bi, r, i, c))

    def prev(c):
        return pl.BlockSpec((None, None, BAND, GROUP_W), lambda bi, r, i: (bi, r, jnp.maximum(i * nq - 1, 0), c))

    return pl.pallas_call(
        functools.partial(_dil_attn_kernel, tq=tq),
        out_shape=(jax.ShapeDtypeStruct((b, d, l, GROUP_W), BF16),
                   jax.ShapeDtypeStruct((b, d, l, LANES), F32)),
        grid=(b, d, l // tq),
        in_specs=[cur(0), cur(1), cur(2), prev(1), prev(2),
                  pl.BlockSpec((2, 8, BAND, 2 * BAND), lambda bi, r, i: (0, 0, 0, 0))],
        out_specs=(cur(0), pl.BlockSpec((None, None, tq, LANES), lambda bi, r, i: (bi, r, i, 0))),
        scratch_shapes=[pltpu.VMEM((BAND + tq, GROUP_W), BF16), pltpu.VMEM((BAND + tq, GROUP_W), BF16)],
        compiler_params=pltpu.CompilerParams(
            dimension_semantics=("arbitrary", "arbitrary", "arbitrary"), vmem_limit_bytes=VMEM_LIMIT),
        name=f"dil_attn_{gi}",
    )(qkv, qkv, qkv, qkv, qkv, bias)


def _swa_attn_kernel(sink_ref, qlo_ref, qhi_ref, kvc_ref, kvp_ref, bias_ref, o_ref, kvbuf, *, tq):
    i = pl.program_id(1)
    kvbuf[0:BAND] = kvp_ref[...]
    kvbuf[BAND:BAND + tq] = kvc_ref[...]
    lo = lax.broadcasted_iota(jnp.int32, (BAND, LANES), 1) < HEAD_DIM
    zero = jnp.zeros((BAND, LANES), BF16)
    for qb in range(tq // BAND):
        sel = jnp.where(i == 0, 1, 0) if qb == 0 else 0
        rq = slice(qb * BAND, (qb + 1) * BAND)
        rk = slice(qb * BAND, qb * BAND + 2 * BAND)
        for g in range(SWA_KV_HEADS):
            q_ref = qlo_ref if g == 0 else qhi_ref
            kk = kvbuf[rk, g * LANES:(g + 1) * LANES]
            vv = kvbuf[rk, (2 + g) * LANES:(3 + g) * LANES]
            masked = []
            for pp in range(4):
                q2 = q_ref[rq, pp * LANES:(pp + 1) * LANES]
                masked += [jnp.where(lo, q2, zero), jnp.where(lo, zero, q2)]
            s_all = lax.dot_general(jnp.concatenate(masked, axis=0), kk, (((1,), (1,)), ((), ())),
                                    preferred_element_type=F32)
            probs, inv = [], []
            for j in range(8):
                h = 8 * g + j
                sink = sink_ref[h] * LOG2E
                s = s_all[j * BAND:(j + 1) * BAND] + bias_ref[sel, h]
                m = jnp.maximum(jnp.max(s, axis=-1, keepdims=True), sink)
                p = jnp.exp2(s - m)
                inv.append(1.0 / (jnp.sum(p, axis=-1, keepdims=True) + jnp.exp2(sink - m)))
                probs.append(p.astype(BF16))
            o_all = jnp.dot(jnp.concatenate(probs, axis=0), vv, preferred_element_type=F32)
            for pp in range(4):
                oa = o_all[(2 * pp) * BAND:(2 * pp + 1) * BAND] * inv[2 * pp]
                ob = o_all[(2 * pp + 1) * BAND:(2 * pp + 2) * BAND] * inv[2 * pp + 1]
                o_ref[rq, (4 * g + pp) * LANES:(4 * g + pp + 1) * LANES] = jnp.where(lo, oa, ob).astype(o_ref.dtype)


def _swa_attention(swa, sinks, *, tq):
    b, t, _ = swa.shape
    tq = min(tq, t)
    nq = tq // BAND
    bias = jnp.asarray(_band_bias(_alibi_slopes(SWA_Q_HEADS), SWA_WINDOW - 1, 1))
    grid_spec = pltpu.PrefetchScalarGridSpec(
        num_scalar_prefetch=1,
        grid=(b, t // tq),
        in_specs=[
            pl.BlockSpec((None, tq, 512), lambda bi, i, s: (bi, i, 0)),
            pl.BlockSpec((None, tq, 512), lambda bi, i, s: (bi, i, 1)),
            pl.BlockSpec((None, tq, 512), lambda bi, i, s: (bi, i, 2)),
            pl.BlockSpec((None, BAND, 512), lambda bi, i, s: (bi, jnp.maximum(i * nq - 1, 0), 2)),
            pl.BlockSpec((2, SWA_Q_HEADS, BAND, 2 * BAND), lambda bi, i, s: (0, 0, 0, 0)),
        ],
        out_specs=pl.BlockSpec((None, tq, SWA_Q_W), lambda bi, i, s: (bi, i, 0)),
        scratch_shapes=[pltpu.VMEM((BAND + tq, 512), BF16)],
    )
    return pl.pallas_call(
        functools.partial(_swa_attn_kernel, tq=tq),
        out_shape=jax.ShapeDtypeStruct((b, t, SWA_Q_W), BF16),
        grid_spec=grid_spec,
        compiler_params=pltpu.CompilerParams(
            dimension_semantics=("arbitrary", "arbitrary"), vmem_limit_bytes=VMEM_LIMIT),
        name="swa_attn",
    )(sinks.astype(F32), swa, swa, swa, swa, bias)


MERGE_ROW_GROUPS = 1
ROUTE_CLASS, ROUTE_RANK, ROUTE_WA, ROUTE_WB = range(4)
ROUTER_LANE0 = MOE_GROUPS
PAIRS_PER_GROUP = EXPERTS_PER_GROUP * (EXPERTS_PER_GROUP - 1) // 2
N_CLASSES = MOE_GROUPS * PAIRS_PER_GROUP
_PAIRS = [(a, b) for a in range(EXPERTS_PER_GROUP) for b in range(a + 1, EXPERTS_PER_GROUP)]
CLASS_EXPERT_A = np.array([g * EXPERTS_PER_GROUP + a for g in range(MOE_GROUPS) for a, _ in _PAIRS], np.int32)
CLASS_EXPERT_B = np.array([g * EXPERTS_PER_GROUP + b for g in range(MOE_GROUPS) for _, b in _PAIRS], np.int32)


def _route(logits, carry_ref, tm):
    lane = lax.broadcasted_iota(jnp.int32, (tm, LANES), 1)
    lanef = lane.astype(F32)

    def first_argmax(v):
        m = jnp.max(v, axis=-1, keepdims=True)
        return m, jnp.min(jnp.where(v == m, lanef, float(LANES)), axis=-1, keepdims=True)

    gl = jnp.where(lane < MOE_GROUPS, logits, NEG_INF)
    gmax, gidx = first_argmax(gl)
    g_w = 1.0 / jnp.sum(jnp.exp(gl - gmax), axis=-1, keepdims=True)
    e_lane = lane - ROUTER_LANE0
    lane_group = (e_lane >> 3).astype(F32)
    in_group = (e_lane >= 0) & (e_lane < N_EXPERTS) & (lane_group == gidx)
    el = jnp.where(in_group, logits, NEG_INF)
    m1, i1 = first_argmax(el)
    m2, i2 = first_argmax(jnp.where(lanef == i1, NEG_INF, el))
    tt = jnp.exp(m2 - m1)
    w1 = g_w / (1.0 + tt)
    w2 = g_w * tt / (1.0 + tt)
    first = float(ROUTER_LANE0) + float(EXPERTS_PER_GROUP) * gidx
    e1 = i1 - first
    e2 = i2 - first
    swap = e2 < e1
    ea = jnp.minimum(e1, e2)
    eb = jnp.maximum(e1, e2)
    wa = jnp.where(swap, w2, w1)
    wb = jnp.where(swap, w1, w2)
    pair = ea * (float(2 * EXPERTS_PER_GROUP - 1) - ea) * 0.5 + (eb - ea - 1.0)
    cls = float(PAIRS_PER_GROUP) * gidx + pair
    oh = jnp.where(lanef == cls, 1.0, 0.0)
    row = lax.broadcasted_iota(jnp.int32, (tm, tm), 0)
    colr = lax.broadcasted_iota(jnp.int32, (tm, tm), 1)
    ltri = jnp.where(colr < row, 1.0, 0.0).astype(BF16)
    before = jnp.dot(ltri, oh.astype(BF16), preferred_element_type=F32) + carry_ref[0:1, :]
    rank = jnp.sum(oh * before, axis=-1, keepdims=True)
    carry_ref[...] = carry_ref[...] + jnp.sum(oh, axis=0, keepdims=True)
    rec = jnp.zeros((tm, LANES), F32)
    for ln, val in ((ROUTE_CLASS, cls), (ROUTE_RANK, rank), (ROUTE_WA, wa), (ROUTE_WB, wb)):
        rec = jnp.where(lane == ln, val, rec)
    return rec


def _merge_kernel(x_ref, o0_ref, l0_ref, o1_ref, l1_ref, o2_ref, l2_ref, ys_ref, gates_ref,
                  wbd_ref, wbs_ref, wo_ref, gffn_ref, wrt_ref, brt_ref, exp_ref,
                  x1_ref, h2_ref, route_ref, cnt_ref,
                  uo1, ul1, uo2, ul2, carry_ref, *, tm):
    first = (pl.program_id(0) == 0) & (pl.program_id(1) == 0)

    @pl.when(first)
    def _():
        carry_ref[...] = jnp.zeros_like(carry_ref)

    nch = GROUP_W // LANES
    for src_o, src_l, dst_o, dst_l, d in ((o1_ref, l1_ref, uo1, ul1, 4), (o2_ref, l2_ref, uo2, ul2, 16)):
        rows = tm // d
        for r in range(d):
            dst_l[pl.ds(r, rows, stride=d), :] = src_l[r]
            for c in range(nch):
                dst_o[c, pl.ds(r, rows, stride=d), :] = src_o[r, :, c * LANES:(c + 1) * LANES].astype(F32)
    def per_head_to_lanes(w):
        hi = w.astype(BF16)
        lo = (w - hi.astype(F32)).astype(BF16)
        return jnp.dot(jnp.concatenate([hi, lo], axis=1), exp_ref[...], preferred_element_type=F32)

    def sig(v):
        return 0.5 * jnp.tanh(0.5 * v) + 0.5

    def rows_to_logits(r0, nr):
        rs = slice(r0, r0 + nr)
        la, lb, lc = l0_ref[rs, :], ul1[rs, :], ul2[rs, :]
        mx = jnp.maximum(jnp.maximum(la, lb), lc)
        ea, eb, ec = jnp.exp(la - mx), jnp.exp(lb - mx), jnp.exp(lc - mx)
        inv = 1.0 / (ea + eb + ec)
        wa, wb, wc = per_head_to_lanes(ea * inv), per_head_to_lanes(eb * inv), per_head_to_lanes(ec * inv)
        ych = []
        for c in range(nch):
            cl = slice(c * LANES, (c + 1) * LANES)
            ych.append((wa[:, cl] * o0_ref[rs, cl].astype(F32) + wb[:, cl] * uo1[c, rs, :]
                        + wc[:, cl] * uo2[c, rs, :]).astype(BF16))
        y = jnp.concatenate(ych, axis=1)
        a = jnp.dot(y, wbd_ref[...], preferred_element_type=F32)
        bsw = jnp.dot(ys_ref[rs, :], wbs_ref[...], preferred_element_type=F32)
        mixed = sig(gates_ref[rs, 0:1024]) * a.astype(BF16) + sig(gates_ref[rs, 1024:2048]) * bsw.astype(BF16)
        x1 = x_ref[rs, :] + jnp.dot(mixed, wo_ref[...], preferred_element_type=F32)
        x1_ref[rs, :] = x1
        ms = jnp.mean(x1 * x1, axis=-1, keepdims=True)
        h2 = x1 * lax.rsqrt(ms + RMS_EPS) * gffn_ref[...]
        for c in range(h2.shape[1] // LANES):
            h2_ref[pl.ds(r0 * ROW_TILE + c, nr, stride=ROW_TILE), :] = h2[:, c * LANES:(c + 1) * LANES]
        h_hi = h2.astype(BF16)
        h_lo = (h2 - h_hi.astype(F32)).astype(BF16)
        hw = jnp.dot(h_hi, wrt_ref[...], preferred_element_type=F32)
        return (hw[:, :LANES] + hw[:, LANES:]
                + jnp.dot(h_lo, wrt_ref[:, :LANES], preferred_element_type=F32) + brt_ref[...])

    nsplit = MERGE_ROW_GROUPS
    logits = jnp.concatenate([rows_to_logits(k * (tm // nsplit), tm // nsplit) for k in range(nsplit)], axis=0)
    route_ref[...] = _route(logits, carry_ref, tm)
    cnt_ref[...] = carry_ref[...]


def _merge_route(x, o0, l0, o1, l1, o2, l2, ys, gates, w_br_dil, w_br_swa, w_out, g_ffn,
                 w_group, b_group, w_router, b_router, *, tm):
    b, t, dm = x.shape
    nt = t // tm
    wrt = jnp.zeros((dm, LANES), F32).at[:, :MOE_GROUPS].set(w_group).at[
        :, ROUTER_LANE0:ROUTER_LANE0 + N_EXPERTS].set(w_router)
    brt = jnp.zeros((1, LANES), F32).at[0, :MOE_GROUPS].set(b_group).at[
        0, ROUTER_LANE0:ROUTER_LANE0 + N_EXPERTS].set(b_router)
    wrt_hi = wrt.astype(BF16)
    wrt_lo = (wrt - wrt_hi.astype(F32)).astype(BF16)
    wrt = jnp.concatenate([wrt_hi, wrt_lo], axis=1)

    def tok(width):
        return pl.BlockSpec((None, tm, width), lambda bi, i: (bi, i, 0))

    def stream(d, width):
        return pl.BlockSpec((None, d, tm // d, width), lambda bi, i: (bi, 0, i, 0))

    def const(shape):
        return pl.BlockSpec(shape, lambda bi, i: (0,) * len(shape), pipeline_mode=pl.Buffered(1))

    head_of_lane = np.arange(GROUP_W) // HEAD_DIM
    spread = (np.arange(LANES)[:, None] == head_of_lane[None, :]).astype(np.float32)
    expand = jnp.asarray(np.concatenate([spread, spread], axis=0), dtype=BF16)

    nch = GROUP_W // LANES
    return pl.pallas_call(
        functools.partial(_merge_kernel, tm=tm),
        out_shape=(jax.ShapeDtypeStruct((b, t, dm), F32), jax.ShapeDtypeStruct((b, t * ROW_TILE, LANES), F32),
                   jax.ShapeDtypeStruct((b, t, LANES), F32), jax.ShapeDtypeStruct((8, LANES), F32)),
        grid=(b, nt),
        in_specs=[tok(dm), tok(GROUP_W), tok(LANES), stream(4, GROUP_W), stream(4, LANES),
                  stream(16, GROUP_W), stream(16, LANES), tok(SWA_Q_W), tok(2048),
                  const((GROUP_W, dm)), const((SWA_Q_W, dm)), const((dm, dm)), const((1, dm)),
                  const((dm, 2 * LANES)), const((1, LANES)), const((2 * LANES, GROUP_W))],
        out_specs=(tok(dm), pl.BlockSpec((None, tm * ROW_TILE, LANES), lambda bi, i: (bi, i, 0)), tok(LANES),
                   pl.BlockSpec((8, LANES), lambda bi, i: (0, 0))),
        scratch_shapes=[pltpu.VMEM((nch, tm, LANES), F32), pltpu.VMEM((tm, LANES), F32),
                        pltpu.VMEM((nch, tm, LANES), F32), pltpu.VMEM((tm, LANES), F32),
                        pltpu.VMEM((8, LANES), F32)],
        compiler_params=pltpu.CompilerParams(
            dimension_semantics=("arbitrary", "arbitrary"), vmem_limit_bytes=VMEM_LIMIT),
        name="merge_route",
    )(x, o0, l0, o1, l1, o2, l2, ys, gates, w_br_dil.astype(BF16), w_br_swa.astype(BF16), w_out.astype(BF16),
      g_ffn.reshape(1, dm), wrt, brt, expand)


def _dispatch_kernel(pos_ref, zoff_ref, nt_ref, h_ref, xs_ref, zbuf, sem, zsem, *, td, tme, nt_max):
    step = pl.program_id(0)

    def _zero_copy(e):
        off = pl.multiple_of(zoff_ref[e] * ROW_TILE, tme * ROW_TILE)
        return pltpu.make_async_copy(zbuf, xs_ref.at[pl.ds(off, tme * ROW_TILE)], zsem)

    def _row_copy(src_row, dst_row):
        return pltpu.make_async_copy(h_ref.at[pl.ds(pl.multiple_of(src_row * ROW_TILE, ROW_TILE), ROW_TILE)],
                                     xs_ref.at[pl.ds(pl.multiple_of(dst_row * ROW_TILE, ROW_TILE), ROW_TILE)], sem)

    def _wait_rows():
        pltpu.make_async_copy(h_ref, xs_ref.at[pl.ds(0, td * ROW_TILE)], sem).wait()

    def _tail_copy(tile):
        off = pl.multiple_of(tile * (tme * ROW_TILE), tme * ROW_TILE)
        return pltpu.make_async_copy(zbuf, xs_ref.at[pl.ds(off, tme * ROW_TILE)], zsem)

    @pl.when(step == 0)
    def _():
        zbuf[...] = jnp.zeros_like(zbuf)
        for e in range(N_CLASSES):
            _zero_copy(e).start()
        for e in range(N_CLASSES):
            _zero_copy(e).wait()

        def tail(tile, carry):
            _tail_copy(tile).start()
            _tail_copy(tile).wait()
            return carry

        lax.fori_loop(nt_ref[0], nt_max, tail, 0)

    def issue(j, carry):
        _row_copy(j, pos_ref[0, 0, j]).start()
        return carry

    lax.fori_loop(0, td, issue, 0, unroll=8)
    _wait_rows()


def _dispatch(h2, pos, zoff, ntiles, nslot, *, td, tme):
    n = h2.shape[0] // ROW_TILE
    nb = n // td
    grid_spec = pltpu.PrefetchScalarGridSpec(
        num_scalar_prefetch=0,
        grid=(nb,),
        in_specs=[
            pl.BlockSpec((1, 1, td), lambda i: (i, 0, 0), memory_space=pltpu.SMEM),
            pl.BlockSpec(memory_space=pltpu.SMEM),
            pl.BlockSpec(memory_space=pltpu.SMEM),
            pl.BlockSpec((td * ROW_TILE, LANES), lambda i: (i, 0)),
        ],
        out_specs=pl.BlockSpec(memory_space=pl.ANY),
        scratch_shapes=[pltpu.VMEM((tme * ROW_TILE, LANES), F32), pltpu.SemaphoreType.DMA(()),
                        pltpu.SemaphoreType.DMA(())],
    )
    return pl.pallas_call(
        functools.partial(_dispatch_kernel, td=td, tme=tme, nt_max=nslot // tme),
        out_shape=jax.ShapeDtypeStruct((nslot * ROW_TILE, LANES), F32),
        grid_spec=grid_spec,
        compiler_params=pltpu.CompilerParams(dimension_semantics=("arbitrary",), has_side_effects=True),
        name="dispatch",
    )(pos.reshape(nb, 1, td), zoff, ntiles, h2)


PAIR_ROWS = 2 * ROW_TILE


def _expert_kernel(ta_ref, tb_ref, tblk_ref, trows_ref, nt_ref, x_ref, wga_ref, wua_ref, wda_ref,
                   wgb_ref, wub_ref, wdb_ref, y_ref, *, tme):
    i = pl.program_id(0)
    live = i < nt_ref[0]
    half_rows = tme // 2

    def run(nrows):
        xb = jnp.concatenate([x_ref[pl.ds(c, nrows, stride=ROW_TILE), :] for c in range(ROW_TILE)],
                             axis=1).astype(BF16)
        nw = 2 * LANES
        for half, (wg_ref, wu_ref, wd_ref) in enumerate(((wga_ref, wua_ref, wda_ref), (wgb_ref, wub_ref, wdb_ref))):
            act = []
            for c0 in range(0, D_EXPERT, nw):
                g = jnp.dot(xb, wg_ref[:, c0:c0 + nw], preferred_element_type=F32)
                u = jnp.dot(xb, wu_ref[:, c0:c0 + nw], preferred_element_type=F32)
                act.append((g * jax.nn.sigmoid(g) * u).astype(BF16))
            a = jnp.concatenate(act, axis=1)
            for c0 in range(0, ROW_TILE * LANES, nw):
                y = jnp.dot(a, wd_ref[:, c0:c0 + nw], preferred_element_type=F32)
                for k in range(nw // LANES):
                    c = c0 // LANES + k
                    y_ref[pl.ds(half * ROW_TILE + c, nrows, stride=PAIR_ROWS), :] = y[:, k * LANES:(k + 1) * LANES]

    @pl.when(live & (trows_ref[i] > half_rows))
    def _():
        run(tme)

    @pl.when(live & (trows_ref[i] <= half_rows))
    def _():
        run(half_rows)
        y_ref[half_rows * PAIR_ROWS:, :] = jnp.zeros((half_rows * PAIR_ROWS, LANES), F32)

    @pl.when(jnp.logical_not(live))
    def _():
        y_ref[...] = jnp.zeros_like(y_ref)


def _experts(xs, tile_a, tile_b, tile_block, tile_rows, ntiles, w_e_gate, w_e_up, w_e_down, *, tme):
    nslot = xs.shape[0] // ROW_TILE
    dm = ROW_TILE * LANES
    nt = nslot // tme
    wg, wu, wd = w_e_gate.astype(BF16), w_e_up.astype(BF16), w_e_down.astype(BF16)

    def w_in_spec(which):
        return pl.BlockSpec((None, dm, D_EXPERT), lambda i, ta, tb, tk, tr, n: ((ta, tb)[which][i], 0, 0))

    def w_out_spec(which):
        return pl.BlockSpec((None, D_EXPERT, dm), lambda i, ta, tb, tk, tr, n: ((ta, tb)[which][i], 0, 0))

    grid_spec = pltpu.PrefetchScalarGridSpec(
        num_scalar_prefetch=5,
        grid=(nt,),
        in_specs=[
            pl.BlockSpec((tme * ROW_TILE, LANES), lambda i, ta, tb, tk, tr, n: (tk[i], 0)),
            w_in_spec(0), w_in_spec(0), w_out_spec(0), w_in_spec(1), w_in_spec(1), w_out_spec(1),
        ],
        out_specs=pl.BlockSpec((tme * PAIR_ROWS, LANES), lambda i, ta, tb, tk, tr, n: (i, 0)),
    )
    return pl.pallas_call(
        functools.partial(_expert_kernel, tme=tme),
        out_shape=jax.ShapeDtypeStruct((nslot * PAIR_ROWS, LANES), F32),
        grid_spec=grid_spec,
        compiler_params=pltpu.CompilerParams(dimension_semantics=("arbitrary",), vmem_limit_bytes=VMEM_LIMIT),
        name="experts",
    )(tile_a, tile_b, tile_block, tile_rows, ntiles, xs, wg, wu, wd, wg, wu, wd)


def _combine_kernel(pos_ref, posn_ref, ys_ref, x1_ref, route_ref, gfin_ref, out_ref, ybuf, sem, *, tc):
    i = pl.program_id(0)
    nb = pl.num_programs(0)
    slot = i % 2

    def gather(p_ref, s):
        def issue(j, carry):
            src = pl.multiple_of(p_ref[0, 0, j] * PAIR_ROWS, PAIR_ROWS)
            dst = pl.multiple_of(j * PAIR_ROWS, PAIR_ROWS)
            pltpu.make_async_copy(ys_ref.at[pl.ds(src, PAIR_ROWS)], ybuf.at[s, pl.ds(dst, PAIR_ROWS)],
                                  sem.at[s]).start()
            return carry

        lax.fori_loop(0, tc, issue, 0, unroll=8)

    @pl.when(i == 0)
    def _():
        gather(pos_ref, 0)

    @pl.when(i + 1 < nb)
    def _():
        gather(posn_ref, 1 - slot)

    pltpu.make_async_copy(ys_ref.at[pl.ds(0, tc * PAIR_ROWS)], ybuf.at[slot], sem.at[slot]).wait()
    rec = route_ref[...]
    wa = rec[:, ROUTE_WA:ROUTE_WA + 1]
    wb = rec[:, ROUTE_WB:ROUTE_WB + 1]
    ya = jnp.concatenate([ybuf[slot, pl.ds(c, tc, stride=PAIR_ROWS), :] for c in range(ROW_TILE)], axis=1)
    yb = jnp.concatenate([ybuf[slot, pl.ds(ROW_TILE + c, tc, stride=PAIR_ROWS), :] for c in range(ROW_TILE)],
                         axis=1)
    z = x1_ref[...] + wa * ya + wb * yb
    ms = jnp.mean(z * z, axis=-1, keepdims=True)
    out_ref[...] = z * lax.rsqrt(ms + RMS_EPS) * gfin_ref[...]


def _combine(ys, pos, x1, route, g_final, *, tc):
    n, dm = x1.shape
    nb = n // tc
    grid_spec = pltpu.PrefetchScalarGridSpec(
        num_scalar_prefetch=0,
        grid=(nb,),
        in_specs=[
            pl.BlockSpec((1, 1, tc), lambda i: (i, 0, 0), memory_space=pltpu.SMEM),
            pl.BlockSpec((1, 1, tc), lambda i: (jnp.minimum(i + 1, nb - 1), 0, 0), memory_space=pltpu.SMEM),
            pl.BlockSpec(memory_space=pl.ANY),
            pl.BlockSpec((tc, dm), lambda i: (i, 0)),
            pl.BlockSpec((tc, LANES), lambda i: (i, 0)),
            pl.BlockSpec((1, dm), lambda i: (0, 0)),
        ],
        out_specs=pl.BlockSpec((tc, dm), lambda i: (i, 0)),
        scratch_shapes=[pltpu.VMEM((2, tc * PAIR_ROWS, LANES), F32), pltpu.SemaphoreType.DMA((2,))],
    )
    return pl.pallas_call(
        functools.partial(_combine_kernel, tc=tc),
        out_shape=jax.ShapeDtypeStruct((n, dm), F32),
        grid_spec=grid_spec,
        compiler_params=pltpu.CompilerParams(dimension_semantics=("arbitrary",), vmem_limit_bytes=VMEM_LIMIT),
        name="combine",
    )(pos.reshape(nb, 1, tc), pos.reshape(nb, 1, tc), ys, x1, route, g_final.reshape(1, dm))


def _slot_layout(route, counts, n, tme):
    cnt = counts[0, :N_CLASSES].astype(jnp.int32)
    tiles_per = (cnt + tme - 1) // tme
    tile_end = jnp.cumsum(tiles_per)
    seg_base = (tile_end - tiles_per) * tme
    ntiles = tile_end[-1]
    nt_max = n // tme + N_CLASSES
    tid = jnp.arange(nt_max, dtype=jnp.int32)
    live = jnp.minimum(tid, ntiles - 1)
    tile_class = jnp.sum((tile_end[None, :] <= live[:, None]).astype(jnp.int32), axis=1)
    tile_a = jnp.take(jnp.asarray(CLASS_EXPERT_A), tile_class)
    tile_b = jnp.take(jnp.asarray(CLASS_EXPERT_B), tile_class)
    first_tile = jnp.take(tile_end - tiles_per, tile_class)
    tile_rows = jnp.clip(jnp.take(cnt, tile_class) - (live - first_tile) * tme, 0, tme)
    cls = route[:, ROUTE_CLASS].astype(jnp.int32)
    rank = route[:, ROUTE_RANK].astype(jnp.int32)
    base = jnp.sum(jnp.where(cls[:, None] == jnp.arange(N_CLASSES, dtype=jnp.int32)[None, :], seg_base[None, :], 0),
                   axis=1)
    pos = base + rank
    zoff = jnp.maximum(tile_end - 1, 0) * tme
    return (pos, zoff.astype(jnp.int32), tile_a.astype(jnp.int32), tile_b.astype(jnp.int32),
            live.astype(jnp.int32), tile_rows.astype(jnp.int32), ntiles.reshape(1).astype(jnp.int32))


def kernel(x, g_mix, w_in, sinks, w_br_dil, w_br_swa, w_out, g_ffn, w_group, b_group, w_router, b_router,
           w_e_gate, w_e_up, w_e_down, g_final):
    b, t, dm = x.shape
    n = b * t
    tme = 256
    qkv0, swa, gates, qkv4, qkv16 = _inproj(x, g_mix[0], w_in[0], tm=512)
    o0, l0 = _dil_attention(qkv0.reshape(b, 1, t, QKV_W), 0, tq=1024)
    o1, l1 = _dil_attention(qkv4, 1, tq=1024)
    o2, l2 = _dil_attention(qkv16, 2, tq=512)
    ys = _swa_attention(swa, sinks[0], tq=1024)
    x1, h2, route, counts = _merge_route(
        x, o0.reshape(b, t, GROUP_W), l0.reshape(b, t, LANES), o1, l1, o2, l2, ys, gates,
        w_br_dil[0], w_br_swa[0], w_out[0], g_ffn[0], w_group[0], b_group[0], w_router[0], b_router[0], tm=512)
    route = route.reshape(n, LANES)
    pos, zoff, tile_a, tile_b, tile_block, tile_rows, ntiles = _slot_layout(route, counts, n, tme)
    nslot = n + N_CLASSES * tme
    xs = _dispatch(h2.reshape(n * ROW_TILE, LANES), pos, zoff, ntiles, nslot, td=2048, tme=tme)
    yslots = _experts(xs, tile_a, tile_b, tile_block, tile_rows, ntiles, w_e_gate[0], w_e_up[0], w_e_down[0],
                      tme=tme)
    out = _combine(yslots, pos, x1.reshape(n, dm), route, g_final, tc=512)
    return out.reshape(b, t, dm)
```

```python
import functools

import numpy as np
import jax
import jax.numpy as jnp
from jax import lax
from jax.experimental import pallas as pl
from jax.experimental.pallas import tpu as pltpu

F32 = jnp.float32
BF16 = jnp.bfloat16

HEAD_DIM = 64
BAND = 128
DIL_PATTERNS = ((128, 1), (512, 4), (2048, 16))
DIL_HEADS_PER_GROUP = 8
DIL_HEADS = 24
GROUP_W = DIL_HEADS_PER_GROUP * HEAD_DIM
QKV_W = 3 * GROUP_W
SWA_WINDOW = 128
SWA_Q_HEADS = 16
SWA_KV_HEADS = 2
SWA_Q_W = SWA_Q_HEADS * HEAD_DIM
SWA_W = SWA_Q_W + 4 * 128
MOE_GROUPS = 4
EXPERTS_PER_GROUP = 8
N_EXPERTS = 32
D_EXPERT = 512
RMS_EPS = 1e-6
LANES = 128
NEG_INF = float("-inf")
LOG2E = 1.4426950408889634
LN2 = 0.6931471805599453
ROW_TILE = 8

VMEM_LIMIT = 56 * 1024 * 1024


def _alibi_slopes(n):
    return (2.0 ** (-8.0 * np.arange(1, n + 1) / n)).astype(np.float32)


def _band_bias(slopes, max_back, unit):
    a = np.arange(BAND)[:, None]
    c = np.arange(2 * BAND)[None, :]
    delta = a + BAND - c
    band = (delta >= 0) & (delta <= max_back)
    pen = (-slopes[:, None, None] * (delta * unit)[None] * LOG2E).astype(np.float32)
    full = np.where(band[None], pen, -np.inf).astype(np.float32)
    first = np.where((band & (c >= BAND))[None], pen, -np.inf).astype(np.float32)
    return np.stack([full, first], axis=0)


def _inproj_kernel(x_ref, g_ref, w_ref, qkv0_ref, swa_ref, gates_ref, qkv4_ref, qkv16_ref, hs_ref, *, tm):
    x = x_ref[...]
    ms = jnp.mean(x * x, axis=-1, keepdims=True)
    h = x * lax.rsqrt(ms + RMS_EPS) * g_ref[...]
    nchunk = h.shape[1] // LANES
    for c in range(nchunk):
        hs_ref[c] = h[:, c * LANES:(c + 1) * LANES]
    hb = h.astype(BF16)

    def proj(lhs, c0, width):
        return jnp.dot(lhs, w_ref[:, c0:c0 + width], preferred_element_type=F32)

    for j in range(3):
        qkv0_ref[:, j * GROUP_W:(j + 1) * GROUP_W] = proj(hb, COL_QKV_D[j], GROUP_W).astype(BF16)
    for c0 in range(0, SWA_Q_W, 512):
        swa_ref[:, c0:c0 + 512] = proj(hb, COL_Q_S + c0, 512).astype(BF16)
    kv = proj(hb, COL_KV_S, 4 * HEAD_DIM)
    part = [kv[:, j * HEAD_DIM:(j + 1) * HEAD_DIM] for j in range(4)]
    swa_ref[:, SWA_Q_W:SWA_W] = jnp.concatenate(
        [part[0], part[0], part[1], part[1], part[2], part[2], part[3], part[3]], axis=1).astype(BF16)
    for c0 in range(0, 2048, 512):
        gates_ref[:, c0:c0 + 512] = proj(hb, COL_GATES + c0, 512).astype(BF16)
    for gi, out_ref, d in ((1, qkv4_ref, 4), (2, qkv16_ref, 16)):
        rows = tm // d
        hp = jnp.concatenate(
            [jnp.concatenate([hs_ref[c, pl.ds(r, rows, stride=d), :] for c in range(nchunk)], axis=1)
             for r in range(d)], axis=0).astype(BF16)
        for j in range(3):
            res = proj(hp, COL_QKV_D[j] + gi * GROUP_W, GROUP_W).astype(BF16)
            for r in range(d):
                out_ref[r, :, j * GROUP_W:(j + 1) * GROUP_W] = res[r * rows:(r + 1) * rows]


COL_QKV_D = (0, 1536, 3072)
COL_Q_S = 4608
COL_KV_S = 5632
COL_GATES = 5888
IN_WIDTH = 7936
Q_SCALE = HEAD_DIM ** -0.5 * LOG2E


def _prep_w_in(w_in):
    col = np.arange(IN_WIDTH)
    is_q = (col < COL_QKV_D[1]) | ((col >= COL_Q_S) & (col < COL_KV_S))
    colscale = np.where(is_q, Q_SCALE, 1.0).astype(np.float32)
    return (w_in * colscale[None, :]).astype(BF16)


def _inproj(x, g_mix, w_in, *, tm):
    b, t, dm = x.shape
    wp = _prep_w_in(w_in)
    nw = wp.shape[1]
    nt = t // tm
    out_shape = (
        jax.ShapeDtypeStruct((b, t, QKV_W), BF16),
        jax.ShapeDtypeStruct((b, t, SWA_W), BF16),
        jax.ShapeDtypeStruct((b, t, 2048), BF16),
        jax.ShapeDtypeStruct((b, 4, t // 4, QKV_W), BF16),
        jax.ShapeDtypeStruct((b, 16, t // 16, QKV_W), BF16),
    )
    return pl.pallas_call(
        functools.partial(_inproj_kernel, tm=tm),
        out_shape=out_shape,
        grid=(b, nt),
        in_specs=[
            pl.BlockSpec((None, tm, dm), lambda bi, i: (bi, i, 0)),
            pl.BlockSpec((1, dm), lambda bi, i: (0, 0)),
            pl.BlockSpec((dm, nw), lambda bi, i: (0, 0), pipeline_mode=pl.Buffered(1)),
        ],
        out_specs=(
            pl.BlockSpec((None, tm, QKV_W), lambda bi, i: (bi, i, 0)),
            pl.BlockSpec((None, tm, SWA_W), lambda bi, i: (bi, i, 0)),
            pl.BlockSpec((None, tm, 2048), lambda bi, i: (bi, i, 0)),
            pl.BlockSpec((None, 4, tm // 4, QKV_W), lambda bi, i: (bi, 0, i, 0)),
            pl.BlockSpec((None, 16, tm // 16, QKV_W), lambda bi, i: (bi, 0, i, 0)),
        ),
        scratch_shapes=[pltpu.VMEM((dm // LANES, tm, LANES), F32)],
        compiler_params=pltpu.CompilerParams(
            dimension_semantics=("arbitrary", "arbitrary"), vmem_limit_bytes=VMEM_LIMIT),
        name="inproj",
    )(x, g_mix.reshape(1, dm), wp)


def _dil_attn_kernel(q_ref, kc_ref, vc_ref, kp_ref, vp_ref, bias_ref, o_ref, lse_ref, kbuf, vbuf, *, tq):
    i = pl.program_id(2)
    kbuf[0:BAND] = kp_ref[...]
    kbuf[BAND:BAND + tq] = kc_ref[...]
    vbuf[0:BAND] = vp_ref[...]
    vbuf[BAND:BAND + tq] = vc_ref[...]
    lane = lax.broadcasted_iota(jnp.int32, (BAND, LANES), 1)
    lo = lane < HEAD_DIM
    zero = jnp.zeros((BAND, LANES), BF16)
    for qb in range(tq // BAND):
        sel = jnp.where(i == 0, 1, 0) if qb == 0 else 0
        rq = slice(qb * BAND, (qb + 1) * BAND)
        rk = slice(qb * BAND, qb * BAND + 2 * BAND)
        lse_parts = []
        for pr in range(GROUP_W // LANES):
            cl = slice(pr * LANES, (pr + 1) * LANES)
            q2 = q_ref[rq, cl]
            kk = kbuf[rk, cl]
            vv = vbuf[rk, cl]
            s_all = lax.dot_general(jnp.concatenate([jnp.where(lo, q2, zero), jnp.where(lo, zero, q2)], axis=0), kk,
                                    (((1,), (1,)), ((), ())), preferred_element_type=F32)
            probs, stats = [], []
            for hh in range(2):
                s = s_all[hh * BAND:(hh + 1) * BAND] + bias_ref[sel, 2 * pr + hh]
                m = jnp.max(s, axis=-1, keepdims=True)
                p = jnp.exp2(s - m)
                l = jnp.sum(p, axis=-1, keepdims=True)
                probs.append(p.astype(BF16))
                stats.append((1.0 / l, (m + jnp.log2(l)) * LN2))
            o_all = jnp.dot(jnp.concatenate(probs, axis=0), vv, preferred_element_type=F32)
            res = [(o_all[hh * BAND:(hh + 1) * BAND] * stats[hh][0], stats[hh][1]) for hh in range(2)]
            o_ref[rq, cl] = jnp.where(lo, res[0][0], res[1][0]).astype(o_ref.dtype)
            lse_parts.append(jnp.where(lane == 2 * pr, res[0][1], jnp.where(lane == 2 * pr + 1, res[1][1], 0.0)))
        lse_ref[rq, :] = (lse_parts[0] + lse_parts[1]) + (lse_parts[2] + lse_parts[3])


def _dil_attention(qkv, gi, *, tq):
    b, d, l, _ = qkv.shape
    window, dil = DIL_PATTERNS[gi]
    assert dil == d
    tq = min(tq, l)
    nq = tq // BAND
    slopes = _alibi_slopes(DIL_HEADS)[gi * 8:(gi + 1) * 8]
    bias = jnp.asarray(_band_bias(slopes, window // dil, dil))

    def cur(c):
        return pl.BlockSpec((None, None, tq, GROUP_W), lambda bi, r, i: (bi, r, i, c))

    def prev(c):
        return pl.BlockSpec((None, None, BAND, GROUP_W), lambda bi, r, i: (bi, r, jnp.maximum(i * nq - 1, 0), c))

    return pl.pallas_call(
        functools.partial(_dil_attn_kernel, tq=tq),
        out_shape=(jax.ShapeDtypeStruct((b, d, l, GROUP_W), BF16),
                   jax.ShapeDtypeStruct((b, d, l, LANES), F32)),
        grid=(b, d, l // tq),
        in_specs=[cur(0), cur(1), cur(2), prev(1), prev(2),
                  pl.BlockSpec((2, 8, BAND, 2 * BAND), lambda bi, r, i: (0, 0, 0, 0))],
        out_specs=(cur(0), pl.BlockSpec((None, None, tq, LANES), lambda bi, r, i: (bi, r, i, 0))),
        scratch_shapes=[pltpu.VMEM((BAND + tq, GROUP_W), BF16), pltpu.VMEM((BAND + tq, GROUP_W), BF16)],
        compiler_params=pltpu.CompilerParams(
            dimension_semantics=("arbitrary", "arbitrary", "arbitrary"), vmem_limit_bytes=VMEM_LIMIT),
        name=f"dil_attn_{gi}",
    )(qkv, qkv, qkv, qkv, qkv, bias)


def _swa_attn_kernel(sink_ref, qlo_ref, qhi_ref, kvc_ref, kvp_ref, bias_ref, o_ref, kvbuf, *, tq):
    i = pl.program_id(1)
    kvbuf[0:BAND] = kvp_ref[...]
    kvbuf[BAND:BAND + tq] = kvc_ref[...]
    lo = lax.broadcasted_iota(jnp.int32, (BAND, LANES), 1) < HEAD_DIM
    zero = jnp.zeros((BAND, LANES), BF16)
    for qb in range(tq // BAND):
        sel = jnp.where(i == 0, 1, 0) if qb == 0 else 0
        rq = slice(qb * BAND, (qb + 1) * BAND)
        rk = slice(qb * BAND, qb * BAND + 2 * BAND)
        for g in range(SWA_KV_HEADS):
            q_ref = qlo_ref if g == 0 else qhi_ref
            kk = kvbuf[rk, g * LANES:(g + 1) * LANES]
            vv = kvbuf[rk, (2 + g) * LANES:(3 + g) * LANES]
            masked = []
            for pp in range(4):
                q2 = q_ref[rq, pp * LANES:(pp + 1) * LANES]
                masked += [jnp.where(lo, q2, zero), jnp.where(lo, zero, q2)]
            s_all = lax.dot_general(jnp.concatenate(masked, axis=0), kk, (((1,), (1,)), ((), ())),
                                    preferred_element_type=F32)
            probs, inv = [], []
            for j in range(8):
                h = 8 * g + j
                sink = sink_ref[h] * LOG2E
                s = s_all[j * BAND:(j + 1) * BAND] + bias_ref[sel, h]
                m = jnp.maximum(jnp.max(s, axis=-1, keepdims=True), sink)
                p = jnp.exp2(s - m)
                inv.append(1.0 / (jnp.sum(p, axis=-1, keepdims=True) + jnp.exp2(sink - m)))
                probs.append(p.astype(BF16))
            o_all = jnp.dot(jnp.concatenate(probs, axis=0), vv, preferred_element_type=F32)
            for pp in range(4):
                oa = o_all[(2 * pp) * BAND:(2 * pp + 1) * BAND] * inv[2 * pp]
                ob = o_all[(2 * pp + 1) * BAND:(2 * pp + 2) * BAND] * inv[2 * pp + 1]
                o_ref[rq, (4 * g + pp) * LANES:(4 * g + pp + 1) * LANES] = jnp.where(lo, oa, ob).astype(o_ref.dtype)


def _swa_attention(swa, sinks, *, tq):
    b, t, _ = swa.shape
    tq = min(tq, t)
    nq = tq // BAND
    bias = jnp.asarray(_band_bias(_alibi_slopes(SWA_Q_HEADS), SWA_WINDOW - 1, 1))
    grid_spec = pltpu.PrefetchScalarGridSpec(
        num_scalar_prefetch=1,
        grid=(b, t // tq),
        in_specs=[
            pl.BlockSpec((None, tq, 512), lambda bi, i, s: (bi, i, 0)),
            pl.BlockSpec((None, tq, 512), lambda bi, i, s: (bi, i, 1)),
            pl.BlockSpec((None, tq, 512), lambda bi, i, s: (bi, i, 2)),
            pl.BlockSpec((None, BAND, 512), lambda bi, i, s: (bi, jnp.maximum(i * nq - 1, 0), 2)),
            pl.BlockSpec((2, SWA_Q_HEADS, BAND, 2 * BAND), lambda bi, i, s: (0, 0, 0, 0)),
        ],
        out_specs=pl.BlockSpec((None, tq, SWA_Q_W), lambda bi, i, s: (bi, i, 0)),
        scratch_shapes=[pltpu.VMEM((BAND + tq, 512), BF16)],
    )
    return pl.pallas_call(
        functools.partial(_swa_attn_kernel, tq=tq),
        out_shape=jax.ShapeDtypeStruct((b, t, SWA_Q_W), BF16),
        grid_spec=grid_spec,
        compiler_params=pltpu.CompilerParams(
            dimension_semantics=("arbitrary", "arbitrary"), vmem_limit_bytes=VMEM_LIMIT),
        name="swa_attn",
    )(sinks.astype(F32), swa, swa, swa, swa, bias)


MERGE_ROW_GROUPS = 1
ROUTE_CLASS, ROUTE_RANK, ROUTE_WA, ROUTE_WB = range(4)
ROUTER_LANE0 = MOE_GROUPS
PAIRS_PER_GROUP = EXPERTS_PER_GROUP * (EXPERTS_PER_GROUP - 1) // 2
N_CLASSES = MOE_GROUPS * PAIRS_PER_GROUP
_PAIRS = [(a, b) for a in range(EXPERTS_PER_GROUP) for b in range(a + 1, EXPERTS_PER_GROUP)]
CLASS_EXPERT_A = np.array([g * EXPERTS_PER_GROUP + a for g in range(MOE_GROUPS) for a, _ in _PAIRS], np.int32)
CLASS_EXPERT_B = np.array([g * EXPERTS_PER_GROUP + b for g in range(MOE_GROUPS) for _, b in _PAIRS], np.int32)


def _route(logits, carry_ref, tm):
    lane = lax.broadcasted_iota(jnp.int32, (tm, LANES), 1)
    lanef = lane.astype(F32)

    def first_argmax(v):
        m = jnp.max(v, axis=-1, keepdims=True)
        return m, jnp.min(jnp.where(v == m, lanef, float(LANES)), axis=-1, keepdims=True)

    gl = jnp.where(lane < MOE_GROUPS, logits, NEG_INF)
    gmax, gidx = first_argmax(gl)
    g_w = 1.0 / jnp.sum(jnp.exp(gl - gmax), axis=-1, keepdims=True)
    e_lane = lane - ROUTER_LANE0
    lane_group = (e_lane >> 3).astype(F32)
    in_group = (e_lane >= 0) & (e_lane < N_EXPERTS) & (lane_group == gidx)
    el = jnp.where(in_group, logits, NEG_INF)
    m1, i1 = first_argmax(el)
    m2, i2 = first_argmax(jnp.where(lanef == i1, NEG_INF, el))
    tt = jnp.exp(m2 - m1)
    w1 = g_w / (1.0 + tt)
    w2 = g_w * tt / (1.0 + tt)
    first = float(ROUTER_LANE0) + float(EXPERTS_PER_GROUP) * gidx
    e1 = i1 - first
    e2 = i2 - first
    swap = e2 < e1
    ea = jnp.minimum(e1, e2)
    eb = jnp.maximum(e1, e2)
    wa = jnp.where(swap, w2, w1)
    wb = jnp.where(swap, w1, w2)
    pair = ea * (float(2 * EXPERTS_PER_GROUP - 1) - ea) * 0.5 + (eb - ea - 1.0)
    cls = float(PAIRS_PER_GROUP) * gidx + pair
    oh = jnp.where(lanef == cls, 1.0, 0.0)
    row = lax.broadcasted_iota(jnp.int32, (tm, tm), 0)
    colr = lax.broadcasted_iota(jnp.int32, (tm, tm), 1)
    ltri = jnp.where(colr < row, 1.0, 0.0).astype(BF16)
    before = jnp.dot(ltri, oh.astype(BF16), preferred_element_type=F32) + carry_ref[0:1, :]
    rank = jnp.sum(oh * before, axis=-1, keepdims=True)
    carry_ref[...] = carry_ref[...] + jnp.sum(oh, axis=0, keepdims=True)
    rec = jnp.zeros((tm, LANES), F32)
    for ln, val in ((ROUTE_CLASS, cls), (ROUTE_RANK, rank), (ROUTE_WA, wa), (ROUTE_WB, wb)):
        rec = jnp.where(lane == ln, val, rec)
    return rec


def _merge_kernel(x_ref, o0_ref, l0_ref, o1_ref, l1_ref, o2_ref, l2_ref, ys_ref, gates_ref,
                  wbd_ref, wbs_ref, wo_ref, gffn_ref, wrt_ref, brt_ref, exp_ref,
                  x1_ref, h2_ref, route_ref, cnt_ref,
                  uo1, ul1, uo2, ul2, carry_ref, *, tm):
    first = (pl.program_id(0) == 0) & (pl.program_id(1) == 0)

    @pl.when(first)
    def _():
        carry_ref[...] = jnp.zeros_like(carry_ref)

    nch = GROUP_W // LANES
    for src_o, src_l, dst_o, dst_l, d in ((o1_ref, l1_ref, uo1, ul1, 4), (o2_ref, l2_ref, uo2, ul2, 16)):
        rows = tm // d
        for r in range(d):
            dst_l[pl.ds(r, rows, stride=d), :] = src_l[r]
            for c in range(nch):
                dst_o[c, pl.ds(r, rows, stride=d), :] = src_o[r, :, c * LANES:(c + 1) * LANES].astype(F32)
    def per_head_to_lanes(w):
        hi = w.astype(BF16)
        lo = (w - hi.astype(F32)).astype(BF16)
        return jnp.dot(jnp.concatenate([hi, lo], axis=1), exp_ref[...], preferred_element_type=F32)

    def sig(v):
        return 0.5 * jnp.tanh(0.5 * v) + 0.5

    def rows_to_logits(r0, nr):
        rs = slice(r0, r0 + nr)
        la, lb, lc = l0_ref[rs, :], ul1[rs, :], ul2[rs, :]
        mx = jnp.maximum(jnp.maximum(la, lb), lc)
        ea, eb, ec = jnp.exp(la - mx), jnp.exp(lb - mx), jnp.exp(lc - mx)
        inv = 1.0 / (ea + eb + ec)
        wa, wb, wc = per_head_to_lanes(ea * inv), per_head_to_lanes(eb * inv), per_head_to_lanes(ec * inv)
        ych = []
        for c in range(nch):
            cl = slice(c * LANES, (c + 1) * LANES)
            ych.append((wa[:, cl] * o0_ref[rs, cl].astype(F32) + wb[:, cl] * uo1[c, rs, :]
                        + wc[:, cl] * uo2[c, rs, :]).astype(BF16))
        y = jnp.concatenate(ych, axis=1)
        a = jnp.dot(y, wbd_ref[...], preferred_element_type=F32)
        bsw = jnp.dot(ys_ref[rs, :], wbs_ref[...], preferred_element_type=F32)
        mixed = sig(gates_ref[rs, 0:1024]) * a.astype(BF16) + sig(gates_ref[rs, 1024:2048]) * bsw.astype(BF16)
        x1 = x_ref[rs, :] + jnp.dot(mixed, wo_ref[...], preferred_element_type=F32)
        x1_ref[rs, :] = x1
        ms = jnp.mean(x1 * x1, axis=-1, keepdims=True)
        h2 = x1 * lax.rsqrt(ms + RMS_EPS) * gffn_ref[...]
        for c in range(h2.shape[1] // LANES):
            h2_ref[pl.ds(r0 * ROW_TILE + c, nr, stride=ROW_TILE), :] = h2[:, c * LANES:(c + 1) * LANES]
        h_hi = h2.astype(BF16)
        h_lo = (h2 - h_hi.astype(F32)).astype(BF16)
        hw = jnp.dot(h_hi, wrt_ref[...], preferred_element_type=F32)
        return (hw[:, :LANES] + hw[:, LANES:]
                + jnp.dot(h_lo, wrt_ref[:, :LANES], preferred_element_type=F32) + brt_ref[...])

    nsplit = MERGE_ROW_GROUPS
    logits = jnp.concatenate([rows_to_logits(k * (tm // nsplit), tm // nsplit) for k in range(nsplit)], axis=0)
    route_ref[...] = _route(logits, carry_ref, tm)
    cnt_ref[...] = carry_ref[...]


def _merge_route(x, o0, l0, o1, l1, o2, l2, ys, gates, w_br_dil, w_br_swa, w_out, g_ffn,
                 w_group, b_group, w_router, b_router, *, tm):
    b, t, dm = x.shape
    nt = t // tm
    wrt = jnp.zeros((dm, LANES), F32).at[:, :MOE_GROUPS].set(w_group).at[
        :, ROUTER_LANE0:ROUTER_LANE0 + N_EXPERTS].set(w_router)
    brt = jnp.zeros((1, LANES), F32).at[0, :MOE_GROUPS].set(b_group).at[
        0, ROUTER_LANE0:ROUTER_LANE0 + N_EXPERTS].set(b_router)
    wrt_hi = wrt.astype(BF16)
    wrt_lo = (wrt - wrt_hi.astype(F32)).astype(BF16)
    wrt = jnp.concatenate([wrt_hi, wrt_lo], axis=1)

    def tok(width):
        return pl.BlockSpec((None, tm, width), lambda bi, i: (bi, i, 0))

    def stream(d, width):
        return pl.BlockSpec((None, d, tm // d, width), lambda bi, i: (bi, 0, i, 0))

    def const(shape):
        return pl.BlockSpec(shape, lambda bi, i: (0,) * len(shape), pipeline_mode=pl.Buffered(1))

    head_of_lane = np.arange(GROUP_W) // HEAD_DIM
    spread = (np.arange(LANES)[:, None] == head_of_lane[None, :]).astype(np.float32)
    expand = jnp.asarray(np.concatenate([spread, spread], axis=0), dtype=BF16)

    nch = GROUP_W // LANES
    return pl.pallas_call(
        functools.partial(_merge_kernel, tm=tm),
        out_shape=(jax.ShapeDtypeStruct((b, t, dm), F32), jax.ShapeDtypeStruct((b, t * ROW_TILE, LANES), F32),
                   jax.ShapeDtypeStruct((b, t, LANES), F32), jax.ShapeDtypeStruct((8, LANES), F32)),
        grid=(b, nt),
        in_specs=[tok(dm), tok(GROUP_W), tok(LANES), stream(4, GROUP_W), stream(4, LANES),
                  stream(16, GROUP_W), stream(16, LANES), tok(SWA_Q_W), tok(2048),
                  const((GROUP_W, dm)), const((SWA_Q_W, dm)), const((dm, dm)), const((1, dm)),
                  const((dm, 2 * LANES)), const((1, LANES)), const((2 * LANES, GROUP_W))],
        out_specs=(tok(dm), pl.BlockSpec((None, tm * ROW_TILE, LANES), lambda bi, i: (bi, i, 0)), tok(LANES),
                   pl.BlockSpec((8, LANES), lambda bi, i: (0, 0))),
        scratch_shapes=[pltpu.VMEM((nch, tm, LANES), F32), pltpu.VMEM((tm, LANES), F32),
                        pltpu.VMEM((nch, tm, LANES), F32), pltpu.VMEM((tm, LANES), F32),
                        pltpu.VMEM((8, LANES), F32)],
        compiler_params=pltpu.CompilerParams(
            dimension_semantics=("arbitrary", "arbitrary"), vmem_limit_bytes=VMEM_LIMIT),
        name="merge_route",
    )(x, o0, l0, o1, l1, o2, l2, ys, gates, w_br_dil.astype(BF16), w_br_swa.astype(BF16), w_out.astype(BF16),
      g_ffn.reshape(1, dm), wrt, brt, expand)


def _dispatch_kernel(pos_ref, zoff_ref, nt_ref, h_ref, xs_ref, zbuf, sem, zsem, *, td, tme, nt_max):
    step = pl.program_id(0)

    def _zero_copy(e):
        off = pl.multiple_of(zoff_ref[e] * ROW_TILE, tme * ROW_TILE)
        return pltpu.make_async_copy(zbuf, xs_ref.at[pl.ds(off, tme * ROW_TILE)], zsem)

    def _row_copy(src_row, dst_row):
        return pltpu.make_async_copy(h_ref.at[pl.ds(pl.multiple_of(src_row * ROW_TILE, ROW_TILE), ROW_TILE)],
                                     xs_ref.at[pl.ds(pl.multiple_of(dst_row * ROW_TILE, ROW_TILE), ROW_TILE)], sem)

    def _wait_rows():
        pltpu.make_async_copy(h_ref, xs_ref.at[pl.ds(0, td * ROW_TILE)], sem).wait()

    def _tail_copy(tile):
        off = pl.multiple_of(tile * (tme * ROW_TILE), tme * ROW_TILE)
        return pltpu.make_async_copy(zbuf, xs_ref.at[pl.ds(off, tme * ROW_TILE)], zsem)

    @pl.when(step == 0)
    def _():
        zbuf[...] = jnp.zeros_like(zbuf)
        for e in range(N_CLASSES):
            _zero_copy(e).start()
        for e in range(N_CLASSES):
            _zero_copy(e).wait()

        def tail(tile, carry):
            _tail_copy(tile).start()
            _tail_copy(tile).wait()
            return carry

        lax.fori_loop(nt_ref[0], nt_max, tail, 0)

    def issue(j, carry):
        _row_copy(j, pos_ref[0, 0, j]).start()
        return carry

    lax.fori_loop(0, td, issue, 0, unroll=8)
    _wait_rows()


def _dispatch(h2, pos, zoff, ntiles, nslot, *, td, tme):
    n = h2.shape[0] // ROW_TILE
    nb = n // td
    grid_spec = pltpu.PrefetchScalarGridSpec(
        num_scalar_prefetch=0,
        grid=(nb,),
        in_specs=[
            pl.BlockSpec((1, 1, td), lambda i: (i, 0, 0), memory_space=pltpu.SMEM),
            pl.BlockSpec(memory_space=pltpu.SMEM),
            pl.BlockSpec(memory_space=pltpu.SMEM),
            pl.BlockSpec((td * ROW_TILE, LANES), lambda i: (i, 0)),
        ],
        out_specs=pl.BlockSpec(memory_space=pl.ANY),
        scratch_shapes=[pltpu.VMEM((tme * ROW_TILE, LANES), F32), pltpu.SemaphoreType.DMA(()),
                        pltpu.SemaphoreType.DMA(())],
    )
    return pl.pallas_call(
        functools.partial(_dispatch_kernel, td=td, tme=tme, nt_max=nslot // tme),
        out_shape=jax.ShapeDtypeStruct((nslot * ROW_TILE, LANES), F32),
        grid_spec=grid_spec,
        compiler_params=pltpu.CompilerParams(dimension_semantics=("arbitrary",), has_side_effects=True),
        name="dispatch",
    )(pos.reshape(nb, 1, td), zoff, ntiles, h2)


PAIR_ROWS = 2 * ROW_TILE


PLAN_FIRST, PLAN_SLOT, PLAN_NEXT, PLAN_HAS_NEXT = range(4)


def _expert_kernel(ta_ref, tb_ref, tblk_ref, trows_ref, plan_ref, nt_ref, x_ref, wg_hbm, wu_hbm, wd_hbm, y_ref,
                   wg_buf, wu_buf, wd_buf, wsem, *, tme):
    i = pl.program_id(0)
    live = i < nt_ref[0]
    half_rows = tme // 2

    def weight_copies(which, expert, slot):
        return [pltpu.make_async_copy(src.at[expert], buf.at[which, slot], wsem.at[which, slot])
                for src, buf in ((wg_hbm, wg_buf), (wu_hbm, wu_buf), (wd_hbm, wd_buf))]

    slots = []
    for which, t_ref in ((0, ta_ref), (1, tb_ref)):
        base = (i * 2 + which) * 4
        first = plan_ref[base + PLAN_FIRST] == 1
        slot = plan_ref[base + PLAN_SLOT]
        slots.append(slot)

        @pl.when(first & (i == 0))
        def _():
            for cp in weight_copies(which, t_ref[i], slot):
                cp.start()

        @pl.when(first)
        def _():
            for cp in weight_copies(which, t_ref[i], slot):
                cp.wait()

        @pl.when(first & (plan_ref[base + PLAN_HAS_NEXT] == 1))
        def _():
            for cp in weight_copies(which, plan_ref[base + PLAN_NEXT], 1 - slot):
                cp.start()

    def run(nrows):
        xb = jnp.concatenate([x_ref[pl.ds(c, nrows, stride=ROW_TILE), :] for c in range(ROW_TILE)],
                             axis=1).astype(BF16)
        nw = 2 * LANES
        for half in range(2):
            wg_ref, wu_ref, wd_ref = (buf.at[half, slots[half]] for buf in (wg_buf, wu_buf, wd_buf))
            act = []
            for c0 in range(0, D_EXPERT, nw):
                g = jnp.dot(xb, wg_ref[:, c0:c0 + nw], preferred_element_type=F32)
                u = jnp.dot(xb, wu_ref[:, c0:c0 + nw], preferred_element_type=F32)
                act.append((g * jax.nn.sigmoid(g) * u).astype(BF16))
            a = jnp.concatenate(act, axis=1)
            for c0 in range(0, ROW_TILE * LANES, nw):
                y = jnp.dot(a, wd_ref[:, c0:c0 + nw], preferred_element_type=F32)
                for k in range(nw // LANES):
                    c = c0 // LANES + k
                    y_ref[pl.ds(half * ROW_TILE + c, nrows, stride=PAIR_ROWS), :] = y[:, k * LANES:(k + 1) * LANES]

    @pl.when(live & (trows_ref[i] > half_rows))
    def _():
        run(tme)

    @pl.when(live & (trows_ref[i] <= half_rows))
    def _():
        run(half_rows)
        y_ref[half_rows * PAIR_ROWS:, :] = jnp.zeros((half_rows * PAIR_ROWS, LANES), F32)

    @pl.when(jnp.logical_not(live))
    def _():
        y_ref[...] = jnp.zeros_like(y_ref)


def _weight_plan(tile_expert):
    nt = tile_expert.shape[0]
    first = jnp.concatenate([jnp.ones((1,), jnp.int32), (tile_expert[1:] != tile_expert[:-1]).astype(jnp.int32)])
    run_id = jnp.cumsum(first) - 1
    next_start = jnp.sum((run_id[None, :] <= run_id[:, None]).astype(jnp.int32), axis=1)
    has_next = (run_id < run_id[-1]).astype(jnp.int32)
    nxt = jnp.take(tile_expert, jnp.minimum(next_start, nt - 1))
    return jnp.stack([first, run_id % 2, nxt, has_next], axis=1).astype(jnp.int32)


def _experts(xs, tile_a, tile_b, tile_block, tile_rows, ntiles, w_e_gate, w_e_up, w_e_down, *, tme):
    nslot = xs.shape[0] // ROW_TILE
    dm = ROW_TILE * LANES
    nt = nslot // tme
    wg, wu, wd = w_e_gate.astype(BF16), w_e_up.astype(BF16), w_e_down.astype(BF16)
    plan = jnp.concatenate([_weight_plan(tile_a), _weight_plan(tile_b)], axis=1).reshape(-1)

    grid_spec = pltpu.PrefetchScalarGridSpec(
        num_scalar_prefetch=6,
        grid=(nt,),
        in_specs=[
            pl.BlockSpec((tme * ROW_TILE, LANES), lambda i, ta, tb, tk, tr, pn, n: (tk[i], 0)),
            pl.BlockSpec(memory_space=pl.ANY), pl.BlockSpec(memory_space=pl.ANY), pl.BlockSpec(memory_space=pl.ANY),
        ],
        out_specs=pl.BlockSpec((tme * PAIR_ROWS, LANES), lambda i, ta, tb, tk, tr, pn, n: (i, 0)),
        scratch_shapes=[pltpu.VMEM((2, 2, dm, D_EXPERT), BF16), pltpu.VMEM((2, 2, dm, D_EXPERT), BF16),
                        pltpu.VMEM((2, 2, D_EXPERT, dm), BF16), pltpu.SemaphoreType.DMA((2, 2))],
    )
    return pl.pallas_call(
        functools.partial(_expert_kernel, tme=tme),
        out_shape=jax.ShapeDtypeStruct((nslot * PAIR_ROWS, LANES), F32),
        grid_spec=grid_spec,
        compiler_params=pltpu.CompilerParams(dimension_semantics=("arbitrary",), vmem_limit_bytes=VMEM_LIMIT),
        name="experts",
    )(tile_a, tile_b, tile_block, tile_rows, plan, ntiles, xs, wg, wu, wd)


def _combine_kernel(pos_ref, posn_ref, ys_ref, x1_ref, route_ref, gfin_ref, out_ref, ybuf, sem, *, tc):
    i = pl.program_id(0)
    nb = pl.num_programs(0)
    slot = i % 2

    def gather(p_ref, s):
        def issue(j, carry):
            src = pl.multiple_of(p_ref[0, 0, j] * PAIR_ROWS, PAIR_ROWS)
            dst = pl.multiple_of(j * PAIR_ROWS, PAIR_ROWS)
            pltpu.make_async_copy(ys_ref.at[pl.ds(src, PAIR_ROWS)], ybuf.at[s, pl.ds(dst, PAIR_ROWS)],
                                  sem.at[s]).start()
            return carry

        lax.fori_loop(0, tc, issue, 0, unroll=8)

    @pl.when(i == 0)
    def _():
        gather(pos_ref, 0)

    @pl.when(i + 1 < nb)
    def _():
        gather(posn_ref, 1 - slot)

    pltpu.make_async_copy(ys_ref.at[pl.ds(0, tc * PAIR_ROWS)], ybuf.at[slot], sem.at[slot]).wait()
    rec = route_ref[...]
    wa = rec[:, ROUTE_WA:ROUTE_WA + 1]
    wb = rec[:, ROUTE_WB:ROUTE_WB + 1]
    ya = jnp.concatenate([ybuf[slot, pl.ds(c, tc, stride=PAIR_ROWS), :] for c in range(ROW_TILE)], axis=1)
    yb = jnp.concatenate([ybuf[slot, pl.ds(ROW_TILE + c, tc, stride=PAIR_ROWS), :] for c in range(ROW_TILE)],
                         axis=1)
    z = x1_ref[...] + wa * ya + wb * yb
    ms = jnp.mean(z * z, axis=-1, keepdims=True)
    out_ref[...] = z * lax.rsqrt(ms + RMS_EPS) * gfin_ref[...]


def _combine(ys, pos, x1, route, g_final, *, tc):
    n, dm = x1.shape
    nb = n // tc
    grid_spec = pltpu.PrefetchScalarGridSpec(
        num_scalar_prefetch=0,
        grid=(nb,),
        in_specs=[
            pl.BlockSpec((1, 1, tc), lambda i: (i, 0, 0), memory_space=pltpu.SMEM),
            pl.BlockSpec((1, 1, tc), lambda i: (jnp.minimum(i + 1, nb - 1), 0, 0), memory_space=pltpu.SMEM),
            pl.BlockSpec(memory_space=pl.ANY),
            pl.BlockSpec((tc, dm), lambda i: (i, 0)),
            pl.BlockSpec((tc, LANES), lambda i: (i, 0)),
            pl.BlockSpec((1, dm), lambda i: (0, 0)),
        ],
        out_specs=pl.BlockSpec((tc, dm), lambda i: (i, 0)),
        scratch_shapes=[pltpu.VMEM((2, tc * PAIR_ROWS, LANES), F32), pltpu.SemaphoreType.DMA((2,))],
    )
    return pl.pallas_call(
        functools.partial(_combine_kernel, tc=tc),
        out_shape=jax.ShapeDtypeStruct((n, dm), F32),
        grid_spec=grid_spec,
        compiler_params=pltpu.CompilerParams(dimension_semantics=("arbitrary",), vmem_limit_bytes=VMEM_LIMIT),
        name="combine",
    )(pos.reshape(nb, 1, tc), pos.reshape(nb, 1, tc), ys, x1, route, g_final.reshape(1, dm))


def _slot_layout(route, counts, n, tme):
    cnt = counts[0, :N_CLASSES].astype(jnp.int32)
    tiles_per = (cnt + tme - 1) // tme
    tile_end = jnp.cumsum(tiles_per)
    seg_base = (tile_end - tiles_per) * tme
    ntiles = tile_end[-1]
    nt_max = n // tme + N_CLASSES
    tid = jnp.arange(nt_max, dtype=jnp.int32)
    live = jnp.minimum(tid, ntiles - 1)
    tile_class = jnp.sum((tile_end[None, :] <= live[:, None]).astype(jnp.int32), axis=1)
    tile_a = jnp.take(jnp.asarray(CLASS_EXPERT_A), tile_class)
    tile_b = jnp.take(jnp.asarray(CLASS_EXPERT_B), tile_class)
    first_tile = jnp.take(tile_end - tiles_per, tile_class)
    tile_rows = jnp.clip(jnp.take(cnt, tile_class) - (live - first_tile) * tme, 0, tme)
    cls = route[:, ROUTE_CLASS].astype(jnp.int32)
    rank = route[:, ROUTE_RANK].astype(jnp.int32)
    base = jnp.sum(jnp.where(cls[:, None] == jnp.arange(N_CLASSES, dtype=jnp.int32)[None, :], seg_base[None, :], 0),
                   axis=1)
    pos = base + rank
    zoff = jnp.maximum(tile_end - 1, 0) * tme
    return (pos, zoff.astype(jnp.int32), tile_a.astype(jnp.int32), tile_b.astype(jnp.int32),
            live.astype(jnp.int32), tile_rows.astype(jnp.int32), ntiles.reshape(1).astype(jnp.int32))


def kernel(x, g_mix, w_in, sinks, w_br_dil, w_br_swa, w_out, g_ffn, w_group, b_group, w_router, b_router,
           w_e_gate, w_e_up, w_e_down, g_final):
    b, t, dm = x.shape
    n = b * t
    tme = 256
    qkv0, swa, gates, qkv4, qkv16 = _inproj(x, g_mix[0], w_in[0], tm=512)
    o0, l0 = _dil_attention(qkv0.reshape(b, 1, t, QKV_W), 0, tq=1024)
    o1, l1 = _dil_attention(qkv4, 1, tq=1024)
    o2, l2 = _dil_attention(qkv16, 2, tq=512)
    ys = _swa_attention(swa, sinks[0], tq=1024)
    x1, h2, route, counts = _merge_route(
        x, o0.reshape(b, t, GROUP_W), l0.reshape(b, t, LANES), o1, l1, o2, l2, ys, gates,
        w_br_dil[0], w_br_swa[0], w_out[0], g_ffn[0], w_group[0], b_group[0], w_router[0], b_router[0], tm=512)
    route = route.reshape(n, LANES)
    pos, zoff, tile_a, tile_b, tile_block, tile_rows, ntiles = _slot_layout(route, counts, n, tme)
    nslot = n + N_CLASSES * tme
    xs = _dispatch(h2.reshape(n * ROW_TILE, LANES), pos, zoff, ntiles, nslot, td=2048, tme=tme)
    yslots = _experts(xs, tile_a, tile_b, tile_block, tile_rows, ntiles, w_e_gate[0], w_e_up[0], w_e_down[0],
                      tme=tme)
    out = _combine(yslots, pos, x1.reshape(n, dm), route, g_final, tc=512)
    return out.reshape(b, t, dm)
```

```python
import functools

import numpy as np
import jax
import jax.numpy as jnp
from jax import lax
from jax.experimental import pallas as pl
from jax.experimental.pallas import tpu as pltpu

F32 = jnp.float32
BF16 = jnp.bfloat16

HEAD_DIM = 64
BAND = 128
DIL_PATTERNS = ((128, 1), (512, 4), (2048, 16))
DIL_HEADS_PER_GROUP = 8
DIL_HEADS = 24
GROUP_W = DIL_HEADS_PER_GROUP * HEAD_DIM
QKV_W = 3 * GROUP_W
SWA_WINDOW = 128
SWA_Q_HEADS = 16
SWA_KV_HEADS = 2
SWA_Q_W = SWA_Q_HEADS * HEAD_DIM
SWA_W = SWA_Q_W + 4 * 128
MOE_GROUPS = 4
EXPERTS_PER_GROUP = 8
N_EXPERTS = 32
D_EXPERT = 512
RMS_EPS = 1e-6
LANES = 128
NEG_INF = float("-inf")
LOG2E = 1.4426950408889634
LN2 = 0.6931471805599453
ROW_TILE = 8

VMEM_LIMIT = 56 * 1024 * 1024


def _alibi_slopes(n):
    return (2.0 ** (-8.0 * np.arange(1, n + 1) / n)).astype(np.float32)


def _band_bias(slopes, max_back, unit):
    a = np.arange(BAND)[:, None]
    c = np.arange(2 * BAND)[None, :]
    delta = a + BAND - c
    band = (delta >= 0) & (delta <= max_back)
    pen = (-slopes[:, None, None] * (delta * unit)[None] * LOG2E).astype(np.float32)
    full = np.where(band[None], pen, -np.inf).astype(np.float32)
    first = np.where((band & (c >= BAND))[None], pen, -np.inf).astype(np.float32)
    return np.stack([full, first], axis=0)


def _inproj_kernel(x_ref, g_ref, w_ref, qkv0_ref, swa_ref, gates_ref, qkv4_ref, qkv16_ref, hs_ref, *, tm):
    x = x_ref[...]
    ms = jnp.mean(x * x, axis=-1, keepdims=True)
    h = x * lax.rsqrt(ms + RMS_EPS) * g_ref[...]
    nchunk = h.shape[1] // LANES
    for c in range(nchunk):
        hs_ref[c] = h[:, c * LANES:(c + 1) * LANES]
    hb = h.astype(BF16)

    def proj(lhs, c0, width):
        return jnp.dot(lhs, w_ref[:, c0:c0 + width], preferred_element_type=F32)

    for j in range(3):
        qkv0_ref[:, j * GROUP_W:(j + 1) * GROUP_W] = proj(hb, COL_QKV_D[j], GROUP_W).astype(BF16)
    for c0 in range(0, SWA_Q_W, 512):
        swa_ref[:, c0:c0 + 512] = proj(hb, COL_Q_S + c0, 512).astype(BF16)
    kv = proj(hb, COL_KV_S, 4 * HEAD_DIM)
    part = [kv[:, j * HEAD_DIM:(j + 1) * HEAD_DIM] for j in range(4)]
    swa_ref[:, SWA_Q_W:SWA_W] = jnp.concatenate(
        [part[0], part[0], part[1], part[1], part[2], part[2], part[3], part[3]], axis=1).astype(BF16)
    for c0 in range(0, 2048, 512):
        gates_ref[:, c0:c0 + 512] = proj(hb, COL_GATES + c0, 512).astype(BF16)
    for gi, out_ref, d in ((1, qkv4_ref, 4), (2, qkv16_ref, 16)):
        rows = tm // d
        hp = jnp.concatenate(
            [jnp.concatenate([hs_ref[c, pl.ds(r, rows, stride=d), :] for c in range(nchunk)], axis=1)
             for r in range(d)], axis=0).astype(BF16)
        for j in range(3):
            res = proj(hp, COL_QKV_D[j] + gi * GROUP_W, GROUP_W).astype(BF16)
            for r in range(d):
                out_ref[r, :, j * GROUP_W:(j + 1) * GROUP_W] = res[r * rows:(r + 1) * rows]


COL_QKV_D = (0, 1536, 3072)
COL_Q_S = 4608
COL_KV_S = 5632
COL_GATES = 5888
IN_WIDTH = 7936
Q_SCALE = HEAD_DIM ** -0.5 * LOG2E


def _prep_w_in(w_in):
    col = np.arange(IN_WIDTH)
    is_q = (col < COL_QKV_D[1]) | ((col >= COL_Q_S) & (col < COL_KV_S))
    colscale = np.where(is_q, Q_SCALE, 1.0).astype(np.float32)
    return (w_in * colscale[None, :]).astype(BF16)


def _inproj(x, g_mix, w_in, *, tm):
    b, t, dm = x.shape
    wp = _prep_w_in(w_in)
    nw = wp.shape[1]
    nt = t // tm
    out_shape = (
        jax.ShapeDtypeStruct((b, t, QKV_W), BF16),
        jax.ShapeDtypeStruct((b, t, SWA_W), BF16),
        jax.ShapeDtypeStruct((b, t, 2048), BF16),
        jax.ShapeDtypeStruct((b, 4, t // 4, QKV_W), BF16),
        jax.ShapeDtypeStruct((b, 16, t // 16, QKV_W), BF16),
    )
    return pl.pallas_call(
        functools.partial(_inproj_kernel, tm=tm),
        out_shape=out_shape,
        grid=(b, nt),
        in_specs=[
            pl.BlockSpec((None, tm, dm), lambda bi, i: (bi, i, 0)),
            pl.BlockSpec((1, dm), lambda bi, i: (0, 0)),
            pl.BlockSpec((dm, nw), lambda bi, i: (0, 0), pipeline_mode=pl.Buffered(1)),
        ],
        out_specs=(
            pl.BlockSpec((None, tm, QKV_W), lambda bi, i: (bi, i, 0)),
            pl.BlockSpec((None, tm, SWA_W), lambda bi, i: (bi, i, 0)),
            pl.BlockSpec((None, tm, 2048), lambda bi, i: (bi, i, 0)),
            pl.BlockSpec((None, 4, tm // 4, QKV_W), lambda bi, i: (bi, 0, i, 0)),
            pl.BlockSpec((None, 16, tm // 16, QKV_W), lambda bi, i: (bi, 0, i, 0)),
        ),
        scratch_shapes=[pltpu.VMEM((dm // LANES, tm, LANES), F32)],
        compiler_params=pltpu.CompilerParams(
            dimension_semantics=("arbitrary", "arbitrary"), vmem_limit_bytes=VMEM_LIMIT),
        name="inproj",
    )(x, g_mix.reshape(1, dm), wp)


def _dil_attn_kernel(q_ref, kc_ref, vc_ref, kp_ref, vp_ref, bias_ref, o_ref, lse_ref, kbuf, vbuf, *, tq):
    i = pl.program_id(2)
    kbuf[0:BAND] = kp_ref[...]
    kbuf[BAND:BAND + tq] = kc_ref[...]
    vbuf[0:BAND] = vp_ref[...]
    vbuf[BAND:BAND + tq] = vc_ref[...]
    lane = lax.broadcasted_iota(jnp.int32, (BAND, LANES), 1)
    lo = lane < HEAD_DIM
    zero = jnp.zeros((BAND, LANES), BF16)
    for qb in range(tq // BAND):
        sel = jnp.where(i == 0, 1, 0) if qb == 0 else 0
        rq = slice(qb * BAND, (qb + 1) * BAND)
        rk = slice(qb * BAND, qb * BAND + 2 * BAND)
        lse_parts = []
        for pr in range(GROUP_W // LANES):
            cl = slice(pr * LANES, (pr + 1) * LANES)
            q2 = q_ref[rq, cl]
            kk = kbuf[rk, cl]
            vv = vbuf[rk, cl]
            s_all = lax.dot_general(jnp.concatenate([jnp.where(lo, q2, zero), jnp.where(lo, zero, q2)], axis=0), kk,
                                    (((1,), (1,)), ((), ())), preferred_element_type=F32)
            probs, stats = [], []
            for hh in range(2):
                s = s_all[hh * BAND:(hh + 1) * BAND] + bias_ref[sel, 2 * pr + hh]
                m = jnp.max(s, axis=-1, keepdims=True)
                p = jnp.exp2(s - m)
                l = jnp.sum(p, axis=-1, keepdims=True)
                probs.append(p.astype(BF16))
                stats.append((1.0 / l, (m + jnp.log2(l)) * LN2))
            o_all = jnp.dot(jnp.concatenate(probs, axis=0), vv, preferred_element_type=F32)
            res = [(o_all[hh * BAND:(hh + 1) * BAND] * stats[hh][0], stats[hh][1]) for hh in range(2)]
            o_ref[rq, cl] = jnp.where(lo, res[0][0], res[1][0]).astype(o_ref.dtype)
            lse_parts.append(jnp.where(lane == 2 * pr, res[0][1], jnp.where(lane == 2 * pr + 1, res[1][1], 0.0)))
        lse_ref[rq, :] = (lse_parts[0] + lse_parts[1]) + (lse_parts[2] + lse_parts[3])


def _dil_attention(qkv, gi, *, tq):
    b, d, l, _ = qkv.shape
    window, dil = DIL_PATTERNS[gi]
    assert dil == d
    tq = min(tq, l)
    nq = tq // BAND
    slopes = _alibi_slopes(DIL_HEADS)[gi * 8:(gi + 1) * 8]
    bias = jnp.asarray(_band_bias(slopes, window // dil, dil))

    def cur(c):
        return pl.BlockSpec((None, None, tq, GROUP_W), lambda bi, r, i: (bi, r, i, c))

    def prev(c):
        return pl.BlockSpec((None, None, BAND, GROUP_W), lambda bi, r, i: (bi, r, jnp.maximum(i * nq - 1, 0), c))

    return pl.pallas_call(
        functools.partial(_dil_attn_kernel, tq=tq),
        out_shape=(jax.ShapeDtypeStruct((b, d, l, GROUP_W), BF16),
                   jax.ShapeDtypeStruct((b, d, l, LANES), F32)),
        grid=(b, d, l // tq),
        in_specs=[cur(0), cur(1), cur(2), prev(1), prev(2),
                  pl.BlockSpec((2, 8, BAND, 2 * BAND), lambda bi, r, i: (0, 0, 0, 0))],
        out_specs=(cur(0), pl.BlockSpec((None, None, tq, LANES), lambda bi, r, i: (bi, r, i, 0))),
        scratch_shapes=[pltpu.VMEM((BAND + tq, GROUP_W), BF16), pltpu.VMEM((BAND + tq, GROUP_W), BF16)],
        compiler_params=pltpu.CompilerParams(
            dimension_semantics=("arbitrary", "arbitrary", "arbitrary"), vmem_limit_bytes=VMEM_LIMIT),
        name=f"dil_attn_{gi}",
    )(qkv, qkv, qkv, qkv, qkv, bias)


def _swa_attn_kernel(sink_ref, qlo_ref, qhi_ref, kvc_ref, kvp_ref, bias_ref, o_ref, kvbuf, *, tq):
    i = pl.program_id(1)
    kvbuf[0:BAND] = kvp_ref[...]
    kvbuf[BAND:BAND + tq] = kvc_ref[...]
    lo = lax.broadcasted_iota(jnp.int32, (BAND, LANES), 1) < HEAD_DIM
    zero = jnp.zeros((BAND, LANES), BF16)
    for qb in range(tq // BAND):
        sel = jnp.where(i == 0, 1, 0) if qb == 0 else 0
        rq = slice(qb * BAND, (qb + 1) * BAND)
        rk = slice(qb * BAND, qb * BAND + 2 * BAND)
        for g in range(SWA_KV_HEADS):
            q_ref = qlo_ref if g == 0 else qhi_ref
            kk = kvbuf[rk, g * LANES:(g + 1) * LANES]
            vv = kvbuf[rk, (2 + g) * LANES:(3 + g) * LANES]
            masked = []
            for pp in range(4):
                q2 = q_ref[rq, pp * LANES:(pp + 1) * LANES]
                masked += [jnp.where(lo, q2, zero), jnp.where(lo, zero, q2)]
            s_all = lax.dot_general(jnp.concatenate(masked, axis=0), kk, (((1,), (1,)), ((), ())),
                                    preferred_element_type=F32)
            probs, inv = [], []
            for j in range(8):
                h = 8 * g + j
                sink = sink_ref[h] * LOG2E
                s = s_all[j * BAND:(j + 1) * BAND] + bias_ref[sel, h]
                m = jnp.maximum(jnp.max(s, axis=-1, keepdims=True), sink)
                p = jnp.exp2(s - m)
                inv.append(1.0 / (jnp.sum(p, axis=-1, keepdims=True) + jnp.exp2(sink - m)))
                probs.append(p.astype(BF16))
            o_all = jnp.dot(jnp.concatenate(probs, axis=0), vv, preferred_element_type=F32)
            for pp in range(4):
                oa = o_all[(2 * pp) * BAND:(2 * pp + 1) * BAND] * inv[2 * pp]
                ob = o_all[(2 * pp + 1) * BAND:(2 * pp + 2) * BAND] * inv[2 * pp + 1]
                o_ref[rq, (4 * g + pp) * LANES:(4 * g + pp + 1) * LANES] = jnp.where(lo, oa, ob).astype(o_ref.dtype)


def _swa_attention(swa, sinks, *, tq):
    b, t, _ = swa.shape
    tq = min(tq, t)
    nq = tq // BAND
    bias = jnp.asarray(_band_bias(_alibi_slopes(SWA_Q_HEADS), SWA_WINDOW - 1, 1))
    grid_spec = pltpu.PrefetchScalarGridSpec(
        num_scalar_prefetch=1,
        grid=(b, t // tq),
        in_specs=[
            pl.BlockSpec((None, tq, 512), lambda bi, i, s: (bi, i, 0)),
            pl.BlockSpec((None, tq, 512), lambda bi, i, s: (bi, i, 1)),
            pl.BlockSpec((None, tq, 512), lambda bi, i, s: (bi, i, 2)),
            pl.BlockSpec((None, BAND, 512), lambda bi, i, s: (bi, jnp.maximum(i * nq - 1, 0), 2)),
            pl.BlockSpec((2, SWA_Q_HEADS, BAND, 2 * BAND), lambda bi, i, s: (0, 0, 0, 0)),
        ],
        out_specs=pl.BlockSpec((None, tq, SWA_Q_W), lambda bi, i, s: (bi, i, 0)),
        scratch_shapes=[pltpu.VMEM((BAND + tq, 512), BF16)],
    )
    return pl.pallas_call(
        functools.partial(_swa_attn_kernel, tq=tq),
        out_shape=jax.ShapeDtypeStruct((b, t, SWA_Q_W), BF16),
        grid_spec=grid_spec,
        compiler_params=pltpu.CompilerParams(
            dimension_semantics=("arbitrary", "arbitrary"), vmem_limit_bytes=VMEM_LIMIT),
        name="swa_attn",
    )(sinks.astype(F32), swa, swa, swa, swa, bias)


MERGE_ROW_GROUPS = 1
ROUTE_CLASS, ROUTE_RANK, ROUTE_WA, ROUTE_WB = range(4)
ROUTER_LANE0 = MOE_GROUPS
PAIRS_PER_GROUP = EXPERTS_PER_GROUP * (EXPERTS_PER_GROUP - 1) // 2
N_CLASSES = MOE_GROUPS * PAIRS_PER_GROUP
_PAIRS = [(a, b) for a in range(EXPERTS_PER_GROUP) for b in range(a + 1, EXPERTS_PER_GROUP)]
CLASS_EXPERT_A = np.array([g * EXPERTS_PER_GROUP + a for g in range(MOE_GROUPS) for a, _ in _PAIRS], np.int32)
CLASS_EXPERT_B = np.array([g * EXPERTS_PER_GROUP + b for g in range(MOE_GROUPS) for _, b in _PAIRS], np.int32)


def _route(logits, carry_ref, ltri_ref, tm):
    lane = lax.broadcasted_iota(jnp.int32, (tm, LANES), 1)
    lanef = lane.astype(F32)

    def first_argmax(v):
        m = jnp.max(v, axis=-1, keepdims=True)
        return m, jnp.min(jnp.where(v == m, lanef, float(LANES)), axis=-1, keepdims=True)

    gl = jnp.where(lane < MOE_GROUPS, logits, NEG_INF)
    gmax, gidx = first_argmax(gl)
    g_w = 1.0 / jnp.sum(jnp.exp(gl - gmax), axis=-1, keepdims=True)
    e_lane = lane - ROUTER_LANE0
    lane_group = (e_lane >> 3).astype(F32)
    in_group = (e_lane >= 0) & (e_lane < N_EXPERTS) & (lane_group == gidx)
    el = jnp.where(in_group, logits, NEG_INF)
    m1, i1 = first_argmax(el)
    m2, i2 = first_argmax(jnp.where(lanef == i1, NEG_INF, el))
    tt = jnp.exp(m2 - m1)
    w1 = g_w / (1.0 + tt)
    w2 = g_w * tt / (1.0 + tt)
    first = float(ROUTER_LANE0) + float(EXPERTS_PER_GROUP) * gidx
    e1 = i1 - first
    e2 = i2 - first
    swap = e2 < e1
    ea = jnp.minimum(e1, e2)
    eb = jnp.maximum(e1, e2)
    wa = jnp.where(swap, w2, w1)
    wb = jnp.where(swap, w1, w2)
    pair = ea * (float(2 * EXPERTS_PER_GROUP - 1) - ea) * 0.5 + (eb - ea - 1.0)
    cls = float(PAIRS_PER_GROUP) * gidx + pair
    oh = jnp.where(lanef == cls, 1.0, 0.0)
    before = jnp.dot(ltri_ref[...], oh.astype(BF16), preferred_element_type=F32) + carry_ref[0:1, :]
    rank = jnp.sum(oh * before, axis=-1, keepdims=True)
    carry_ref[...] = carry_ref[...] + jnp.sum(oh, axis=0, keepdims=True)
    rec = jnp.zeros((tm, LANES), F32)
    for ln, val in ((ROUTE_CLASS, cls), (ROUTE_RANK, rank), (ROUTE_WA, wa), (ROUTE_WB, wb)):
        rec = jnp.where(lane == ln, val, rec)
    return rec


def _merge_kernel(x_ref, o0_ref, l0_ref, o1_ref, l1_ref, o2_ref, l2_ref, ys_ref, gates_ref,
                  wbd_ref, wbs_ref, wo_ref, gffn_ref, wrt_ref, brt_ref, exp_ref, ltri_ref,
                  x1_ref, h2_ref, route_ref, cnt_ref,
                  uo1, ul1, uo2, ul2, carry_ref, *, tm):
    first = (pl.program_id(0) == 0) & (pl.program_id(1) == 0)

    @pl.when(first)
    def _():
        carry_ref[...] = jnp.zeros_like(carry_ref)

    nch = GROUP_W // LANES
    for src_o, src_l, dst_o, dst_l, d in ((o1_ref, l1_ref, uo1, ul1, 4), (o2_ref, l2_ref, uo2, ul2, 16)):
        rows = tm // d
        for r in range(d):
            dst_l[pl.ds(r, rows, stride=d), :] = src_l[r]
            for c in range(nch):
                dst_o[c, pl.ds(r, rows, stride=d), :] = src_o[r, :, c * LANES:(c + 1) * LANES].astype(F32)
    def per_head_to_lanes(w):
        hi = w.astype(BF16)
        lo = (w - hi.astype(F32)).astype(BF16)
        return jnp.dot(jnp.concatenate([hi, lo], axis=1), exp_ref[...], preferred_element_type=F32)

    def sig(v):
        return 0.5 * jnp.tanh(0.5 * v) + 0.5

    def rows_to_logits(r0, nr):
        rs = slice(r0, r0 + nr)
        la, lb, lc = l0_ref[rs, :], ul1[rs, :], ul2[rs, :]
        mx = jnp.maximum(jnp.maximum(la, lb), lc)
        ea, eb, ec = jnp.exp(la - mx), jnp.exp(lb - mx), jnp.exp(lc - mx)
        inv = 1.0 / (ea + eb + ec)
        wa, wb, wc = per_head_to_lanes(ea * inv), per_head_to_lanes(eb * inv), per_head_to_lanes(ec * inv)
        ych = []
        for c in range(nch):
            cl = slice(c * LANES, (c + 1) * LANES)
            ych.append((wa[:, cl] * o0_ref[rs, cl].astype(F32) + wb[:, cl] * uo1[c, rs, :]
                        + wc[:, cl] * uo2[c, rs, :]).astype(BF16))
        y = jnp.concatenate(ych, axis=1)
        a = jnp.dot(y, wbd_ref[...], preferred_element_type=F32)
        bsw = jnp.dot(ys_ref[rs, :], wbs_ref[...], preferred_element_type=F32)
        mixed = sig(gates_ref[rs, 0:1024]) * a.astype(BF16) + sig(gates_ref[rs, 1024:2048]) * bsw.astype(BF16)
        x1 = x_ref[rs, :] + jnp.dot(mixed, wo_ref[...], preferred_element_type=F32)
        x1_ref[rs, :] = x1
        ms = jnp.mean(x1 * x1, axis=-1, keepdims=True)
        h2 = x1 * lax.rsqrt(ms + RMS_EPS) * gffn_ref[...]
        for c in range(h2.shape[1] // LANES):
            h2_ref[pl.ds(r0 * ROW_TILE + c, nr, stride=ROW_TILE), :] = h2[:, c * LANES:(c + 1) * LANES]
        h_hi = h2.astype(BF16)
        h_lo = (h2 - h_hi.astype(F32)).astype(BF16)
        hw = jnp.dot(h_hi, wrt_ref[...], preferred_element_type=F32)
        return (hw[:, :LANES] + hw[:, LANES:]
                + jnp.dot(h_lo, wrt_ref[:, :LANES], preferred_element_type=F32) + brt_ref[...])

    nsplit = MERGE_ROW_GROUPS
    logits = jnp.concatenate([rows_to_logits(k * (tm // nsplit), tm // nsplit) for k in range(nsplit)], axis=0)
    route_ref[...] = _route(logits, carry_ref, ltri_ref, tm)
    cnt_ref[...] = carry_ref[...]


def _merge_route(x, o0, l0, o1, l1, o2, l2, ys, gates, w_br_dil, w_br_swa, w_out, g_ffn,
                 w_group, b_group, w_router, b_router, *, tm):
    b, t, dm = x.shape
    nt = t // tm
    wrt = jnp.zeros((dm, LANES), F32).at[:, :MOE_GROUPS].set(w_group).at[
        :, ROUTER_LANE0:ROUTER_LANE0 + N_EXPERTS].set(w_router)
    brt = jnp.zeros((1, LANES), F32).at[0, :MOE_GROUPS].set(b_group).at[
        0, ROUTER_LANE0:ROUTER_LANE0 + N_EXPERTS].set(b_router)
    wrt_hi = wrt.astype(BF16)
    wrt_lo = (wrt - wrt_hi.astype(F32)).astype(BF16)
    wrt = jnp.concatenate([wrt_hi, wrt_lo], axis=1)

    def tok(width):
        return pl.BlockSpec((None, tm, width), lambda bi, i: (bi, i, 0))

    def stream(d, width):
        return pl.BlockSpec((None, d, tm // d, width), lambda bi, i: (bi, 0, i, 0))

    def const(shape):
        return pl.BlockSpec(shape, lambda bi, i: (0,) * len(shape), pipeline_mode=pl.Buffered(1))

    head_of_lane = np.arange(GROUP_W) // HEAD_DIM
    spread = (np.arange(LANES)[:, None] == head_of_lane[None, :]).astype(np.float32)
    expand = jnp.asarray(np.concatenate([spread, spread], axis=0), dtype=BF16)

    nch = GROUP_W // LANES
    return pl.pallas_call(
        functools.partial(_merge_kernel, tm=tm),
        out_shape=(jax.ShapeDtypeStruct((b, t, dm), F32), jax.ShapeDtypeStruct((b, t * ROW_TILE, LANES), F32),
                   jax.ShapeDtypeStruct((b, t, LANES), F32), jax.ShapeDtypeStruct((8, LANES), F32)),
        grid=(b, nt),
        in_specs=[tok(dm), tok(GROUP_W), tok(LANES), stream(4, GROUP_W), stream(4, LANES),
                  stream(16, GROUP_W), stream(16, LANES), tok(SWA_Q_W), tok(2048),
                  const((GROUP_W, dm)), const((SWA_Q_W, dm)), const((dm, dm)), const((1, dm)),
                  const((dm, 2 * LANES)), const((1, LANES)), const((2 * LANES, GROUP_W)), const((tm, tm))],
        out_specs=(tok(dm), pl.BlockSpec((None, tm * ROW_TILE, LANES), lambda bi, i: (bi, i, 0)), tok(LANES),
                   pl.BlockSpec((8, LANES), lambda bi, i: (0, 0))),
        scratch_shapes=[pltpu.VMEM((nch, tm, LANES), F32), pltpu.VMEM((tm, LANES), F32),
                        pltpu.VMEM((nch, tm, LANES), F32), pltpu.VMEM((tm, LANES), F32),
                        pltpu.VMEM((8, LANES), F32)],
        compiler_params=pltpu.CompilerParams(
            dimension_semantics=("arbitrary", "arbitrary"), vmem_limit_bytes=VMEM_LIMIT),
        name="merge_route",
    )(x, o0, l0, o1, l1, o2, l2, ys, gates, w_br_dil.astype(BF16), w_br_swa.astype(BF16), w_out.astype(BF16),
      g_ffn.reshape(1, dm), wrt, brt, expand, jnp.asarray(np.tril(np.ones((tm, tm), np.float32), -1), dtype=BF16))


def _dispatch_kernel(pos_ref, zoff_ref, nt_ref, h_ref, xs_ref, zbuf, sem, zsem, *, td, tme, nt_max):
    step = pl.program_id(0)

    def _zero_copy(e):
        off = pl.multiple_of(zoff_ref[e] * ROW_TILE, tme * ROW_TILE)
        return pltpu.make_async_copy(zbuf, xs_ref.at[pl.ds(off, tme * ROW_TILE)], zsem)

    def _row_copy(src_row, dst_row):
        return pltpu.make_async_copy(h_ref.at[pl.ds(pl.multiple_of(src_row * ROW_TILE, ROW_TILE), ROW_TILE)],
                                     xs_ref.at[pl.ds(pl.multiple_of(dst_row * ROW_TILE, ROW_TILE), ROW_TILE)], sem)

    def _wait_rows():
        pltpu.make_async_copy(h_ref, xs_ref.at[pl.ds(0, td * ROW_TILE)], sem).wait()

    def _tail_copy(tile):
        off = pl.multiple_of(tile * (tme * ROW_TILE), tme * ROW_TILE)
        return pltpu.make_async_copy(zbuf, xs_ref.at[pl.ds(off, tme * ROW_TILE)], zsem)

    @pl.when(step == 0)
    def _():
        zbuf[...] = jnp.zeros_like(zbuf)
        for e in range(N_CLASSES):
            _zero_copy(e).start()
        for e in range(N_CLASSES):
            _zero_copy(e).wait()

        def tail(tile, carry):
            _tail_copy(tile).start()
            _tail_copy(tile).wait()
            return carry

        lax.fori_loop(nt_ref[0], nt_max, tail, 0)

    def issue(j, carry):
        _row_copy(j, pos_ref[0, 0, j]).start()
        return carry

    lax.fori_loop(0, td, issue, 0, unroll=8)
    _wait_rows()


def _dispatch(h2, pos, zoff, ntiles, nslot, *, td, tme):
    n = h2.shape[0] // ROW_TILE
    nb = n // td
    grid_spec = pltpu.PrefetchScalarGridSpec(
        num_scalar_prefetch=0,
        grid=(nb,),
        in_specs=[
            pl.BlockSpec((1, 1, td), lambda i: (i, 0, 0), memory_space=pltpu.SMEM),
            pl.BlockSpec(memory_space=pltpu.SMEM),
            pl.BlockSpec(memory_space=pltpu.SMEM),
            pl.BlockSpec((td * ROW_TILE, LANES), lambda i: (i, 0)),
        ],
        out_specs=pl.BlockSpec(memory_space=pl.ANY),
        scratch_shapes=[pltpu.VMEM((tme * ROW_TILE, LANES), F32), pltpu.SemaphoreType.DMA(()),
                        pltpu.SemaphoreType.DMA(())],
    )
    return pl.pallas_call(
        functools.partial(_dispatch_kernel, td=td, tme=tme, nt_max=nslot // tme),
        out_shape=jax.ShapeDtypeStruct((nslot * ROW_TILE, LANES), F32),
        grid_spec=grid_spec,
        compiler_params=pltpu.CompilerParams(dimension_semantics=("arbitrary",), has_side_effects=True,
                                             vmem_limit_bytes=VMEM_LIMIT),
        name="dispatch",
    )(pos.reshape(nb, 1, td), zoff, ntiles, h2)


PAIR_ROWS = 2 * ROW_TILE


PLAN_FIRST, PLAN_SLOT, PLAN_NEXT, PLAN_HAS_NEXT = range(4)


def _expert_kernel(ta_ref, tb_ref, tblk_ref, trows_ref, plan_ref, nt_ref, x_ref, wg_hbm, wu_hbm, wd_hbm, y_ref,
                   wg_buf, wu_buf, wd_buf, wsem, *, tme):
    i = pl.program_id(0)
    live = i < nt_ref[0]
    half_rows = tme // 2

    def weight_copies(which, expert, slot):
        return [pltpu.make_async_copy(src.at[expert], buf.at[which, slot], wsem.at[which, slot])
                for src, buf in ((wg_hbm, wg_buf), (wu_hbm, wu_buf), (wd_hbm, wd_buf))]

    slots = []
    for which, t_ref in ((0, ta_ref), (1, tb_ref)):
        base = (i * 2 + which) * 4
        first = plan_ref[base + PLAN_FIRST] == 1
        slot = plan_ref[base + PLAN_SLOT]
        slots.append(slot)

        @pl.when(first & (i == 0))
        def _():
            for cp in weight_copies(which, t_ref[i], slot):
                cp.start()

        @pl.when(first)
        def _():
            for cp in weight_copies(which, t_ref[i], slot):
                cp.wait()

        @pl.when(first & (plan_ref[base + PLAN_HAS_NEXT] == 1))
        def _():
            for cp in weight_copies(which, plan_ref[base + PLAN_NEXT], 1 - slot):
                cp.start()

    def run(nrows):
        xb = jnp.concatenate([x_ref[pl.ds(c, nrows, stride=ROW_TILE), :] for c in range(ROW_TILE)], axis=1)
        nw = 2 * LANES
        for half in range(2):
            wg_ref, wu_ref, wd_ref = (buf.at[half, slots[half]] for buf in (wg_buf, wu_buf, wd_buf))
            act = []
            for c0 in range(0, D_EXPERT, nw):
                g = jnp.dot(xb, wg_ref[:, c0:c0 + nw], preferred_element_type=F32)
                u = jnp.dot(xb, wu_ref[:, c0:c0 + nw], preferred_element_type=F32)
                act.append(g * jax.nn.sigmoid(g) * u)
            a = jnp.concatenate(act, axis=1)
            for c0 in range(0, ROW_TILE * LANES, nw):
                y = jnp.dot(a, wd_ref[:, c0:c0 + nw], preferred_element_type=F32)
                for k in range(nw // LANES):
                    c = c0 // LANES + k
                    y_ref[pl.ds(half * ROW_TILE + c, nrows, stride=PAIR_ROWS), :] = y[:, k * LANES:(k + 1) * LANES]

    @pl.when(live & (trows_ref[i] > half_rows))
    def _():
        run(tme)

    @pl.when(live & (trows_ref[i] <= half_rows))
    def _():
        run(half_rows)
        y_ref[half_rows * PAIR_ROWS:, :] = jnp.zeros((half_rows * PAIR_ROWS, LANES), F32)

    @pl.when(jnp.logical_not(live))
    def _():
        y_ref[...] = jnp.zeros_like(y_ref)


def _weight_plan(tile_expert):
    nt = tile_expert.shape[0]
    first = jnp.concatenate([jnp.ones((1,), jnp.int32), (tile_expert[1:] != tile_expert[:-1]).astype(jnp.int32)])
    run_id = jnp.cumsum(first) - 1
    next_start = jnp.sum((run_id[None, :] <= run_id[:, None]).astype(jnp.int32), axis=1)
    has_next = (run_id < run_id[-1]).astype(jnp.int32)
    nxt = jnp.take(tile_expert, jnp.minimum(next_start, nt - 1))
    return jnp.stack([first, run_id % 2, nxt, has_next], axis=1).astype(jnp.int32)


def _experts(xs, tile_a, tile_b, tile_block, tile_rows, ntiles, w_e_gate, w_e_up, w_e_down, *, tme):
    nslot = xs.shape[0] // ROW_TILE
    dm = ROW_TILE * LANES
    nt = nslot // tme
    wg, wu, wd = w_e_gate, w_e_up, w_e_down
    plan =jnp.concatenate([_weight_plan(tile_a), _weight_plan(tile_b)], axis=1).reshape(-1)

    grid_spec = pltpu.PrefetchScalarGridSpec(
        num_scalar_prefetch=6,
        grid=(nt,),
        in_specs=[
            pl.BlockSpec((tme * ROW_TILE, LANES), lambda i, ta, tb, tk, tr, pn, n: (tk[i], 0)),
            pl.BlockSpec(memory_space=pl.ANY), pl.BlockSpec(memory_space=pl.ANY), pl.BlockSpec(memory_space=pl.ANY),
        ],
        out_specs=pl.BlockSpec((tme * PAIR_ROWS, LANES), lambda i, ta, tb, tk, tr, pn, n: (i, 0)),
        scratch_shapes=[pltpu.VMEM((2, 2, dm, D_EXPERT), F32), pltpu.VMEM((2, 2, dm, D_EXPERT), F32),
                        pltpu.VMEM((2, 2, D_EXPERT, dm), F32), pltpu.SemaphoreType.DMA((2, 2))],
    )
    return pl.pallas_call(
        functools.partial(_expert_kernel, tme=tme),
        out_shape=jax.ShapeDtypeStruct((nslot * PAIR_ROWS, LANES), F32),
        grid_spec=grid_spec,
        compiler_params=pltpu.CompilerParams(dimension_semantics=("arbitrary",), vmem_limit_bytes=VMEM_LIMIT),
        name="experts",
    )(tile_a, tile_b, tile_block, tile_rows, plan, ntiles, xs, wg, wu, wd)


def _combine_kernel(pos_ref, posn_ref, ys_ref, x1_ref, route_ref, gfin_ref, out_ref, ybuf, sem, *, tc):
    i = pl.program_id(0)
    nb = pl.num_programs(0)
    slot = i % 2

    def gather(p_ref, s):
        def issue(j, carry):
            src = pl.multiple_of(p_ref[0, 0, j] * PAIR_ROWS, PAIR_ROWS)
            dst = pl.multiple_of(j * PAIR_ROWS, PAIR_ROWS)
            pltpu.make_async_copy(ys_ref.at[pl.ds(src, PAIR_ROWS)], ybuf.at[s, pl.ds(dst, PAIR_ROWS)],
                                  sem.at[s]).start()
            return carry

        lax.fori_loop(0, tc, issue, 0, unroll=8)

    @pl.when(i == 0)
    def _():
        gather(pos_ref, 0)

    @pl.when(i + 1 < nb)
    def _():
        gather(posn_ref, 1 - slot)

    pltpu.make_async_copy(ys_ref.at[pl.ds(0, tc * PAIR_ROWS)], ybuf.at[slot], sem.at[slot]).wait()
    rec = route_ref[...]
    wa = rec[:, ROUTE_WA:ROUTE_WA + 1]
    wb = rec[:, ROUTE_WB:ROUTE_WB + 1]
    ya = jnp.concatenate([ybuf[slot, pl.ds(c, tc, stride=PAIR_ROWS), :] for c in range(ROW_TILE)], axis=1)
    yb = jnp.concatenate([ybuf[slot, pl.ds(ROW_TILE + c, tc, stride=PAIR_ROWS), :] for c in range(ROW_TILE)],
                         axis=1)
    z = x1_ref[...] + wa * ya + wb * yb
    ms = jnp.mean(z * z, axis=-1, keepdims=True)
    out_ref[...] = z * lax.rsqrt(ms + RMS_EPS) * gfin_ref[...]


def _combine(ys, pos, x1, route, g_final, *, tc):
    n, dm = x1.shape
    nb = n // tc
    grid_spec = pltpu.PrefetchScalarGridSpec(
        num_scalar_prefetch=0,
        grid=(nb,),
        in_specs=[
            pl.BlockSpec((1, 1, tc), lambda i: (i, 0, 0), memory_space=pltpu.SMEM),
            pl.BlockSpec((1, 1, tc), lambda i: (jnp.minimum(i + 1, nb - 1), 0, 0), memory_space=pltpu.SMEM),
            pl.BlockSpec(memory_space=pl.ANY),
            pl.BlockSpec((tc, dm), lambda i: (i, 0)),
            pl.BlockSpec((tc, LANES), lambda i: (i, 0)),
            pl.BlockSpec((1, dm), lambda i: (0, 0)),
        ],
        out_specs=pl.BlockSpec((tc, dm), lambda i: (i, 0)),
        scratch_shapes=[pltpu.VMEM((2, tc * PAIR_ROWS, LANES), F32), pltpu.SemaphoreType.DMA((2,))],
    )
    return pl.pallas_call(
        functools.partial(_combine_kernel, tc=tc),
        out_shape=jax.ShapeDtypeStruct((n, dm), F32),
        grid_spec=grid_spec,
        compiler_params=pltpu.CompilerParams(dimension_semantics=("arbitrary",), vmem_limit_bytes=VMEM_LIMIT),
        name="combine",
    )(pos.reshape(nb, 1, tc), pos.reshape(nb, 1, tc), ys, x1, route, g_final.reshape(1, dm))


def _slot_layout(route, counts, n, tme):
    cnt = counts[0, :N_CLASSES].astype(jnp.int32)
    tiles_per = (cnt + tme - 1) // tme
    tile_end = jnp.cumsum(tiles_per)
    seg_base = (tile_end - tiles_per) * tme
    ntiles = tile_end[-1]
    nt_max = n // tme + N_CLASSES
    tid = jnp.arange(nt_max, dtype=jnp.int32)
    live = jnp.minimum(tid, ntiles - 1)
    tile_class = jnp.sum((tile_end[None, :] <= live[:, None]).astype(jnp.int32), axis=1)
    tile_a = jnp.take(jnp.asarray(CLASS_EXPERT_A), tile_class)
    tile_b = jnp.take(jnp.asarray(CLASS_EXPERT_B), tile_class)
    first_tile = jnp.take(tile_end - tiles_per, tile_class)
    tile_rows = jnp.clip(jnp.take(cnt, tile_class) - (live - first_tile) * tme, 0, tme)
    cls = route[:, ROUTE_CLASS].astype(jnp.int32)
    rank = route[:, ROUTE_RANK].astype(jnp.int32)
    base = jnp.sum(jnp.where(cls[:, None] == jnp.arange(N_CLASSES, dtype=jnp.int32)[None, :], seg_base[None, :], 0),
                   axis=1)
    pos = base + rank
    zoff = jnp.maximum(tile_end - 1, 0) * tme
    return (pos, zoff.astype(jnp.int32), tile_a.astype(jnp.int32), tile_b.astype(jnp.int32),
            live.astype(jnp.int32), tile_rows.astype(jnp.int32), ntiles.reshape(1).astype(jnp.int32))


def kernel(x, g_mix, w_in, sinks, w_br_dil, w_br_swa, w_out, g_ffn, w_group, b_group, w_router, b_router,
           w_e_gate, w_e_up, w_e_down, g_final):
    b, t, dm = x.shape
    n = b * t
    tme = 256
    qkv0, swa, gates, qkv4, qkv16 = _inproj(x, g_mix[0], w_in[0], tm=512)
    o0, l0 = _dil_attention(qkv0.reshape(b, 1, t, QKV_W), 0, tq=1024)
    o1, l1 = _dil_attention(qkv4, 1, tq=1024)
    o2, l2 = _dil_attention(qkv16, 2, tq=512)
    ys = _swa_attention(swa, sinks[0], tq=1024)
    x1, h2, route, counts = _merge_route(
        x, o0.reshape(b, t, GROUP_W), l0.reshape(b, t, LANES), o1, l1, o2, l2, ys, gates,
        w_br_dil[0], w_br_swa[0], w_out[0], g_ffn[0], w_group[0], b_group[0], w_router[0], b_router[0], tm=512)
    route = route.reshape(n, LANES)
    pos, zoff, tile_a, tile_b, tile_block, tile_rows, ntiles = _slot_layout(route, counts, n, tme)
    nslot = n + N_CLASSES * tme
    xs = _dispatch(h2.reshape(n * ROW_TILE, LANES), pos, zoff, ntiles, nslot, td=4096, tme=tme)
    yslots = _experts(xs, tile_a, tile_b, tile_block, tile_rows, ntiles, w_e_gate[0], w_e_up[0], w_e_down[0],
                      tme=tme)
    out = _combine(yslots, pos, x1.reshape(n, dm), route, g_final, tc=1024)
    return out.reshape(b, t, dm)
```

```python
import functools

import numpy as np
import jax
import jax.numpy as jnp
from jax import lax
from jax.experimental import pallas as pl
from jax.experimental.pallas import tpu as pltpu

F32 = jnp.float32
BF16 = jnp.bfloat16

HEAD_DIM = 64
BAND = 128
DIL_PATTERNS = ((128, 1), (512, 4), (2048, 16))
DIL_HEADS_PER_GROUP = 8
DIL_HEADS = 24
GROUP_W = DIL_HEADS_PER_GROUP * HEAD_DIM
QKV_W = 3 * GROUP_W
SWA_WINDOW = 128
SWA_Q_HEADS = 16
SWA_KV_HEADS = 2
SWA_Q_W = SWA_Q_HEADS * HEAD_DIM
SWA_W = SWA_Q_W + 4 * 128
MOE_GROUPS = 4
EXPERTS_PER_GROUP = 8
N_EXPERTS = 32
D_EXPERT = 512
RMS_EPS = 1e-6
LANES = 128
NEG_INF = float("-inf")
LOG2E = 1.4426950408889634
LN2 = 0.6931471805599453
ROW_TILE = 8

VMEM_LIMIT = 56 * 1024 * 1024


def _alibi_slopes(n):
    return (2.0 ** (-8.0 * np.arange(1, n + 1) / n)).astype(np.float32)


def _band_bias(slopes, max_back, unit):
    a = np.arange(BAND)[:, None]
    c = np.arange(2 * BAND)[None, :]
    delta = a + BAND - c
    band = (delta >= 0) & (delta <= max_back)
    pen = (-slopes[:, None, None] * (delta * unit)[None] * LOG2E).astype(np.float32)
    full = np.where(band[None], pen, -np.inf).astype(np.float32)
    first = np.where((band & (c >= BAND))[None], pen, -np.inf).astype(np.float32)
    return np.stack([full, first], axis=0)


def _inproj_kernel(x_ref, g_ref, w_ref, qkv0_ref, swa_ref, gates_ref, qkv4_ref, qkv16_ref, hs_ref, *, tm):
    x = x_ref[...]
    ms = jnp.mean(x * x, axis=-1, keepdims=True)
    h = x * lax.rsqrt(ms + RMS_EPS) * g_ref[...]
    nchunk = h.shape[1] // LANES
    for c in range(nchunk):
        hs_ref[c] = h[:, c * LANES:(c + 1) * LANES]
    hb = h.astype(BF16)

    def proj(lhs, c0, width):
        return jnp.dot(lhs, w_ref[:, c0:c0 + width], preferred_element_type=F32)

    for j in range(3):
        qkv0_ref[:, j * GROUP_W:(j + 1) * GROUP_W] = proj(hb, COL_QKV_D[j], GROUP_W).astype(BF16)
    for c0 in range(0, SWA_Q_W, 512):
        swa_ref[:, c0:c0 + 512] = proj(hb, COL_Q_S + c0, 512).astype(BF16)
    kv = proj(hb, COL_KV_S, 4 * HEAD_DIM)
    part = [kv[:, j * HEAD_DIM:(j + 1) * HEAD_DIM] for j in range(4)]
    swa_ref[:, SWA_Q_W:SWA_W] = jnp.concatenate(
        [part[0], part[0], part[1], part[1], part[2], part[2], part[3], part[3]], axis=1).astype(BF16)
    for c0 in range(0, 2048, 512):
        gates_ref[:, c0:c0 + 512] = proj(hb, COL_GATES + c0, 512).astype(BF16)
    for gi, out_ref, d in ((1, qkv4_ref, 4), (2, qkv16_ref, 16)):
        rows = tm // d
        hp = jnp.concatenate(
            [jnp.concatenate([hs_ref[c, pl.ds(r, rows, stride=d), :] for c in range(nchunk)], axis=1)
             for r in range(d)], axis=0).astype(BF16)
        for j in range(3):
            res = proj(hp, COL_QKV_D[j] + gi * GROUP_W, GROUP_W).astype(BF16)
            for r in range(d):
                out_ref[r, :, j * GROUP_W:(j + 1) * GROUP_W] = res[r * rows:(r + 1) * rows]


COL_QKV_D = (0, 1536, 3072)
COL_Q_S = 4608
COL_KV_S = 5632
COL_GATES = 5888
IN_WIDTH = 7936
Q_SCALE = HEAD_DIM ** -0.5 * LOG2E


def _prep_w_in(w_in):
    col = np.arange(IN_WIDTH)
    is_q = (col < COL_QKV_D[1]) | ((col >= COL_Q_S) & (col < COL_KV_S))
    colscale = np.where(is_q, Q_SCALE, 1.0).astype(np.float32)
    return (w_in * colscale[None, :]).astype(BF16)


def _inproj(x, g_mix, w_in, *, tm):
    b, t, dm = x.shape
    wp = _prep_w_in(w_in)
    nw = wp.shape[1]
    nt = t // tm
    out_shape = (
        jax.ShapeDtypeStruct((b, t, QKV_W), BF16),
        jax.ShapeDtypeStruct((b, t, SWA_W), BF16),
        jax.ShapeDtypeStruct((b, t, 2048), BF16),
        jax.ShapeDtypeStruct((b, 4, t // 4, QKV_W), BF16),
        jax.ShapeDtypeStruct((b, 16, t // 16, QKV_W), BF16),
    )
    return pl.pallas_call(
        functools.partial(_inproj_kernel, tm=tm),
        out_shape=out_shape,
        grid=(b, nt),
        in_specs=[
            pl.BlockSpec((None, tm, dm), lambda bi, i: (bi, i, 0)),
            pl.BlockSpec((1, dm), lambda bi, i: (0, 0)),
            pl.BlockSpec((dm, nw), lambda bi, i: (0, 0), pipeline_mode=pl.Buffered(1)),
        ],
        out_specs=(
            pl.BlockSpec((None, tm, QKV_W), lambda bi, i: (bi, i, 0)),
            pl.BlockSpec((None, tm, SWA_W), lambda bi, i: (bi, i, 0)),
            pl.BlockSpec((None, tm, 2048), lambda bi, i: (bi, i, 0)),
            pl.BlockSpec((None, 4, tm // 4, QKV_W), lambda bi, i: (bi, 0, i, 0)),
            pl.BlockSpec((None, 16, tm // 16, QKV_W), lambda bi, i: (bi, 0, i, 0)),
        ),
        scratch_shapes=[pltpu.VMEM((dm // LANES, tm, LANES), F32)],
        compiler_params=pltpu.CompilerParams(
            dimension_semantics=("arbitrary", "arbitrary"), vmem_limit_bytes=VMEM_LIMIT),
        name="inproj",
    )(x, g_mix.reshape(1, dm), wp)


def _dil_attn_kernel(q_ref, kc_ref, vc_ref, kp_ref, vp_ref, bias_ref, o_ref, lse_ref, kbuf, vbuf, *, tq):
    i = pl.program_id(2)
    kbuf[0:BAND] = kp_ref[...]
    kbuf[BAND:BAND + tq] = kc_ref[...]
    vbuf[0:BAND] = vp_ref[...]
    vbuf[BAND:BAND + tq] = vc_ref[...]
    lane = lax.broadcasted_iota(jnp.int32, (BAND, LANES), 1)
    lo = lane < HEAD_DIM
    zero = jnp.zeros((BAND, LANES), BF16)
    for qb in range(tq // BAND):
        sel = jnp.where(i == 0, 1, 0) if qb == 0 else 0
        rq = slice(qb * BAND, (qb + 1) * BAND)
        rk = slice(qb * BAND, qb * BAND + 2 * BAND)
        lse_parts = []
        for pr in range(GROUP_W // LANES):
            cl = slice(pr * LANES, (pr + 1) * LANES)
            q2 = q_ref[rq, cl]
            kk = kbuf[rk, cl]
            vv = vbuf[rk, cl]
            s_all = lax.dot_general(jnp.concatenate([jnp.where(lo, q2, zero), jnp.where(lo, zero, q2)], axis=0), kk,
                                    (((1,), (1,)), ((), ())), preferred_element_type=F32)
            probs, stats = [], []
            for hh in range(2):
                s = s_all[hh * BAND:(hh + 1) * BAND] + bias_ref[sel, 2 * pr + hh]
                m = jnp.max(s, axis=-1, keepdims=True)
                p = jnp.exp2(s - m)
                l = jnp.sum(p, axis=-1, keepdims=True)
                probs.append(p.astype(BF16))
                stats.append((1.0 / l, (m + jnp.log2(l)) * LN2))
            o_all = jnp.dot(jnp.concatenate(probs, axis=0), vv, preferred_element_type=F32)
            res = [(o_all[hh * BAND:(hh + 1) * BAND] * stats[hh][0], stats[hh][1]) for hh in range(2)]
            o_ref[rq, cl] = jnp.where(lo, res[0][0], res[1][0]).astype(o_ref.dtype)
            lse_parts.append(jnp.where(lane == 2 * pr, res[0][1], jnp.where(lane == 2 * pr + 1, res[1][1], 0.0)))
        lse_ref[rq, :] = (lse_parts[0] + lse_parts[1]) + (lse_parts[2] + lse_parts[3])


def _dil_attention(qkv, gi, *, tq):
    b, d, l, _ = qkv.shape
    window, dil = DIL_PATTERNS[gi]
    assert dil == d
    tq = min(tq, l)
    nq = tq // BAND
    slopes = _alibi_slopes(DIL_HEADS)[gi * 8:(gi + 1) * 8]
    bias = jnp.asarray(_band_bias(slopes, window // dil, dil))

    def cur(c):
        return pl.BlockSpec((None, None, tq, GROUP_W), lambda bi, r, i: (bi, r, i, c))

    def prev(c):
        return pl.BlockSpec((None, None, BAND, GROUP_W), lambda bi, r, i: (bi, r, jnp.maximum(i * nq - 1, 0), c))

    return pl.pallas_call(
        functools.partial(_dil_attn_kernel, tq=tq),
        out_shape=(jax.ShapeDtypeStruct((b, d, l, GROUP_W), BF16),
                   jax.ShapeDtypeStruct((b, d, l, LANES), F32)),
        grid=(b, d, l // tq),
        in_specs=[cur(0), cur(1), cur(2), prev(1), prev(2),
                  pl.BlockSpec((2, 8, BAND, 2 * BAND), lambda bi, r, i: (0, 0, 0, 0))],
        out_specs=(cur(0), pl.BlockSpec((None, None, tq, LANES), lambda bi, r, i: (bi, r, i, 0))),
        scratch_shapes=[pltpu.VMEM((BAND + tq, GROUP_W), BF16), pltpu.VMEM((BAND + tq, GROUP_W), BF16)],
        compiler_params=pltpu.CompilerParams(
            dimension_semantics=("arbitrary", "arbitrary", "arbitrary"), vmem_limit_bytes=VMEM_LIMIT),
        name=f"dil_attn_{gi}",
    )(qkv, qkv, qkv, qkv, qkv, bias)


def _swa_attn_kernel(sink_ref, qlo_ref, qhi_ref, kvc_ref, kvp_ref, bias_ref, o_ref, kvbuf, *, tq):
    i = pl.program_id(1)
    kvbuf[0:BAND] = kvp_ref[...]
    kvbuf[BAND:BAND + tq] = kvc_ref[...]
    lo = lax.broadcasted_iota(jnp.int32, (BAND, LANES), 1) < HEAD_DIM
    zero = jnp.zeros((BAND, LANES), BF16)
    for qb in range(tq // BAND):
        sel = jnp.where(i == 0, 1, 0) if qb == 0 else 0
        rq = slice(qb * BAND, (qb + 1) * BAND)
        rk = slice(qb * BAND, qb * BAND + 2 * BAND)
        for g in range(SWA_KV_HEADS):
            q_ref = qlo_ref if g == 0 else qhi_ref
            kk = kvbuf[rk, g * LANES:(g + 1) * LANES]
            vv = kvbuf[rk, (2 + g) * LANES:(3 + g) * LANES]
            masked = []
            for pp in range(4):
                q2 = q_ref[rq, pp * LANES:(pp + 1) * LANES]
                masked += [jnp.where(lo, q2, zero), jnp.where(lo, zero, q2)]
            s_all = lax.dot_general(jnp.concatenate(masked, axis=0), kk, (((1,), (1,)), ((), ())),
                                    preferred_element_type=F32)
            probs, inv = [], []
            for j in range(8):
                h = 8 * g + j
                sink = sink_ref[h] * LOG2E
                s = s_all[j * BAND:(j + 1) * BAND] + bias_ref[sel, h]
                m = jnp.maximum(jnp.max(s, axis=-1, keepdims=True), sink)
                p = jnp.exp2(s - m)
                inv.append(1.0 / (jnp.sum(p, axis=-1, keepdims=True) + jnp.exp2(sink - m)))
                probs.append(p.astype(BF16))
            o_all = jnp.dot(jnp.concatenate(probs, axis=0), vv, preferred_element_type=F32)
            for pp in range(4):
                oa = o_all[(2 * pp) * BAND:(2 * pp + 1) * BAND] * inv[2 * pp]
                ob = o_all[(2 * pp + 1) * BAND:(2 * pp + 2) * BAND] * inv[2 * pp + 1]
                o_ref[rq, (4 * g + pp) * LANES:(4 * g + pp + 1) * LANES] = jnp.where(lo, oa, ob).astype(o_ref.dtype)


def _swa_attention(swa, sinks, *, tq):
    b, t, _ = swa.shape
    tq = min(tq, t)
    nq = tq // BAND
    bias = jnp.asarray(_band_bias(_alibi_slopes(SWA_Q_HEADS), SWA_WINDOW - 1, 1))
    grid_spec = pltpu.PrefetchScalarGridSpec(
        num_scalar_prefetch=1,
        grid=(b, t // tq),
        in_specs=[
            pl.BlockSpec((None, tq, 512), lambda bi, i, s: (bi, i, 0)),
            pl.BlockSpec((None, tq, 512), lambda bi, i, s: (bi, i, 1)),
            pl.BlockSpec((None, tq, 512), lambda bi, i, s: (bi, i, 2)),
            pl.BlockSpec((None, BAND, 512), lambda bi, i, s: (bi, jnp.maximum(i * nq - 1, 0), 2)),
            pl.BlockSpec((2, SWA_Q_HEADS, BAND, 2 * BAND), lambda bi, i, s: (0, 0, 0, 0)),
        ],
        out_specs=pl.BlockSpec((None, tq, SWA_Q_W), lambda bi, i, s: (bi, i, 0)),
        scratch_shapes=[pltpu.VMEM((BAND + tq, 512), BF16)],
    )
    return pl.pallas_call(
        functools.partial(_swa_attn_kernel, tq=tq),
        out_shape=jax.ShapeDtypeStruct((b, t, SWA_Q_W), BF16),
        grid_spec=grid_spec,
        compiler_params=pltpu.CompilerParams(
            dimension_semantics=("arbitrary", "arbitrary"), vmem_limit_bytes=VMEM_LIMIT),
        name="swa_attn",
    )(sinks.astype(F32), swa, swa, swa, swa, bias)


ROUTE_CLASS, ROUTE_RANK, ROUTE_WA, ROUTE_WB = range(4)
ROUTER_LANE0 = MOE_GROUPS
PAIRS_PER_GROUP = EXPERTS_PER_GROUP * (EXPERTS_PER_GROUP - 1) // 2
N_CLASSES = MOE_GROUPS * PAIRS_PER_GROUP
_PAIRS = [(a, b) for a in range(EXPERTS_PER_GROUP) for b in range(a + 1, EXPERTS_PER_GROUP)]
CLASS_EXPERT_A = np.array([g * EXPERTS_PER_GROUP + a for g in range(MOE_GROUPS) for a, _ in _PAIRS], np.int32)
CLASS_EXPERT_B = np.array([g * EXPERTS_PER_GROUP + b for g in range(MOE_GROUPS) for _, b in _PAIRS], np.int32)


def _route(logits, carry_ref, ltri_ref, tm, active):
    lane = lax.broadcasted_iota(jnp.int32, (tm, LANES), 1)
    lanef = lane.astype(F32)

    def first_argmax(v):
        m = jnp.max(v, axis=-1, keepdims=True)
        return m, jnp.min(jnp.where(v == m, lanef, float(LANES)), axis=-1, keepdims=True)

    gl = jnp.where(lane < MOE_GROUPS, logits, NEG_INF)
    gmax, gidx = first_argmax(gl)
    g_w = 1.0 / jnp.sum(jnp.exp(gl - gmax), axis=-1, keepdims=True)
    e_lane = lane - ROUTER_LANE0
    lane_group = (e_lane >> 3).astype(F32)
    in_group = (e_lane >= 0) & (e_lane < N_EXPERTS) & (lane_group == gidx)
    el = jnp.where(in_group, logits, NEG_INF)
    m1, i1 = first_argmax(el)
    m2, i2 = first_argmax(jnp.where(lanef == i1, NEG_INF, el))
    tt = jnp.exp(m2 - m1)
    w1 = g_w / (1.0 + tt)
    w2 = g_w * tt / (1.0 + tt)
    first = float(ROUTER_LANE0) + float(EXPERTS_PER_GROUP) * gidx
    e1 = i1 - first
    e2 = i2 - first
    swap = e2 < e1
    ea = jnp.minimum(e1, e2)
    eb = jnp.maximum(e1, e2)
    wa = jnp.where(swap, w2, w1)
    wb = jnp.where(swap, w1, w2)
    pair = ea * (float(2 * EXPERTS_PER_GROUP - 1) - ea) * 0.5 + (eb - ea - 1.0)
    cls = float(PAIRS_PER_GROUP) * gidx + pair
    oh = jnp.where(lanef == cls, 1.0, 0.0)
    before = jnp.dot(ltri_ref[...], oh.astype(BF16), preferred_element_type=F32) + carry_ref[0:1, :]
    rank = jnp.sum(oh * before, axis=-1, keepdims=True)
    carry_ref[...] = carry_ref[...] + jnp.where(active, jnp.sum(oh, axis=0, keepdims=True), 0.0)
    rec = jnp.zeros((tm, LANES), F32)
    for ln, val in ((ROUTE_CLASS, cls), (ROUTE_RANK, rank), (ROUTE_WA, wa), (ROUTE_WB, wb)):
        rec = jnp.where(lane == ln, val, rec)
    return rec


def _merge_kernel(x_ref, o0_ref, l0_ref, o1_ref, l1_ref, o2_ref, l2_ref, ys_ref, gates_ref,
                  wbd_ref, wbs_ref, wo_ref, gffn_ref, wrt_ref, brt_ref, exp_ref, ltri_ref,
                  x1_ref, h2_ref, route_ref, cnt_ref,
                  uo1, ul1, uo2, ul2, carry_ref, logit_buf, *, tm):
    step = pl.program_id(0)

    @pl.when(step == 0)
    def _():
        carry_ref[...] = jnp.zeros_like(carry_ref)
        logit_buf[...] = jnp.zeros_like(logit_buf)

    route_ref[...] = _route(logit_buf[...], carry_ref, ltri_ref, tm, step > 0)
    cnt_ref[...] = carry_ref[...]

    nch = GROUP_W // LANES
    for src_o, src_l, dst_o, dst_l, d in ((o1_ref, l1_ref, uo1, ul1, 4), (o2_ref, l2_ref, uo2, ul2, 16)):
        rows = tm // d
        for r in range(d):
            dst_l[pl.ds(r, rows, stride=d), :] = src_l[r]
            for c in range(nch):
                dst_o[c, pl.ds(r, rows, stride=d), :] = src_o[r, :, c * LANES:(c + 1) * LANES].astype(F32)
    def per_head_to_lanes(w):
        hi = w.astype(BF16)
        lo = (w - hi.astype(F32)).astype(BF16)
        return jnp.dot(jnp.concatenate([hi, lo], axis=1), exp_ref[...], preferred_element_type=F32)

    def sig(v):
        return 0.5 * jnp.tanh(0.5 * v) + 0.5

    def rows_to_logits(r0, nr):
        rs = slice(r0, r0 + nr)
        la, lb, lc = l0_ref[rs, :], ul1[rs, :], ul2[rs, :]
        mx = jnp.maximum(jnp.maximum(la, lb), lc)
        ea, eb, ec = jnp.exp(la - mx), jnp.exp(lb - mx), jnp.exp(lc - mx)
        inv = 1.0 / (ea + eb + ec)
        wa, wb, wc = per_head_to_lanes(ea * inv), per_head_to_lanes(eb * inv), per_head_to_lanes(ec * inv)
        ych = []
        for c in range(nch):
            cl = slice(c * LANES, (c + 1) * LANES)
            ych.append((wa[:, cl] * o0_ref[rs, cl].astype(F32) + wb[:, cl] * uo1[c, rs, :]
                        + wc[:, cl] * uo2[c, rs, :]).astype(BF16))
        y = jnp.concatenate(ych, axis=1)
        a = jnp.dot(y, wbd_ref[...], preferred_element_type=F32)
        bsw = jnp.dot(ys_ref[rs, :], wbs_ref[...], preferred_element_type=F32)
        mixed = sig(gates_ref[rs, 0:1024]) * a.astype(BF16) + sig(gates_ref[rs, 1024:2048]) * bsw.astype(BF16)
        x1 = x_ref[rs, :] + jnp.dot(mixed, wo_ref[...], preferred_element_type=F32)
        x1_ref[rs, :] = x1
        ms = jnp.mean(x1 * x1, axis=-1, keepdims=True)
        h2 = x1 * lax.rsqrt(ms + RMS_EPS) * gffn_ref[...]
        for c in range(h2.shape[1] // LANES):
            h2_ref[pl.ds(r0 * ROW_TILE + c, nr, stride=ROW_TILE), :] = h2[:, c * LANES:(c + 1) * LANES]
        h_hi = h2.astype(BF16)
        h_lo = (h2 - h_hi.astype(F32)).astype(BF16)
        hw = jnp.dot(h_hi, wrt_ref[...], preferred_element_type=F32)
        return (hw[:, :LANES] + hw[:, LANES:]
                + jnp.dot(h_lo, wrt_ref[:, :LANES], preferred_element_type=F32) + brt_ref[...])

    logit_buf[...] = rows_to_logits(0, tm)


def _merge_route(x, o0, l0, o1, l1, o2, l2, ys, gates, w_br_dil, w_br_swa, w_out, g_ffn,
                 w_group, b_group, w_router, b_router, *, tm):
    b, t, dm = x.shape
    nt = t // tm
    wrt = jnp.zeros((dm, LANES), F32).at[:, :MOE_GROUPS].set(w_group).at[
        :, ROUTER_LANE0:ROUTER_LANE0 + N_EXPERTS].set(w_router)
    brt = jnp.zeros((1, LANES), F32).at[0, :MOE_GROUPS].set(b_group).at[
        0, ROUTER_LANE0:ROUTER_LANE0 + N_EXPERTS].set(b_router)
    wrt_hi = wrt.astype(BF16)
    wrt_lo = (wrt - wrt_hi.astype(F32)).astype(BF16)
    wrt = jnp.concatenate([wrt_hi, wrt_lo], axis=1)

    nsteps = b * nt + 1

    def tile_of(s, lag=0):
        tile = jnp.clip(s - lag, 0, nsteps - 2)
        return tile // nt, tile % nt

    def tok(width, lag=0):
        return pl.BlockSpec((None, tm, width), lambda s: (*tile_of(s, lag), 0))

    def stream(d, width):
        return pl.BlockSpec((None, d, tm // d, width), lambda s: (tile_of(s)[0], 0, tile_of(s)[1], 0))

    def const(shape):
        return pl.BlockSpec(shape, lambda s: (0,) * len(shape), pipeline_mode=pl.Buffered(1))

    head_of_lane = np.arange(GROUP_W) // HEAD_DIM
    spread = (np.arange(LANES)[:, None] == head_of_lane[None, :]).astype(np.float32)
    expand = jnp.asarray(np.concatenate([spread, spread], axis=0), dtype=BF16)

    nch = GROUP_W // LANES
    return pl.pallas_call(
        functools.partial(_merge_kernel, tm=tm),
        out_shape=(jax.ShapeDtypeStruct((b, t, dm), F32), jax.ShapeDtypeStruct((b, t * ROW_TILE, LANES), F32),
                   jax.ShapeDtypeStruct((b, t, LANES), F32), jax.ShapeDtypeStruct((8, LANES), F32)),
        grid=(nsteps,),
        in_specs=[tok(dm), tok(GROUP_W), tok(LANES), stream(4, GROUP_W), stream(4, LANES),
                  stream(16, GROUP_W), stream(16, LANES), tok(SWA_Q_W), tok(2048),
                  const((GROUP_W, dm)), const((SWA_Q_W, dm)), const((dm, dm)), const((1, dm)),
                  const((dm, 2 * LANES)), const((1, LANES)), const((2 * LANES, GROUP_W)), const((tm, tm))],
        out_specs=(tok(dm), pl.BlockSpec((None, tm * ROW_TILE, LANES), lambda s: (*tile_of(s), 0)),
                   tok(LANES, lag=1), pl.BlockSpec((8, LANES), lambda s: (0, 0))),
        scratch_shapes=[pltpu.VMEM((nch, tm, LANES), F32), pltpu.VMEM((tm, LANES), F32),
                        pltpu.VMEM((nch, tm, LANES), F32), pltpu.VMEM((tm, LANES), F32),
                        pltpu.VMEM((8, LANES), F32), pltpu.VMEM((tm, LANES), F32)],
        compiler_params=pltpu.CompilerParams(
            dimension_semantics=("arbitrary",), vmem_limit_bytes=VMEM_LIMIT),
        name="merge_route",
    )(x, o0, l0, o1, l1, o2, l2, ys, gates, w_br_dil.astype(BF16), w_br_swa.astype(BF16), w_out.astype(BF16),
      g_ffn.reshape(1, dm), wrt, brt, expand, jnp.asarray(np.tril(np.ones((tm, tm), np.float32), -1), dtype=BF16))


def _dispatch_kernel(pos_ref, zoff_ref, nt_ref, h_ref, xs_ref, zbuf, sem, zsem, *, td, tme, nt_max):
    step = pl.program_id(0)

    def _zero_copy(e):
        off = pl.multiple_of(zoff_ref[e] * ROW_TILE, tme * ROW_TILE)
        return pltpu.make_async_copy(zbuf, xs_ref.at[pl.ds(off, tme * ROW_TILE)], zsem)

    def _row_copy(src_row, dst_row):
        return pltpu.make_async_copy(h_ref.at[pl.ds(pl.multiple_of(src_row * ROW_TILE, ROW_TILE), ROW_TILE)],
                                     xs_ref.at[pl.ds(pl.multiple_of(dst_row * ROW_TILE, ROW_TILE), ROW_TILE)], sem)

    def _wait_rows():
        pltpu.make_async_copy(h_ref, xs_ref.at[pl.ds(0, td * ROW_TILE)], sem).wait()

    def _tail_copy(tile):
        off = pl.multiple_of(tile * (tme * ROW_TILE), tme * ROW_TILE)
        return pltpu.make_async_copy(zbuf, xs_ref.at[pl.ds(off, tme * ROW_TILE)], zsem)

    @pl.when(step == 0)
    def _():
        zbuf[...] = jnp.zeros_like(zbuf)
        for e in range(N_CLASSES):
            _zero_copy(e).start()
        for e in range(N_CLASSES):
            _zero_copy(e).wait()

        def tail(tile, carry):
            _tail_copy(tile).start()
            _tail_copy(tile).wait()
            return carry

        lax.fori_loop(nt_ref[0], nt_max, tail, 0)

    def issue(j, carry):
        _row_copy(j, pos_ref[0, 0, j]).start()
        return carry

    lax.fori_loop(0, td, issue, 0, unroll=8)
    _wait_rows()


def _dispatch(h2, pos, zoff, ntiles, nslot, *, td, tme):
    n = h2.shape[0] // ROW_TILE
    nb = n // td
    grid_spec = pltpu.PrefetchScalarGridSpec(
        num_scalar_prefetch=0,
        grid=(nb,),
        in_specs=[
            pl.BlockSpec((1, 1, td), lambda i: (i, 0, 0), memory_space=pltpu.SMEM),
            pl.BlockSpec(memory_space=pltpu.SMEM),
            pl.BlockSpec(memory_space=pltpu.SMEM),
            pl.BlockSpec((td * ROW_TILE, LANES), lambda i: (i, 0)),
        ],
        out_specs=pl.BlockSpec(memory_space=pl.ANY),
        scratch_shapes=[pltpu.VMEM((tme * ROW_TILE, LANES), F32), pltpu.SemaphoreType.DMA(()),
                        pltpu.SemaphoreType.DMA(())],
    )
    return pl.pallas_call(
        functools.partial(_dispatch_kernel, td=td, tme=tme, nt_max=nslot // tme),
        out_shape=jax.ShapeDtypeStruct((nslot * ROW_TILE, LANES), F32),
        grid_spec=grid_spec,
        compiler_params=pltpu.CompilerParams(dimension_semantics=("arbitrary",), has_side_effects=True,
                                             vmem_limit_bytes=VMEM_LIMIT),
        name="dispatch",
    )(pos.reshape(nb, 1, td), zoff, ntiles, h2)


PAIR_ROWS = 2 * ROW_TILE


PLAN_FIRST, PLAN_SLOT, PLAN_NEXT, PLAN_HAS_NEXT = range(4)


def _expert_kernel(ta_ref, tb_ref, tblk_ref, trows_ref, plan_ref, nt_ref, x_ref, wg_hbm, wu_hbm, wd_hbm, y_ref,
                   wg_buf, wu_buf, wd_buf, wsem, *, tme):
    i = pl.program_id(0)
    live = i < nt_ref[0]
    half_rows = tme // 2

    def weight_copies(which, expert, slot):
        return [pltpu.make_async_copy(src.at[expert], buf.at[which, slot], wsem.at[which, slot])
                for src, buf in ((wg_hbm, wg_buf), (wu_hbm, wu_buf), (wd_hbm, wd_buf))]

    slots = []
    for which, t_ref in ((0, ta_ref), (1, tb_ref)):
        base = (i * 2 + which) * 4
        first = plan_ref[base + PLAN_FIRST] == 1
        slot = plan_ref[base + PLAN_SLOT]
        slots.append(slot)

        @pl.when(first & (i == 0))
        def _():
            for cp in weight_copies(which, t_ref[i], slot):
                cp.start()

        @pl.when(first)
        def _():
            for cp in weight_copies(which, t_ref[i], slot):
                cp.wait()

        @pl.when(first & (plan_ref[base + PLAN_HAS_NEXT] == 1))
        def _():
            for cp in weight_copies(which, plan_ref[base + PLAN_NEXT], 1 - slot):
                cp.start()

    def run(nrows):
        xb = jnp.concatenate([x_ref[pl.ds(c, nrows, stride=ROW_TILE), :] for c in range(ROW_TILE)], axis=1)
        nw = 2 * LANES
        for half in range(2):
            wg_ref, wu_ref, wd_ref = (buf.at[half, slots[half]] for buf in (wg_buf, wu_buf, wd_buf))
            act = []
            for c0 in range(0, D_EXPERT, nw):
                g = jnp.dot(xb, wg_ref[:, c0:c0 + nw], preferred_element_type=F32)
                u = jnp.dot(xb, wu_ref[:, c0:c0 + nw], preferred_element_type=F32)
                act.append(g * jax.nn.sigmoid(g) * u)
            a = jnp.concatenate(act, axis=1)
            for c0 in range(0, ROW_TILE * LANES, nw):
                y = jnp.dot(a, wd_ref[:, c0:c0 + nw], preferred_element_type=F32)
                for k in range(nw // LANES):
                    c = c0 // LANES + k
                    y_ref[pl.ds(half * ROW_TILE + c, nrows, stride=PAIR_ROWS), :] = y[:, k * LANES:(k + 1) * LANES]

    @pl.when(live & (trows_ref[i] > half_rows))
    def _():
        run(tme)

    @pl.when(live & (trows_ref[i] <= half_rows))
    def _():
        run(half_rows)
        y_ref[half_rows * PAIR_ROWS:, :] = jnp.zeros((half_rows * PAIR_ROWS, LANES), F32)

    @pl.when(jnp.logical_not(live))
    def _():
        y_ref[...] = jnp.zeros_like(y_ref)


def _weight_plan(tile_expert):
    nt = tile_expert.shape[0]
    first = jnp.concatenate([jnp.ones((1,), jnp.int32), (tile_expert[1:] != tile_expert[:-1]).astype(jnp.int32)])
    run_id = jnp.cumsum(first) - 1
    next_start = jnp.sum((run_id[None, :] <= run_id[:, None]).astype(jnp.int32), axis=1)
    has_next = (run_id < run_id[-1]).astype(jnp.int32)
    nxt = jnp.take(tile_expert, jnp.minimum(next_start, nt - 1))
    return jnp.stack([first, run_id % 2, nxt, has_next], axis=1).astype(jnp.int32)


def _experts(xs, tile_a, tile_b, tile_block, tile_rows, ntiles, w_e_gate, w_e_up, w_e_down, *, tme):
    nslot = xs.shape[0] // ROW_TILE
    dm = ROW_TILE * LANES
    nt = nslot // tme
    wg, wu, wd = w_e_gate, w_e_up, w_e_down
    plan =jnp.concatenate([_weight_plan(tile_a), _weight_plan(tile_b)], axis=1).reshape(-1)

    grid_spec = pltpu.PrefetchScalarGridSpec(
        num_scalar_prefetch=6,
        grid=(nt,),
        in_specs=[
            pl.BlockSpec((tme * ROW_TILE, LANES), lambda i, ta, tb, tk, tr, pn, n: (tk[i], 0)),
            pl.BlockSpec(memory_space=pl.ANY), pl.BlockSpec(memory_space=pl.ANY), pl.BlockSpec(memory_space=pl.ANY),
        ],
        out_specs=pl.BlockSpec((tme * PAIR_ROWS, LANES), lambda i, ta, tb, tk, tr, pn, n: (i, 0)),
        scratch_shapes=[pltpu.VMEM((2, 2, dm, D_EXPERT), F32), pltpu.VMEM((2, 2, dm, D_EXPERT), F32),
                        pltpu.VMEM((2, 2, D_EXPERT, dm), F32), pltpu.SemaphoreType.DMA((2, 2))],
    )
    return pl.pallas_call(
        functools.partial(_expert_kernel, tme=tme),
        out_shape=jax.ShapeDtypeStruct((nslot * PAIR_ROWS, LANES), F32),
        grid_spec=grid_spec,
        compiler_params=pltpu.CompilerParams(dimension_semantics=("arbitrary",), vmem_limit_bytes=VMEM_LIMIT),
        name="experts",
    )(tile_a, tile_b, tile_block, tile_rows, plan, ntiles, xs, wg, wu, wd)


def _combine_kernel(pos_ref, posn_ref, ys_ref, x1_ref, route_ref, gfin_ref, out_ref, ybuf, sem, *, tc):
    i = pl.program_id(0)
    nb = pl.num_programs(0)
    slot = i % 2

    def gather(p_ref, s):
        def issue(j, carry):
            src = pl.multiple_of(p_ref[0, 0, j] * PAIR_ROWS, PAIR_ROWS)
            dst = pl.multiple_of(j * PAIR_ROWS, PAIR_ROWS)
            pltpu.make_async_copy(ys_ref.at[pl.ds(src, PAIR_ROWS)], ybuf.at[s, pl.ds(dst, PAIR_ROWS)],
                                  sem.at[s]).start()
            return carry

        lax.fori_loop(0, tc, issue, 0, unroll=8)

    @pl.when(i == 0)
    def _():
        gather(pos_ref, 0)

    @pl.when(i + 1 < nb)
    def _():
        gather(posn_ref, 1 - slot)

    pltpu.make_async_copy(ys_ref.at[pl.ds(0, tc * PAIR_ROWS)], ybuf.at[slot], sem.at[slot]).wait()
    rec = route_ref[...]
    wa = rec[:, ROUTE_WA:ROUTE_WA + 1]
    wb = rec[:, ROUTE_WB:ROUTE_WB + 1]
    ya = jnp.concatenate([ybuf[slot, pl.ds(c, tc, stride=PAIR_ROWS), :] for c in range(ROW_TILE)], axis=1)
    yb = jnp.concatenate([ybuf[slot, pl.ds(ROW_TILE + c, tc, stride=PAIR_ROWS), :] for c in range(ROW_TILE)],
                         axis=1)
    z = x1_ref[...] + wa * ya + wb * yb
    ms = jnp.mean(z * z, axis=-1, keepdims=True)
    out_ref[...] = z * lax.rsqrt(ms + RMS_EPS) * gfin_ref[...]


def _combine(ys, pos, x1, route, g_final, *, tc):
    n, dm = x1.shape
    nb = n // tc
    grid_spec = pltpu.PrefetchScalarGridSpec(
        num_scalar_prefetch=0,
        grid=(nb,),
        in_specs=[
            pl.BlockSpec((1, 1, tc), lambda i: (i, 0, 0), memory_space=pltpu.SMEM),
            pl.BlockSpec((1, 1, tc), lambda i: (jnp.minimum(i + 1, nb - 1), 0, 0), memory_space=pltpu.SMEM),
            pl.BlockSpec(memory_space=pl.ANY),
            pl.BlockSpec((tc, dm), lambda i: (i, 0)),
            pl.BlockSpec((tc, LANES), lambda i: (i, 0)),
            pl.BlockSpec((1, dm), lambda i: (0, 0)),
        ],
        out_specs=pl.BlockSpec((tc, dm), lambda i: (i, 0)),
        scratch_shapes=[pltpu.VMEM((2, tc * PAIR_ROWS, LANES), F32), pltpu.SemaphoreType.DMA((2,))],
    )
    return pl.pallas_call(
        functools.partial(_combine_kernel, tc=tc),
        out_shape=jax.ShapeDtypeStruct((n, dm), F32),
        grid_spec=grid_spec,
        compiler_params=pltpu.CompilerParams(dimension_semantics=("arbitrary",), vmem_limit_bytes=VMEM_LIMIT),
        name="combine",
    )(pos.reshape(nb, 1, tc), pos.reshape(nb, 1, tc), ys, x1, route, g_final.reshape(1, dm))


def _slot_layout(route, counts, n, tme):
    cnt = counts[0, :N_CLASSES].astype(jnp.int32)
    tiles_per = (cnt + tme - 1) // tme
    tile_end = jnp.cumsum(tiles_per)
    seg_base = (tile_end - tiles_per) * tme
    ntiles = tile_end[-1]
    nt_max = n // tme + N_CLASSES
    tid = jnp.arange(nt_max, dtype=jnp.int32)
    live = jnp.minimum(tid, ntiles - 1)
    tile_class = jnp.sum((tile_end[None, :] <= live[:, None]).astype(jnp.int32), axis=1)
    tile_a = jnp.take(jnp.asarray(CLASS_EXPERT_A), tile_class)
    tile_b = jnp.take(jnp.asarray(CLASS_EXPERT_B), tile_class)
    first_tile = jnp.take(tile_end - tiles_per, tile_class)
    tile_rows = jnp.clip(jnp.take(cnt, tile_class) - (live - first_tile) * tme, 0, tme)
    cls = route[:, ROUTE_CLASS].astype(jnp.int32)
    rank = route[:, ROUTE_RANK].astype(jnp.int32)
    base = jnp.sum(jnp.where(cls[:, None] == jnp.arange(N_CLASSES, dtype=jnp.int32)[None, :], seg_base[None, :], 0),
                   axis=1)
    pos = base + rank
    zoff = jnp.maximum(tile_end - 1, 0) * tme
    return (pos, zoff.astype(jnp.int32), tile_a.astype(jnp.int32), tile_b.astype(jnp.int32),
            live.astype(jnp.int32), tile_rows.astype(jnp.int32), ntiles.reshape(1).astype(jnp.int32))


def kernel(x, g_mix, w_in, sinks, w_br_dil, w_br_swa, w_out, g_ffn, w_group, b_group, w_router, b_router,
           w_e_gate, w_e_up, w_e_down, g_final):
    b, t, dm = x.shape
    n = b * t
    tme = 256
    qkv0, swa, gates, qkv4, qkv16 = _inproj(x, g_mix[0], w_in[0], tm=512)
    o0, l0 = _dil_attention(qkv0.reshape(b, 1, t, QKV_W), 0, tq=1024)
    o1, l1 = _dil_attention(qkv4, 1, tq=1024)
    o2, l2 = _dil_attention(qkv16, 2, tq=512)
    ys = _swa_attention(swa, sinks[0], tq=1024)
    x1, h2, route, counts = _merge_route(
        x, o0.reshape(b, t, GROUP_W), l0.reshape(b, t, LANES), o1, l1, o2, l2, ys, gates,
        w_br_dil[0], w_br_swa[0], w_out[0], g_ffn[0], w_group[0], b_group[0], w_router[0], b_router[0], tm=512)
    route = route.reshape(n, LANES)
    pos, zoff, tile_a, tile_b, tile_block, tile_rows, ntiles = _slot_layout(route, counts, n, tme)
    nslot = n + N_CLASSES * tme
    xs = _dispatch(h2.reshape(n * ROW_TILE, LANES), pos, zoff, ntiles, nslot, td=4096, tme=tme)
    yslots = _experts(xs, tile_a, tile_b, tile_block, tile_rows, ntiles, w_e_gate[0], w_e_up[0], w_e_down[0],
                      tme=tme)
    out = _combine(yslots, pos, x1.reshape(n, dm), route, g_final, tc=512)
    return out.reshape(b, t, dm)
```

```python
import functools

import numpy as np
import jax
import jax.numpy as jnp
from jax import lax
from jax.experimental import pallas as pl
from jax.experimental.pallas import tpu as pltpu

F32 = jnp.float32
BF16 = jnp.bfloat16

HEAD_DIM = 64
BAND = 128
DIL_PATTERNS = ((128, 1), (512, 4), (2048, 16))
DIL_HEADS_PER_GROUP = 8
DIL_HEADS = 24
GROUP_W = DIL_HEADS_PER_GROUP * HEAD_DIM
QKV_W = 3 * GROUP_W
SWA_WINDOW = 128
SWA_Q_HEADS = 16
SWA_KV_HEADS = 2
SWA_Q_W = SWA_Q_HEADS * HEAD_DIM
SWA_W = SWA_Q_W + 4 * 128
MOE_GROUPS = 4
EXPERTS_PER_GROUP = 8
N_EXPERTS = 32
D_EXPERT = 512
RMS_EPS = 1e-6
LANES = 128
NEG_INF = float("-inf")
LOG2E = 1.4426950408889634
LN2 = 0.6931471805599453
ROW_TILE = 8

VMEM_LIMIT = 56 * 1024 * 1024


def _alibi_slopes(n):
    return (2.0 ** (-8.0 * np.arange(1, n + 1) / n)).astype(np.float32)


def _band_bias(slopes, max_back, unit):
    a = np.arange(BAND)[:, None]
    c = np.arange(2 * BAND)[None, :]
    delta = a + BAND - c
    band = (delta >= 0) & (delta <= max_back)
    pen = (-slopes[:, None, None] * (delta * unit)[None] * LOG2E).astype(np.float32)
    full = np.where(band[None], pen, -np.inf).astype(np.float32)
    first = np.where((band & (c >= BAND))[None], pen, -np.inf).astype(np.float32)
    return np.stack([full, first], axis=0)


def _inproj_kernel(x_ref, g_ref, w_ref, qkv0_ref, swa_ref, gates_ref, qkv4_ref, qkv16_ref, hs_ref, *, tm):
    x = x_ref[...]
    ms = jnp.mean(x * x, axis=-1, keepdims=True)
    h = x * lax.rsqrt(ms + RMS_EPS) * g_ref[...]
    nchunk = h.shape[1] // LANES
    for c in range(nchunk):
        hs_ref[c] = h[:, c * LANES:(c + 1) * LANES]
    hb = h.astype(BF16)

    def proj(lhs, c0, width):
        return jnp.dot(lhs, w_ref[:, c0:c0 + width], preferred_element_type=F32)

    for j in range(3):
        qkv0_ref[:, j * GROUP_W:(j + 1) * GROUP_W] = proj(hb, COL_QKV_D[j], GROUP_W).astype(BF16)
    for c0 in range(0, SWA_Q_W, 512):
        swa_ref[:, c0:c0 + 512] = proj(hb, COL_Q_S + c0, 512).astype(BF16)
    kv = proj(hb, COL_KV_S, 4 * HEAD_DIM)
    part = [kv[:, j * HEAD_DIM:(j + 1) * HEAD_DIM] for j in range(4)]
    swa_ref[:, SWA_Q_W:SWA_W] = jnp.concatenate(
        [part[0], part[0], part[1], part[1], part[2], part[2], part[3], part[3]], axis=1).astype(BF16)
    for c0 in range(0, 2048, 512):
        gates_ref[:, c0:c0 + 512] = proj(hb, COL_GATES + c0, 512).astype(BF16)
    for gi, out_ref, d in ((1, qkv4_ref, 4), (2, qkv16_ref, 16)):
        rows = tm // d
        hp = jnp.concatenate(
            [jnp.concatenate([hs_ref[c, pl.ds(r, rows, stride=d), :] for c in range(nchunk)], axis=1)
             for r in range(d)], axis=0).astype(BF16)
        for j in range(3):
            res = proj(hp, COL_QKV_D[j] + gi * GROUP_W, GROUP_W).astype(BF16)
            for r in range(d):
                out_ref[r, :, j * GROUP_W:(j + 1) * GROUP_W] = res[r * rows:(r + 1) * rows]


COL_QKV_D = (0, 1536, 3072)
COL_Q_S = 4608
COL_KV_S = 5632
COL_GATES = 5888
IN_WIDTH = 7936
Q_SCALE = HEAD_DIM ** -0.5 * LOG2E


def _prep_w_in(w_in):
    col = np.arange(IN_WIDTH)
    is_q = (col < COL_QKV_D[1]) | ((col >= COL_Q_S) & (col < COL_KV_S))
    colscale = np.where(is_q, Q_SCALE, 1.0).astype(np.float32)
    return (w_in * colscale[None, :]).astype(BF16)


def _inproj(x, g_mix, w_in, *, tm):
    b, t, dm = x.shape
    wp = _prep_w_in(w_in)
    nw = wp.shape[1]
    nt = t // tm
    out_shape = (
        jax.ShapeDtypeStruct((b, t, QKV_W), BF16),
        jax.ShapeDtypeStruct((b, t, SWA_W), BF16),
        jax.ShapeDtypeStruct((b, t, 2048), BF16),
        jax.ShapeDtypeStruct((b, 4, t // 4, QKV_W), BF16),
        jax.ShapeDtypeStruct((b, 16, t // 16, QKV_W), BF16),
    )
    return pl.pallas_call(
        functools.partial(_inproj_kernel, tm=tm),
        out_shape=out_shape,
        grid=(b, nt),
        in_specs=[
            pl.BlockSpec((None, tm, dm), lambda bi, i: (bi, i, 0)),
            pl.BlockSpec((1, dm), lambda bi, i: (0, 0)),
            pl.BlockSpec((dm, nw), lambda bi, i: (0, 0), pipeline_mode=pl.Buffered(1)),
        ],
        out_specs=(
            pl.BlockSpec((None, tm, QKV_W), lambda bi, i: (bi, i, 0)),
            pl.BlockSpec((None, tm, SWA_W), lambda bi, i: (bi, i, 0)),
            pl.BlockSpec((None, tm, 2048), lambda bi, i: (bi, i, 0)),
            pl.BlockSpec((None, 4, tm // 4, QKV_W), lambda bi, i: (bi, 0, i, 0)),
            pl.BlockSpec((None, 16, tm // 16, QKV_W), lambda bi, i: (bi, 0, i, 0)),
        ),
        scratch_shapes=[pltpu.VMEM((dm // LANES, tm, LANES), F32)],
        compiler_params=pltpu.CompilerParams(
            dimension_semantics=("arbitrary", "arbitrary"), vmem_limit_bytes=VMEM_LIMIT),
        name="inproj",
    )(x, g_mix.reshape(1, dm), wp)


def _dil_attn_kernel(q_ref, kc_ref, vc_ref, kp_ref, vp_ref, bias_ref, o_ref, lse_ref, kbuf, vbuf, *, tq):
    i = pl.program_id(2)
    kbuf[0:BAND] = kp_ref[...]
    kbuf[BAND:BAND + tq] = kc_ref[...]
    vbuf[0:BAND] = vp_ref[...]
    vbuf[BAND:BAND + tq] = vc_ref[...]
    lane = lax.broadcasted_iota(jnp.int32, (BAND, LANES), 1)
    lo = lane < HEAD_DIM
    zero = jnp.zeros((BAND, LANES), BF16)
    for qb in range(tq // BAND):
        sel = jnp.where(i == 0, 1, 0) if qb == 0 else 0
        rq = slice(qb * BAND, (qb + 1) * BAND)
        rk = slice(qb * BAND, qb * BAND + 2 * BAND)
        lse_parts = []
        for pr in range(GROUP_W // LANES):
            cl = slice(pr * LANES, (pr + 1) * LANES)
            q2 = q_ref[rq, cl]
            kk = kbuf[rk, cl]
            vv = vbuf[rk, cl]
            s_all = lax.dot_general(jnp.concatenate([jnp.where(lo, q2, zero), jnp.where(lo, zero, q2)], axis=0), kk,
                                    (((1,), (1,)), ((), ())), preferred_element_type=F32)
            probs, stats = [], []
            for hh in range(2):
                s = s_all[hh * BAND:(hh + 1) * BAND] + bias_ref[sel, 2 * pr + hh]
                m = jnp.max(s, axis=-1, keepdims=True)
                p = jnp.exp2(s - m)
                l = jnp.sum(p, axis=-1, keepdims=True)
                probs.append(p.astype(BF16))
                stats.append((1.0 / l, (m + jnp.log2(l)) * LN2))
            o_all = jnp.dot(jnp.concatenate(probs, axis=0), vv, preferred_element_type=F32)
            res = [(o_all[hh * BAND:(hh + 1) * BAND] * stats[hh][0], stats[hh][1]) for hh in range(2)]
            o_ref[rq, cl] = jnp.where(lo, res[0][0], res[1][0]).astype(o_ref.dtype)
            lse_parts.append(jnp.where(lane == 2 * pr, res[0][1], jnp.where(lane == 2 * pr + 1, res[1][1], 0.0)))
        lse_ref[rq, :] = (lse_parts[0] + lse_parts[1]) + (lse_parts[2] + lse_parts[3])


def _dil_attention(qkv, gi, *, tq):
    b, d, l, _ = qkv.shape
    window, dil = DIL_PATTERNS[gi]
    assert dil == d
    tq = min(tq, l)
    nq = tq // BAND
    slopes = _alibi_slopes(DIL_HEADS)[gi * 8:(gi + 1) * 8]
    bias = jnp.asarray(_band_bias(slopes, window // dil, dil))

    def cur(c):
        return pl.BlockSpec((None, None, tq, GROUP_W), lambda bi, r, i: (bi, r, i, c))

    def prev(c):
        return pl.BlockSpec((None, None, BAND, GROUP_W), lambda bi, r, i: (bi, r, jnp.maximum(i * nq - 1, 0), c))

    return pl.pallas_call(
        functools.partial(_dil_attn_kernel, tq=tq),
        out_shape=(jax.ShapeDtypeStruct((b, d, l, GROUP_W), BF16),
                   jax.ShapeDtypeStruct((b, d, l, LANES), F32)),
        grid=(b, d, l // tq),
        in_specs=[cur(0), cur(1), cur(2), prev(1), prev(2),
                  pl.BlockSpec((2, 8, BAND, 2 * BAND), lambda bi, r, i: (0, 0, 0, 0))],
        out_specs=(cur(0), pl.BlockSpec((None, None, tq, LANES), lambda bi, r, i: (bi, r, i, 0))),
        scratch_shapes=[pltpu.VMEM((BAND + tq, GROUP_W), BF16), pltpu.VMEM((BAND + tq, GROUP_W), BF16)],
        compiler_params=pltpu.CompilerParams(
            dimension_semantics=("arbitrary", "arbitrary", "arbitrary"), vmem_limit_bytes=VMEM_LIMIT),
        name=f"dil_attn_{gi}",
    )(qkv, qkv, qkv, qkv, qkv, bias)


def _swa_attn_kernel(sink_ref, qlo_ref, qhi_ref, kvc_ref, kvp_ref, bias_ref, o_ref, kvbuf, *, tq):
    i = pl.program_id(1)
    kvbuf[0:BAND] = kvp_ref[...]
    kvbuf[BAND:BAND + tq] = kvc_ref[...]
    lo = lax.broadcasted_iota(jnp.int32, (BAND, LANES), 1) < HEAD_DIM
    zero = jnp.zeros((BAND, LANES), BF16)
    for qb in range(tq // BAND):
        sel = jnp.where(i == 0, 1, 0) if qb == 0 else 0
        rq = slice(qb * BAND, (qb + 1) * BAND)
        rk = slice(qb * BAND, qb * BAND + 2 * BAND)
        for g in range(SWA_KV_HEADS):
            q_ref = qlo_ref if g == 0 else qhi_ref
            kk = kvbuf[rk, g * LANES:(g + 1) * LANES]
            vv = kvbuf[rk, (2 + g) * LANES:(3 + g) * LANES]
            masked = []
            for pp in range(4):
                q2 = q_ref[rq, pp * LANES:(pp + 1) * LANES]
                masked += [jnp.where(lo, q2, zero), jnp.where(lo, zero, q2)]
            s_all = lax.dot_general(jnp.concatenate(masked, axis=0), kk, (((1,), (1,)), ((), ())),
                                    preferred_element_type=F32)
            probs, inv = [], []
            for j in range(8):
                h = 8 * g + j
                sink = sink_ref[h] * LOG2E
                s = s_all[j * BAND:(j + 1) * BAND] + bias_ref[sel, h]
                m = jnp.maximum(jnp.max(s, axis=-1, keepdims=True), sink)
                p = jnp.exp2(s - m)
                inv.append(1.0 / (jnp.sum(p, axis=-1, keepdims=True) + jnp.exp2(sink - m)))
                probs.append(p.astype(BF16))
            o_all = jnp.dot(jnp.concatenate(probs, axis=0), vv, preferred_element_type=F32)
            for pp in range(4):
                oa = o_all[(2 * pp) * BAND:(2 * pp + 1) * BAND] * inv[2 * pp]
                ob = o_all[(2 * pp + 1) * BAND:(2 * pp + 2) * BAND] * inv[2 * pp + 1]
                o_ref[rq, (4 * g + pp) * LANES:(4 * g + pp + 1) * LANES] = jnp.where(lo, oa, ob).astype(o_ref.dtype)


def _swa_attention(swa, sinks, *, tq):
    b, t, _ = swa.shape
    tq = min(tq, t)
    nq = tq // BAND
    bias = jnp.asarray(_band_bias(_alibi_slopes(SWA_Q_HEADS), SWA_WINDOW - 1, 1))
    grid_spec = pltpu.PrefetchScalarGridSpec(
        num_scalar_prefetch=1,
        grid=(b, t // tq),
        in_specs=[
            pl.BlockSpec((None, tq, 512), lambda bi, i, s: (bi, i, 0)),
            pl.BlockSpec((None, tq, 512), lambda bi, i, s: (bi, i, 1)),
            pl.BlockSpec((None, tq, 512), lambda bi, i, s: (bi, i, 2)),
            pl.BlockSpec((None, BAND, 512), lambda bi, i, s: (bi, jnp.maximum(i * nq - 1, 0), 2)),
            pl.BlockSpec((2, SWA_Q_HEADS, BAND, 2 * BAND), lambda bi, i, s: (0, 0, 0, 0)),
        ],
        out_specs=pl.BlockSpec((None, tq, SWA_Q_W), lambda bi, i, s: (bi, i, 0)),
        scratch_shapes=[pltpu.VMEM((BAND + tq, 512), BF16)],
    )
    return pl.pallas_call(
        functools.partial(_swa_attn_kernel, tq=tq),
        out_shape=jax.ShapeDtypeStruct((b, t, SWA_Q_W), BF16),
        grid_spec=grid_spec,
        compiler_params=pltpu.CompilerParams(
            dimension_semantics=("arbitrary", "arbitrary"), vmem_limit_bytes=VMEM_LIMIT),
        name="swa_attn",
    )(sinks.astype(F32), swa, swa, swa, swa, bias)


ROUTE_CLASS, ROUTE_RANK, ROUTE_WA, ROUTE_WB = range(4)
ROUTER_LANE0 = MOE_GROUPS
PAIRS_PER_GROUP = EXPERTS_PER_GROUP * (EXPERTS_PER_GROUP - 1) // 2
N_CLASSES = MOE_GROUPS * PAIRS_PER_GROUP
_PAIRS = [(a, b) for a in range(EXPERTS_PER_GROUP) for b in range(a + 1, EXPERTS_PER_GROUP)]
CLASS_EXPERT_A = np.array([g * EXPERTS_PER_GROUP + a for g in range(MOE_GROUPS) for a, _ in _PAIRS], np.int32)
CLASS_EXPERT_B = np.array([g * EXPERTS_PER_GROUP + b for g in range(MOE_GROUPS) for _, b in _PAIRS], np.int32)


def _route(logits, carry_ref, ltri_ref, tm, active):
    lane = lax.broadcasted_iota(jnp.int32, (tm, LANES), 1)
    lanef = lane.astype(F32)

    def first_argmax(v):
        m = jnp.max(v, axis=-1, keepdims=True)
        return m, jnp.min(jnp.where(v == m, lanef, float(LANES)), axis=-1, keepdims=True)

    gl = jnp.where(lane < MOE_GROUPS, logits, NEG_INF)
    gmax, gidx = first_argmax(gl)
    g_w = 1.0 / jnp.sum(jnp.exp(gl - gmax), axis=-1, keepdims=True)
    e_lane = lane - ROUTER_LANE0
    lane_group = (e_lane >> 3).astype(F32)
    in_group = (e_lane >= 0) & (e_lane < N_EXPERTS) & (lane_group == gidx)
    el = jnp.where(in_group, logits, NEG_INF)
    m1, i1 = first_argmax(el)
    m2, i2 = first_argmax(jnp.where(lanef == i1, NEG_INF, el))
    tt = jnp.exp(m2 - m1)
    w1 = g_w / (1.0 + tt)
    w2 = g_w * tt / (1.0 + tt)
    first = float(ROUTER_LANE0) + float(EXPERTS_PER_GROUP) * gidx
    e1 = i1 - first
    e2 = i2 - first
    swap = e2 < e1
    ea = jnp.minimum(e1, e2)
    eb = jnp.maximum(e1, e2)
    wa = jnp.where(swap, w2, w1)
    wb = jnp.where(swap, w1, w2)
    pair = ea * (float(2 * EXPERTS_PER_GROUP - 1) - ea) * 0.5 + (eb - ea - 1.0)
    cls = float(PAIRS_PER_GROUP) * gidx + pair
    oh = jnp.where(lanef == cls, 1.0, 0.0)
    before = jnp.dot(ltri_ref[...], oh.astype(BF16), preferred_element_type=F32) + carry_ref[0:1, :]
    rank = jnp.sum(oh * before, axis=-1, keepdims=True)
    carry_ref[...] = carry_ref[...] + jnp.where(active, jnp.sum(oh, axis=0, keepdims=True), 0.0)
    rec = jnp.zeros((tm, LANES), F32)
    for ln, val in ((ROUTE_CLASS, cls), (ROUTE_RANK, rank), (ROUTE_WA, wa), (ROUTE_WB, wb)):
        rec = jnp.where(lane == ln, val, rec)
    return rec


def _merge_kernel(x_ref, o0_ref, l0_ref, o1_ref, l1_ref, o2_ref, l2_ref, ys_ref, gates_ref,
                  wbd_ref, wbs_ref, wo_ref, gffn_ref, wrt_ref, brt_ref, exp_ref, ltri_ref,
                  x1_ref, h2_ref, route_ref, cnt_ref,
                  uo1, ul1, uo2, ul2, carry_ref, logit_buf, *, tm):
    step = pl.program_id(0)

    @pl.when(step == 0)
    def _():
        carry_ref[...] = jnp.zeros_like(carry_ref)
        logit_buf[...] = jnp.zeros_like(logit_buf)

    route_ref[...] = _route(logit_buf[...], carry_ref, ltri_ref, tm, step > 0)
    cnt_ref[...] = carry_ref[...]

    nch = GROUP_W // LANES
    for src_o, src_l, dst_o, dst_l, d in ((o1_ref, l1_ref, uo1, ul1, 4), (o2_ref, l2_ref, uo2, ul2, 16)):
        rows = tm // d
        for r in range(d):
            dst_l[pl.ds(r, rows, stride=d), :] = src_l[r]
            for c in range(nch):
                dst_o[c, pl.ds(r, rows, stride=d), :] = src_o[r, :, c * LANES:(c + 1) * LANES].astype(F32)
    def per_head_to_lanes(w):
        hi = w.astype(BF16)
        lo = (w - hi.astype(F32)).astype(BF16)
        return jnp.dot(jnp.concatenate([hi, lo], axis=1), exp_ref[...], preferred_element_type=F32)

    def sig(v):
        return 0.5 * jnp.tanh(0.5 * v) + 0.5

    def rows_to_logits(r0, nr):
        rs = slice(r0, r0 + nr)
        la, lb, lc = l0_ref[rs, :], ul1[rs, :], ul2[rs, :]
        mx = jnp.maximum(jnp.maximum(la, lb), lc)
        ea, eb, ec = jnp.exp(la - mx), jnp.exp(lb - mx), jnp.exp(lc - mx)
        inv = 1.0 / (ea + eb + ec)
        wa, wb, wc = per_head_to_lanes(ea * inv), per_head_to_lanes(eb * inv), per_head_to_lanes(ec * inv)
        ych = []
        for c in range(nch):
            cl = slice(c * LANES, (c + 1) * LANES)
            ych.append((wa[:, cl] * o0_ref[rs, cl].astype(F32) + wb[:, cl] * uo1[c, rs, :]
                        + wc[:, cl] * uo2[c, rs, :]).astype(BF16))
        y = jnp.concatenate(ych, axis=1)
        a = jnp.dot(y, wbd_ref[...], preferred_element_type=F32)
        bsw = jnp.dot(ys_ref[rs, :], wbs_ref[...], preferred_element_type=F32)
        mixed = sig(gates_ref[rs, 0:1024]) * a.astype(BF16) + sig(gates_ref[rs, 1024:2048]) * bsw.astype(BF16)
        x1 = x_ref[rs, :] + jnp.dot(mixed, wo_ref[...], preferred_element_type=F32)
        x1_ref[rs, :] = x1
        ms = jnp.mean(x1 * x1, axis=-1, keepdims=True)
        h2 = x1 * lax.rsqrt(ms + RMS_EPS) * gffn_ref[...]
        for c in range(h2.shape[1] // LANES):
            h2_ref[pl.ds(r0 * ROW_TILE + c, nr, stride=ROW_TILE), :] = h2[:, c * LANES:(c + 1) * LANES]
        h_hi = h2.astype(BF16)
        h_lo = (h2 - h_hi.astype(F32)).astype(BF16)
        hw = jnp.dot(h_hi, wrt_ref[...], preferred_element_type=F32)
        return (hw[:, :LANES] + hw[:, LANES:]
                + jnp.dot(h_lo, wrt_ref[:, :LANES], preferred_element_type=F32) + brt_ref[...])

    logit_buf[...] = rows_to_logits(0, tm)


def _merge_route(x, o0, l0, o1, l1, o2, l2, ys, gates, w_br_dil, w_br_swa, w_out, g_ffn,
                 w_group, b_group, w_router, b_router, *, tm):
    b, t, dm = x.shape
    nt = t // tm
    wrt = jnp.zeros((dm, LANES), F32).at[:, :MOE_GROUPS].set(w_group).at[
        :, ROUTER_LANE0:ROUTER_LANE0 + N_EXPERTS].set(w_router)
    brt = jnp.zeros((1, LANES), F32).at[0, :MOE_GROUPS].set(b_group).at[
        0, ROUTER_LANE0:ROUTER_LANE0 + N_EXPERTS].set(b_router)
    wrt_hi = wrt.astype(BF16)
    wrt_lo = (wrt - wrt_hi.astype(F32)).astype(BF16)
    wrt = jnp.concatenate([wrt_hi, wrt_lo], axis=1)

    nsteps = b * nt + 1

    def tile_of(s, lag=0):
        tile = jnp.clip(s - lag, 0, nsteps - 2)
        return tile // nt, tile % nt

    def tok(width, lag=0):
        return pl.BlockSpec((None, tm, width), lambda s: (*tile_of(s, lag), 0))

    def stream(d, width):
        return pl.BlockSpec((None, d, tm // d, width), lambda s: (tile_of(s)[0], 0, tile_of(s)[1], 0))

    def const(shape):
        return pl.BlockSpec(shape, lambda s: (0,) * len(shape), pipeline_mode=pl.Buffered(1))

    head_of_lane = np.arange(GROUP_W) // HEAD_DIM
    spread = (np.arange(LANES)[:, None] == head_of_lane[None, :]).astype(np.float32)
    expand = jnp.asarray(np.concatenate([spread, spread], axis=0), dtype=BF16)

    nch = GROUP_W // LANES
    return pl.pallas_call(
        functools.partial(_merge_kernel, tm=tm),
        out_shape=(jax.ShapeDtypeStruct((b, t, dm), F32), jax.ShapeDtypeStruct((b, t * ROW_TILE, LANES), F32),
                   jax.ShapeDtypeStruct((b, t, LANES), F32), jax.ShapeDtypeStruct((8, LANES), F32)),
        grid=(nsteps,),
        in_specs=[tok(dm), tok(GROUP_W), tok(LANES), stream(4, GROUP_W), stream(4, LANES),
                  stream(16, GROUP_W), stream(16, LANES), tok(SWA_Q_W), tok(2048),
                  const((GROUP_W, dm)), const((SWA_Q_W, dm)), const((dm, dm)), const((1, dm)),
                  const((dm, 2 * LANES)), const((1, LANES)), const((2 * LANES, GROUP_W)), const((tm, tm))],
        out_specs=(tok(dm), pl.BlockSpec((None, tm * ROW_TILE, LANES), lambda s: (*tile_of(s), 0)),
                   tok(LANES, lag=1), pl.BlockSpec((8, LANES), lambda s: (0, 0))),
        scratch_shapes=[pltpu.VMEM((nch, tm, LANES), F32), pltpu.VMEM((tm, LANES), F32),
                        pltpu.VMEM((nch, tm, LANES), F32), pltpu.VMEM((tm, LANES), F32),
                        pltpu.VMEM((8, LANES), F32), pltpu.VMEM((tm, LANES), F32)],
        compiler_params=pltpu.CompilerParams(
            dimension_semantics=("arbitrary",), vmem_limit_bytes=VMEM_LIMIT),
        name="merge_route",
    )(x, o0, l0, o1, l1, o2, l2, ys, gates, w_br_dil.astype(BF16), w_br_swa.astype(BF16), w_out.astype(BF16),
      g_ffn.reshape(1, dm), wrt, brt, expand, jnp.asarray(np.tril(np.ones((tm, tm), np.float32), -1), dtype=BF16))


def _dispatch_kernel(pos_ref, zoff_ref, nt_ref, h_ref, xs_ref, zbuf, sem, zsem, *, td, tme, nt_max):
    step = pl.program_id(0)

    def _zero_copy(e):
        off = pl.multiple_of(zoff_ref[e] * ROW_TILE, tme * ROW_TILE)
        return pltpu.make_async_copy(zbuf, xs_ref.at[pl.ds(off, tme * ROW_TILE)], zsem)

    def _row_copy(src_row, dst_row):
        return pltpu.make_async_copy(h_ref.at[pl.ds(pl.multiple_of(src_row * ROW_TILE, ROW_TILE), ROW_TILE)],
                                     xs_ref.at[pl.ds(pl.multiple_of(dst_row * ROW_TILE, ROW_TILE), ROW_TILE)], sem)

    def _wait_rows():
        pltpu.make_async_copy(h_ref, xs_ref.at[pl.ds(0, td * ROW_TILE)], sem).wait()

    def _tail_copy(tile):
        off = pl.multiple_of(tile * (tme * ROW_TILE), tme * ROW_TILE)
        return pltpu.make_async_copy(zbuf, xs_ref.at[pl.ds(off, tme * ROW_TILE)], zsem)

    @pl.when(step == 0)
    def _():
        zbuf[...] = jnp.zeros_like(zbuf)
        for e in range(N_CLASSES):
            _zero_copy(e).start()
        for e in range(N_CLASSES):
            _zero_copy(e).wait()

        def tail(tile, carry):
            _tail_copy(tile).start()
            _tail_copy(tile).wait()
            return carry

        lax.fori_loop(nt_ref[0], nt_max, tail, 0)

    def issue(j, carry):
        _row_copy(j, pos_ref[0, 0, j]).start()
        return carry

    lax.fori_loop(0, td, issue, 0, unroll=8)
    _wait_rows()


def _dispatch(h2, pos, zoff, ntiles, nslot, *, td, tme):
    n = h2.shape[0] // ROW_TILE
    nb = n // td
    grid_spec = pltpu.PrefetchScalarGridSpec(
        num_scalar_prefetch=0,
        grid=(nb,),
        in_specs=[
            pl.BlockSpec((1, 1, td), lambda i: (i, 0, 0), memory_space=pltpu.SMEM),
            pl.BlockSpec(memory_space=pltpu.SMEM),
            pl.BlockSpec(memory_space=pltpu.SMEM),
            pl.BlockSpec((td * ROW_TILE, LANES), lambda i: (i, 0)),
        ],
        out_specs=pl.BlockSpec(memory_space=pl.ANY),
        scratch_shapes=[pltpu.VMEM((tme * ROW_TILE, LANES), F32), pltpu.SemaphoreType.DMA(()),
                        pltpu.SemaphoreType.DMA(())],
    )
    return pl.pallas_call(
        functools.partial(_dispatch_kernel, td=td, tme=tme, nt_max=nslot // tme),
        out_shape=jax.ShapeDtypeStruct((nslot * ROW_TILE, LANES), F32),
        grid_spec=grid_spec,
        compiler_params=pltpu.CompilerParams(dimension_semantics=("arbitrary",), has_side_effects=True,
                                             vmem_limit_bytes=VMEM_LIMIT),
        name="dispatch",
    )(pos.reshape(nb, 1, td), zoff, ntiles, h2)


PAIR_ROWS = 2 * ROW_TILE


PLAN_FIRST, PLAN_SLOT, PLAN_NEXT, PLAN_HAS_NEXT = range(4)


def _expert_kernel(ta_ref, tb_ref, tblk_ref, trows_ref, plan_ref, nt_ref, x_ref, wg_hbm, wu_hbm, wd_hbm, y_ref,
                   wg_stage, wu_stage, wd_stage, wg_buf, wu_buf, wd_buf, wsem, *, tme):
    i = pl.program_id(0)
    live = i < nt_ref[0]
    half_rows = tme // 2
    pairs = ((wg_hbm, wg_stage, wg_buf), (wu_hbm, wu_stage, wu_buf), (wd_hbm, wd_stage, wd_buf))

    def weight_copies(which, expert, slot):
        return [pltpu.make_async_copy(src.at[expert], stage.at[which, slot], wsem.at[which, slot])
                for src, stage, _ in pairs]

    for which, t_ref in ((0, ta_ref), (1, tb_ref)):
        base = (i * 2 + which) * 4
        first = plan_ref[base + PLAN_FIRST] == 1
        slot = plan_ref[base + PLAN_SLOT]

        @pl.when(first & (i == 0))
        def _():
            for cp in weight_copies(which, t_ref[i], slot):
                cp.start()

        @pl.when(first)
        def _():
            for cp in weight_copies(which, t_ref[i], slot):
                cp.wait()
            for _, stage, buf in pairs:
                buf[which] = stage[which, slot].astype(BF16)

        @pl.when(first & (plan_ref[base + PLAN_HAS_NEXT] == 1))
        def _():
            for cp in weight_copies(which, plan_ref[base + PLAN_NEXT], 1 - slot):
                cp.start()

    def run(nrows):
        xb = jnp.concatenate([x_ref[pl.ds(c, nrows, stride=ROW_TILE), :] for c in range(ROW_TILE)],
                             axis=1).astype(BF16)
        nw = 2 * LANES
        for half in range(2):
            wg_ref, wu_ref, wd_ref = (buf.at[half] for buf in (wg_buf, wu_buf, wd_buf))
            act = []
            for c0 in range(0, D_EXPERT, nw):
                g = jnp.dot(xb, wg_ref[:, c0:c0 + nw], preferred_element_type=F32)
                u = jnp.dot(xb, wu_ref[:, c0:c0 + nw], preferred_element_type=F32)
                act.append((g * jax.nn.sigmoid(g) * u).astype(BF16))
            a = jnp.concatenate(act, axis=1)
            for c0 in range(0, ROW_TILE * LANES, nw):
                y = jnp.dot(a, wd_ref[:, c0:c0 + nw], preferred_element_type=F32)
                for k in range(nw // LANES):
                    c = c0 // LANES + k
                    y_ref[pl.ds(half * ROW_TILE + c, nrows, stride=PAIR_ROWS), :] = y[:, k * LANES:(k + 1) * LANES]

    @pl.when(live & (trows_ref[i] > half_rows))
    def _():
        run(tme)

    @pl.when(live & (trows_ref[i] <= half_rows))
    def _():
        run(half_rows)
        y_ref[half_rows * PAIR_ROWS:, :] = jnp.zeros((half_rows * PAIR_ROWS, LANES), F32)

    @pl.when(jnp.logical_not(live))
    def _():
        y_ref[...] = jnp.zeros_like(y_ref)


def _weight_plan(tile_expert):
    nt = tile_expert.shape[0]
    first = jnp.concatenate([jnp.ones((1,), jnp.int32), (tile_expert[1:] != tile_expert[:-1]).astype(jnp.int32)])
    run_id = jnp.cumsum(first) - 1
    next_start = jnp.sum((run_id[None, :] <= run_id[:, None]).astype(jnp.int32), axis=1)
    has_next = (run_id < run_id[-1]).astype(jnp.int32)
    nxt = jnp.take(tile_expert, jnp.minimum(next_start, nt - 1))
    return jnp.stack([first, run_id % 2, nxt, has_next], axis=1).astype(jnp.int32)


def _experts(xs, tile_a, tile_b, tile_block, tile_rows, ntiles, w_e_gate, w_e_up, w_e_down, *, tme):
    nslot = xs.shape[0] // ROW_TILE
    dm = ROW_TILE * LANES
    nt = nslot // tme
    wg, wu, wd = w_e_gate, w_e_up, w_e_down
    plan =jnp.concatenate([_weight_plan(tile_a), _weight_plan(tile_b)], axis=1).reshape(-1)

    grid_spec = pltpu.PrefetchScalarGridSpec(
        num_scalar_prefetch=6,
        grid=(nt,),
        in_specs=[
            pl.BlockSpec((tme * ROW_TILE, LANES), lambda i, ta, tb, tk, tr, pn, n: (tk[i], 0)),
            pl.BlockSpec(memory_space=pl.ANY), pl.BlockSpec(memory_space=pl.ANY), pl.BlockSpec(memory_space=pl.ANY),
        ],
        out_specs=pl.BlockSpec((tme * PAIR_ROWS, LANES), lambda i, ta, tb, tk, tr, pn, n: (i, 0)),
        scratch_shapes=[pltpu.VMEM((2, 2, dm, D_EXPERT), F32), pltpu.VMEM((2, 2, dm, D_EXPERT), F32),
                        pltpu.VMEM((2, 2, D_EXPERT, dm), F32),
                        pltpu.VMEM((2, dm, D_EXPERT), BF16), pltpu.VMEM((2, dm, D_EXPERT), BF16),
                        pltpu.VMEM((2, D_EXPERT, dm), BF16), pltpu.SemaphoreType.DMA((2, 2))],
    )
    return pl.pallas_call(
        functools.partial(_expert_kernel, tme=tme),
        out_shape=jax.ShapeDtypeStruct((nslot * PAIR_ROWS, LANES), F32),
        grid_spec=grid_spec,
        compiler_params=pltpu.CompilerParams(dimension_semantics=("arbitrary",), vmem_limit_bytes=VMEM_LIMIT),
        name="experts",
    )(tile_a, tile_b, tile_block, tile_rows, plan, ntiles, xs, wg, wu, wd)


def _combine_kernel(pos_ref, posn_ref, ys_ref, x1_ref, route_ref, gfin_ref, out_ref, ybuf, sem, *, tc):
    i = pl.program_id(0)
    nb = pl.num_programs(0)
    slot = i % 2

    def gather(p_ref, s):
        def issue(j, carry):
            src = pl.multiple_of(p_ref[0, 0, j] * PAIR_ROWS, PAIR_ROWS)
            dst = pl.multiple_of(j * PAIR_ROWS, PAIR_ROWS)
            pltpu.make_async_copy(ys_ref.at[pl.ds(src, PAIR_ROWS)], ybuf.at[s, pl.ds(dst, PAIR_ROWS)],
                                  sem.at[s]).start()
            return carry

        lax.fori_loop(0, tc, issue, 0, unroll=8)

    @pl.when(i == 0)
    def _():
        gather(pos_ref, 0)

    @pl.when(i + 1 < nb)
    def _():
        gather(posn_ref, 1 - slot)

    pltpu.make_async_copy(ys_ref.at[pl.ds(0, tc * PAIR_ROWS)], ybuf.at[slot], sem.at[slot]).wait()
    rec = route_ref[...]
    wa = rec[:, ROUTE_WA:ROUTE_WA + 1]
    wb = rec[:, ROUTE_WB:ROUTE_WB + 1]
    ya = jnp.concatenate([ybuf[slot, pl.ds(c, tc, stride=PAIR_ROWS), :] for c in range(ROW_TILE)], axis=1)
    yb = jnp.concatenate([ybuf[slot, pl.ds(ROW_TILE + c, tc, stride=PAIR_ROWS), :] for c in range(ROW_TILE)],
                         axis=1)
    z = x1_ref[...] + wa * ya + wb * yb
    ms = jnp.mean(z * z, axis=-1, keepdims=True)
    out_ref[...] = z * lax.rsqrt(ms + RMS_EPS) * gfin_ref[...]


def _combine(ys, pos, x1, route, g_final, *, tc):
    n, dm = x1.shape
    nb = n // tc
    grid_spec = pltpu.PrefetchScalarGridSpec(
        num_scalar_prefetch=0,
        grid=(nb,),
        in_specs=[
            pl.BlockSpec((1, 1, tc), lambda i: (i, 0, 0), memory_space=pltpu.SMEM),
            pl.BlockSpec((1, 1, tc), lambda i: (jnp.minimum(i + 1, nb - 1), 0, 0), memory_space=pltpu.SMEM),
            pl.BlockSpec(memory_space=pl.ANY),
            pl.BlockSpec((tc, dm), lambda i: (i, 0)),
            pl.BlockSpec((tc, LANES), lambda i: (i, 0)),
            pl.BlockSpec((1, dm), lambda i: (0, 0)),
        ],
        out_specs=pl.BlockSpec((tc, dm), lambda i: (i, 0)),
        scratch_shapes=[pltpu.VMEM((2, tc * PAIR_ROWS, LANES), F32), pltpu.SemaphoreType.DMA((2,))],
    )
    return pl.pallas_call(
        functools.partial(_combine_kernel, tc=tc),
        out_shape=jax.ShapeDtypeStruct((n, dm), F32),
        grid_spec=grid_spec,
        compiler_params=pltpu.CompilerParams(dimension_semantics=("arbitrary",), vmem_limit_bytes=VMEM_LIMIT),
        name="combine",
    )(pos.reshape(nb, 1, tc), pos.reshape(nb, 1, tc), ys, x1, route, g_final.reshape(1, dm))


def _slot_layout(route, counts, n, tme):
    cnt = counts[0, :N_CLASSES].astype(jnp.int32)
    tiles_per = (cnt + tme - 1) // tme
    tile_end = jnp.cumsum(tiles_per)
    seg_base = (tile_end - tiles_per) * tme
    ntiles = tile_end[-1]
    nt_max = n // tme + N_CLASSES
    tid = jnp.arange(nt_max, dtype=jnp.int32)
    live = jnp.minimum(tid, ntiles - 1)
    tile_class = jnp.sum((tile_end[None, :] <= live[:, None]).astype(jnp.int32), axis=1)
    tile_a = jnp.take(jnp.asarray(CLASS_EXPERT_A), tile_class)
    tile_b = jnp.take(jnp.asarray(CLASS_EXPERT_B), tile_class)
    first_tile = jnp.take(tile_end - tiles_per, tile_class)
    tile_rows = jnp.clip(jnp.take(cnt, tile_class) - (live - first_tile) * tme, 0, tme)
    cls = route[:, ROUTE_CLASS].astype(jnp.int32)
    rank = route[:, ROUTE_RANK].astype(jnp.int32)
    base = jnp.sum(jnp.where(cls[:, None] == jnp.arange(N_CLASSES, dtype=jnp.int32)[None, :], seg_base[None, :], 0),
                   axis=1)
    pos = base + rank
    zoff = jnp.maximum(tile_end - 1, 0) * tme
    return (pos, zoff.astype(jnp.int32), tile_a.astype(jnp.int32), tile_b.astype(jnp.int32),
            live.astype(jnp.int32), tile_rows.astype(jnp.int32), ntiles.reshape(1).astype(jnp.int32))


def kernel(x, g_mix, w_in, sinks, w_br_dil, w_br_swa, w_out, g_ffn, w_group, b_group, w_router, b_router,
           w_e_gate, w_e_up, w_e_down, g_final):
    b, t, dm = x.shape
    n = b * t
    tme = 256
    qkv0, swa, gates, qkv4, qkv16 = _inproj(x, g_mix[0], w_in[0], tm=512)
    o0, l0 = _dil_attention(qkv0.reshape(b, 1, t, QKV_W), 0, tq=1024)
    o1, l1 = _dil_attention(qkv4, 1, tq=1024)
    o2, l2 = _dil_attention(qkv16, 2, tq=512)
    ys = _swa_attention(swa, sinks[0], tq=1024)
    x1, h2, route, counts = _merge_route(
        x, o0.reshape(b, t, GROUP_W), l0.reshape(b, t, LANES), o1, l1, o2, l2, ys, gates,
        w_br_dil[0], w_br_swa[0], w_out[0], g_ffn[0], w_group[0], b_group[0], w_router[0], b_router[0], tm=512)
    route = route.reshape(n, LANES)
    pos, zoff, tile_a, tile_b, tile_block, tile_rows, ntiles = _slot_layout(route, counts, n, tme)
    nslot = n + N_CLASSES * tme
    xs = _dispatch(h2.reshape(n * ROW_TILE, LANES), pos, zoff, ntiles, nslot, td=4096, tme=tme)
    yslots = _experts(xs, tile_a, tile_b, tile_block, tile_rows, ntiles, w_e_gate[0], w_e_up[0], w_e_down[0],
                      tme=tme)
    out = _combine(yslots, pos, x1.reshape(n, dm), route, g_final, tc=512)
    return out.reshape(b, t, dm)
```

```python
import functools

import numpy as np
import jax
import jax.numpy as jnp
from jax import lax
from jax.experimental import pallas as pl
from jax.experimental.pallas import tpu as pltpu

F32 = jnp.float32
BF16 = jnp.bfloat16

HEAD_DIM = 64
BAND = 128
DIL_PATTERNS = ((128, 1), (512, 4), (2048, 16))
DIL_HEADS_PER_GROUP = 8
DIL_HEADS = 24
GROUP_W = DIL_HEADS_PER_GROUP * HEAD_DIM
QKV_W = 3 * GROUP_W
SWA_WINDOW = 128
SWA_Q_HEADS = 16
SWA_KV_HEADS = 2
SWA_Q_W = SWA_Q_HEADS * HEAD_DIM
SWA_W = SWA_Q_W + 4 * 128
MOE_GROUPS = 4
EXPERTS_PER_GROUP = 8
N_EXPERTS = 32
D_EXPERT = 512
RMS_EPS = 1e-6
LANES = 128
NEG_INF = float("-inf")
LOG2E = 1.4426950408889634
LN2 = 0.6931471805599453
ROW_TILE = 8

VMEM_LIMIT = 56 * 1024 * 1024


def _alibi_slopes(n):
    return (2.0 ** (-8.0 * np.arange(1, n + 1) / n)).astype(np.float32)


def _band_bias(slopes, max_back, unit):
    a = np.arange(BAND)[:, None]
    c = np.arange(2 * BAND)[None, :]
    delta = a + BAND - c
    band = (delta >= 0) & (delta <= max_back)
    pen = (-slopes[:, None, None] * (delta * unit)[None] * LOG2E).astype(np.float32)
    full = np.where(band[None], pen, -np.inf).astype(np.float32)
    first = np.where((band & (c >= BAND))[None], pen, -np.inf).astype(np.float32)
    return np.stack([full, first], axis=0)


def _inproj_kernel(x_ref, g_ref, w_ref, qkv0_ref, swa_ref, gates_ref, qkv4_ref, qkv16_ref, hs_ref, *, tm):
    x = x_ref[...]
    ms = jnp.mean(x * x, axis=-1, keepdims=True)
    h = x * lax.rsqrt(ms + RMS_EPS) * g_ref[...]
    nchunk = h.shape[1] // LANES
    for c in range(nchunk):
        hs_ref[c] = h[:, c * LANES:(c + 1) * LANES]
    hb = h.astype(BF16)

    def proj(lhs, c0, width):
        return jnp.dot(lhs, w_ref[:, c0:c0 + width], preferred_element_type=F32)

    for j in range(3):
        qkv0_ref[:, j * GROUP_W:(j + 1) * GROUP_W] = proj(hb, COL_QKV_D[j], GROUP_W).astype(BF16)
    for c0 in range(0, SWA_Q_W, 512):
        swa_ref[:, c0:c0 + 512] = proj(hb, COL_Q_S + c0, 512).astype(BF16)
    kv = proj(hb, COL_KV_S, 4 * HEAD_DIM)
    part = [kv[:, j * HEAD_DIM:(j + 1) * HEAD_DIM] for j in range(4)]
    swa_ref[:, SWA_Q_W:SWA_W] = jnp.concatenate(
        [part[0], part[0], part[1], part[1], part[2], part[2], part[3], part[3]], axis=1).astype(BF16)
    for c0 in range(0, 2048, 512):
        gates_ref[:, c0:c0 + 512] = proj(hb, COL_GATES + c0, 512).astype(BF16)
    for gi, out_ref, d in ((1, qkv4_ref, 4), (2, qkv16_ref, 16)):
        rows = tm // d
        hp = jnp.concatenate(
            [jnp.concatenate([hs_ref[c, pl.ds(r, rows, stride=d), :] for c in range(nchunk)], axis=1)
             for r in range(d)], axis=0).astype(BF16)
        for j in range(3):
            res = proj(hp, COL_QKV_D[j] + gi * GROUP_W, GROUP_W).astype(BF16)
            for r in range(d):
                out_ref[r, :, j * GROUP_W:(j + 1) * GROUP_W] = res[r * rows:(r + 1) * rows]


COL_QKV_D = (0, 1536, 3072)
COL_Q_S = 4608
COL_KV_S = 5632
COL_GATES = 5888
IN_WIDTH = 7936
Q_SCALE = HEAD_DIM ** -0.5 * LOG2E


def _prep_w_in(w_in):
    col = np.arange(IN_WIDTH)
    is_q = (col < COL_QKV_D[1]) | ((col >= COL_Q_S) & (col < COL_KV_S))
    colscale = np.where(is_q, Q_SCALE, 1.0).astype(np.float32)
    return (w_in * colscale[None, :]).astype(BF16)


def _inproj(x, g_mix, w_in, *, tm):
    b, t, dm = x.shape
    wp = _prep_w_in(w_in)
    nw = wp.shape[1]
    nt = t // tm
    out_shape = (
        jax.ShapeDtypeStruct((b, t, QKV_W), BF16),
        jax.ShapeDtypeStruct((b, t, SWA_W), BF16),
        jax.ShapeDtypeStruct((b, t, 2048), BF16),
        jax.ShapeDtypeStruct((b, 4, t // 4, QKV_W), BF16),
        jax.ShapeDtypeStruct((b, 16, t // 16, QKV_W), BF16),
    )
    return pl.pallas_call(
        functools.partial(_inproj_kernel, tm=tm),
        out_shape=out_shape,
        grid=(b, nt),
        in_specs=[
            pl.BlockSpec((None, tm, dm), lambda bi, i: (bi, i, 0)),
            pl.BlockSpec((1, dm), lambda bi, i: (0, 0)),
            pl.BlockSpec((dm, nw), lambda bi, i: (0, 0), pipeline_mode=pl.Buffered(1)),
        ],
        out_specs=(
            pl.BlockSpec((None, tm, QKV_W), lambda bi, i: (bi, i, 0)),
            pl.BlockSpec((None, tm, SWA_W), lambda bi, i: (bi, i, 0)),
            pl.BlockSpec((None, tm, 2048), lambda bi, i: (bi, i, 0)),
            pl.BlockSpec((None, 4, tm // 4, QKV_W), lambda bi, i: (bi, 0, i, 0)),
            pl.BlockSpec((None, 16, tm // 16, QKV_W), lambda bi, i: (bi, 0, i, 0)),
        ),
        scratch_shapes=[pltpu.VMEM((dm // LANES, tm, LANES), F32)],
        compiler_params=pltpu.CompilerParams(
            dimension_semantics=("arbitrary", "arbitrary"), vmem_limit_bytes=VMEM_LIMIT),
        name="inproj",
    )(x, g_mix.reshape(1, dm), wp)


def _dil_attn_kernel(q_all, kc_all, vc_all, kp_all, vp_all, bias_ref, o_all_ref, lse_all_ref, kbuf_all, vbuf_all,
                     *, tq, sb):
    for st in range(sb):
        _dil_attn_stream(q_all.at[st], kc_all.at[st], vc_all.at[st], kp_all.at[st], vp_all.at[st], bias_ref,
                         o_all_ref.at[st], lse_all_ref.at[st], kbuf_all.at[st], vbuf_all.at[st], tq=tq)


def _dil_attn_stream(q_ref, kc_ref, vc_ref, kp_ref, vp_ref, bias_ref, o_ref, lse_ref, kbuf, vbuf, *, tq):
    i = pl.program_id(2)
    kbuf[0:BAND] = kp_ref[...]
    kbuf[BAND:BAND + tq] = kc_ref[...]
    vbuf[0:BAND] = vp_ref[...]
    vbuf[BAND:BAND + tq] = vc_ref[...]
    lane = lax.broadcasted_iota(jnp.int32, (BAND, LANES), 1)
    lo = lane < HEAD_DIM
    zero = jnp.zeros((BAND, LANES), BF16)
    for qb in range(tq // BAND):
        sel = jnp.where(i == 0, 1, 0) if qb == 0 else 0
        rq = slice(qb * BAND, (qb + 1) * BAND)
        rk = slice(qb * BAND, qb * BAND + 2 * BAND)
        lse_parts = []
        for pr in range(GROUP_W // LANES):
            cl = slice(pr * LANES, (pr + 1) * LANES)
            q2 = q_ref[rq, cl]
            kk = kbuf[rk, cl]
            vv = vbuf[rk, cl]
            s_all = lax.dot_general(jnp.concatenate([jnp.where(lo, q2, zero), jnp.where(lo, zero, q2)], axis=0), kk,
                                    (((1,), (1,)), ((), ())), preferred_element_type=F32)
            probs, stats = [], []
            for hh in range(2):
                s = s_all[hh * BAND:(hh + 1) * BAND] + bias_ref[sel, 2 * pr + hh]
                m = jnp.max(s, axis=-1, keepdims=True)
                p = jnp.exp2(s - m)
                l = jnp.sum(p, axis=-1, keepdims=True)
                probs.append(p.astype(BF16))
                stats.append((1.0 / l, (m + jnp.log2(l)) * LN2))
            o_all = jnp.dot(jnp.concatenate(probs, axis=0), vv, preferred_element_type=F32)
            res = [(o_all[hh * BAND:(hh + 1) * BAND] * stats[hh][0], stats[hh][1]) for hh in range(2)]
            o_ref[rq, cl] = jnp.where(lo, res[0][0], res[1][0]).astype(o_ref.dtype)
            lse_parts.append(jnp.where(lane == 2 * pr, res[0][1], jnp.where(lane == 2 * pr + 1, res[1][1], 0.0)))
        lse_ref[rq, :] = (lse_parts[0] + lse_parts[1]) + (lse_parts[2] + lse_parts[3])


def _dil_attention(qkv, gi, *, tq):
    b, d, l, _ = qkv.shape
    window, dil = DIL_PATTERNS[gi]
    assert dil == d
    rows_per_step = tq
    tq = min(tq, l)
    sb = min(d, max(1, rows_per_step // tq))
    nq = tq // BAND
    slopes = _alibi_slopes(DIL_HEADS)[gi * 8:(gi + 1) * 8]
    bias = jnp.asarray(_band_bias(slopes, window // dil, dil))

    def cur(c):
        return pl.BlockSpec((None, sb, tq, GROUP_W), lambda bi, r, i: (bi, r, i, c))

    def prev(c):
        return pl.BlockSpec((None, sb, BAND, GROUP_W), lambda bi, r, i: (bi, r, jnp.maximum(i * nq - 1, 0), c))

    return pl.pallas_call(
        functools.partial(_dil_attn_kernel, tq=tq, sb=sb),
        out_shape=(jax.ShapeDtypeStruct((b, d, l, GROUP_W), BF16),
                   jax.ShapeDtypeStruct((b, d, l, LANES), F32)),
        grid=(b, d // sb, l // tq),
        in_specs=[cur(0), cur(1), cur(2), prev(1), prev(2),
                  pl.BlockSpec((2, 8, BAND, 2 * BAND), lambda bi, r, i: (0, 0, 0, 0))],
        out_specs=(cur(0), pl.BlockSpec((None, sb, tq, LANES), lambda bi, r, i: (bi, r, i, 0))),
        scratch_shapes=[pltpu.VMEM((sb, BAND + tq, GROUP_W), BF16), pltpu.VMEM((sb, BAND + tq, GROUP_W), BF16)],
        compiler_params=pltpu.CompilerParams(
            dimension_semantics=("arbitrary", "arbitrary", "arbitrary"), vmem_limit_bytes=VMEM_LIMIT),
        name=f"dil_attn_{gi}",
    )(qkv, qkv, qkv, qkv, qkv, bias)


def _swa_attn_kernel(sink_ref, qlo_ref, qhi_ref, kvc_ref, kvp_ref, bias_ref, o_ref, kvbuf, *, tq):
    i = pl.program_id(1)
    kvbuf[0:BAND] = kvp_ref[...]
    kvbuf[BAND:BAND + tq] = kvc_ref[...]
    lo = lax.broadcasted_iota(jnp.int32, (BAND, LANES), 1) < HEAD_DIM
    zero = jnp.zeros((BAND, LANES), BF16)
    for qb in range(tq // BAND):
        sel = jnp.where(i == 0, 1, 0) if qb == 0 else 0
        rq = slice(qb * BAND, (qb + 1) * BAND)
        rk = slice(qb * BAND, qb * BAND + 2 * BAND)
        for g in range(SWA_KV_HEADS):
            q_ref = qlo_ref if g == 0 else qhi_ref
            kk = kvbuf[rk, g * LANES:(g + 1) * LANES]
            vv = kvbuf[rk, (2 + g) * LANES:(3 + g) * LANES]
            masked = []
            for pp in range(4):
                q2 = q_ref[rq, pp * LANES:(pp + 1) * LANES]
                masked += [jnp.where(lo, q2, zero), jnp.where(lo, zero, q2)]
            s_all = lax.dot_general(jnp.concatenate(masked, axis=0), kk, (((1,), (1,)), ((), ())),
                                    preferred_element_type=F32)
            probs, inv = [], []
            for j in range(8):
                h = 8 * g + j
                sink = sink_ref[h] * LOG2E
                s = s_all[j * BAND:(j + 1) * BAND] + bias_ref[sel, h]
                m = jnp.maximum(jnp.max(s, axis=-1, keepdims=True), sink)
                p = jnp.exp2(s - m)
                inv.append(1.0 / (jnp.sum(p, axis=-1, keepdims=True) + jnp.exp2(sink - m)))
                probs.append(p.astype(BF16))
            o_all = jnp.dot(jnp.concatenate(probs, axis=0), vv, preferred_element_type=F32)
            for pp in range(4):
                oa = o_all[(2 * pp) * BAND:(2 * pp + 1) * BAND] * inv[2 * pp]
                ob = o_all[(2 * pp + 1) * BAND:(2 * pp + 2) * BAND] * inv[2 * pp + 1]
                o_ref[rq, (4 * g + pp) * LANES:(4 * g + pp + 1) * LANES] = jnp.where(lo, oa, ob).astype(o_ref.dtype)


def _swa_attention(swa, sinks, *, tq):
    b, t, _ = swa.shape
    tq = min(tq, t)
    nq = tq // BAND
    bias = jnp.asarray(_band_bias(_alibi_slopes(SWA_Q_HEADS), SWA_WINDOW - 1, 1))
    grid_spec = pltpu.PrefetchScalarGridSpec(
        num_scalar_prefetch=1,
        grid=(b, t // tq),
        in_specs=[
            pl.BlockSpec((None, tq, 512), lambda bi, i, s: (bi, i, 0)),
            pl.BlockSpec((None, tq, 512), lambda bi, i, s: (bi, i, 1)),
            pl.BlockSpec((None, tq, 512), lambda bi, i, s: (bi, i, 2)),
            pl.BlockSpec((None, BAND, 512), lambda bi, i, s: (bi, jnp.maximum(i * nq - 1, 0), 2)),
            pl.BlockSpec((2, SWA_Q_HEADS, BAND, 2 * BAND), lambda bi, i, s: (0, 0, 0, 0)),
        ],
        out_specs=pl.BlockSpec((None, tq, SWA_Q_W), lambda bi, i, s: (bi, i, 0)),
        scratch_shapes=[pltpu.VMEM((BAND + tq, 512), BF16)],
    )
    return pl.pallas_call(
        functools.partial(_swa_attn_kernel, tq=tq),
        out_shape=jax.ShapeDtypeStruct((b, t, SWA_Q_W), BF16),
        grid_spec=grid_spec,
        compiler_params=pltpu.CompilerParams(
            dimension_semantics=("arbitrary", "arbitrary"), vmem_limit_bytes=VMEM_LIMIT),
        name="swa_attn",
    )(sinks.astype(F32), swa, swa, swa, swa, bias)


ROUTE_CLASS, ROUTE_RANK, ROUTE_WA, ROUTE_WB = range(4)
ROUTER_LANE0 = MOE_GROUPS
PAIRS_PER_GROUP = EXPERTS_PER_GROUP * (EXPERTS_PER_GROUP - 1) // 2
N_CLASSES = MOE_GROUPS * PAIRS_PER_GROUP
_PAIRS = [(a, b) for a in range(EXPERTS_PER_GROUP) for b in range(a + 1, EXPERTS_PER_GROUP)]
CLASS_EXPERT_A = np.array([g * EXPERTS_PER_GROUP + a for g in range(MOE_GROUPS) for a, _ in _PAIRS], np.int32)
CLASS_EXPERT_B = np.array([g * EXPERTS_PER_GROUP + b for g in range(MOE_GROUPS) for _, b in _PAIRS], np.int32)


def _route(logits, carry_ref, ltri_ref, tm, active):
    lane = lax.broadcasted_iota(jnp.int32, (tm, LANES), 1)
    lanef = lane.astype(F32)

    def first_argmax(v):
        m = jnp.max(v, axis=-1, keepdims=True)
        return m, jnp.min(jnp.where(v == m, lanef, float(LANES)), axis=-1, keepdims=True)

    gl = jnp.where(lane < MOE_GROUPS, logits, NEG_INF)
    gmax, gidx = first_argmax(gl)
    g_w = 1.0 / jnp.sum(jnp.exp(gl - gmax), axis=-1, keepdims=True)
    e_lane = lane - ROUTER_LANE0
    lane_group = (e_lane >> 3).astype(F32)
    in_group = (e_lane >= 0) & (e_lane < N_EXPERTS) & (lane_group == gidx)
    el = jnp.where(in_group, logits, NEG_INF)
    m1, i1 = first_argmax(el)
    m2, i2 = first_argmax(jnp.where(lanef == i1, NEG_INF, el))
    tt = jnp.exp(m2 - m1)
    w1 = g_w / (1.0 + tt)
    w2 = g_w * tt / (1.0 + tt)
    first = float(ROUTER_LANE0) + float(EXPERTS_PER_GROUP) * gidx
    e1 = i1 - first
    e2 = i2 - first
    swap = e2 < e1
    ea = jnp.minimum(e1, e2)
    eb = jnp.maximum(e1, e2)
    wa = jnp.where(swap, w2, w1)
    wb = jnp.where(swap, w1, w2)
    pair = ea * (float(2 * EXPERTS_PER_GROUP - 1) - ea) * 0.5 + (eb - ea - 1.0)
    cls = float(PAIRS_PER_GROUP) * gidx + pair
    oh = jnp.where(lanef == cls, 1.0, 0.0)
    before = jnp.dot(ltri_ref[...], oh.astype(BF16), preferred_element_type=F32) + carry_ref[0:1, :]
    rank = jnp.sum(oh * before, axis=-1, keepdims=True)
    carry_ref[...] = carry_ref[...] + jnp.where(active, jnp.sum(oh, axis=0, keepdims=True), 0.0)
    rec = jnp.zeros((tm, LANES), F32)
    for ln, val in ((ROUTE_CLASS, cls), (ROUTE_RANK, rank), (ROUTE_WA, wa), (ROUTE_WB, wb)):
        rec = jnp.where(lane == ln, val, rec)
    return rec


def _merge_kernel(x_ref, o0_ref, l0_ref, o1_ref, l1_ref, o2_ref, l2_ref, ys_ref, gates_ref,
                  wbd_ref, wbs_ref, wo_ref, gffn_ref, wrt_ref, brt_ref, exp_ref, ltri_ref,
                  x1_ref, h2_ref, route_ref, cnt_ref,
                  uo1, ul1, uo2, ul2, carry_ref, logit_buf, *, tm):
    step = pl.program_id(0)

    @pl.when(step == 0)
    def _():
        carry_ref[...] = jnp.zeros_like(carry_ref)
        logit_buf[...] = jnp.zeros_like(logit_buf)

    route_ref[...] = _route(logit_buf[...], carry_ref, ltri_ref, tm, step > 0)
    cnt_ref[...] = carry_ref[...]

    nch = GROUP_W // LANES
    for src_o, src_l, dst_o, dst_l, d in ((o1_ref, l1_ref, uo1, ul1, 4), (o2_ref, l2_ref, uo2, ul2, 16)):
        rows = tm // d
        for r in range(d):
            dst_l[pl.ds(r, rows, stride=d), :] = src_l[r]
            for c in range(nch):
                dst_o[c, pl.ds(r, rows, stride=d), :] = src_o[r, :, c * LANES:(c + 1) * LANES].astype(F32)
    def per_head_to_lanes(w):
        hi = w.astype(BF16)
        lo = (w - hi.astype(F32)).astype(BF16)
        return jnp.dot(jnp.concatenate([hi, lo], axis=1), exp_ref[...], preferred_element_type=F32)

    def sig(v):
        return 0.5 * jnp.tanh(0.5 * v) + 0.5

    def rows_to_logits(r0, nr):
        rs = slice(r0, r0 + nr)
        la, lb, lc = l0_ref[rs, :], ul1[rs, :], ul2[rs, :]
        mx = jnp.maximum(jnp.maximum(la, lb), lc)
        ea, eb, ec = jnp.exp(la - mx), jnp.exp(lb - mx), jnp.exp(lc - mx)
        inv = 1.0 / (ea + eb + ec)
        wa, wb, wc = per_head_to_lanes(ea * inv), per_head_to_lanes(eb * inv), per_head_to_lanes(ec * inv)
        ych = []
        for c in range(nch):
            cl = slice(c * LANES, (c + 1) * LANES)
            ych.append((wa[:, cl] * o0_ref[rs, cl].astype(F32) + wb[:, cl] * uo1[c, rs, :]
                        + wc[:, cl] * uo2[c, rs, :]).astype(BF16))
        y = jnp.concatenate(ych, axis=1)
        a = jnp.dot(y, wbd_ref[...], preferred_element_type=F32)
        bsw = jnp.dot(ys_ref[rs, :], wbs_ref[...], preferred_element_type=F32)
        mixed = sig(gates_ref[rs, 0:1024]) * a.astype(BF16) + sig(gates_ref[rs, 1024:2048]) * bsw.astype(BF16)
        x1 = x_ref[rs, :] + jnp.dot(mixed, wo_ref[...], preferred_element_type=F32)
        x1_ref[rs, :] = x1
        ms = jnp.mean(x1 * x1, axis=-1, keepdims=True)
        h2 = x1 * lax.rsqrt(ms + RMS_EPS) * gffn_ref[...]
        for c in range(h2.shape[1] // LANES):
            h2_ref[pl.ds(r0 * ROW_TILE + c, nr, stride=ROW_TILE), :] = h2[:, c * LANES:(c + 1) * LANES]
        h_hi = h2.astype(BF16)
        h_lo = (h2 - h_hi.astype(F32)).astype(BF16)
        hw = jnp.dot(h_hi, wrt_ref[...], preferred_element_type=F32)
        return (hw[:, :LANES] + hw[:, LANES:]
                + jnp.dot(h_lo, wrt_ref[:, :LANES], preferred_element_type=F32) + brt_ref[...])

    logit_buf[...] = rows_to_logits(0, tm)


def _merge_route(x, o0, l0, o1, l1, o2, l2, ys, gates, w_br_dil, w_br_swa, w_out, g_ffn,
                 w_group, b_group, w_router, b_router, *, tm):
    b, t, dm = x.shape
    nt = t // tm
    wrt = jnp.zeros((dm, LANES), F32).at[:, :MOE_GROUPS].set(w_group).at[
        :, ROUTER_LANE0:ROUTER_LANE0 + N_EXPERTS].set(w_router)
    brt = jnp.zeros((1, LANES), F32).at[0, :MOE_GROUPS].set(b_group).at[
        0, ROUTER_LANE0:ROUTER_LANE0 + N_EXPERTS].set(b_router)
    wrt_hi = wrt.astype(BF16)
    wrt_lo = (wrt - wrt_hi.astype(F32)).astype(BF16)
    wrt = jnp.concatenate([wrt_hi, wrt_lo], axis=1)

    nsteps = b * nt + 1

    def tile_of(s, lag=0):
        tile = jnp.clip(s - lag, 0, nsteps - 2)
        return tile // nt, tile % nt

    def tok(width, lag=0):
        return pl.BlockSpec((None, tm, width), lambda s: (*tile_of(s, lag), 0))

    def stream(d, width):
        return pl.BlockSpec((None, d, tm // d, width), lambda s: (tile_of(s)[0], 0, tile_of(s)[1], 0))

    def const(shape):
        return pl.BlockSpec(shape, lambda s: (0,) * len(shape), pipeline_mode=pl.Buffered(1))

    head_of_lane = np.arange(GROUP_W) // HEAD_DIM
    spread = (np.arange(LANES)[:, None] == head_of_lane[None, :]).astype(np.float32)
    expand = jnp.asarray(np.concatenate([spread, spread], axis=0), dtype=BF16)

    nch = GROUP_W // LANES
    return pl.pallas_call(
        functools.partial(_merge_kernel, tm=tm),
        out_shape=(jax.ShapeDtypeStruct((b, t, dm), F32), jax.ShapeDtypeStruct((b, t * ROW_TILE, LANES), F32),
                   jax.ShapeDtypeStruct((b, t, LANES), F32), jax.ShapeDtypeStruct((8, LANES), F32)),
        grid=(nsteps,),
        in_specs=[tok(dm), tok(GROUP_W), tok(LANES), stream(4, GROUP_W), stream(4, LANES),
                  stream(16, GROUP_W), stream(16, LANES), tok(SWA_Q_W), tok(2048),
                  const((GROUP_W, dm)), const((SWA_Q_W, dm)), const((dm, dm)), const((1, dm)),
                  const((dm, 2 * LANES)), const((1, LANES)), const((2 * LANES, GROUP_W)), const((tm, tm))],
        out_specs=(tok(dm), pl.BlockSpec((None, tm * ROW_TILE, LANES), lambda s: (*tile_of(s), 0)),
                   tok(LANES, lag=1), pl.BlockSpec((8, LANES), lambda s: (0, 0))),
        scratch_shapes=[pltpu.VMEM((nch, tm, LANES), F32), pltpu.VMEM((tm, LANES), F32),
                        pltpu.VMEM((nch, tm, LANES), F32), pltpu.VMEM((tm, LANES), F32),
                        pltpu.VMEM((8, LANES), F32), pltpu.VMEM((tm, LANES), F32)],
        compiler_params=pltpu.CompilerParams(
            dimension_semantics=("arbitrary",), vmem_limit_bytes=VMEM_LIMIT),
        name="merge_route",
    )(x, o0, l0, o1, l1, o2, l2, ys, gates, w_br_dil.astype(BF16), w_br_swa.astype(BF16), w_out.astype(BF16),
      g_ffn.reshape(1, dm), wrt, brt, expand, jnp.asarray(np.tril(np.ones((tm, tm), np.float32), -1), dtype=BF16))


def _dispatch_kernel(pos_ref, zoff_ref, nt_ref, h_ref, xs_ref, zbuf, sem, zsem, *, td, tme, nt_max):
    step = pl.program_id(0)

    def _zero_copy(e):
        off = pl.multiple_of(zoff_ref[e] * ROW_TILE, tme * ROW_TILE)
        return pltpu.make_async_copy(zbuf, xs_ref.at[pl.ds(off, tme * ROW_TILE)], zsem)

    def _row_copy(src_row, dst_row):
        return pltpu.make_async_copy(h_ref.at[pl.ds(pl.multiple_of(src_row * ROW_TILE, ROW_TILE), ROW_TILE)],
                                     xs_ref.at[pl.ds(pl.multiple_of(dst_row * ROW_TILE, ROW_TILE), ROW_TILE)], sem)

    def _wait_rows():
        pltpu.make_async_copy(h_ref, xs_ref.at[pl.ds(0, td * ROW_TILE)], sem).wait()

    def _tail_copy(tile):
        off = pl.multiple_of(tile * (tme * ROW_TILE), tme * ROW_TILE)
        return pltpu.make_async_copy(zbuf, xs_ref.at[pl.ds(off, tme * ROW_TILE)], zsem)

    @pl.when(step == 0)
    def _():
        zbuf[...] = jnp.zeros_like(zbuf)
        for e in range(N_CLASSES):
            _zero_copy(e).start()
        for e in range(N_CLASSES):
            _zero_copy(e).wait()

        def tail(tile, carry):
            _tail_copy(tile).start()
            _tail_copy(tile).wait()
            return carry

        lax.fori_loop(nt_ref[0], nt_max, tail, 0)

    def issue(j, carry):
        _row_copy(j, pos_ref[0, 0, j]).start()
        return carry

    lax.fori_loop(0, td, issue, 0, unroll=8)
    _wait_rows()


def _dispatch(h2, pos, zoff, ntiles, nslot, *, td, tme):
    n = h2.shape[0] // ROW_TILE
    nb = n // td
    grid_spec = pltpu.PrefetchScalarGridSpec(
        num_scalar_prefetch=0,
        grid=(nb,),
        in_specs=[
            pl.BlockSpec((1, 1, td), lambda i: (i, 0, 0), memory_space=pltpu.SMEM),
            pl.BlockSpec(memory_space=pltpu.SMEM),
            pl.BlockSpec(memory_space=pltpu.SMEM),
            pl.BlockSpec((td * ROW_TILE, LANES), lambda i: (i, 0)),
        ],
        out_specs=pl.BlockSpec(memory_space=pl.ANY),
        scratch_shapes=[pltpu.VMEM((tme * ROW_TILE, LANES), F32), pltpu.SemaphoreType.DMA(()),
                        pltpu.SemaphoreType.DMA(())],
    )
    return pl.pallas_call(
        functools.partial(_dispatch_kernel, td=td, tme=tme, nt_max=nslot // tme),
        out_shape=jax.ShapeDtypeStruct((nslot * ROW_TILE, LANES), F32),
        grid_spec=grid_spec,
        compiler_params=pltpu.CompilerParams(dimension_semantics=("arbitrary",), has_side_effects=True,
                                             vmem_limit_bytes=VMEM_LIMIT),
        name="dispatch",
    )(pos.reshape(nb, 1, td), zoff, ntiles, h2)


PAIR_ROWS = 2 * ROW_TILE


PLAN_FIRST, PLAN_SLOT, PLAN_NEXT, PLAN_HAS_NEXT = range(4)


def _expert_kernel(ta_ref, tb_ref, tblk_ref, trows_ref, plan_ref, nt_ref, x_ref, wg_hbm, wu_hbm, wd_hbm, y_ref,
                   wg_buf, wu_buf, wd_buf, wsem, *, tme):
    i = pl.program_id(0)
    live = i < nt_ref[0]
    half_rows = tme // 2

    def weight_copies(which, expert, slot):
        return [pltpu.make_async_copy(src.at[expert], buf.at[which, slot], wsem.at[which, slot])
                for src, buf in ((wg_hbm, wg_buf), (wu_hbm, wu_buf), (wd_hbm, wd_buf))]

    slots = []
    for which, t_ref in ((0, ta_ref), (1, tb_ref)):
        base = (i * 2 + which) * 4
        first = plan_ref[base + PLAN_FIRST] == 1
        slot = plan_ref[base + PLAN_SLOT]
        slots.append(slot)

        @pl.when(first & (i == 0))
        def _():
            for cp in weight_copies(which, t_ref[i], slot):
                cp.start()

        @pl.when(first)
        def _():
            for cp in weight_copies(which, t_ref[i], slot):
                cp.wait()

        @pl.when(first & (plan_ref[base + PLAN_HAS_NEXT] == 1))
        def _():
            for cp in weight_copies(which, plan_ref[base + PLAN_NEXT], 1 - slot):
                cp.start()

    def run(nrows):
        xb = jnp.concatenate([x_ref[pl.ds(c, nrows, stride=ROW_TILE), :] for c in range(ROW_TILE)], axis=1)
        nw = 2 * LANES
        for half in range(2):
            wg_ref, wu_ref, wd_ref = (buf.at[half, slots[half]] for buf in (wg_buf, wu_buf, wd_buf))
            act = []
            for c0 in range(0, D_EXPERT, nw):
                g = jnp.dot(xb, wg_ref[:, c0:c0 + nw], preferred_element_type=F32)
                u = jnp.dot(xb, wu_ref[:, c0:c0 + nw], preferred_element_type=F32)
                act.append(g * jax.nn.sigmoid(g) * u)
            a = jnp.concatenate(act, axis=1)
            for c0 in range(0, ROW_TILE * LANES, nw):
                y = jnp.dot(a, wd_ref[:, c0:c0 + nw], preferred_element_type=F32)
                for k in range(nw // LANES):
                    c = c0 // LANES + k
                    y_ref[pl.ds(half * ROW_TILE + c, nrows, stride=PAIR_ROWS), :] = y[:, k * LANES:(k + 1) * LANES]

    @pl.when(live & (trows_ref[i] > half_rows))
    def _():
        run(tme)

    @pl.when(live & (trows_ref[i] <= half_rows))
    def _():
        run(half_rows)
        y_ref[half_rows * PAIR_ROWS:, :] = jnp.zeros((half_rows * PAIR_ROWS, LANES), F32)

    @pl.when(jnp.logical_not(live))
    def _():
        y_ref[...] = jnp.zeros_like(y_ref)


def _weight_plan(tile_expert):
    nt = tile_expert.shape[0]
    first = jnp.concatenate([jnp.ones((1,), jnp.int32), (tile_expert[1:] != tile_expert[:-1]).astype(jnp.int32)])
    run_id = jnp.cumsum(first) - 1
    next_start = jnp.sum((run_id[None, :] <= run_id[:, None]).astype(jnp.int32), axis=1)
    has_next = (run_id < run_id[-1]).astype(jnp.int32)
    nxt = jnp.take(tile_expert, jnp.minimum(next_start, nt - 1))
    return jnp.stack([first, run_id % 2, nxt, has_next], axis=1).astype(jnp.int32)


def _experts(xs, tile_a, tile_b, tile_block, tile_rows, ntiles, w_e_gate, w_e_up, w_e_down, *, tme):
    nslot = xs.shape[0] // ROW_TILE
    dm = ROW_TILE * LANES
    nt = nslot // tme
    wg, wu, wd = w_e_gate, w_e_up, w_e_down
    plan =jnp.concatenate([_weight_plan(tile_a), _weight_plan(tile_b)], axis=1).reshape(-1)

    grid_spec = pltpu.PrefetchScalarGridSpec(
        num_scalar_prefetch=6,
        grid=(nt,),
        in_specs=[
            pl.BlockSpec((tme * ROW_TILE, LANES), lambda i, ta, tb, tk, tr, pn, n: (tk[i], 0)),
            pl.BlockSpec(memory_space=pl.ANY), pl.BlockSpec(memory_space=pl.ANY), pl.BlockSpec(memory_space=pl.ANY),
        ],
        out_specs=pl.BlockSpec((tme * PAIR_ROWS, LANES), lambda i, ta, tb, tk, tr, pn, n: (i, 0)),
        scratch_shapes=[pltpu.VMEM((2, 2, dm, D_EXPERT), F32), pltpu.VMEM((2, 2, dm, D_EXPERT), F32),
                        pltpu.VMEM((2, 2, D_EXPERT, dm), F32), pltpu.SemaphoreType.DMA((2, 2))],
    )
    return pl.pallas_call(
        functools.partial(_expert_kernel, tme=tme),
        out_shape=jax.ShapeDtypeStruct((nslot * PAIR_ROWS, LANES), F32),
        grid_spec=grid_spec,
        compiler_params=pltpu.CompilerParams(dimension_semantics=("arbitrary",), vmem_limit_bytes=VMEM_LIMIT),
        name="experts",
    )(tile_a, tile_b, tile_block, tile_rows, plan, ntiles, xs, wg, wu, wd)


def _combine_kernel(pos_ref, posn_ref, ys_ref, x1_ref, route_ref, gfin_ref, out_ref, ybuf, sem, *, tc):
    i = pl.program_id(0)
    nb = pl.num_programs(0)
    slot = i % 2

    def gather(p_ref, s):
        def issue(j, carry):
            src = pl.multiple_of(p_ref[0, 0, j] * PAIR_ROWS, PAIR_ROWS)
            dst = pl.multiple_of(j * PAIR_ROWS, PAIR_ROWS)
            pltpu.make_async_copy(ys_ref.at[pl.ds(src, PAIR_ROWS)], ybuf.at[s, pl.ds(dst, PAIR_ROWS)],
                                  sem.at[s]).start()
            return carry

        lax.fori_loop(0, tc, issue, 0, unroll=8)

    @pl.when(i == 0)
    def _():
        gather(pos_ref, 0)

    @pl.when(i + 1 < nb)
    def _():
        gather(posn_ref, 1 - slot)

    pltpu.make_async_copy(ys_ref.at[pl.ds(0, tc * PAIR_ROWS)], ybuf.at[slot], sem.at[slot]).wait()
    rec = route_ref[...]
    wa = rec[:, ROUTE_WA:ROUTE_WA + 1]
    wb = rec[:, ROUTE_WB:ROUTE_WB + 1]
    ya = jnp.concatenate([ybuf[slot, pl.ds(c, tc, stride=PAIR_ROWS), :] for c in range(ROW_TILE)], axis=1)
    yb = jnp.concatenate([ybuf[slot, pl.ds(ROW_TILE + c, tc, stride=PAIR_ROWS), :] for c in range(ROW_TILE)],
                         axis=1)
    z = x1_ref[...] + wa * ya + wb * yb
    ms = jnp.mean(z * z, axis=-1, keepdims=True)
    out_ref[...] = z * lax.rsqrt(ms + RMS_EPS) * gfin_ref[...]


def _combine(ys, pos, x1, route, g_final, *, tc):
    n, dm = x1.shape
    nb = n // tc
    grid_spec = pltpu.PrefetchScalarGridSpec(
        num_scalar_prefetch=0,
        grid=(nb,),
        in_specs=[
            pl.BlockSpec((1, 1, tc), lambda i: (i, 0, 0), memory_space=pltpu.SMEM),
            pl.BlockSpec((1, 1, tc), lambda i: (jnp.minimum(i + 1, nb - 1), 0, 0), memory_space=pltpu.SMEM),
            pl.BlockSpec(memory_space=pl.ANY),
            pl.BlockSpec((tc, dm), lambda i: (i, 0)),
            pl.BlockSpec((tc, LANES), lambda i: (i, 0)),
            pl.BlockSpec((1, dm), lambda i: (0, 0)),
        ],
        out_specs=pl.BlockSpec((tc, dm), lambda i: (i, 0)),
        scratch_shapes=[pltpu.VMEM((2, tc * PAIR_ROWS, LANES), F32), pltpu.SemaphoreType.DMA((2,))],
    )
    return pl.pallas_call(
        functools.partial(_combine_kernel, tc=tc),
        out_shape=jax.ShapeDtypeStruct((n, dm), F32),
        grid_spec=grid_spec,
        compiler_params=pltpu.CompilerParams(dimension_semantics=("arbitrary",), vmem_limit_bytes=VMEM_LIMIT),
        name="combine",
    )(pos.reshape(nb, 1, tc), pos.reshape(nb, 1, tc), ys, x1, route, g_final.reshape(1, dm))


def _slot_layout(route, counts, n, tme):
    cnt = counts[0, :N_CLASSES].astype(jnp.int32)
    tiles_per = (cnt + tme - 1) // tme
    tile_end = jnp.cumsum(tiles_per)
    seg_base = (tile_end - tiles_per) * tme
    ntiles = tile_end[-1]
    nt_max = n // tme + N_CLASSES
    tid = jnp.arange(nt_max, dtype=jnp.int32)
    live = jnp.minimum(tid, ntiles - 1)
    tile_class = jnp.sum((tile_end[None, :] <= live[:, None]).astype(jnp.int32), axis=1)
    tile_a = jnp.take(jnp.asarray(CLASS_EXPERT_A), tile_class)
    tile_b = jnp.take(jnp.asarray(CLASS_EXPERT_B), tile_class)
    first_tile = jnp.take(tile_end - tiles_per, tile_class)
    tile_rows = jnp.clip(jnp.take(cnt, tile_class) - (live - first_tile) * tme, 0, tme)
    cls = route[:, ROUTE_CLASS].astype(jnp.int32)
    rank = route[:, ROUTE_RANK].astype(jnp.int32)
    base = jnp.sum(jnp.where(cls[:, None] == jnp.arange(N_CLASSES, dtype=jnp.int32)[None, :], seg_base[None, :], 0),
                   axis=1)
    pos = base + rank
    zoff = jnp.maximum(tile_end - 1, 0) * tme
    return (pos, zoff.astype(jnp.int32), tile_a.astype(jnp.int32), tile_b.astype(jnp.int32),
            live.astype(jnp.int32), tile_rows.astype(jnp.int32), ntiles.reshape(1).astype(jnp.int32))


def kernel(x, g_mix, w_in, sinks, w_br_dil, w_br_swa, w_out, g_ffn, w_group, b_group, w_router, b_router,
           w_e_gate, w_e_up, w_e_down, g_final):
    b, t, dm = x.shape
    n = b * t
    tme = 256
    qkv0, swa, gates, qkv4, qkv16 = _inproj(x, g_mix[0], w_in[0], tm=512)
    o0, l0 = _dil_attention(qkv0.reshape(b, 1, t, QKV_W), 0, tq=2048)
    o1, l1 = _dil_attention(qkv4, 1, tq=2048)
    o2, l2 = _dil_attention(qkv16, 2, tq=2048)
    ys = _swa_attention(swa, sinks[0], tq=2048)
    x1, h2, route, counts = _merge_route(
        x, o0.reshape(b, t, GROUP_W), l0.reshape(b, t, LANES), o1, l1, o2, l2, ys, gates,
        w_br_dil[0], w_br_swa[0], w_out[0], g_ffn[0], w_group[0], b_group[0], w_router[0], b_router[0], tm=512)
    route = route.reshape(n, LANES)
    pos, zoff, tile_a, tile_b, tile_block, tile_rows, ntiles = _slot_layout(route, counts, n, tme)
    nslot = n + N_CLASSES * tme
    xs = _dispatch(h2.reshape(n * ROW_TILE, LANES), pos, zoff, ntiles, nslot, td=4096, tme=tme)
    yslots = _experts(xs, tile_a, tile_b, tile_block, tile_rows, ntiles, w_e_gate[0], w_e_up[0], w_e_down[0],
                      tme=tme)
    out = _combine(yslots, pos, x1.reshape(n, dm), route, g_final, tc=512)
    return out.reshape(b, t, dm)
```

```python
import functools

import numpy as np
import jax
import jax.numpy as jnp
from jax import lax
from jax.experimental import pallas as pl
from jax.experimental.pallas import tpu as pltpu

F32 = jnp.float32
BF16 = jnp.bfloat16

HEAD_DIM = 64
BAND = 128
DIL_PATTERNS = ((128, 1), (512, 4), (2048, 16))
DIL_HEADS_PER_GROUP = 8
DIL_HEADS = 24
GROUP_W = DIL_HEADS_PER_GROUP * HEAD_DIM
QKV_W = 3 * GROUP_W
SWA_WINDOW = 128
SWA_Q_HEADS = 16
SWA_KV_HEADS = 2
SWA_Q_W = SWA_Q_HEADS * HEAD_DIM
SWA_W = SWA_Q_W + 4 * 128
MOE_GROUPS = 4
EXPERTS_PER_GROUP = 8
N_EXPERTS = 32
D_EXPERT = 512
RMS_EPS = 1e-6
LANES = 128
NEG_INF = float("-inf")
LOG2E = 1.4426950408889634
LN2 = 0.6931471805599453
ROW_TILE = 8

VMEM_LIMIT = 56 * 1024 * 1024


def _alibi_slopes(n):
    return (2.0 ** (-8.0 * np.arange(1, n + 1) / n)).astype(np.float32)


def _band_bias(slopes, max_back, unit):
    a = np.arange(BAND)[:, None]
    c = np.arange(2 * BAND)[None, :]
    delta = a + BAND - c
    band = (delta >= 0) & (delta <= max_back)
    pen = (-slopes[:, None, None] * (delta * unit)[None] * LOG2E).astype(np.float32)
    full = np.where(band[None], pen, -np.inf).astype(np.float32)
    first = np.where((band & (c >= BAND))[None], pen, -np.inf).astype(np.float32)
    return np.stack([full, first], axis=0)


def _inproj_kernel(x_ref, g_ref, w_ref, qkv0_ref, swa_ref, gates_ref, qkv4_ref, qkv16_ref, hs_ref, *, tm):
    x = x_ref[...]
    ms = jnp.mean(x * x, axis=-1, keepdims=True)
    h = x * lax.rsqrt(ms + RMS_EPS) * g_ref[...]
    nchunk = h.shape[1] // LANES
    for c in range(nchunk):
        hs_ref[c] = h[:, c * LANES:(c + 1) * LANES]
    hb = h.astype(BF16)

    def proj(lhs, c0, width):
        return jnp.dot(lhs, w_ref[:, c0:c0 + width], preferred_element_type=F32)

    for j in range(3):
        qkv0_ref[:, j * GROUP_W:(j + 1) * GROUP_W] = proj(hb, COL_QKV_D[j], GROUP_W).astype(BF16)
    for c0 in range(0, SWA_Q_W, 512):
        swa_ref[:, c0:c0 + 512] = proj(hb, COL_Q_S + c0, 512).astype(BF16)
    kv = proj(hb, COL_KV_S, 4 * HEAD_DIM)
    part = [kv[:, j * HEAD_DIM:(j + 1) * HEAD_DIM] for j in range(4)]
    swa_ref[:, SWA_Q_W:SWA_W] = jnp.concatenate(
        [part[0], part[0], part[1], part[1], part[2], part[2], part[3], part[3]], axis=1).astype(BF16)
    for c0 in range(0, 2048, 512):
        gates_ref[:, c0:c0 + 512] = proj(hb, COL_GATES + c0, 512).astype(BF16)
    for gi, out_ref, d in ((1, qkv4_ref, 4), (2, qkv16_ref, 16)):
        rows = tm // d
        hp = jnp.concatenate(
            [jnp.concatenate([hs_ref[c, pl.ds(r, rows, stride=d), :] for c in range(nchunk)], axis=1)
             for r in range(d)], axis=0).astype(BF16)
        for j in range(3):
            res = proj(hp, COL_QKV_D[j] + gi * GROUP_W, GROUP_W).astype(BF16)
            for r in range(d):
                out_ref[r, :, j * GROUP_W:(j + 1) * GROUP_W] = res[r * rows:(r + 1) * rows]


COL_QKV_D = (0, 1536, 3072)
COL_Q_S = 4608
COL_KV_S = 5632
COL_GATES = 5888
IN_WIDTH = 7936
Q_SCALE = HEAD_DIM ** -0.5 * LOG2E


def _prep_w_in(w_in):
    col = np.arange(IN_WIDTH)
    is_q = (col < COL_QKV_D[1]) | ((col >= COL_Q_S) & (col < COL_KV_S))
    colscale = np.where(is_q, Q_SCALE, 1.0).astype(np.float32)
    return (w_in * colscale[None, :]).astype(BF16)


def _inproj(x, g_mix, w_in, *, tm):
    b, t, dm = x.shape
    wp = _prep_w_in(w_in)
    nw = wp.shape[1]
    nt = t // tm
    out_shape = (
        jax.ShapeDtypeStruct((b, t, QKV_W), BF16),
        jax.ShapeDtypeStruct((b, t, SWA_W), BF16),
        jax.ShapeDtypeStruct((b, t, 2048), BF16),
        jax.ShapeDtypeStruct((b, 4, t // 4, QKV_W), BF16),
        jax.ShapeDtypeStruct((b, 16, t // 16, QKV_W), BF16),
    )
    return pl.pallas_call(
        functools.partial(_inproj_kernel, tm=tm),
        out_shape=out_shape,
        grid=(b, nt),
        in_specs=[
            pl.BlockSpec((None, tm, dm), lambda bi, i: (bi, i, 0)),
            pl.BlockSpec((1, dm), lambda bi, i: (0, 0)),
            pl.BlockSpec((dm, nw), lambda bi, i: (0, 0), pipeline_mode=pl.Buffered(1)),
        ],
        out_specs=(
            pl.BlockSpec((None, tm, QKV_W), lambda bi, i: (bi, i, 0)),
            pl.BlockSpec((None, tm, SWA_W), lambda bi, i: (bi, i, 0)),
            pl.BlockSpec((None, tm, 2048), lambda bi, i: (bi, i, 0)),
            pl.BlockSpec((None, 4, tm // 4, QKV_W), lambda bi, i: (bi, 0, i, 0)),
            pl.BlockSpec((None, 16, tm // 16, QKV_W), lambda bi, i: (bi, 0, i, 0)),
        ),
        scratch_shapes=[pltpu.VMEM((dm // LANES, tm, LANES), F32)],
        compiler_params=pltpu.CompilerParams(
            dimension_semantics=("arbitrary", "arbitrary"), vmem_limit_bytes=VMEM_LIMIT),
        name="inproj",
    )(x, g_mix.reshape(1, dm), wp)


def _dil_attn_kernel(q_all, kc_all, vc_all, kp_all, vp_all, bias_ref, o_all_ref, lse_all_ref, kbuf_all, vbuf_all,
                     *, tq, sb):
    for st in range(sb):
        _dil_attn_stream(q_all.at[st], kc_all.at[st], vc_all.at[st], kp_all.at[st], vp_all.at[st], bias_ref,
                         o_all_ref.at[st], lse_all_ref.at[st], kbuf_all.at[st], vbuf_all.at[st], tq=tq)


def _dil_attn_stream(q_ref, kc_ref, vc_ref, kp_ref, vp_ref, bias_ref, o_ref, lse_ref, kbuf, vbuf, *, tq):
    i = pl.program_id(2)
    kbuf[0:BAND] = kp_ref[...]
    kbuf[BAND:BAND + tq] = kc_ref[...]
    vbuf[0:BAND] = vp_ref[...]
    vbuf[BAND:BAND + tq] = vc_ref[...]
    lane = lax.broadcasted_iota(jnp.int32, (BAND, LANES), 1)
    lo = lane < HEAD_DIM
    zero = jnp.zeros((BAND, LANES), BF16)
    for qb in range(tq // BAND):
        sel = jnp.where(i == 0, 1, 0) if qb == 0 else 0
        rq = slice(qb * BAND, (qb + 1) * BAND)
        rk = slice(qb * BAND, qb * BAND + 2 * BAND)
        lse_parts = []
        for pr in range(GROUP_W // LANES):
            cl = slice(pr * LANES, (pr + 1) * LANES)
            q2 = q_ref[rq, cl]
            kk = kbuf[rk, cl]
            vv = vbuf[rk, cl]
            s_all = lax.dot_general(jnp.concatenate([jnp.where(lo, q2, zero), jnp.where(lo, zero, q2)], axis=0), kk,
                                    (((1,), (1,)), ((), ())), preferred_element_type=F32)
            probs, stats = [], []
            for hh in range(2):
                s = s_all[hh * BAND:(hh + 1) * BAND] + bias_ref[sel, 2 * pr + hh]
                m = jnp.max(s, axis=-1, keepdims=True)
                p = jnp.exp2(s - m)
                l = jnp.sum(p, axis=-1, keepdims=True)
                probs.append(p.astype(BF16))
                stats.append((1.0 / l, (m + jnp.log2(l)) * LN2))
            o_all = jnp.dot(jnp.concatenate(probs, axis=0), vv, preferred_element_type=F32)
            res = [(o_all[hh * BAND:(hh + 1) * BAND] * stats[hh][0], stats[hh][1]) for hh in range(2)]
            o_ref[rq, cl] = jnp.where(lo, res[0][0], res[1][0]).astype(o_ref.dtype)
            lse_parts.append(jnp.where(lane == 2 * pr, res[0][1], jnp.where(lane == 2 * pr + 1, res[1][1], 0.0)))
        lse_ref[rq, :] = (lse_parts[0] + lse_parts[1]) + (lse_parts[2] + lse_parts[3])


def _dil_attention(qkv, gi, *, tq):
    b, d, l, _ = qkv.shape
    window, dil = DIL_PATTERNS[gi]
    assert dil == d
    rows_per_step = tq
    tq = min(tq, l)
    sb = min(d, max(1, rows_per_step // tq))
    nq = tq // BAND
    slopes = _alibi_slopes(DIL_HEADS)[gi * 8:(gi + 1) * 8]
    bias = jnp.asarray(_band_bias(slopes, window // dil, dil))

    def cur(c):
        return pl.BlockSpec((None, sb, tq, GROUP_W), lambda bi, r, i: (bi, r, i, c))

    def prev(c):
        return pl.BlockSpec((None, sb, BAND, GROUP_W), lambda bi, r, i: (bi, r, jnp.maximum(i * nq - 1, 0), c))

    return pl.pallas_call(
        functools.partial(_dil_attn_kernel, tq=tq, sb=sb),
        out_shape=(jax.ShapeDtypeStruct((b, d, l, GROUP_W), BF16),
                   jax.ShapeDtypeStruct((b, d, l, LANES), F32)),
        grid=(b, d // sb, l // tq),
        in_specs=[cur(0), cur(1), cur(2), prev(1), prev(2),
                  pl.BlockSpec((2, 8, BAND, 2 * BAND), lambda bi, r, i: (0, 0, 0, 0))],
        out_specs=(cur(0), pl.BlockSpec((None, sb, tq, LANES), lambda bi, r, i: (bi, r, i, 0))),
        scratch_shapes=[pltpu.VMEM((sb, BAND + tq, GROUP_W), BF16), pltpu.VMEM((sb, BAND + tq, GROUP_W), BF16)],
        compiler_params=pltpu.CompilerParams(
            dimension_semantics=("arbitrary", "arbitrary", "arbitrary"), vmem_limit_bytes=VMEM_LIMIT),
        name=f"dil_attn_{gi}",
    )(qkv, qkv, qkv, qkv, qkv, bias)


def _swa_attn_kernel(sink_ref, qlo_ref, qhi_ref, kvc_ref, kvp_ref, bias_ref, o_ref, kvbuf, *, tq):
    i = pl.program_id(1)
    kvbuf[0:BAND] = kvp_ref[...]
    kvbuf[BAND:BAND + tq] = kvc_ref[...]
    lo = lax.broadcasted_iota(jnp.int32, (BAND, LANES), 1) < HEAD_DIM
    zero = jnp.zeros((BAND, LANES), BF16)
    for qb in range(tq // BAND):
        sel = jnp.where(i == 0, 1, 0) if qb == 0 else 0
        rq = slice(qb * BAND, (qb + 1) * BAND)
        rk = slice(qb * BAND, qb * BAND + 2 * BAND)
        for g in range(SWA_KV_HEADS):
            q_ref = qlo_ref if g == 0 else qhi_ref
            kk = kvbuf[rk, g * LANES:(g + 1) * LANES]
            vv = kvbuf[rk, (2 + g) * LANES:(3 + g) * LANES]
            masked = []
            for pp in range(4):
                q2 = q_ref[rq, pp * LANES:(pp + 1) * LANES]
                masked += [jnp.where(lo, q2, zero), jnp.where(lo, zero, q2)]
            s_all = lax.dot_general(jnp.concatenate(masked, axis=0), kk, (((1,), (1,)), ((), ())),
                                    preferred_element_type=F32)
            probs, inv = [], []
            for j in range(8):
                h = 8 * g + j
                sink = sink_ref[h] * LOG2E
                s = s_all[j * BAND:(j + 1) * BAND] + bias_ref[sel, h]
                m = jnp.maximum(jnp.max(s, axis=-1, keepdims=True), sink)
                p = jnp.exp2(s - m)
                inv.append(1.0 / (jnp.sum(p, axis=-1, keepdims=True) + jnp.exp2(sink - m)))
                probs.append(p.astype(BF16))
            o_all = jnp.dot(jnp.concatenate(probs, axis=0), vv, preferred_element_type=F32)
            for pp in range(4):
                oa = o_all[(2 * pp) * BAND:(2 * pp + 1) * BAND] * inv[2 * pp]
                ob = o_all[(2 * pp + 1) * BAND:(2 * pp + 2) * BAND] * inv[2 * pp + 1]
                o_ref[rq, (4 * g + pp) * LANES:(4 * g + pp + 1) * LANES] = jnp.where(lo, oa, ob).astype(o_ref.dtype)


def _swa_attention(swa, sinks, *, tq):
    b, t, _ = swa.shape
    tq = min(tq, t)
    nq = tq // BAND
    bias = jnp.asarray(_band_bias(_alibi_slopes(SWA_Q_HEADS), SWA_WINDOW - 1, 1))
    grid_spec = pltpu.PrefetchScalarGridSpec(
        num_scalar_prefetch=1,
        grid=(b, t // tq),
        in_specs=[
            pl.BlockSpec((None, tq, 512), lambda bi, i, s: (bi, i, 0)),
            pl.BlockSpec((None, tq, 512), lambda bi, i, s: (bi, i, 1)),
            pl.BlockSpec((None, tq, 512), lambda bi, i, s: (bi, i, 2)),
            pl.BlockSpec((None, BAND, 512), lambda bi, i, s: (bi, jnp.maximum(i * nq - 1, 0), 2)),
            pl.BlockSpec((2, SWA_Q_HEADS, BAND, 2 * BAND), lambda bi, i, s: (0, 0, 0, 0)),
        ],
        out_specs=pl.BlockSpec((None, tq, SWA_Q_W), lambda bi, i, s: (bi, i, 0)),
        scratch_shapes=[pltpu.VMEM((BAND + tq, 512), BF16)],
    )
    return pl.pallas_call(
        functools.partial(_swa_attn_kernel, tq=tq),
        out_shape=jax.ShapeDtypeStruct((b, t, SWA_Q_W), BF16),
        grid_spec=grid_spec,
        compiler_params=pltpu.CompilerParams(
            dimension_semantics=("arbitrary", "arbitrary"), vmem_limit_bytes=VMEM_LIMIT),
        name="swa_attn",
    )(sinks.astype(F32), swa, swa, swa, swa, bias)


ROUTE_CLASS, ROUTE_RANK, ROUTE_WA, ROUTE_WB = range(4)
ROUTER_LANE0 = MOE_GROUPS
PAIRS_PER_GROUP = EXPERTS_PER_GROUP * (EXPERTS_PER_GROUP - 1) // 2
N_CLASSES = MOE_GROUPS * PAIRS_PER_GROUP
_PAIRS = [(a, b) for a in range(EXPERTS_PER_GROUP) for b in range(a + 1, EXPERTS_PER_GROUP)]
CLASS_EXPERT_A = np.array([g * EXPERTS_PER_GROUP + a for g in range(MOE_GROUPS) for a, _ in _PAIRS], np.int32)
CLASS_EXPERT_B = np.array([g * EXPERTS_PER_GROUP + b for g in range(MOE_GROUPS) for _, b in _PAIRS], np.int32)


def _route(logits, carry_ref, ltri_ref, tm, active):
    lane = lax.broadcasted_iota(jnp.int32, (tm, LANES), 1)
    lanef = lane.astype(F32)

    def first_argmax(v):
        m = jnp.max(v, axis=-1, keepdims=True)
        return m, jnp.min(jnp.where(v == m, lanef, float(LANES)), axis=-1, keepdims=True)

    gl = jnp.where(lane < MOE_GROUPS, logits, NEG_INF)
    gmax, gidx = first_argmax(gl)
    g_w = 1.0 / jnp.sum(jnp.exp(gl - gmax), axis=-1, keepdims=True)
    e_lane = lane - ROUTER_LANE0
    lane_group = (e_lane >> 3).astype(F32)
    in_group = (e_lane >= 0) & (e_lane < N_EXPERTS) & (lane_group == gidx)
    el = jnp.where(in_group, logits, NEG_INF)
    m1, i1 = first_argmax(el)
    m2, i2 = first_argmax(jnp.where(lanef == i1, NEG_INF, el))
    tt = jnp.exp(m2 - m1)
    w1 = g_w / (1.0 + tt)
    w2 = g_w * tt / (1.0 + tt)
    first = float(ROUTER_LANE0) + float(EXPERTS_PER_GROUP) * gidx
    e1 = i1 - first
    e2 = i2 - first
    swap = e2 < e1
    ea = jnp.minimum(e1, e2)
    eb = jnp.maximum(e1, e2)
    wa = jnp.where(swap, w2, w1)
    wb = jnp.where(swap, w1, w2)
    pair = ea * (float(2 * EXPERTS_PER_GROUP - 1) - ea) * 0.5 + (eb - ea - 1.0)
    cls = float(PAIRS_PER_GROUP) * gidx + pair
    oh = jnp.where(lanef == cls, 1.0, 0.0)
    before = jnp.dot(ltri_ref[...], oh.astype(BF16), preferred_element_type=F32) + carry_ref[0:1, :]
    rank = jnp.sum(oh * before, axis=-1, keepdims=True)
    carry_ref[...] = carry_ref[...] + jnp.where(active, jnp.sum(oh, axis=0, keepdims=True), 0.0)
    rec = jnp.zeros((tm, LANES), F32)
    for ln, val in ((ROUTE_CLASS, cls), (ROUTE_RANK, rank), (ROUTE_WA, wa), (ROUTE_WB, wb)):
        rec = jnp.where(lane == ln, val, rec)
    return rec


def _merge_kernel(x_ref, o0_ref, l0_ref, o1_ref, l1_ref, o2_ref, l2_ref, ys_ref, gates_ref,
                  wbd_ref, wbs_ref, wo_ref, gffn_ref, wrt_ref, brt_ref, exp_ref, ltri_ref,
                  x1_ref, h2_ref, route_ref, cnt_ref,
                  uo1, ul1, uo2, ul2, carry_ref, logit_buf, *, tm):
    step = pl.program_id(0)

    @pl.when(step == 0)
    def _():
        carry_ref[...] = jnp.zeros_like(carry_ref)
        logit_buf[...] = jnp.zeros_like(logit_buf)

    route_ref[...] = _route(logit_buf[...], carry_ref, ltri_ref, tm, step > 0)
    cnt_ref[...] = carry_ref[...]

    nch = GROUP_W // LANES
    for src_o, src_l, dst_o, dst_l, d in ((o1_ref, l1_ref, uo1, ul1, 4), (o2_ref, l2_ref, uo2, ul2, 16)):
        rows = tm // d
        for r in range(d):
            dst_l[pl.ds(r, rows, stride=d), :] = src_l[r]
            for c in range(nch):
                dst_o[c, pl.ds(r, rows, stride=d), :] = src_o[r, :, c * LANES:(c + 1) * LANES].astype(F32)
    def per_head_to_lanes(w):
        hi = w.astype(BF16)
        lo = (w - hi.astype(F32)).astype(BF16)
        return jnp.dot(jnp.concatenate([hi, lo], axis=1), exp_ref[...], preferred_element_type=F32)

    def sig(v):
        return 0.5 * jnp.tanh(0.5 * v) + 0.5

    def rows_to_logits(r0, nr):
        rs = slice(r0, r0 + nr)
        la, lb, lc = l0_ref[rs, :], ul1[rs, :], ul2[rs, :]
        mx = jnp.maximum(jnp.maximum(la, lb), lc)
        ea, eb, ec = jnp.exp(la - mx), jnp.exp(lb - mx), jnp.exp(lc - mx)
        inv = 1.0 / (ea + eb + ec)
        wa, wb, wc = per_head_to_lanes(ea * inv), per_head_to_lanes(eb * inv), per_head_to_lanes(ec * inv)
        ych = []
        for c in range(nch):
            cl = slice(c * LANES, (c + 1) * LANES)
            ych.append((wa[:, cl] * o0_ref[rs, cl].astype(F32) + wb[:, cl] * uo1[c, rs, :]
                        + wc[:, cl] * uo2[c, rs, :]).astype(BF16))
        y = jnp.concatenate(ych, axis=1)
        a = jnp.dot(y, wbd_ref[...], preferred_element_type=F32)
        bsw = jnp.dot(ys_ref[rs, :], wbs_ref[...], preferred_element_type=F32)
        mixed = sig(gates_ref[rs, 0:1024]) * a.astype(BF16) + sig(gates_ref[rs, 1024:2048]) * bsw.astype(BF16)
        x1 = x_ref[rs, :] + jnp.dot(mixed, wo_ref[...], preferred_element_type=F32)
        x1_ref[rs, :] = x1
        ms = jnp.mean(x1 * x1, axis=-1, keepdims=True)
        h2 = x1 * lax.rsqrt(ms + RMS_EPS) * gffn_ref[...]
        for c in range(h2.shape[1] // LANES):
            h2_ref[pl.ds(r0 * ROW_TILE + c, nr, stride=ROW_TILE), :] = h2[:, c * LANES:(c + 1) * LANES]
        h_hi = h2.astype(BF16)
        h_lo = (h2 - h_hi.astype(F32)).astype(BF16)
        hw = jnp.dot(h_hi, wrt_ref[...], preferred_element_type=F32)
        return (hw[:, :LANES] + hw[:, LANES:]
                + jnp.dot(h_lo, wrt_ref[:, :LANES], preferred_element_type=F32) + brt_ref[...])

    logit_buf[...] = rows_to_logits(0, tm)


def _merge_route(x, o0, l0, o1, l1, o2, l2, ys, gates, w_br_dil, w_br_swa, w_out, g_ffn,
                 w_group, b_group, w_router, b_router, *, tm):
    b, t, dm = x.shape
    nt = t // tm
    wrt = jnp.zeros((dm, LANES), F32).at[:, :MOE_GROUPS].set(w_group).at[
        :, ROUTER_LANE0:ROUTER_LANE0 + N_EXPERTS].set(w_router)
    brt = jnp.zeros((1, LANES), F32).at[0, :MOE_GROUPS].set(b_group).at[
        0, ROUTER_LANE0:ROUTER_LANE0 + N_EXPERTS].set(b_router)
    wrt_hi = wrt.astype(BF16)
    wrt_lo = (wrt - wrt_hi.astype(F32)).astype(BF16)
    wrt = jnp.concatenate([wrt_hi, wrt_lo], axis=1)

    nsteps = b * nt + 1

    def tile_of(s, lag=0):
        tile = jnp.clip(s - lag, 0, nsteps - 2)
        return tile // nt, tile % nt

    def tok(width, lag=0):
        return pl.BlockSpec((None, tm, width), lambda s: (*tile_of(s, lag), 0))

    def stream(d, width):
        return pl.BlockSpec((None, d, tm // d, width), lambda s: (tile_of(s)[0], 0, tile_of(s)[1], 0))

    def const(shape):
        return pl.BlockSpec(shape, lambda s: (0,) * len(shape), pipeline_mode=pl.Buffered(1))

    head_of_lane = np.arange(GROUP_W) // HEAD_DIM
    spread = (np.arange(LANES)[:, None] == head_of_lane[None, :]).astype(np.float32)
    expand = jnp.asarray(np.concatenate([spread, spread], axis=0), dtype=BF16)

    nch = GROUP_W // LANES
    return pl.pallas_call(
        functools.partial(_merge_kernel, tm=tm),
        out_shape=(jax.ShapeDtypeStruct((b, t, dm), F32), jax.ShapeDtypeStruct((b, t * ROW_TILE, LANES), F32),
                   jax.ShapeDtypeStruct((b, t, LANES), F32), jax.ShapeDtypeStruct((8, LANES), F32)),
        grid=(nsteps,),
        in_specs=[tok(dm), tok(GROUP_W), tok(LANES), stream(4, GROUP_W), stream(4, LANES),
                  stream(16, GROUP_W), stream(16, LANES), tok(SWA_Q_W), tok(2048),
                  const((GROUP_W, dm)), const((SWA_Q_W, dm)), const((dm, dm)), const((1, dm)),
                  const((dm, 2 * LANES)), const((1, LANES)), const((2 * LANES, GROUP_W)), const((tm, tm))],
        out_specs=(tok(dm), pl.BlockSpec((None, tm * ROW_TILE, LANES), lambda s: (*tile_of(s), 0)),
                   tok(LANES, lag=1), pl.BlockSpec((8, LANES), lambda s: (0, 0))),
        scratch_shapes=[pltpu.VMEM((nch, tm, LANES), F32), pltpu.VMEM((tm, LANES), F32),
                        pltpu.VMEM((nch, tm, LANES), F32), pltpu.VMEM((tm, LANES), F32),
                        pltpu.VMEM((8, LANES), F32), pltpu.VMEM((tm, LANES), F32)],
        compiler_params=pltpu.CompilerParams(
            dimension_semantics=("arbitrary",), vmem_limit_bytes=VMEM_LIMIT),
        name="merge_route",
    )(x, o0, l0, o1, l1, o2, l2, ys, gates, w_br_dil.astype(BF16), w_br_swa.astype(BF16), w_out.astype(BF16),
      g_ffn.reshape(1, dm), wrt, brt, expand, jnp.asarray(np.tril(np.ones((tm, tm), np.float32), -1), dtype=BF16))


def _dispatch_kernel(pos_ref, zoff_ref, nt_ref, h_ref, xs_ref, zbuf, sem, zsem, *, td, tme, nt_max):
    step = pl.program_id(0)

    def _zero_copy(e):
        off = pl.multiple_of(zoff_ref[e] * ROW_TILE, tme * ROW_TILE)
        return pltpu.make_async_copy(zbuf, xs_ref.at[pl.ds(off, tme * ROW_TILE)], zsem)

    def _row_copy(src_row, dst_row):
        return pltpu.make_async_copy(h_ref.at[pl.ds(pl.multiple_of(src_row * ROW_TILE, ROW_TILE), ROW_TILE)],
                                     xs_ref.at[pl.ds(pl.multiple_of(dst_row * ROW_TILE, ROW_TILE), ROW_TILE)], sem)

    def _wait_rows():
        pltpu.make_async_copy(h_ref, xs_ref.at[pl.ds(0, td * ROW_TILE)], sem).wait()

    def _tail_copy(tile):
        off = pl.multiple_of(tile * (tme * ROW_TILE), tme * ROW_TILE)
        return pltpu.make_async_copy(zbuf, xs_ref.at[pl.ds(off, tme * ROW_TILE)], zsem)

    @pl.when(step == 0)
    def _():
        zbuf[...] = jnp.zeros_like(zbuf)
        for e in range(N_CLASSES):
            _zero_copy(e).start()
        for e in range(N_CLASSES):
            _zero_copy(e).wait()

        def tail(tile, carry):
            _tail_copy(tile).start()
            _tail_copy(tile).wait()
            return carry

        lax.fori_loop(nt_ref[0], nt_max, tail, 0)

    def issue(j, carry):
        _row_copy(j, pos_ref[0, 0, j]).start()
        return carry

    lax.fori_loop(0, td, issue, 0, unroll=8)
    _wait_rows()


def _dispatch(h2, pos, zoff, ntiles, nslot, *, td, tme):
    n = h2.shape[0] // ROW_TILE
    nb = n // td
    grid_spec = pltpu.PrefetchScalarGridSpec(
        num_scalar_prefetch=0,
        grid=(nb,),
        in_specs=[
            pl.BlockSpec((1, 1, td), lambda i: (i, 0, 0), memory_space=pltpu.SMEM),
            pl.BlockSpec(memory_space=pltpu.SMEM),
            pl.BlockSpec(memory_space=pltpu.SMEM),
            pl.BlockSpec((td * ROW_TILE, LANES), lambda i: (i, 0)),
        ],
        out_specs=pl.BlockSpec(memory_space=pl.ANY),
        scratch_shapes=[pltpu.VMEM((tme * ROW_TILE, LANES), F32), pltpu.SemaphoreType.DMA(()),
                        pltpu.SemaphoreType.DMA(())],
    )
    return pl.pallas_call(
        functools.partial(_dispatch_kernel, td=td, tme=tme, nt_max=nslot // tme),
        out_shape=jax.ShapeDtypeStruct((nslot * ROW_TILE, LANES), F32),
        grid_spec=grid_spec,
        compiler_params=pltpu.CompilerParams(dimension_semantics=("arbitrary",), has_side_effects=True,
                                             vmem_limit_bytes=VMEM_LIMIT),
        name="dispatch",
    )(pos.reshape(nb, 1, td), zoff, ntiles, h2)


PAIR_ROWS = 2 * ROW_TILE


PLAN_FIRST, PLAN_SLOT, PLAN_NEXT, PLAN_HAS_NEXT = range(4)


def _expert_kernel(ta_ref, tb_ref, tblk_ref, trows_ref, plan_ref, nt_ref, x_ref, wg_hbm, wu_hbm, wd_hbm, y_ref,
                   wg_buf, wu_buf, wd_buf, wsem, *, tme):
    i = pl.program_id(0)
    live = i < nt_ref[0]
    half_rows = tme // 2

    def weight_copies(which, expert, slot):
        return [pltpu.make_async_copy(src.at[expert], buf.at[which, slot], wsem.at[which, slot])
                for src, buf in ((wg_hbm, wg_buf), (wu_hbm, wu_buf), (wd_hbm, wd_buf))]

    slots = []
    for which, t_ref in ((0, ta_ref), (1, tb_ref)):
        base = (i * 2 + which) * 4
        first = plan_ref[base + PLAN_FIRST] == 1
        slot = plan_ref[base + PLAN_SLOT]
        slots.append(slot)

        @pl.when(first & (i == 0))
        def _():
            for cp in weight_copies(which, t_ref[i], slot):
                cp.start()

        @pl.when(first)
        def _():
            for cp in weight_copies(which, t_ref[i], slot):
                cp.wait()

        @pl.when(first & (plan_ref[base + PLAN_HAS_NEXT] == 1))
        def _():
            for cp in weight_copies(which, plan_ref[base + PLAN_NEXT], 1 - slot):
                cp.start()

    def run(nrows):
        xb = jnp.concatenate([x_ref[pl.ds(c, nrows, stride=ROW_TILE), :] for c in range(ROW_TILE)], axis=1)
        nw = 2 * LANES
        for half in range(2):
            wg_ref, wu_ref, wd_ref = (buf.at[half, slots[half]] for buf in (wg_buf, wu_buf, wd_buf))
            act = []
            for c0 in range(0, D_EXPERT, nw):
                g = jnp.dot(xb, wg_ref[:, c0:c0 + nw], preferred_element_type=F32)
                u = jnp.dot(xb, wu_ref[:, c0:c0 + nw], preferred_element_type=F32)
                act.append(g * jax.nn.sigmoid(g) * u)
            a = jnp.concatenate(act, axis=1)
            for c0 in range(0, ROW_TILE * LANES, nw):
                y = jnp.dot(a, wd_ref[:, c0:c0 + nw], preferred_element_type=F32)
                for k in range(nw // LANES):
                    c = c0 // LANES + k
                    y_ref[pl.ds(half * ROW_TILE + c, nrows, stride=PAIR_ROWS), :] = y[:, k * LANES:(k + 1) * LANES]

    @pl.when(live & (trows_ref[i] > half_rows))
    def _():
        run(tme)

    @pl.when(live & (trows_ref[i] <= half_rows))
    def _():
        run(half_rows)
        y_ref[half_rows * PAIR_ROWS:, :] = jnp.zeros((half_rows * PAIR_ROWS, LANES), F32)

    @pl.when(jnp.logical_not(live))
    def _():
        y_ref[...] = jnp.zeros_like(y_ref)


def _weight_plan(tile_expert):
    nt = tile_expert.shape[0]
    first = jnp.concatenate([jnp.ones((1,), jnp.int32), (tile_expert[1:] != tile_expert[:-1]).astype(jnp.int32)])
    run_id = jnp.cumsum(first) - 1
    next_start = jnp.sum((run_id[None, :] <= run_id[:, None]).astype(jnp.int32), axis=1)
    has_next = (run_id < run_id[-1]).astype(jnp.int32)
    nxt = jnp.take(tile_expert, jnp.minimum(next_start, nt - 1))
    return jnp.stack([first, run_id % 2, nxt, has_next], axis=1).astype(jnp.int32)


def _experts(xs, tile_a, tile_b, tile_block, tile_rows, ntiles, w_e_gate, w_e_up, w_e_down, *, tme):
    nslot = xs.shape[0] // ROW_TILE
    dm = ROW_TILE * LANES
    nt = nslot // tme
    wg, wu, wd = w_e_gate, w_e_up, w_e_down
    plan =jnp.concatenate([_weight_plan(tile_a), _weight_plan(tile_b)], axis=1).reshape(-1)

    grid_spec = pltpu.PrefetchScalarGridSpec(
        num_scalar_prefetch=6,
        grid=(nt,),
        in_specs=[
            pl.BlockSpec((tme * ROW_TILE, LANES), lambda i, ta, tb, tk, tr, pn, n: (tk[i], 0)),
            pl.BlockSpec(memory_space=pl.ANY), pl.BlockSpec(memory_space=pl.ANY), pl.BlockSpec(memory_space=pl.ANY),
        ],
        out_specs=pl.BlockSpec((tme * PAIR_ROWS, LANES), lambda i, ta, tb, tk, tr, pn, n: (i, 0)),
        scratch_shapes=[pltpu.VMEM((2, 2, dm, D_EXPERT), F32), pltpu.VMEM((2, 2, dm, D_EXPERT), F32),
                        pltpu.VMEM((2, 2, D_EXPERT, dm), F32), pltpu.SemaphoreType.DMA((2, 2))],
    )
    return pl.pallas_call(
        functools.partial(_expert_kernel, tme=tme),
        out_shape=jax.ShapeDtypeStruct((nslot * PAIR_ROWS, LANES), F32),
        grid_spec=grid_spec,
        compiler_params=pltpu.CompilerParams(dimension_semantics=("arbitrary",), vmem_limit_bytes=VMEM_LIMIT),
        name="experts",
    )(tile_a, tile_b, tile_block, tile_rows, plan, ntiles, xs, wg, wu, wd)


def _combine_kernel(pos_ref, posn_ref, ys_ref, ysflat_ref, x1_ref, route_ref, gfin_ref, out_ref, ybuf, sem, *, tc):
    i = pl.program_id(0)
    nb = pl.num_programs(0)
    slot = i % 2

    def gather(p_ref, s):
        def issue(j, carry):
            dst = pl.multiple_of(j * ROW_TILE, ROW_TILE)
            pltpu.make_async_copy(ys_ref.at[p_ref[0, 0, j]], ybuf.at[s, :, pl.ds(dst, ROW_TILE), :],
                                  sem.at[s]).start()
            return carry

        lax.fori_loop(0, tc, issue, 0, unroll=8)

    @pl.when(i == 0)
    def _():
        gather(pos_ref, 0)

    @pl.when(i + 1 < nb)
    def _():
        gather(posn_ref, 1 - slot)

    for plane in range(2):
        pltpu.make_async_copy(ysflat_ref.at[pl.ds(0, tc * ROW_TILE)], ybuf.at[slot, plane], sem.at[slot]).wait()
    rec = route_ref[...]
    wa = rec[:, ROUTE_WA:ROUTE_WA + 1]
    wb = rec[:, ROUTE_WB:ROUTE_WB + 1]
    ya = jnp.concatenate([ybuf[slot, 0, pl.ds(c, tc, stride=ROW_TILE), :] for c in range(ROW_TILE)], axis=1)
    yb = jnp.concatenate([ybuf[slot, 1, pl.ds(c, tc, stride=ROW_TILE), :] for c in range(ROW_TILE)], axis=1)
    z = x1_ref[...] + wa * ya + wb * yb
    ms = jnp.mean(z * z, axis=-1, keepdims=True)
    out_ref[...] = z * lax.rsqrt(ms + RMS_EPS) * gfin_ref[...]


def _combine(ys, pos, x1, route, g_final, *, tc):
    n, dm = x1.shape
    nb = n // tc
    grid_spec = pltpu.PrefetchScalarGridSpec(
        num_scalar_prefetch=0,
        grid=(nb,),
        in_specs=[
            pl.BlockSpec((1, 1, tc), lambda i: (i, 0, 0), memory_space=pltpu.SMEM),
            pl.BlockSpec((1, 1, tc), lambda i: (jnp.minimum(i + 1, nb - 1), 0, 0), memory_space=pltpu.SMEM),
            pl.BlockSpec(memory_space=pl.ANY),
            pl.BlockSpec(memory_space=pl.ANY),
            pl.BlockSpec((tc, dm), lambda i: (i, 0)),
            pl.BlockSpec((tc, LANES), lambda i: (i, 0)),
            pl.BlockSpec((1, dm), lambda i: (0, 0)),
        ],
        out_specs=pl.BlockSpec((tc, dm), lambda i: (i, 0)),
        scratch_shapes=[pltpu.VMEM((2, 2, tc * ROW_TILE, LANES), F32), pltpu.SemaphoreType.DMA((2,))],
    )
    nslot = ys.shape[0] // PAIR_ROWS
    return pl.pallas_call(
        functools.partial(_combine_kernel, tc=tc),
        out_shape=jax.ShapeDtypeStruct((n, dm), F32),
        grid_spec=grid_spec,
        compiler_params=pltpu.CompilerParams(dimension_semantics=("arbitrary",), vmem_limit_bytes=VMEM_LIMIT),
        name="combine",
    )(pos.reshape(nb, 1, tc), pos.reshape(nb, 1, tc), ys.reshape(nslot, 2, ROW_TILE, LANES), ys, x1, route,
      g_final.reshape(1, dm))


def _slot_layout(route, counts, n, tme):
    cnt = counts[0, :N_CLASSES].astype(jnp.int32)
    tiles_per = (cnt + tme - 1) // tme
    tile_end = jnp.cumsum(tiles_per)
    seg_base = (tile_end - tiles_per) * tme
    ntiles = tile_end[-1]
    nt_max = n // tme + N_CLASSES
    tid = jnp.arange(nt_max, dtype=jnp.int32)
    live = jnp.minimum(tid, ntiles - 1)
    tile_class = jnp.sum((tile_end[None, :] <= live[:, None]).astype(jnp.int32), axis=1)
    tile_a = jnp.take(jnp.asarray(CLASS_EXPERT_A), tile_class)
    tile_b = jnp.take(jnp.asarray(CLASS_EXPERT_B), tile_class)
    first_tile = jnp.take(tile_end - tiles_per, tile_class)
    tile_rows = jnp.clip(jnp.take(cnt, tile_class) - (live - first_tile) * tme, 0, tme)
    cls = route[:, ROUTE_CLASS].astype(jnp.int32)
    rank = route[:, ROUTE_RANK].astype(jnp.int32)
    base = jnp.sum(jnp.where(cls[:, None] == jnp.arange(N_CLASSES, dtype=jnp.int32)[None, :], seg_base[None, :], 0),
                   axis=1)
    pos = base + rank
    zoff = jnp.maximum(tile_end - 1, 0) * tme
    return (pos, zoff.astype(jnp.int32), tile_a.astype(jnp.int32), tile_b.astype(jnp.int32),
            live.astype(jnp.int32), tile_rows.astype(jnp.int32), ntiles.reshape(1).astype(jnp.int32))


def kernel(x, g_mix, w_in, sinks, w_br_dil, w_br_swa, w_out, g_ffn, w_group, b_group, w_router, b_router,
           w_e_gate, w_e_up, w_e_down, g_final):
    b, t, dm = x.shape
    n = b * t
    tme = 256
    qkv0, swa, gates, qkv4, qkv16 = _inproj(x, g_mix[0], w_in[0], tm=512)
    o0, l0 = _dil_attention(qkv0.reshape(b, 1, t, QKV_W), 0, tq=2048)
    o1, l1 = _dil_attention(qkv4, 1, tq=2048)
    o2, l2 = _dil_attention(qkv16, 2, tq=2048)
    ys = _swa_attention(swa, sinks[0], tq=2048)
    x1, h2, route, counts = _merge_route(
        x, o0.reshape(b, t, GROUP_W), l0.reshape(b, t, LANES), o1, l1, o2, l2, ys, gates,
        w_br_dil[0], w_br_swa[0], w_out[0], g_ffn[0], w_group[0], b_group[0], w_router[0], b_router[0], tm=512)
    route = route.reshape(n, LANES)
    pos, zoff, tile_a, tile_b, tile_block, tile_rows, ntiles = _slot_layout(route, counts, n, tme)
    nslot = n + N_CLASSES * tme
    xs = _dispatch(h2.reshape(n * ROW_TILE, LANES), pos, zoff, ntiles, nslot, td=4096, tme=tme)
    yslots = _experts(xs, tile_a, tile_b, tile_block, tile_rows, ntiles, w_e_gate[0], w_e_up[0], w_e_down[0],
                      tme=tme)
    out = _combine(yslots, pos, x1.reshape(n, dm), route, g_final, tc=512)
    return out.reshape(b, t, dm)
```

```python
import functools

import numpy as np
import jax
import jax.numpy as jnp
from jax import lax
from jax.experimental import pallas as pl
from jax.experimental.pallas import tpu as pltpu

F32 = jnp.float32
BF16 = jnp.bfloat16

HEAD_DIM = 64
BAND = 128
DIL_PATTERNS = ((128, 1), (512, 4), (2048, 16))
DIL_HEADS_PER_GROUP = 8
DIL_HEADS = 24
GROUP_W = DIL_HEADS_PER_GROUP * HEAD_DIM
QKV_W = 3 * GROUP_W
SWA_WINDOW = 128
SWA_Q_HEADS = 16
SWA_KV_HEADS = 2
SWA_Q_W = SWA_Q_HEADS * HEAD_DIM
SWA_W = SWA_Q_W + 4 * 128
MOE_GROUPS = 4
EXPERTS_PER_GROUP = 8
N_EXPERTS = 32
D_EXPERT = 512
RMS_EPS = 1e-6
LANES = 128
NEG_INF = float("-inf")
LOG2E = 1.4426950408889634
LN2 = 0.6931471805599453
ROW_TILE = 8

VMEM_LIMIT = 56 * 1024 * 1024


def _alibi_slopes(n):
    return (2.0 ** (-8.0 * np.arange(1, n + 1) / n)).astype(np.float32)


def _band_bias(slopes, max_back, unit):
    a = np.arange(BAND)[:, None]
    c = np.arange(2 * BAND)[None, :]
    delta = a + BAND - c
    band = (delta >= 0) & (delta <= max_back)
    pen = (-slopes[:, None, None] * (delta * unit)[None] * LOG2E).astype(np.float32)
    full = np.where(band[None], pen, -np.inf).astype(np.float32)
    first = np.where((band & (c >= BAND))[None], pen, -np.inf).astype(np.float32)
    return np.stack([full, first], axis=0)


def _inproj_kernel(x_ref, g_ref, w_ref, qkv0_ref, swa_ref, gates_ref, qkv4_ref, qkv16_ref, hs_ref, *, tm):
    x = x_ref[...]
    ms = jnp.mean(x * x, axis=-1, keepdims=True)
    h = x * lax.rsqrt(ms + RMS_EPS) * g_ref[...]
    nchunk = h.shape[1] // LANES
    for c in range(nchunk):
        hs_ref[c] = h[:, c * LANES:(c + 1) * LANES]
    hb = h.astype(BF16)

    def proj(lhs, c0, width):
        return jnp.dot(lhs, w_ref[:, c0:c0 + width], preferred_element_type=F32)

    for j in range(3):
        qkv0_ref[:, j * GROUP_W:(j + 1) * GROUP_W] = proj(hb, COL_QKV_D[j], GROUP_W).astype(BF16)
    for c0 in range(0, SWA_Q_W, 512):
        swa_ref[:, c0:c0 + 512] = proj(hb, COL_Q_S + c0, 512).astype(BF16)
    kv = proj(hb, COL_KV_S, 4 * HEAD_DIM)
    part = [kv[:, j * HEAD_DIM:(j + 1) * HEAD_DIM] for j in range(4)]
    swa_ref[:, SWA_Q_W:SWA_W] = jnp.concatenate(
        [part[0], part[0], part[1], part[1], part[2], part[2], part[3], part[3]], axis=1).astype(BF16)
    for c0 in range(0, 2048, 512):
        gates_ref[:, c0:c0 + 512] = proj(hb, COL_GATES + c0, 512).astype(BF16)
    for gi, out_ref, d in ((1, qkv4_ref, 4), (2, qkv16_ref, 16)):
        rows = tm // d
        hp = jnp.concatenate(
            [jnp.concatenate([hs_ref[c, pl.ds(r, rows, stride=d), :] for c in range(nchunk)], axis=1)
             for r in range(d)], axis=0).astype(BF16)
        for j in range(3):
            res = proj(hp, COL_QKV_D[j] + gi * GROUP_W, GROUP_W).astype(BF16)
            for r in range(d):
                out_ref[r, :, j * GROUP_W:(j + 1) * GROUP_W] = res[r * rows:(r + 1) * rows]


COL_QKV_D = (0, 1536, 3072)
COL_Q_S = 4608
COL_KV_S = 5632
COL_GATES = 5888
IN_WIDTH = 7936
Q_SCALE = HEAD_DIM ** -0.5 * LOG2E


def _prep_w_in(w_in):
    col = np.arange(IN_WIDTH)
    is_q = (col < COL_QKV_D[1]) | ((col >= COL_Q_S) & (col < COL_KV_S))
    colscale = np.where(is_q, Q_SCALE, 1.0).astype(np.float32)
    return (w_in * colscale[None, :]).astype(BF16)


def _inproj(x, g_mix, w_in, *, tm):
    b, t, dm = x.shape
    wp = _prep_w_in(w_in)
    nw = wp.shape[1]
    nt = t // tm
    out_shape = (
        jax.ShapeDtypeStruct((b, t, QKV_W), BF16),
        jax.ShapeDtypeStruct((b, t, SWA_W), BF16),
        jax.ShapeDtypeStruct((b, t, 2048), BF16),
        jax.ShapeDtypeStruct((b, 4, t // 4, QKV_W), BF16),
        jax.ShapeDtypeStruct((b, 16, t // 16, QKV_W), BF16),
    )
    return pl.pallas_call(
        functools.partial(_inproj_kernel, tm=tm),
        out_shape=out_shape,
        grid=(b, nt),
        in_specs=[
            pl.BlockSpec((None, tm, dm), lambda bi, i: (bi, i, 0)),
            pl.BlockSpec((1, dm), lambda bi, i: (0, 0)),
            pl.BlockSpec((dm, nw), lambda bi, i: (0, 0), pipeline_mode=pl.Buffered(1)),
        ],
        out_specs=(
            pl.BlockSpec((None, tm, QKV_W), lambda bi, i: (bi, i, 0)),
            pl.BlockSpec((None, tm, SWA_W), lambda bi, i: (bi, i, 0)),
            pl.BlockSpec((None, tm, 2048), lambda bi, i: (bi, i, 0)),
            pl.BlockSpec((None, 4, tm // 4, QKV_W), lambda bi, i: (bi, 0, i, 0)),
            pl.BlockSpec((None, 16, tm // 16, QKV_W), lambda bi, i: (bi, 0, i, 0)),
        ),
        scratch_shapes=[pltpu.VMEM((dm // LANES, tm, LANES), F32)],
        compiler_params=pltpu.CompilerParams(
            dimension_semantics=("arbitrary", "arbitrary"), vmem_limit_bytes=VMEM_LIMIT),
        name="inproj",
    )(x, g_mix.reshape(1, dm), wp)


def _dil_attn_kernel(q_all, kc_all, vc_all, kp_all, vp_all, bias_ref, o_all_ref, lse_all_ref, kbuf_all, vbuf_all,
                     *, tq, sb):
    for st in range(sb):
        _dil_attn_stream(q_all.at[st], kc_all.at[st], vc_all.at[st], kp_all.at[st], vp_all.at[st], bias_ref,
                         o_all_ref.at[st], lse_all_ref.at[st], kbuf_all.at[st], vbuf_all.at[st], tq=tq)


def _dil_attn_stream(q_ref, kc_ref, vc_ref, kp_ref, vp_ref, bias_ref, o_ref, lse_ref, kbuf, vbuf, *, tq):
    i = pl.program_id(2)
    kbuf[0:BAND] = kp_ref[...]
    kbuf[BAND:BAND + tq] = kc_ref[...]
    vbuf[0:BAND] = vp_ref[...]
    vbuf[BAND:BAND + tq] = vc_ref[...]
    lane = lax.broadcasted_iota(jnp.int32, (BAND, LANES), 1)
    lo = lane < HEAD_DIM
    zero = jnp.zeros((BAND, LANES), BF16)
    for qb in range(tq // BAND):
        sel = jnp.where(i == 0, 1, 0) if qb == 0 else 0
        rq = slice(qb * BAND, (qb + 1) * BAND)
        rk = slice(qb * BAND, qb * BAND + 2 * BAND)
        lse_parts = []
        for pr in range(GROUP_W // LANES):
            cl = slice(pr * LANES, (pr + 1) * LANES)
            q2 = q_ref[rq, cl]
            kk = kbuf[rk, cl]
            vv = vbuf[rk, cl]
            s_all = lax.dot_general(jnp.concatenate([jnp.where(lo, q2, zero), jnp.where(lo, zero, q2)], axis=0), kk,
                                    (((1,), (1,)), ((), ())), preferred_element_type=F32)
            probs, stats = [], []
            for hh in range(2):
                s = s_all[hh * BAND:(hh + 1) * BAND] + bias_ref[sel, 2 * pr + hh]
                m = jnp.max(s, axis=-1, keepdims=True)
                p = jnp.exp2(s - m)
                l = jnp.sum(p, axis=-1, keepdims=True)
                probs.append(p.astype(BF16))
                stats.append((1.0 / l, (m + jnp.log2(l)) * LN2))
            o_all = jnp.dot(jnp.concatenate(probs, axis=0), vv, preferred_element_type=F32)
            res = [(o_all[hh * BAND:(hh + 1) * BAND] * stats[hh][0], stats[hh][1]) for hh in range(2)]
            o_ref[rq, cl] = jnp.where(lo, res[0][0], res[1][0]).astype(o_ref.dtype)
            lse_parts.append(jnp.where(lane == 2 * pr, res[0][1], jnp.where(lane == 2 * pr + 1, res[1][1], 0.0)))
        lse_ref[rq, :] = (lse_parts[0] + lse_parts[1]) + (lse_parts[2] + lse_parts[3])


def _dil_attention(qkv, gi, *, tq):
    b, d, l, _ = qkv.shape
    window, dil = DIL_PATTERNS[gi]
    assert dil == d
    rows_per_step = tq
    tq = min(tq, l)
    sb = min(d, max(1, rows_per_step // tq))
    nq = tq // BAND
    slopes = _alibi_slopes(DIL_HEADS)[gi * 8:(gi + 1) * 8]
    bias = jnp.asarray(_band_bias(slopes, window // dil, dil))

    def cur(c):
        return pl.BlockSpec((None, sb, tq, GROUP_W), lambda bi, r, i: (bi, r, i, c))

    def prev(c):
        return pl.BlockSpec((None, sb, BAND, GROUP_W), lambda bi, r, i: (bi, r, jnp.maximum(i * nq - 1, 0), c))

    return pl.pallas_call(
        functools.partial(_dil_attn_kernel, tq=tq, sb=sb),
        out_shape=(jax.ShapeDtypeStruct((b, d, l, GROUP_W), BF16),
                   jax.ShapeDtypeStruct((b, d, l, LANES), F32)),
        grid=(b, d // sb, l // tq),
        in_specs=[cur(0), cur(1), cur(2), prev(1), prev(2),
                  pl.BlockSpec((2, 8, BAND, 2 * BAND), lambda bi, r, i: (0, 0, 0, 0))],
        out_specs=(cur(0), pl.BlockSpec((None, sb, tq, LANES), lambda bi, r, i: (bi, r, i, 0))),
        scratch_shapes=[pltpu.VMEM((sb, BAND + tq, GROUP_W), BF16), pltpu.VMEM((sb, BAND + tq, GROUP_W), BF16)],
        compiler_params=pltpu.CompilerParams(
            dimension_semantics=("arbitrary", "arbitrary", "arbitrary"), vmem_limit_bytes=VMEM_LIMIT),
        name=f"dil_attn_{gi}",
    )(qkv, qkv, qkv, qkv, qkv, bias)


def _swa_attn_kernel(sink_ref, qlo_ref, qhi_ref, kvc_ref, kvp_ref, bias_ref, o_ref, kvbuf, *, tq):
    i = pl.program_id(1)
    kvbuf[0:BAND] = kvp_ref[...]
    kvbuf[BAND:BAND + tq] = kvc_ref[...]
    lo = lax.broadcasted_iota(jnp.int32, (BAND, LANES), 1) < HEAD_DIM
    zero = jnp.zeros((BAND, LANES), BF16)
    for qb in range(tq // BAND):
        sel = jnp.where(i == 0, 1, 0) if qb == 0 else 0
        rq = slice(qb * BAND, (qb + 1) * BAND)
        rk = slice(qb * BAND, qb * BAND + 2 * BAND)
        for g in range(SWA_KV_HEADS):
            q_ref = qlo_ref if g == 0 else qhi_ref
            kk = kvbuf[rk, g * LANES:(g + 1) * LANES]
            vv = kvbuf[rk, (2 + g) * LANES:(3 + g) * LANES]
            masked = []
            for pp in range(4):
                q2 = q_ref[rq, pp * LANES:(pp + 1) * LANES]
                masked += [jnp.where(lo, q2, zero), jnp.where(lo, zero, q2)]
            s_all = lax.dot_general(jnp.concatenate(masked, axis=0), kk, (((1,), (1,)), ((), ())),
                                    preferred_element_type=F32)
            probs, inv = [], []
            for j in range(8):
                h = 8 * g + j
                sink = sink_ref[h] * LOG2E
                s = s_all[j * BAND:(j + 1) * BAND] + bias_ref[sel, h]
                m = jnp.maximum(jnp.max(s, axis=-1, keepdims=True), sink)
                p = jnp.exp2(s - m)
                inv.append(1.0 / (jnp.sum(p, axis=-1, keepdims=True) + jnp.exp2(sink - m)))
                probs.append(p.astype(BF16))
            o_all = jnp.dot(jnp.concatenate(probs, axis=0), vv, preferred_element_type=F32)
            for pp in range(4):
                oa = o_all[(2 * pp) * BAND:(2 * pp + 1) * BAND] * inv[2 * pp]
                ob = o_all[(2 * pp + 1) * BAND:(2 * pp + 2) * BAND] * inv[2 * pp + 1]
                o_ref[rq, (4 * g + pp) * LANES:(4 * g + pp + 1) * LANES] = jnp.where(lo, oa, ob).astype(o_ref.dtype)


def _swa_attention(swa, sinks, *, tq):
    b, t, _ = swa.shape
    tq = min(tq, t)
    nq = tq // BAND
    bias = jnp.asarray(_band_bias(_alibi_slopes(SWA_Q_HEADS), SWA_WINDOW - 1, 1))
    grid_spec = pltpu.PrefetchScalarGridSpec(
        num_scalar_prefetch=1,
        grid=(b, t // tq),
        in_specs=[
            pl.BlockSpec((None, tq, 512), lambda bi, i, s: (bi, i, 0)),
            pl.BlockSpec((None, tq, 512), lambda bi, i, s: (bi, i, 1)),
            pl.BlockSpec((None, tq, 512), lambda bi, i, s: (bi, i, 2)),
            pl.BlockSpec((None, BAND, 512), lambda bi, i, s: (bi, jnp.maximum(i * nq - 1, 0), 2)),
            pl.BlockSpec((2, SWA_Q_HEADS, BAND, 2 * BAND), lambda bi, i, s: (0, 0, 0, 0)),
        ],
        out_specs=pl.BlockSpec((None, tq, SWA_Q_W), lambda bi, i, s: (bi, i, 0)),
        scratch_shapes=[pltpu.VMEM((BAND + tq, 512), BF16)],
    )
    return pl.pallas_call(
        functools.partial(_swa_attn_kernel, tq=tq),
        out_shape=jax.ShapeDtypeStruct((b, t, SWA_Q_W), BF16),
        grid_spec=grid_spec,
        compiler_params=pltpu.CompilerParams(
            dimension_semantics=("arbitrary", "arbitrary"), vmem_limit_bytes=VMEM_LIMIT),
        name="swa_attn",
    )(sinks.astype(F32), swa, swa, swa, swa, bias)


ROUTE_CLASS, ROUTE_RANK, ROUTE_WA, ROUTE_WB = range(4)
ROUTER_LANE0 = MOE_GROUPS
PAIRS_PER_GROUP = EXPERTS_PER_GROUP * (EXPERTS_PER_GROUP - 1) // 2
N_CLASSES = MOE_GROUPS * PAIRS_PER_GROUP
_PAIRS = [(a, b) for a in range(EXPERTS_PER_GROUP) for b in range(a + 1, EXPERTS_PER_GROUP)]
CLASS_EXPERT_A = np.array([g * EXPERTS_PER_GROUP + a for g in range(MOE_GROUPS) for a, _ in _PAIRS], np.int32)
CLASS_EXPERT_B = np.array([g * EXPERTS_PER_GROUP + b for g in range(MOE_GROUPS) for _, b in _PAIRS], np.int32)


def _route(logits, carry_ref, ltri_ref, tm, active):
    lane = lax.broadcasted_iota(jnp.int32, (tm, LANES), 1)
    lanef = lane.astype(F32)

    def first_argmax(v):
        m = jnp.max(v, axis=-1, keepdims=True)
        return m, jnp.min(jnp.where(v == m, lanef, float(LANES)), axis=-1, keepdims=True)

    gl = jnp.where(lane < MOE_GROUPS, logits, NEG_INF)
    gmax, gidx = first_argmax(gl)
    g_w = 1.0 / jnp.sum(jnp.exp(gl - gmax), axis=-1, keepdims=True)
    e_lane = lane - ROUTER_LANE0
    lane_group = (e_lane >> 3).astype(F32)
    in_group = (e_lane >= 0) & (e_lane < N_EXPERTS) & (lane_group == gidx)
    el = jnp.where(in_group, logits, NEG_INF)
    m1, i1 = first_argmax(el)
    m2, i2 = first_argmax(jnp.where(lanef == i1, NEG_INF, el))
    tt = jnp.exp(m2 - m1)
    w1 = g_w / (1.0 + tt)
    w2 = g_w * tt / (1.0 + tt)
    first = float(ROUTER_LANE0) + float(EXPERTS_PER_GROUP) * gidx
    e1 = i1 - first
    e2 = i2 - first
    swap = e2 < e1
    ea = jnp.minimum(e1, e2)
    eb = jnp.maximum(e1, e2)
    wa = jnp.where(swap, w2, w1)
    wb = jnp.where(swap, w1, w2)
    pair = ea * (float(2 * EXPERTS_PER_GROUP - 1) - ea) * 0.5 + (eb - ea - 1.0)
    cls = float(PAIRS_PER_GROUP) * gidx + pair
    oh = jnp.where(lanef == cls, 1.0, 0.0)
    before = jnp.dot(ltri_ref[...], oh.astype(BF16), preferred_element_type=F32) + carry_ref[0:1, :]
    rank = jnp.sum(oh * before, axis=-1, keepdims=True)
    carry_ref[...] = carry_ref[...] + jnp.where(active, jnp.sum(oh, axis=0, keepdims=True), 0.0)
    rec = jnp.zeros((tm, LANES), F32)
    for ln, val in ((ROUTE_CLASS, cls), (ROUTE_RANK, rank), (ROUTE_WA, wa), (ROUTE_WB, wb)):
        rec = jnp.where(lane == ln, val, rec)
    return rec


def _merge_kernel(x_ref, o0_ref, l0_ref, o1_ref, l1_ref, o2_ref, l2_ref, ys_ref, gates_ref,
                  wbd_ref, wbs_ref, wo_ref, gffn_ref, wrt_ref, brt_ref, exp_ref, ltri_ref,
                  x1_ref, h2_ref, route_ref, cnt_ref,
                  uo1, ul1, uo2, ul2, carry_ref, logit_buf, *, tm):
    step = pl.program_id(0)

    @pl.when(step == 0)
    def _():
        carry_ref[...] = jnp.zeros_like(carry_ref)
        logit_buf[...] = jnp.zeros_like(logit_buf)

    route_ref[...] = _route(logit_buf[...], carry_ref, ltri_ref, tm, step > 0)
    cnt_ref[...] = carry_ref[...]

    nch = GROUP_W // LANES
    for src_o, src_l, dst_o, dst_l, d in ((o1_ref, l1_ref, uo1, ul1, 4), (o2_ref, l2_ref, uo2, ul2, 16)):
        rows = tm // d
        for r in range(d):
            dst_l[pl.ds(r, rows, stride=d), :] = src_l[r]
            for c in range(nch):
                dst_o[c, pl.ds(r, rows, stride=d), :] = src_o[r, :, c * LANES:(c + 1) * LANES].astype(F32)
    def per_head_to_lanes(w):
        hi = w.astype(BF16)
        lo = (w - hi.astype(F32)).astype(BF16)
        return jnp.dot(jnp.concatenate([hi, lo], axis=1), exp_ref[...], preferred_element_type=F32)

    def sig(v):
        return 0.5 * jnp.tanh(0.5 * v) + 0.5

    def rows_to_logits(r0, nr):
        rs = slice(r0, r0 + nr)
        la, lb, lc = l0_ref[rs, :], ul1[rs, :], ul2[rs, :]
        mx = jnp.maximum(jnp.maximum(la, lb), lc)
        ea, eb, ec = jnp.exp(la - mx), jnp.exp(lb - mx), jnp.exp(lc - mx)
        inv = 1.0 / (ea + eb + ec)
        wa, wb, wc = per_head_to_lanes(ea * inv), per_head_to_lanes(eb * inv), per_head_to_lanes(ec * inv)
        ych = []
        for c in range(nch):
            cl = slice(c * LANES, (c + 1) * LANES)
            ych.append((wa[:, cl] * o0_ref[rs, cl].astype(F32) + wb[:, cl] * uo1[c, rs, :]
                        + wc[:, cl] * uo2[c, rs, :]).astype(BF16))
        y = jnp.concatenate(ych, axis=1)
        a = jnp.dot(y, wbd_ref[...], preferred_element_type=F32)
        bsw = jnp.dot(ys_ref[rs, :], wbs_ref[...], preferred_element_type=F32)
        mixed = sig(gates_ref[rs, 0:1024]) * a.astype(BF16) + sig(gates_ref[rs, 1024:2048]) * bsw.astype(BF16)
        x1 = x_ref[rs, :] + jnp.dot(mixed, wo_ref[...], preferred_element_type=F32)
        x1_ref[rs, :] = x1
        ms = jnp.mean(x1 * x1, axis=-1, keepdims=True)
        h2 = x1 * lax.rsqrt(ms + RMS_EPS) * gffn_ref[...]
        for c in range(h2.shape[1] // LANES):
            h2_ref[pl.ds(r0 * ROW_TILE + c, nr, stride=ROW_TILE), :] = h2[:, c * LANES:(c + 1) * LANES]
        h_hi = h2.astype(BF16)
        h_lo = (h2 - h_hi.astype(F32)).astype(BF16)
        hw = jnp.dot(h_hi, wrt_ref[...], preferred_element_type=F32)
        return (hw[:, :LANES] + hw[:, LANES:]
                + jnp.dot(h_lo, wrt_ref[:, :LANES], preferred_element_type=F32) + brt_ref[...])

    logit_buf[...] = rows_to_logits(0, tm)


def _merge_route(x, o0, l0, o1, l1, o2, l2, ys, gates, w_br_dil, w_br_swa, w_out, g_ffn,
                 w_group, b_group, w_router, b_router, *, tm):
    b, t, dm = x.shape
    nt = t // tm
    wrt = jnp.zeros((dm, LANES), F32).at[:, :MOE_GROUPS].set(w_group).at[
        :, ROUTER_LANE0:ROUTER_LANE0 + N_EXPERTS].set(w_router)
    brt = jnp.zeros((1, LANES), F32).at[0, :MOE_GROUPS].set(b_group).at[
        0, ROUTER_LANE0:ROUTER_LANE0 + N_EXPERTS].set(b_router)
    wrt_hi = wrt.astype(BF16)
    wrt_lo = (wrt - wrt_hi.astype(F32)).astype(BF16)
    wrt = jnp.concatenate([wrt_hi, wrt_lo], axis=1)

    nsteps = b * nt + 1

    def tile_of(s, lag=0):
        tile = jnp.clip(s - lag, 0, nsteps - 2)
        return tile // nt, tile % nt

    def tok(width, lag=0):
        return pl.BlockSpec((None, tm, width), lambda s: (*tile_of(s, lag), 0))

    def stream(d, width):
        return pl.BlockSpec((None, d, tm // d, width), lambda s: (tile_of(s)[0], 0, tile_of(s)[1], 0))

    def const(shape):
        return pl.BlockSpec(shape, lambda s: (0,) * len(shape), pipeline_mode=pl.Buffered(1))

    head_of_lane = np.arange(GROUP_W) // HEAD_DIM
    spread = (np.arange(LANES)[:, None] == head_of_lane[None, :]).astype(np.float32)
    expand = jnp.asarray(np.concatenate([spread, spread], axis=0), dtype=BF16)

    nch = GROUP_W // LANES
    return pl.pallas_call(
        functools.partial(_merge_kernel, tm=tm),
        out_shape=(jax.ShapeDtypeStruct((b, t, dm), F32), jax.ShapeDtypeStruct((b, t * ROW_TILE, LANES), F32),
                   jax.ShapeDtypeStruct((b, t, LANES), F32), jax.ShapeDtypeStruct((8, LANES), F32)),
        grid=(nsteps,),
        in_specs=[tok(dm), tok(GROUP_W), tok(LANES), stream(4, GROUP_W), stream(4, LANES),
                  stream(16, GROUP_W), stream(16, LANES), tok(SWA_Q_W), tok(2048),
                  const((GROUP_W, dm)), const((SWA_Q_W, dm)), const((dm, dm)), const((1, dm)),
                  const((dm, 2 * LANES)), const((1, LANES)), const((2 * LANES, GROUP_W)), const((tm, tm))],
        out_specs=(tok(dm), pl.BlockSpec((None, tm * ROW_TILE, LANES), lambda s: (*tile_of(s), 0)),
                   tok(LANES, lag=1), pl.BlockSpec((8, LANES), lambda s: (0, 0))),
        scratch_shapes=[pltpu.VMEM((nch, tm, LANES), F32), pltpu.VMEM((tm, LANES), F32),
                        pltpu.VMEM((nch, tm, LANES), F32), pltpu.VMEM((tm, LANES), F32),
                        pltpu.VMEM((8, LANES), F32), pltpu.VMEM((tm, LANES), F32)],
        compiler_params=pltpu.CompilerParams(
            dimension_semantics=("arbitrary",), vmem_limit_bytes=VMEM_LIMIT),
        name="merge_route",
    )(x, o0, l0, o1, l1, o2, l2, ys, gates, w_br_dil.astype(BF16), w_br_swa.astype(BF16), w_out.astype(BF16),
      g_ffn.reshape(1, dm), wrt, brt, expand, jnp.asarray(np.tril(np.ones((tm, tm), np.float32), -1), dtype=BF16))


def _dispatch_kernel(pos_ref, zoff_ref, nt_ref, h_ref, xs_ref, zbuf, sem, zsem, *, td, tme, nt_max):
    step = pl.program_id(0)

    def _zero_copy(e):
        off = pl.multiple_of(zoff_ref[e] * ROW_TILE, tme * ROW_TILE)
        return pltpu.make_async_copy(zbuf, xs_ref.at[pl.ds(off, tme * ROW_TILE)], zsem)

    def _row_copy(src_row, dst_row):
        return pltpu.make_async_copy(h_ref.at[pl.ds(pl.multiple_of(src_row * ROW_TILE, ROW_TILE), ROW_TILE)],
                                     xs_ref.at[pl.ds(pl.multiple_of(dst_row * ROW_TILE, ROW_TILE), ROW_TILE)], sem)

    def _wait_rows():
        pltpu.make_async_copy(h_ref, xs_ref.at[pl.ds(0, td * ROW_TILE)], sem).wait()

    def _tail_copy(tile):
        off = pl.multiple_of(tile * (tme * ROW_TILE), tme * ROW_TILE)
        return pltpu.make_async_copy(zbuf, xs_ref.at[pl.ds(off, tme * ROW_TILE)], zsem)

    @pl.when(step == 0)
    def _():
        zbuf[...] = jnp.zeros_like(zbuf)
        for e in range(N_CLASSES):
            _zero_copy(e).start()
        for e in range(N_CLASSES):
            _zero_copy(e).wait()

        def tail(tile, carry):
            _tail_copy(tile).start()
            _tail_copy(tile).wait()
            return carry

        lax.fori_loop(nt_ref[0], nt_max, tail, 0)

    def issue(j, carry):
        _row_copy(j, pos_ref[0, 0, j]).start()
        return carry

    lax.fori_loop(0, td, issue, 0, unroll=8)
    _wait_rows()


def _dispatch(h2, pos, zoff, ntiles, nslot, *, td, tme):
    n = h2.shape[0] // ROW_TILE
    nb = n // td
    grid_spec = pltpu.PrefetchScalarGridSpec(
        num_scalar_prefetch=0,
        grid=(nb,),
        in_specs=[
            pl.BlockSpec((1, 1, td), lambda i: (i, 0, 0), memory_space=pltpu.SMEM),
            pl.BlockSpec(memory_space=pltpu.SMEM),
            pl.BlockSpec(memory_space=pltpu.SMEM),
            pl.BlockSpec((td * ROW_TILE, LANES), lambda i: (i, 0)),
        ],
        out_specs=pl.BlockSpec(memory_space=pl.ANY),
        scratch_shapes=[pltpu.VMEM((tme * ROW_TILE, LANES), F32), pltpu.SemaphoreType.DMA(()),
                        pltpu.SemaphoreType.DMA(())],
    )
    return pl.pallas_call(
        functools.partial(_dispatch_kernel, td=td, tme=tme, nt_max=nslot // tme),
        out_shape=jax.ShapeDtypeStruct((nslot * ROW_TILE, LANES), F32),
        grid_spec=grid_spec,
        compiler_params=pltpu.CompilerParams(dimension_semantics=("arbitrary",), has_side_effects=True,
                                             vmem_limit_bytes=VMEM_LIMIT),
        name="dispatch",
    )(pos.reshape(nb, 1, td), zoff, ntiles, h2)


PAIR_ROWS = 2 * ROW_TILE


PLAN_FIRST, PLAN_SLOT, PLAN_NEXT, PLAN_HAS_NEXT = range(4)


def _expert_kernel(ta_ref, tb_ref, tblk_ref, trows_ref, plan_ref, nt_ref, x_ref, wg_hbm, wu_hbm, wd_hbm, y_ref,
                   wg_buf, wu_buf, wd_buf, wsem, yscr, *, tme):
    i = pl.program_id(0)
    live = i < nt_ref[0]
    half_rows = tme // 2

    def weight_copies(which, expert, slot):
        return [pltpu.make_async_copy(src.at[expert], buf.at[which, slot], wsem.at[which, slot])
                for src, buf in ((wg_hbm, wg_buf), (wu_hbm, wu_buf), (wd_hbm, wd_buf))]

    slots = []
    for which, t_ref in ((0, ta_ref), (1, tb_ref)):
        base = (i * 2 + which) * 4
        first = plan_ref[base + PLAN_FIRST] == 1
        slot = plan_ref[base + PLAN_SLOT]
        slots.append(slot)

        @pl.when(first & (i == 0))
        def _():
            for cp in weight_copies(which, t_ref[i], slot):
                cp.start()

        @pl.when(first)
        def _():
            for cp in weight_copies(which, t_ref[i], slot):
                cp.wait()

        @pl.when(first & (plan_ref[base + PLAN_HAS_NEXT] == 1))
        def _():
            for cp in weight_copies(which, plan_ref[base + PLAN_NEXT], 1 - slot):
                cp.start()

    def run(nrows):
        xb = jnp.concatenate([x_ref[pl.ds(c, nrows, stride=ROW_TILE), :] for c in range(ROW_TILE)], axis=1)
        nw = 2 * LANES
        for half in range(2):
            wg_ref, wu_ref, wd_ref = (buf.at[half, slots[half]] for buf in (wg_buf, wu_buf, wd_buf))
            act = []
            for c0 in range(0, D_EXPERT, nw):
                g = jnp.dot(xb, wg_ref[:, c0:c0 + nw], preferred_element_type=F32)
                u = jnp.dot(xb, wu_ref[:, c0:c0 + nw], preferred_element_type=F32)
                act.append(g * jax.nn.sigmoid(g) * u)
            a = jnp.concatenate(act, axis=1)
            for c0 in range(0, ROW_TILE * LANES, nw):
                y = jnp.dot(a, wd_ref[:, c0:c0 + nw], preferred_element_type=F32)
                for k in range(nw // LANES):
                    c = c0 // LANES + k
                    yscr[half, pl.ds(c, nrows, stride=ROW_TILE), :] = y[:, k * LANES:(k + 1) * LANES]
            y_ref[0:nrows, half] = yscr[half, 0:nrows * ROW_TILE, :].reshape(nrows, ROW_TILE, LANES)

    @pl.when(live & (trows_ref[i] > half_rows))
    def _():
        run(tme)

    @pl.when(live & (trows_ref[i] <= half_rows))
    def _():
        run(half_rows)
        y_ref[half_rows:] = jnp.zeros((half_rows, 2, ROW_TILE, LANES), F32)

    @pl.when(jnp.logical_not(live))
    def _():
        y_ref[...] = jnp.zeros_like(y_ref)


def _weight_plan(tile_expert):
    nt = tile_expert.shape[0]
    first = jnp.concatenate([jnp.ones((1,), jnp.int32), (tile_expert[1:] != tile_expert[:-1]).astype(jnp.int32)])
    run_id = jnp.cumsum(first) - 1
    next_start = jnp.sum((run_id[None, :] <= run_id[:, None]).astype(jnp.int32), axis=1)
    has_next = (run_id < run_id[-1]).astype(jnp.int32)
    nxt = jnp.take(tile_expert, jnp.minimum(next_start, nt - 1))
    return jnp.stack([first, run_id % 2, nxt, has_next], axis=1).astype(jnp.int32)


def _experts(xs, tile_a, tile_b, tile_block, tile_rows, ntiles, w_e_gate, w_e_up, w_e_down, *, tme):
    nslot = xs.shape[0] // ROW_TILE
    dm = ROW_TILE * LANES
    nt = nslot // tme
    wg, wu, wd = w_e_gate, w_e_up, w_e_down
    plan =jnp.concatenate([_weight_plan(tile_a), _weight_plan(tile_b)], axis=1).reshape(-1)

    grid_spec = pltpu.PrefetchScalarGridSpec(
        num_scalar_prefetch=6,
        grid=(nt,),
        in_specs=[
            pl.BlockSpec((tme * ROW_TILE, LANES), lambda i, ta, tb, tk, tr, pn, n: (tk[i], 0)),
            pl.BlockSpec(memory_space=pl.ANY), pl.BlockSpec(memory_space=pl.ANY), pl.BlockSpec(memory_space=pl.ANY),
        ],
        out_specs=pl.BlockSpec((tme, 2, ROW_TILE, LANES), lambda i, ta, tb, tk, tr, pn, n: (i, 0, 0, 0)),
        scratch_shapes=[pltpu.VMEM((2, 2, dm, D_EXPERT), F32), pltpu.VMEM((2, 2, dm, D_EXPERT), F32),
                        pltpu.VMEM((2, 2, D_EXPERT, dm), F32), pltpu.SemaphoreType.DMA((2, 2)),
                        pltpu.VMEM((2, tme * ROW_TILE, LANES), F32)],
    )
    return pl.pallas_call(
        functools.partial(_expert_kernel, tme=tme),
        out_shape=jax.ShapeDtypeStruct((nslot, 2, ROW_TILE, LANES), F32),
        grid_spec=grid_spec,
        compiler_params=pltpu.CompilerParams(dimension_semantics=("arbitrary",), vmem_limit_bytes=VMEM_LIMIT),
        name="experts",
    )(tile_a, tile_b, tile_block, tile_rows, plan, ntiles, xs, wg, wu, wd)


def _combine_kernel(pos_ref, posn_ref, ys_ref, ysflat_ref, x1_ref, route_ref, gfin_ref, out_ref, ybuf, sem, *, tc):
    i = pl.program_id(0)
    nb = pl.num_programs(0)
    slot = i % 2

    def gather(p_ref, s):
        def issue(j, carry):
            dst = pl.multiple_of(j * ROW_TILE, ROW_TILE)
            pltpu.make_async_copy(ys_ref.at[p_ref[0, 0, j]], ybuf.at[s, :, pl.ds(dst, ROW_TILE), :],
                                  sem.at[s]).start()
            return carry

        lax.fori_loop(0, tc, issue, 0, unroll=8)

    @pl.when(i == 0)
    def _():
        gather(pos_ref, 0)

    @pl.when(i + 1 < nb)
    def _():
        gather(posn_ref, 1 - slot)

    for plane in range(2):
        pltpu.make_async_copy(ysflat_ref.at[pl.ds(0, tc * ROW_TILE)], ybuf.at[slot, plane], sem.at[slot]).wait()
    rec = route_ref[...]
    wa = rec[:, ROUTE_WA:ROUTE_WA + 1]
    wb = rec[:, ROUTE_WB:ROUTE_WB + 1]
    ya = jnp.concatenate([ybuf[slot, 0, pl.ds(c, tc, stride=ROW_TILE), :] for c in range(ROW_TILE)], axis=1)
    yb = jnp.concatenate([ybuf[slot, 1, pl.ds(c, tc, stride=ROW_TILE), :] for c in range(ROW_TILE)], axis=1)
    z = x1_ref[...] + wa * ya + wb * yb
    ms = jnp.mean(z * z, axis=-1, keepdims=True)
    out_ref[...] = z * lax.rsqrt(ms + RMS_EPS) * gfin_ref[...]


def _combine(ys, pos, x1, route, g_final, *, tc):
    n, dm = x1.shape
    nb = n // tc
    grid_spec = pltpu.PrefetchScalarGridSpec(
        num_scalar_prefetch=0,
        grid=(nb,),
        in_specs=[
            pl.BlockSpec((1, 1, tc), lambda i: (i, 0, 0), memory_space=pltpu.SMEM),
            pl.BlockSpec((1, 1, tc), lambda i: (jnp.minimum(i + 1, nb - 1), 0, 0), memory_space=pltpu.SMEM),
            pl.BlockSpec(memory_space=pl.ANY),
            pl.BlockSpec(memory_space=pl.ANY),
            pl.BlockSpec((tc, dm), lambda i: (i, 0)),
            pl.BlockSpec((tc, LANES), lambda i: (i, 0)),
            pl.BlockSpec((1, dm), lambda i: (0, 0)),
        ],
        out_specs=pl.BlockSpec((tc, dm), lambda i: (i, 0)),
        scratch_shapes=[pltpu.VMEM((2, 2, tc * ROW_TILE, LANES), F32), pltpu.SemaphoreType.DMA((2,))],
    )
    nslot = ys.shape[0]
    return pl.pallas_call(
        functools.partial(_combine_kernel, tc=tc),
        out_shape=jax.ShapeDtypeStruct((n, dm), F32),
        grid_spec=grid_spec,
        compiler_params=pltpu.CompilerParams(dimension_semantics=("arbitrary",), vmem_limit_bytes=VMEM_LIMIT),
        name="combine",
    )(pos.reshape(nb, 1, tc), pos.reshape(nb, 1, tc), ys, ys.reshape(nslot * PAIR_ROWS, LANES), x1, route,
      g_final.reshape(1, dm))


def _slot_layout(route, counts, n, tme):
    cnt = counts[0, :N_CLASSES].astype(jnp.int32)
    tiles_per = (cnt + tme - 1) // tme
    tile_end = jnp.cumsum(tiles_per)
    seg_base = (tile_end - tiles_per) * tme
    ntiles = tile_end[-1]
    nt_max = n // tme + N_CLASSES
    tid = jnp.arange(nt_max, dtype=jnp.int32)
    live = jnp.minimum(tid, ntiles - 1)
    tile_class = jnp.sum((tile_end[None, :] <= live[:, None]).astype(jnp.int32), axis=1)
    tile_a = jnp.take(jnp.asarray(CLASS_EXPERT_A), tile_class)
    tile_b = jnp.take(jnp.asarray(CLASS_EXPERT_B), tile_class)
    first_tile = jnp.take(tile_end - tiles_per, tile_class)
    tile_rows = jnp.clip(jnp.take(cnt, tile_class) - (live - first_tile) * tme, 0, tme)
    cls = route[:, ROUTE_CLASS].astype(jnp.int32)
    rank = route[:, ROUTE_RANK].astype(jnp.int32)
    base = jnp.sum(jnp.where(cls[:, None] == jnp.arange(N_CLASSES, dtype=jnp.int32)[None, :], seg_base[None, :], 0),
                   axis=1)
    pos = base + rank
    zoff = jnp.maximum(tile_end - 1, 0) * tme
    return (pos, zoff.astype(jnp.int32), tile_a.astype(jnp.int32), tile_b.astype(jnp.int32),
            live.astype(jnp.int32), tile_rows.astype(jnp.int32), ntiles.reshape(1).astype(jnp.int32))


def kernel(x, g_mix, w_in, sinks, w_br_dil, w_br_swa, w_out, g_ffn, w_group, b_group, w_router, b_router,
           w_e_gate, w_e_up, w_e_down, g_final):
    b, t, dm = x.shape
    n = b * t
    tme = 256
    qkv0, swa, gates, qkv4, qkv16 = _inproj(x, g_mix[0], w_in[0], tm=512)
    o0, l0 = _dil_attention(qkv0.reshape(b, 1, t, QKV_W), 0, tq=2048)
    o1, l1 = _dil_attention(qkv4, 1, tq=2048)
    o2, l2 = _dil_attention(qkv16, 2, tq=2048)
    ys = _swa_attention(swa, sinks[0], tq=2048)
    x1, h2, route, counts = _merge_route(
        x, o0.reshape(b, t, GROUP_W), l0.reshape(b, t, LANES), o1, l1, o2, l2, ys, gates,
        w_br_dil[0], w_br_swa[0], w_out[0], g_ffn[0], w_group[0], b_group[0], w_router[0], b_router[0], tm=512)
    route = route.reshape(n, LANES)
    pos, zoff, tile_a, tile_b, tile_block, tile_rows, ntiles = _slot_layout(route, counts, n, tme)
    nslot = n + N_CLASSES * tme
    xs = _dispatch(h2.reshape(n * ROW_TILE, LANES), pos, zoff, ntiles, nslot, td=4096, tme=tme)
    yslots = _experts(xs, tile_a, tile_b, tile_block, tile_rows, ntiles, w_e_gate[0], w_e_up[0], w_e_down[0],
                      tme=tme)
    out = _combine(yslots, pos, x1.reshape(n, dm), route, g_final, tc=512)
    return out.reshape(b, t, dm)
```

```python
import functools

import numpy as np
import jax
import jax.numpy as jnp
from jax import lax
from jax.experimental import pallas as pl
from jax.experimental.pallas import tpu as pltpu

F32 = jnp.float32
BF16 = jnp.bfloat16

HEAD_DIM = 64
BAND = 128
DIL_PATTERNS = ((128, 1), (512, 4), (2048, 16))
DIL_HEADS_PER_GROUP = 8
DIL_HEADS = 24
GROUP_W = DIL_HEADS_PER_GROUP * HEAD_DIM
QKV_W = 3 * GROUP_W
SWA_WINDOW = 128
SWA_Q_HEADS = 16
SWA_KV_HEADS = 2
SWA_Q_W = SWA_Q_HEADS * HEAD_DIM
SWA_W = SWA_Q_W + 4 * 128
MOE_GROUPS = 4
EXPERTS_PER_GROUP = 8
N_EXPERTS = 32
D_EXPERT = 512
RMS_EPS = 1e-6
LANES = 128
NEG_INF = float("-inf")
LOG2E = 1.4426950408889634
LN2 = 0.6931471805599453
ROW_TILE = 8
ROWS_PER_ISSUE = 8

VMEM_LIMIT = 56 * 1024 * 1024


def _alibi_slopes(n):
    return (2.0 ** (-8.0 * np.arange(1, n + 1) / n)).astype(np.float32)


def _band_bias(slopes, max_back, unit):
    a = np.arange(BAND)[:, None]
    c = np.arange(2 * BAND)[None, :]
    delta = a + BAND - c
    band = (delta >= 0) & (delta <= max_back)
    pen = (-slopes[:, None, None] * (delta * unit)[None] * LOG2E).astype(np.float32)
    full = np.where(band[None], pen, -np.inf).astype(np.float32)
    first = np.where((band & (c >= BAND))[None], pen, -np.inf).astype(np.float32)
    return np.stack([full, first], axis=0)


def _inproj_kernel(x_ref, g_ref, w_ref, qkv0_ref, swa_ref, gates_ref, qkv4_ref, qkv16_ref, hs_ref, *, tm):
    x = x_ref[...]
    ms = jnp.mean(x * x, axis=-1, keepdims=True)
    h = x * lax.rsqrt(ms + RMS_EPS) * g_ref[...]
    nchunk = h.shape[1] // LANES
    for c in range(nchunk):
        hs_ref[c] = h[:, c * LANES:(c + 1) * LANES]
    hb = h.astype(BF16)

    def proj(lhs, c0, width):
        return jnp.dot(lhs, w_ref[:, c0:c0 + width], preferred_element_type=F32)

    for j in range(3):
        qkv0_ref[:, j * GROUP_W:(j + 1) * GROUP_W] = proj(hb, COL_QKV_D[j], GROUP_W).astype(BF16)
    for c0 in range(0, SWA_Q_W, 512):
        swa_ref[:, c0:c0 + 512] = proj(hb, COL_Q_S + c0, 512).astype(BF16)
    kv = proj(hb, COL_KV_S, 4 * HEAD_DIM)
    part = [kv[:, j * HEAD_DIM:(j + 1) * HEAD_DIM] for j in range(4)]
    swa_ref[:, SWA_Q_W:SWA_W] = jnp.concatenate(
        [part[0], part[0], part[1], part[1], part[2], part[2], part[3], part[3]], axis=1).astype(BF16)
    for c0 in range(0, 2048, 512):
        gates_ref[:, c0:c0 + 512] = proj(hb, COL_GATES + c0, 512).astype(BF16)
    for gi, out_ref, d in ((1, qkv4_ref, 4), (2, qkv16_ref, 16)):
        rows = tm // d
        hp = jnp.concatenate(
            [jnp.concatenate([hs_ref[c, pl.ds(r, rows, stride=d), :] for c in range(nchunk)], axis=1)
             for r in range(d)], axis=0).astype(BF16)
        for j in range(3):
            res = proj(hp, COL_QKV_D[j] + gi * GROUP_W, GROUP_W).astype(BF16)
            for r in range(d):
                out_ref[r, :, j * GROUP_W:(j + 1) * GROUP_W] = res[r * rows:(r + 1) * rows]


COL_QKV_D = (0, 1536, 3072)
COL_Q_S = 4608
COL_KV_S = 5632
COL_GATES = 5888
IN_WIDTH = 7936
Q_SCALE = HEAD_DIM ** -0.5 * LOG2E


def _prep_w_in(w_in):
    col = np.arange(IN_WIDTH)
    is_q = (col < COL_QKV_D[1]) | ((col >= COL_Q_S) & (col < COL_KV_S))
    colscale = np.where(is_q, Q_SCALE, 1.0).astype(np.float32)
    return (w_in * colscale[None, :]).astype(BF16)


def _inproj(x, g_mix, w_in, *, tm):
    b, t, dm = x.shape
    wp = _prep_w_in(w_in)
    nw = wp.shape[1]
    nt = t // tm
    out_shape = (
        jax.ShapeDtypeStruct((b, t, QKV_W), BF16),
        jax.ShapeDtypeStruct((b, t, SWA_W), BF16),
        jax.ShapeDtypeStruct((b, t, 2048), BF16),
        jax.ShapeDtypeStruct((b, 4, t // 4, QKV_W), BF16),
        jax.ShapeDtypeStruct((b, 16, t // 16, QKV_W), BF16),
    )
    return pl.pallas_call(
        functools.partial(_inproj_kernel, tm=tm),
        out_shape=out_shape,
        grid=(b, nt),
        in_specs=[
            pl.BlockSpec((None, tm, dm), lambda bi, i: (bi, i, 0)),
            pl.BlockSpec((1, dm), lambda bi, i: (0, 0)),
            pl.BlockSpec((dm, nw), lambda bi, i: (0, 0), pipeline_mode=pl.Buffered(1)),
        ],
        out_specs=(
            pl.BlockSpec((None, tm, QKV_W), lambda bi, i: (bi, i, 0)),
            pl.BlockSpec((None, tm, SWA_W), lambda bi, i: (bi, i, 0)),
            pl.BlockSpec((None, tm, 2048), lambda bi, i: (bi, i, 0)),
            pl.BlockSpec((None, 4, tm // 4, QKV_W), lambda bi, i: (bi, 0, i, 0)),
            pl.BlockSpec((None, 16, tm // 16, QKV_W), lambda bi, i: (bi, 0, i, 0)),
        ),
        scratch_shapes=[pltpu.VMEM((dm // LANES, tm, LANES), F32)],
        compiler_params=pltpu.CompilerParams(
            dimension_semantics=("arbitrary", "arbitrary"), vmem_limit_bytes=VMEM_LIMIT),
        name="inproj",
    )(x, g_mix.reshape(1, dm), wp)


def _dil_attn_kernel(q_all, kc_all, vc_all, kp_all, vp_all, bias_ref, o_all_ref, lse_all_ref, kbuf_all, vbuf_all,
                     *, tq, sb):
    for st in range(sb):
        _dil_attn_stream(q_all.at[st], kc_all.at[st], vc_all.at[st], kp_all.at[st], vp_all.at[st], bias_ref,
                         o_all_ref.at[st], lse_all_ref.at[st], kbuf_all.at[st], vbuf_all.at[st], tq=tq)


def _dil_attn_stream(q_ref, kc_ref, vc_ref, kp_ref, vp_ref, bias_ref, o_ref, lse_ref, kbuf, vbuf, *, tq):
    i = pl.program_id(2)
    kbuf[0:BAND] = kp_ref[...]
    kbuf[BAND:BAND + tq] = kc_ref[...]
    vbuf[0:BAND] = vp_ref[...]
    vbuf[BAND:BAND + tq] = vc_ref[...]
    lane = lax.broadcasted_iota(jnp.int32, (BAND, LANES), 1)
    lo = lane < HEAD_DIM
    zero = jnp.zeros((BAND, LANES), BF16)
    for qb in range(tq // BAND):
        sel = jnp.where(i == 0, 1, 0) if qb == 0 else 0
        rq = slice(qb * BAND, (qb + 1) * BAND)
        rk = slice(qb * BAND, qb * BAND + 2 * BAND)
        lse_parts = []
        for pr in range(GROUP_W // LANES):
            cl = slice(pr * LANES, (pr + 1) * LANES)
            q2 = q_ref[rq, cl]
            kk = kbuf[rk, cl]
            vv = vbuf[rk, cl]
            s_all = lax.dot_general(jnp.concatenate([jnp.where(lo, q2, zero), jnp.where(lo, zero, q2)], axis=0), kk,
                                    (((1,), (1,)), ((), ())), preferred_element_type=F32)
            probs, stats = [], []
            for hh in range(2):
                s = s_all[hh * BAND:(hh + 1) * BAND] + bias_ref[sel, 2 * pr + hh]
                m = jnp.max(s, axis=-1, keepdims=True)
                p = jnp.exp2(s - m)
                l = jnp.sum(p, axis=-1, keepdims=True)
                probs.append(p.astype(BF16))
                stats.append((1.0 / l, (m + jnp.log2(l)) * LN2))
            o_all = jnp.dot(jnp.concatenate(probs, axis=0), vv, preferred_element_type=F32)
            res = [(o_all[hh * BAND:(hh + 1) * BAND] * stats[hh][0], stats[hh][1]) for hh in range(2)]
            o_ref[rq, cl] = jnp.where(lo, res[0][0], res[1][0]).astype(o_ref.dtype)
            lse_parts.append(jnp.where(lane == 2 * pr, res[0][1], jnp.where(lane == 2 * pr + 1, res[1][1], 0.0)))
        lse_ref[rq, :] = (lse_parts[0] + lse_parts[1]) + (lse_parts[2] + lse_parts[3])


def _dil_attention(qkv, gi, *, tq):
    b, d, l, _ = qkv.shape
    window, dil = DIL_PATTERNS[gi]
    assert dil == d
    rows_per_step = tq
    tq = min(tq, l)
    sb = min(d, max(1, rows_per_step // tq))
    nq = tq // BAND
    slopes = _alibi_slopes(DIL_HEADS)[gi * 8:(gi + 1) * 8]
    bias = jnp.asarray(_band_bias(slopes, window // dil, dil))

    def cur(c):
        return pl.BlockSpec((None, sb, tq, GROUP_W), lambda bi, r, i: (bi, r, i, c))

    def prev(c):
        return pl.BlockSpec((None, sb, BAND, GROUP_W), lambda bi, r, i: (bi, r, jnp.maximum(i * nq - 1, 0), c))

    return pl.pallas_call(
        functools.partial(_dil_attn_kernel, tq=tq, sb=sb),
        out_shape=(jax.ShapeDtypeStruct((b, d, l, GROUP_W), BF16),
                   jax.ShapeDtypeStruct((b, d, l, LANES), F32)),
        grid=(b, d // sb, l // tq),
        in_specs=[cur(0), cur(1), cur(2), prev(1), prev(2),
                  pl.BlockSpec((2, 8, BAND, 2 * BAND), lambda bi, r, i: (0, 0, 0, 0))],
        out_specs=(cur(0), pl.BlockSpec((None, sb, tq, LANES), lambda bi, r, i: (bi, r, i, 0))),
        scratch_shapes=[pltpu.VMEM((sb, BAND + tq, GROUP_W), BF16), pltpu.VMEM((sb, BAND + tq, GROUP_W), BF16)],
        compiler_params=pltpu.CompilerParams(
            dimension_semantics=("arbitrary", "arbitrary", "arbitrary"), vmem_limit_bytes=VMEM_LIMIT),
        name=f"dil_attn_{gi}",
    )(qkv, qkv, qkv, qkv, qkv, bias)


def _swa_attn_kernel(sink_ref, qlo_ref, qhi_ref, kvc_ref, kvp_ref, bias_ref, o_ref, kvbuf, *, tq):
    i = pl.program_id(1)
    kvbuf[0:BAND] = kvp_ref[...]
    kvbuf[BAND:BAND + tq] = kvc_ref[...]
    lo = lax.broadcasted_iota(jnp.int32, (BAND, LANES), 1) < HEAD_DIM
    zero = jnp.zeros((BAND, LANES), BF16)
    for qb in range(tq // BAND):
        sel = jnp.where(i == 0, 1, 0) if qb == 0 else 0
        rq = slice(qb * BAND, (qb + 1) * BAND)
        rk = slice(qb * BAND, qb * BAND + 2 * BAND)
        for g in range(SWA_KV_HEADS):
            q_ref = qlo_ref if g == 0 else qhi_ref
            kk = kvbuf[rk, g * LANES:(g + 1) * LANES]
            vv = kvbuf[rk, (2 + g) * LANES:(3 + g) * LANES]
            masked = []
            for pp in range(4):
                q2 = q_ref[rq, pp * LANES:(pp + 1) * LANES]
                masked += [jnp.where(lo, q2, zero), jnp.where(lo, zero, q2)]
            s_all = lax.dot_general(jnp.concatenate(masked, axis=0), kk, (((1,), (1,)), ((), ())),
                                    preferred_element_type=F32)
            probs, inv = [], []
            for j in range(8):
                h = 8 * g + j
                sink = sink_ref[h] * LOG2E
                s = s_all[j * BAND:(j + 1) * BAND] + bias_ref[sel, h]
                m = jnp.maximum(jnp.max(s, axis=-1, keepdims=True), sink)
                p = jnp.exp2(s - m)
                inv.append(1.0 / (jnp.sum(p, axis=-1, keepdims=True) + jnp.exp2(sink - m)))
                probs.append(p.astype(BF16))
            o_all = jnp.dot(jnp.concatenate(probs, axis=0), vv, preferred_element_type=F32)
            for pp in range(4):
                oa = o_all[(2 * pp) * BAND:(2 * pp + 1) * BAND] * inv[2 * pp]
                ob = o_all[(2 * pp + 1) * BAND:(2 * pp + 2) * BAND] * inv[2 * pp + 1]
                o_ref[rq, (4 * g + pp) * LANES:(4 * g + pp + 1) * LANES] = jnp.where(lo, oa, ob).astype(o_ref.dtype)


def _swa_attention(swa, sinks, *, tq):
    b, t, _ = swa.shape
    tq = min(tq, t)
    nq = tq // BAND
    bias = jnp.asarray(_band_bias(_alibi_slopes(SWA_Q_HEADS), SWA_WINDOW - 1, 1))
    grid_spec = pltpu.PrefetchScalarGridSpec(
        num_scalar_prefetch=1,
        grid=(b, t // tq),
        in_specs=[
            pl.BlockSpec((None, tq, 512), lambda bi, i, s: (bi, i, 0)),
            pl.BlockSpec((None, tq, 512), lambda bi, i, s: (bi, i, 1)),
            pl.BlockSpec((None, tq, 512), lambda bi, i, s: (bi, i, 2)),
            pl.BlockSpec((None, BAND, 512), lambda bi, i, s: (bi, jnp.maximum(i * nq - 1, 0), 2)),
            pl.BlockSpec((2, SWA_Q_HEADS, BAND, 2 * BAND), lambda bi, i, s: (0, 0, 0, 0)),
        ],
        out_specs=pl.BlockSpec((None, tq, SWA_Q_W), lambda bi, i, s: (bi, i, 0)),
        scratch_shapes=[pltpu.VMEM((BAND + tq, 512), BF16)],
    )
    return pl.pallas_call(
        functools.partial(_swa_attn_kernel, tq=tq),
        out_shape=jax.ShapeDtypeStruct((b, t, SWA_Q_W), BF16),
        grid_spec=grid_spec,
        compiler_params=pltpu.CompilerParams(
            dimension_semantics=("arbitrary", "arbitrary"), vmem_limit_bytes=VMEM_LIMIT),
        name="swa_attn",
    )(sinks.astype(F32), swa, swa, swa, swa, bias)


ROUTE_CLASS, ROUTE_RANK, ROUTE_WA, ROUTE_WB = range(4)
ROUTER_LANE0 = MOE_GROUPS
PAIRS_PER_GROUP = EXPERTS_PER_GROUP * (EXPERTS_PER_GROUP - 1) // 2
N_CLASSES = MOE_GROUPS * PAIRS_PER_GROUP
_PAIRS = [(a, b) for a in range(EXPERTS_PER_GROUP) for b in range(a + 1, EXPERTS_PER_GROUP)]
CLASS_EXPERT_A = np.array([g * EXPERTS_PER_GROUP + a for g in range(MOE_GROUPS) for a, _ in _PAIRS], np.int32)
CLASS_EXPERT_B = np.array([g * EXPERTS_PER_GROUP + b for g in range(MOE_GROUPS) for _, b in _PAIRS], np.int32)


def _route(logits, carry_ref, ltri_ref, tm, active):
    lane = lax.broadcasted_iota(jnp.int32, (tm, LANES), 1)
    lanef = lane.astype(F32)

    def first_argmax(v):
        m = jnp.max(v, axis=-1, keepdims=True)
        return m, jnp.min(jnp.where(v == m, lanef, float(LANES)), axis=-1, keepdims=True)

    gl = jnp.where(lane < MOE_GROUPS, logits, NEG_INF)
    gmax, gidx = first_argmax(gl)
    g_w = 1.0 / jnp.sum(jnp.exp(gl - gmax), axis=-1, keepdims=True)
    e_lane = lane - ROUTER_LANE0
    lane_group = (e_lane >> 3).astype(F32)
    in_group = (e_lane >= 0) & (e_lane < N_EXPERTS) & (lane_group == gidx)
    el = jnp.where(in_group, logits, NEG_INF)
    m1, i1 = first_argmax(el)
    m2, i2 = first_argmax(jnp.where(lanef == i1, NEG_INF, el))
    tt = jnp.exp(m2 - m1)
    w1 = g_w / (1.0 + tt)
    w2 = g_w * tt / (1.0 + tt)
    first = float(ROUTER_LANE0) + float(EXPERTS_PER_GROUP) * gidx
    e1 = i1 - first
    e2 = i2 - first
    swap = e2 < e1
    ea = jnp.minimum(e1, e2)
    eb = jnp.maximum(e1, e2)
    wa = jnp.where(swap, w2, w1)
    wb = jnp.where(swap, w1, w2)
    pair = ea * (float(2 * EXPERTS_PER_GROUP - 1) - ea) * 0.5 + (eb - ea - 1.0)
    cls = float(PAIRS_PER_GROUP) * gidx + pair
    oh = jnp.where(lanef == cls, 1.0, 0.0)
    before = jnp.dot(ltri_ref[...], oh.astype(BF16), preferred_element_type=F32) + carry_ref[0:1, :]
    rank = jnp.sum(oh * before, axis=-1, keepdims=True)
    carry_ref[...] = carry_ref[...] + jnp.where(active, jnp.sum(oh, axis=0, keepdims=True), 0.0)
    rec = jnp.zeros((tm, LANES), F32)
    for ln, val in ((ROUTE_CLASS, cls), (ROUTE_RANK, rank), (ROUTE_WA, wa), (ROUTE_WB, wb)):
        rec = jnp.where(lane == ln, val, rec)
    return rec


def _merge_kernel(x_ref, o0_ref, l0_ref, o1_ref, l1_ref, o2_ref, l2_ref, ys_ref, gates_ref,
                  wbd_ref, wbs_ref, wo_ref, gffn_ref, wrt_ref, brt_ref, exp_ref, ltri_ref,
                  x1_ref, h2_ref, route_ref, cnt_ref,
                  uo1, ul1, uo2, ul2, carry_ref, logit_buf, *, tm):
    step = pl.program_id(0)

    @pl.when(step == 0)
    def _():
        carry_ref[...] = jnp.zeros_like(carry_ref)
        logit_buf[...] = jnp.zeros_like(logit_buf)

    route_ref[...] = _route(logit_buf[...], carry_ref, ltri_ref, tm, step > 0)
    cnt_ref[...] = carry_ref[...]

    nch = GROUP_W // LANES
    for src_o, src_l, dst_o, dst_l, d in ((o1_ref, l1_ref, uo1, ul1, 4), (o2_ref, l2_ref, uo2, ul2, 16)):
        rows = tm // d
        for r in range(d):
            dst_l[pl.ds(r, rows, stride=d), :] = src_l[r]
            for c in range(nch):
                dst_o[c, pl.ds(r, rows, stride=d), :] = src_o[r, :, c * LANES:(c + 1) * LANES].astype(F32)
    def per_head_to_lanes(w):
        hi = w.astype(BF16)
        lo = (w - hi.astype(F32)).astype(BF16)
        return jnp.dot(jnp.concatenate([hi, lo], axis=1), exp_ref[...], preferred_element_type=F32)

    def sig(v):
        return 0.5 * jnp.tanh(0.5 * v) + 0.5

    def rows_to_logits(r0, nr):
        rs = slice(r0, r0 + nr)
        la, lb, lc = l0_ref[rs, :], ul1[rs, :], ul2[rs, :]
        mx = jnp.maximum(jnp.maximum(la, lb), lc)
        ea, eb, ec = jnp.exp(la - mx), jnp.exp(lb - mx), jnp.exp(lc - mx)
        inv = 1.0 / (ea + eb + ec)
        wa, wb, wc = per_head_to_lanes(ea * inv), per_head_to_lanes(eb * inv), per_head_to_lanes(ec * inv)
        ych = []
        for c in range(nch):
            cl = slice(c * LANES, (c + 1) * LANES)
            ych.append((wa[:, cl] * o0_ref[rs, cl].astype(F32) + wb[:, cl] * uo1[c, rs, :]
                        + wc[:, cl] * uo2[c, rs, :]).astype(BF16))
        y = jnp.concatenate(ych, axis=1)
        a = jnp.dot(y, wbd_ref[...], preferred_element_type=F32)
        bsw = jnp.dot(ys_ref[rs, :], wbs_ref[...], preferred_element_type=F32)
        mixed = sig(gates_ref[rs, 0:1024]) * a.astype(BF16) + sig(gates_ref[rs, 1024:2048]) * bsw.astype(BF16)
        x1 = x_ref[rs, :] + jnp.dot(mixed, wo_ref[...], preferred_element_type=F32)
        x1_ref[rs, :] = x1
        ms = jnp.mean(x1 * x1, axis=-1, keepdims=True)
        h2 = x1 * lax.rsqrt(ms + RMS_EPS) * gffn_ref[...]
        for c in range(h2.shape[1] // LANES):
            h2_ref[pl.ds(r0 * ROW_TILE + c, nr, stride=ROW_TILE), :] = h2[:, c * LANES:(c + 1) * LANES]
        h_hi = h2.astype(BF16)
        h_lo = (h2 - h_hi.astype(F32)).astype(BF16)
        hw = jnp.dot(h_hi, wrt_ref[...], preferred_element_type=F32)
        return (hw[:, :LANES] + hw[:, LANES:]
                + jnp.dot(h_lo, wrt_ref[:, :LANES], preferred_element_type=F32) + brt_ref[...])

    logit_buf[...] = rows_to_logits(0, tm)


def _merge_route(x, o0, l0, o1, l1, o2, l2, ys, gates, w_br_dil, w_br_swa, w_out, g_ffn,
                 w_group, b_group, w_router, b_router, *, tm):
    b, t, dm = x.shape
    nt = t // tm
    wrt = jnp.zeros((dm, LANES), F32).at[:, :MOE_GROUPS].set(w_group).at[
        :, ROUTER_LANE0:ROUTER_LANE0 + N_EXPERTS].set(w_router)
    brt = jnp.zeros((1, LANES), F32).at[0, :MOE_GROUPS].set(b_group).at[
        0, ROUTER_LANE0:ROUTER_LANE0 + N_EXPERTS].set(b_router)
    wrt_hi = wrt.astype(BF16)
    wrt_lo = (wrt - wrt_hi.astype(F32)).astype(BF16)
    wrt = jnp.concatenate([wrt_hi, wrt_lo], axis=1)

    nsteps = b * nt + 1

    def tile_of(s, lag=0):
        tile = jnp.clip(s - lag, 0, nsteps - 2)
        return tile // nt, tile % nt

    def tok(width, lag=0):
        return pl.BlockSpec((None, tm, width), lambda s: (*tile_of(s, lag), 0))

    def stream(d, width):
        return pl.BlockSpec((None, d, tm // d, width), lambda s: (tile_of(s)[0], 0, tile_of(s)[1], 0))

    def const(shape):
        return pl.BlockSpec(shape, lambda s: (0,) * len(shape), pipeline_mode=pl.Buffered(1))

    head_of_lane = np.arange(GROUP_W) // HEAD_DIM
    spread = (np.arange(LANES)[:, None] == head_of_lane[None, :]).astype(np.float32)
    expand = jnp.asarray(np.concatenate([spread, spread], axis=0), dtype=BF16)

    nch = GROUP_W // LANES
    return pl.pallas_call(
        functools.partial(_merge_kernel, tm=tm),
        out_shape=(jax.ShapeDtypeStruct((b, t, dm), F32), jax.ShapeDtypeStruct((b, t * ROW_TILE, LANES), F32),
                   jax.ShapeDtypeStruct((b, t, LANES), F32), jax.ShapeDtypeStruct((8, LANES), F32)),
        grid=(nsteps,),
        in_specs=[tok(dm), tok(GROUP_W), tok(LANES), stream(4, GROUP_W), stream(4, LANES),
                  stream(16, GROUP_W), stream(16, LANES), tok(SWA_Q_W), tok(2048),
                  const((GROUP_W, dm)), const((SWA_Q_W, dm)), const((dm, dm)), const((1, dm)),
                  const((dm, 2 * LANES)), const((1, LANES)), const((2 * LANES, GROUP_W)), const((tm, tm))],
        out_specs=(tok(dm), pl.BlockSpec((None, tm * ROW_TILE, LANES), lambda s: (*tile_of(s), 0)),
                   tok(LANES, lag=1), pl.BlockSpec((8, LANES), lambda s: (0, 0))),
        scratch_shapes=[pltpu.VMEM((nch, tm, LANES), F32), pltpu.VMEM((tm, LANES), F32),
                        pltpu.VMEM((nch, tm, LANES), F32), pltpu.VMEM((tm, LANES), F32),
                        pltpu.VMEM((8, LANES), F32), pltpu.VMEM((tm, LANES), F32)],
        compiler_params=pltpu.CompilerParams(
            dimension_semantics=("arbitrary",), vmem_limit_bytes=VMEM_LIMIT),
        name="merge_route",
    )(x, o0, l0, o1, l1, o2, l2, ys, gates, w_br_dil.astype(BF16), w_br_swa.astype(BF16), w_out.astype(BF16),
      g_ffn.reshape(1, dm), wrt, brt, expand, jnp.asarray(np.tril(np.ones((tm, tm), np.float32), -1), dtype=BF16))


def _dispatch_kernel(pos_ref, zoff_ref, nt_ref, h_ref, xs_ref, zbuf, sem, zsem, *, td, tme, nt_max):
    step = pl.program_id(0)

    def _zero_copy(e):
        off = pl.multiple_of(zoff_ref[e] * ROW_TILE, tme * ROW_TILE)
        return pltpu.make_async_copy(zbuf, xs_ref.at[pl.ds(off, tme * ROW_TILE)], zsem)

    def _row_copy(src_row, dst_row):
        return pltpu.make_async_copy(h_ref.at[pl.ds(pl.multiple_of(src_row * ROW_TILE, ROW_TILE), ROW_TILE)],
                                     xs_ref.at[pl.ds(pl.multiple_of(dst_row * ROW_TILE, ROW_TILE), ROW_TILE)], sem)

    def _wait_rows():
        pltpu.make_async_copy(h_ref, xs_ref.at[pl.ds(0, td * ROW_TILE)], sem).wait()

    def _tail_copy(tile):
        off = pl.multiple_of(tile * (tme * ROW_TILE), tme * ROW_TILE)
        return pltpu.make_async_copy(zbuf, xs_ref.at[pl.ds(off, tme * ROW_TILE)], zsem)

    @pl.when(step == 0)
    def _():
        zbuf[...] = jnp.zeros_like(zbuf)
        for e in range(N_CLASSES):
            _zero_copy(e).start()
        for e in range(N_CLASSES):
            _zero_copy(e).wait()

        def tail(tile, carry):
            _tail_copy(tile).start()
            _tail_copy(tile).wait()
            return carry

        lax.fori_loop(nt_ref[0], nt_max, tail, 0)

    def issue(jb, carry):
        for u in range(ROWS_PER_ISSUE):
            j = jb * ROWS_PER_ISSUE + u
            _row_copy(j, pos_ref[0, 0, j]).start(priority=u % 2)
        return carry

    lax.fori_loop(0, td // ROWS_PER_ISSUE, issue, 0)
    _wait_rows()


def _dispatch(h2, pos, zoff, ntiles, nslot, *, td, tme):
    n = h2.shape[0] // ROW_TILE
    nb = n // td
    grid_spec = pltpu.PrefetchScalarGridSpec(
        num_scalar_prefetch=0,
        grid=(nb,),
        in_specs=[
            pl.BlockSpec((1, 1, td), lambda i: (i, 0, 0), memory_space=pltpu.SMEM),
            pl.BlockSpec(memory_space=pltpu.SMEM),
            pl.BlockSpec(memory_space=pltpu.SMEM),
            pl.BlockSpec((td * ROW_TILE, LANES), lambda i: (i, 0)),
        ],
        out_specs=pl.BlockSpec(memory_space=pl.ANY),
        scratch_shapes=[pltpu.VMEM((tme * ROW_TILE, LANES), F32), pltpu.SemaphoreType.DMA(()),
                        pltpu.SemaphoreType.DMA(())],
    )
    return pl.pallas_call(
        functools.partial(_dispatch_kernel, td=td, tme=tme, nt_max=nslot // tme),
        out_shape=jax.ShapeDtypeStruct((nslot * ROW_TILE, LANES), F32),
        grid_spec=grid_spec,
        compiler_params=pltpu.CompilerParams(dimension_semantics=("arbitrary",), has_side_effects=True,
                                             vmem_limit_bytes=VMEM_LIMIT),
        name="dispatch",
    )(pos.reshape(nb, 1, td), zoff, ntiles, h2)


PAIR_ROWS = 2 * ROW_TILE


PLAN_FIRST, PLAN_SLOT, PLAN_NEXT, PLAN_HAS_NEXT = range(4)


def _expert_kernel(ta_ref, tb_ref, tblk_ref, trows_ref, plan_ref, nt_ref, x_ref, wg_hbm, wu_hbm, wd_hbm, y_ref,
                   wg_buf, wu_buf, wd_buf, wsem, yscr, *, tme):
    i = pl.program_id(0)
    live = i < nt_ref[0]
    half_rows = tme // 2

    def weight_copies(which, expert, slot):
        return [pltpu.make_async_copy(src.at[expert], buf.at[which, slot], wsem.at[which, slot])
                for src, buf in ((wg_hbm, wg_buf), (wu_hbm, wu_buf), (wd_hbm, wd_buf))]

    slots = []
    for which, t_ref in ((0, ta_ref), (1, tb_ref)):
        base = (i * 2 + which) * 4
        first = plan_ref[base + PLAN_FIRST] == 1
        slot = plan_ref[base + PLAN_SLOT]
        slots.append(slot)

        @pl.when(first & (i == 0))
        def _():
            for cp in weight_copies(which, t_ref[i], slot):
                cp.start()

        @pl.when(first)
        def _():
            for cp in weight_copies(which, t_ref[i], slot):
                cp.wait()

        @pl.when(first & (plan_ref[base + PLAN_HAS_NEXT] == 1))
        def _():
            for cp in weight_copies(which, plan_ref[base + PLAN_NEXT], 1 - slot):
                cp.start()

    def run(nrows):
        xb = jnp.concatenate([x_ref[pl.ds(c, nrows, stride=ROW_TILE), :] for c in range(ROW_TILE)], axis=1)
        nw = 2 * LANES
        for half in range(2):
            wg_ref, wu_ref, wd_ref = (buf.at[half, slots[half]] for buf in (wg_buf, wu_buf, wd_buf))
            act = []
            for c0 in range(0, D_EXPERT, nw):
                g = jnp.dot(xb, wg_ref[:, c0:c0 + nw], preferred_element_type=F32)
                u = jnp.dot(xb, wu_ref[:, c0:c0 + nw], preferred_element_type=F32)
                act.append(g * jax.nn.sigmoid(g) * u)
            a = jnp.concatenate(act, axis=1)
            for c0 in range(0, ROW_TILE * LANES, nw):
                y = jnp.dot(a, wd_ref[:, c0:c0 + nw], preferred_element_type=F32)
                for k in range(nw // LANES):
                    c = c0 // LANES + k
                    yscr[half, pl.ds(c, nrows, stride=ROW_TILE), :] = y[:, k * LANES:(k + 1) * LANES]
            y_ref[0:nrows, half] = yscr[half, 0:nrows * ROW_TILE, :].reshape(nrows, ROW_TILE, LANES)

    @pl.when(live & (trows_ref[i] > half_rows))
    def _():
        run(tme)

    @pl.when(live & (trows_ref[i] <= half_rows))
    def _():
        run(half_rows)
        y_ref[half_rows:] = jnp.zeros((half_rows, 2, ROW_TILE, LANES), F32)

    @pl.when(jnp.logical_not(live))
    def _():
        y_ref[...] = jnp.zeros_like(y_ref)


def _weight_plan(tile_expert):
    nt = tile_expert.shape[0]
    first = jnp.concatenate([jnp.ones((1,), jnp.int32), (tile_expert[1:] != tile_expert[:-1]).astype(jnp.int32)])
    run_id = jnp.cumsum(first) - 1
    next_start = jnp.sum((run_id[None, :] <= run_id[:, None]).astype(jnp.int32), axis=1)
    has_next = (run_id < run_id[-1]).astype(jnp.int32)
    nxt = jnp.take(tile_expert, jnp.minimum(next_start, nt - 1))
    return jnp.stack([first, run_id % 2, nxt, has_next], axis=1).astype(jnp.int32)


def _experts(xs, tile_a, tile_b, tile_block, tile_rows, ntiles, w_e_gate, w_e_up, w_e_down, *, tme):
    nslot = xs.shape[0] // ROW_TILE
    dm = ROW_TILE * LANES
    nt = nslot // tme
    wg, wu, wd = w_e_gate, w_e_up, w_e_down
    plan =jnp.concatenate([_weight_plan(tile_a), _weight_plan(tile_b)], axis=1).reshape(-1)

    grid_spec = pltpu.PrefetchScalarGridSpec(
        num_scalar_prefetch=6,
        grid=(nt,),
        in_specs=[
            pl.BlockSpec((tme * ROW_TILE, LANES), lambda i, ta, tb, tk, tr, pn, n: (tk[i], 0)),
            pl.BlockSpec(memory_space=pl.ANY), pl.BlockSpec(memory_space=pl.ANY), pl.BlockSpec(memory_space=pl.ANY),
        ],
        out_specs=pl.BlockSpec((tme, 2, ROW_TILE, LANES), lambda i, ta, tb, tk, tr, pn, n: (i, 0, 0, 0)),
        scratch_shapes=[pltpu.VMEM((2, 2, dm, D_EXPERT), F32), pltpu.VMEM((2, 2, dm, D_EXPERT), F32),
                        pltpu.VMEM((2, 2, D_EXPERT, dm), F32), pltpu.SemaphoreType.DMA((2, 2)),
                        pltpu.VMEM((2, tme * ROW_TILE, LANES), F32)],
    )
    return pl.pallas_call(
        functools.partial(_expert_kernel, tme=tme),
        out_shape=jax.ShapeDtypeStruct((nslot, 2, ROW_TILE, LANES), F32),
        grid_spec=grid_spec,
        compiler_params=pltpu.CompilerParams(dimension_semantics=("arbitrary",), vmem_limit_bytes=VMEM_LIMIT),
        name="experts",
    )(tile_a, tile_b, tile_block, tile_rows, plan, ntiles, xs, wg, wu, wd)


def _combine_kernel(pos_ref, posn_ref, ys_ref, ysflat_ref, x1_ref, route_ref, gfin_ref, out_ref, ybuf, sem, *, tc):
    i = pl.program_id(0)
    nb = pl.num_programs(0)
    slot = i % 2

    def gather(p_ref, s):
        def issue(jb, carry):
            for u in range(ROWS_PER_ISSUE):
                j = jb * ROWS_PER_ISSUE + u
                dst = pl.multiple_of(j * ROW_TILE, ROW_TILE)
                pltpu.make_async_copy(ys_ref.at[p_ref[0, 0, j]], ybuf.at[s, :, pl.ds(dst, ROW_TILE), :],
                                      sem.at[s]).start(priority=u % 2)
            return carry

        lax.fori_loop(0, tc // ROWS_PER_ISSUE, issue, 0)

    @pl.when(i == 0)
    def _():
        gather(pos_ref, 0)

    @pl.when(i + 1 < nb)
    def _():
        gather(posn_ref, 1 - slot)

    for plane in range(2):
        pltpu.make_async_copy(ysflat_ref.at[pl.ds(0, tc * ROW_TILE)], ybuf.at[slot, plane], sem.at[slot]).wait()
    rec = route_ref[...]
    wa = rec[:, ROUTE_WA:ROUTE_WA + 1]
    wb = rec[:, ROUTE_WB:ROUTE_WB + 1]
    ya = jnp.concatenate([ybuf[slot, 0, pl.ds(c, tc, stride=ROW_TILE), :] for c in range(ROW_TILE)], axis=1)
    yb = jnp.concatenate([ybuf[slot, 1, pl.ds(c, tc, stride=ROW_TILE), :] for c in range(ROW_TILE)], axis=1)
    z = x1_ref[...] + wa * ya + wb * yb
    ms = jnp.mean(z * z, axis=-1, keepdims=True)
    out_ref[...] = z * lax.rsqrt(ms + RMS_EPS) * gfin_ref[...]


def _combine(ys, pos, x1, route, g_final, *, tc):
    n, dm = x1.shape
    nb = n // tc
    grid_spec = pltpu.PrefetchScalarGridSpec(
        num_scalar_prefetch=0,
        grid=(nb,),
        in_specs=[
            pl.BlockSpec((1, 1, tc), lambda i: (i, 0, 0), memory_space=pltpu.SMEM),
            pl.BlockSpec((1, 1, tc), lambda i: (jnp.minimum(i + 1, nb - 1), 0, 0), memory_space=pltpu.SMEM),
            pl.BlockSpec(memory_space=pl.ANY),
            pl.BlockSpec(memory_space=pl.ANY),
            pl.BlockSpec((tc, dm), lambda i: (i, 0)),
            pl.BlockSpec((tc, LANES), lambda i: (i, 0)),
            pl.BlockSpec((1, dm), lambda i: (0, 0)),
        ],
        out_specs=pl.BlockSpec((tc, dm), lambda i: (i, 0)),
        scratch_shapes=[pltpu.VMEM((2, 2, tc * ROW_TILE, LANES), F32), pltpu.SemaphoreType.DMA((2,))],
    )
    nslot = ys.shape[0]
    return pl.pallas_call(
        functools.partial(_combine_kernel, tc=tc),
        out_shape=jax.ShapeDtypeStruct((n, dm), F32),
        grid_spec=grid_spec,
        compiler_params=pltpu.CompilerParams(dimension_semantics=("arbitrary",), vmem_limit_bytes=VMEM_LIMIT),
        name="combine",
    )(pos.reshape(nb, 1, tc), pos.reshape(nb, 1, tc), ys, ys.reshape(nslot * PAIR_ROWS, LANES), x1, route,
      g_final.reshape(1, dm))


def _slot_layout(route, counts, n, tme):
    cnt = counts[0, :N_CLASSES].astype(jnp.int32)
    tiles_per = (cnt + tme - 1) // tme
    tile_end = jnp.cumsum(tiles_per)
    seg_base = (tile_end - tiles_per) * tme
    ntiles = tile_end[-1]
    nt_max = n // tme + N_CLASSES
    tid = jnp.arange(nt_max, dtype=jnp.int32)
    live = jnp.minimum(tid, ntiles - 1)
    tile_class = jnp.sum((tile_end[None, :] <= live[:, None]).astype(jnp.int32), axis=1)
    tile_a = jnp.take(jnp.asarray(CLASS_EXPERT_A), tile_class)
    tile_b = jnp.take(jnp.asarray(CLASS_EXPERT_B), tile_class)
    first_tile = jnp.take(tile_end - tiles_per, tile_class)
    tile_rows = jnp.clip(jnp.take(cnt, tile_class) - (live - first_tile) * tme, 0, tme)
    cls = route[:, ROUTE_CLASS].astype(jnp.int32)
    rank = route[:, ROUTE_RANK].astype(jnp.int32)
    base = jnp.sum(jnp.where(cls[:, None] == jnp.arange(N_CLASSES, dtype=jnp.int32)[None, :], seg_base[None, :], 0),
                   axis=1)
    pos = base + rank
    zoff = jnp.maximum(tile_end - 1, 0) * tme
    return (pos, zoff.astype(jnp.int32), tile_a.astype(jnp.int32), tile_b.astype(jnp.int32),
            live.astype(jnp.int32), tile_rows.astype(jnp.int32), ntiles.reshape(1).astype(jnp.int32))


def kernel(x, g_mix, w_in, sinks, w_br_dil, w_br_swa, w_out, g_ffn, w_group, b_group, w_router, b_router,
           w_e_gate, w_e_up, w_e_down, g_final):
    b, t, dm = x.shape
    n = b * t
    tme = 256
    qkv0, swa, gates, qkv4, qkv16 = _inproj(x, g_mix[0], w_in[0], tm=512)
    o0, l0 = _dil_attention(qkv0.reshape(b, 1, t, QKV_W), 0, tq=2048)
    o1, l1 = _dil_attention(qkv4, 1, tq=2048)
    o2, l2 = _dil_attention(qkv16, 2, tq=2048)
    ys = _swa_attention(swa, sinks[0], tq=2048)
    x1, h2, route, counts = _merge_route(
        x, o0.reshape(b, t, GROUP_W), l0.reshape(b, t, LANES), o1, l1, o2, l2, ys, gates,
        w_br_dil[0], w_br_swa[0], w_out[0], g_ffn[0], w_group[0], b_group[0], w_router[0], b_router[0], tm=512)
    route = route.reshape(n, LANES)
    pos, zoff, tile_a, tile_b, tile_block, tile_rows, ntiles = _slot_layout(route, counts, n, tme)
    nslot = n + N_CLASSES * tme
    xs = _dispatch(h2.reshape(n * ROW_TILE, LANES), pos, zoff, ntiles, nslot, td=4096, tme=tme)
    yslots = _experts(xs, tile_a, tile_b, tile_block, tile_rows, ntiles, w_e_gate[0], w_e_up[0], w_e_down[0],
                      tme=tme)
    out = _combine(yslots, pos, x1.reshape(n, dm), route, g_final, tc=512)
    return out.reshape(b, t, dm)
```

```python
import functools

import numpy as np
import jax
import jax.numpy as jnp
from jax import lax
from jax.experimental import pallas as pl
from jax.experimental.pallas import tpu as pltpu

F32 = jnp.float32
BF16 = jnp.bfloat16

HEAD_DIM = 64
BAND = 128
DIL_PATTERNS = ((128, 1), (512, 4), (2048, 16))
DIL_HEADS_PER_GROUP = 8
DIL_HEADS = 24
GROUP_W = DIL_HEADS_PER_GROUP * HEAD_DIM
QKV_W = 3 * GROUP_W
SWA_WINDOW = 128
SWA_Q_HEADS = 16
SWA_KV_HEADS = 2
SWA_Q_W = SWA_Q_HEADS * HEAD_DIM
SWA_W = SWA_Q_W + 4 * 128
MOE_GROUPS = 4
EXPERTS_PER_GROUP = 8
N_EXPERTS = 32
D_EXPERT = 512
RMS_EPS = 1e-6
LANES = 128
NEG_INF = float("-inf")
LOG2E = 1.4426950408889634
LN2 = 0.6931471805599453
ROW_TILE = 8
ROWS_PER_ISSUE = 8

VMEM_LIMIT = 56 * 1024 * 1024


def _alibi_slopes(n):
    return (2.0 ** (-8.0 * np.arange(1, n + 1) / n)).astype(np.float32)


def _band_bias(slopes, max_back, unit):
    a = np.arange(BAND)[:, None]
    c = np.arange(2 * BAND)[None, :]
    delta = a + BAND - c
    band = (delta >= 0) & (delta <= max_back)
    pen = (-slopes[:, None, None] * (delta * unit)[None] * LOG2E).astype(np.float32)
    full = np.where(band[None], pen, -np.inf).astype(np.float32)
    first = np.where((band & (c >= BAND))[None], pen, -np.inf).astype(np.float32)
    return np.stack([full, first], axis=0)


def _inproj_kernel(x_ref, g_ref, w_ref, qkv0_ref, swa_ref, gates_ref, qkv4_ref, qkv16_ref, hs_ref, *, tm):
    x = x_ref[...]
    ms = jnp.mean(x * x, axis=-1, keepdims=True)
    h = x * lax.rsqrt(ms + RMS_EPS) * g_ref[...]
    nchunk = h.shape[1] // LANES
    for c in range(nchunk):
        hs_ref[c] = h[:, c * LANES:(c + 1) * LANES]
    hb = h.astype(BF16)

    def proj(lhs, c0, width):
        return jnp.dot(lhs, w_ref[:, c0:c0 + width], preferred_element_type=F32)

    for j in range(3):
        qkv0_ref[:, j * GROUP_W:(j + 1) * GROUP_W] = proj(hb, COL_QKV_D[j], GROUP_W).astype(BF16)
    for c0 in range(0, SWA_Q_W, 512):
        swa_ref[:, c0:c0 + 512] = proj(hb, COL_Q_S + c0, 512).astype(BF16)
    kv = proj(hb, COL_KV_S, 4 * HEAD_DIM)
    part = [kv[:, j * HEAD_DIM:(j + 1) * HEAD_DIM] for j in range(4)]
    swa_ref[:, SWA_Q_W:SWA_W] = jnp.concatenate(
        [part[0], part[0], part[1], part[1], part[2], part[2], part[3], part[3]], axis=1).astype(BF16)
    for c0 in range(0, 2048, 512):
        gates_ref[:, c0:c0 + 512] = proj(hb, COL_GATES + c0, 512).astype(BF16)
    for gi, out_ref, d in ((1, qkv4_ref, 4), (2, qkv16_ref, 16)):
        rows = tm // d
        hp = jnp.concatenate(
            [jnp.concatenate([hs_ref[c, pl.ds(r, rows, stride=d), :] for c in range(nchunk)], axis=1)
             for r in range(d)], axis=0).astype(BF16)
        for j in range(3):
            res = proj(hp, COL_QKV_D[j] + gi * GROUP_W, GROUP_W).astype(BF16)
            for r in range(d):
                out_ref[r, :, j * GROUP_W:(j + 1) * GROUP_W] = res[r * rows:(r + 1) * rows]


COL_QKV_D = (0, 1536, 3072)
COL_Q_S = 4608
COL_KV_S = 5632
COL_GATES = 5888
IN_WIDTH = 7936
Q_SCALE = HEAD_DIM ** -0.5 * LOG2E


def _prep_w_in(w_in):
    col = np.arange(IN_WIDTH)
    is_q = (col < COL_QKV_D[1]) | ((col >= COL_Q_S) & (col < COL_KV_S))
    colscale = np.where(is_q, Q_SCALE, 1.0).astype(np.float32)
    return (w_in * colscale[None, :]).astype(BF16)


def _inproj(x, g_mix, w_in, *, tm):
    b, t, dm = x.shape
    wp = _prep_w_in(w_in)
    nw = wp.shape[1]
    nt = t // tm
    out_shape = (
        jax.ShapeDtypeStruct((b, t, QKV_W), BF16),
        jax.ShapeDtypeStruct((b, t, SWA_W), BF16),
        jax.ShapeDtypeStruct((b, t, 2048), BF16),
        jax.ShapeDtypeStruct((b, 4, t // 4, QKV_W), BF16),
        jax.ShapeDtypeStruct((b, 16, t // 16, QKV_W), BF16),
    )
    return pl.pallas_call(
        functools.partial(_inproj_kernel, tm=tm),
        out_shape=out_shape,
        grid=(b, nt),
        in_specs=[
            pl.BlockSpec((None, tm, dm), lambda bi, i: (bi, i, 0)),
            pl.BlockSpec((1, dm), lambda bi, i: (0, 0)),
            pl.BlockSpec((dm, nw), lambda bi, i: (0, 0), pipeline_mode=pl.Buffered(1)),
        ],
        out_specs=(
            pl.BlockSpec((None, tm, QKV_W), lambda bi, i: (bi, i, 0)),
            pl.BlockSpec((None, tm, SWA_W), lambda bi, i: (bi, i, 0)),
            pl.BlockSpec((None, tm, 2048), lambda bi, i: (bi, i, 0)),
            pl.BlockSpec((None, 4, tm // 4, QKV_W), lambda bi, i: (bi, 0, i, 0)),
            pl.BlockSpec((None, 16, tm // 16, QKV_W), lambda bi, i: (bi, 0, i, 0)),
        ),
        scratch_shapes=[pltpu.VMEM((dm // LANES, tm, LANES), F32)],
        compiler_params=pltpu.CompilerParams(
            dimension_semantics=("arbitrary", "arbitrary"), vmem_limit_bytes=VMEM_LIMIT),
        name="inproj",
    )(x, g_mix.reshape(1, dm), wp)


def _dil_attn_kernel(q_all, kc_all, vc_all, kp_all, vp_all, bias_ref, o_all_ref, lse_all_ref, kbuf_all, vbuf_all,
                     *, tq, sb):
    for st in range(sb):
        _dil_attn_stream(q_all.at[st], kc_all.at[st], vc_all.at[st], kp_all.at[st], vp_all.at[st], bias_ref,
                         o_all_ref.at[st], lse_all_ref.at[st], kbuf_all.at[st], vbuf_all.at[st], tq=tq)


def _dil_attn_stream(q_ref, kc_ref, vc_ref, kp_ref, vp_ref, bias_ref, o_ref, lse_ref, kbuf, vbuf, *, tq):
    i = pl.program_id(2)
    kbuf[0:BAND] = kp_ref[...]
    kbuf[BAND:BAND + tq] = kc_ref[...]
    vbuf[0:BAND] = vp_ref[...]
    vbuf[BAND:BAND + tq] = vc_ref[...]
    lane = lax.broadcasted_iota(jnp.int32, (BAND, LANES), 1)
    lo = lane < HEAD_DIM
    zero = jnp.zeros((BAND, LANES), BF16)
    for qb in range(tq // BAND):
        sel = jnp.where(i == 0, 1, 0) if qb == 0 else 0
        rq = slice(qb * BAND, (qb + 1) * BAND)
        rk = slice(qb * BAND, qb * BAND + 2 * BAND)
        lse_parts = []
        for pr in range(GROUP_W // LANES):
            cl = slice(pr * LANES, (pr + 1) * LANES)
            q2 = q_ref[rq, cl]
            kk = kbuf[rk, cl]
            vv = vbuf[rk, cl]
            s_all = lax.dot_general(jnp.concatenate([jnp.where(lo, q2, zero), jnp.where(lo, zero, q2)], axis=0), kk,
                                    (((1,), (1,)), ((), ())), preferred_element_type=F32)
            probs, stats = [], []
            for hh in range(2):
                s = s_all[hh * BAND:(hh + 1) * BAND] + bias_ref[sel, 2 * pr + hh]
                m = jnp.max(s, axis=-1, keepdims=True)
                p = jnp.exp2(s - m)
                l = jnp.sum(p, axis=-1, keepdims=True)
                probs.append(p.astype(BF16))
                stats.append((1.0 / l, (m + jnp.log2(l)) * LN2))
            o_all = jnp.dot(jnp.concatenate(probs, axis=0), vv, preferred_element_type=F32)
            res = [(o_all[hh * BAND:(hh + 1) * BAND] * stats[hh][0], stats[hh][1]) for hh in range(2)]
            o_ref[rq, cl] = jnp.where(lo, res[0][0], res[1][0]).astype(o_ref.dtype)
            lse_parts.append(jnp.where(lane == 2 * pr, res[0][1], jnp.where(lane == 2 * pr + 1, res[1][1], 0.0)))
        lse_ref[rq, :] = (lse_parts[0] + lse_parts[1]) + (lse_parts[2] + lse_parts[3])


def _dil_attention(qkv, gi, *, tq):
    b, d, l, _ = qkv.shape
    window, dil = DIL_PATTERNS[gi]
    assert dil == d
    rows_per_step = tq
    tq = min(tq, l)
    sb = min(d, max(1, rows_per_step // tq))
    nq = tq // BAND
    slopes = _alibi_slopes(DIL_HEADS)[gi * 8:(gi + 1) * 8]
    bias = jnp.asarray(_band_bias(slopes, window // dil, dil))

    def cur(c):
        return pl.BlockSpec((None, sb, tq, GROUP_W), lambda bi, r, i: (bi, r, i, c))

    def prev(c):
        return pl.BlockSpec((None, sb, BAND, GROUP_W), lambda bi, r, i: (bi, r, jnp.maximum(i * nq - 1, 0), c))

    return pl.pallas_call(
        functools.partial(_dil_attn_kernel, tq=tq, sb=sb),
        out_shape=(jax.ShapeDtypeStruct((b, d, l, GROUP_W), BF16),
                   jax.ShapeDtypeStruct((b, d, l, LANES), F32)),
        grid=(b, d // sb, l // tq),
        in_specs=[cur(0), cur(1), cur(2), prev(1), prev(2),
                  pl.BlockSpec((2, 8, BAND, 2 * BAND), lambda bi, r, i: (0, 0, 0, 0))],
        out_specs=(cur(0), pl.BlockSpec((None, sb, tq, LANES), lambda bi, r, i: (bi, r, i, 0))),
        scratch_shapes=[pltpu.VMEM((sb, BAND + tq, GROUP_W), BF16), pltpu.VMEM((sb, BAND + tq, GROUP_W), BF16)],
        compiler_params=pltpu.CompilerParams(
            dimension_semantics=("arbitrary", "arbitrary", "arbitrary"), vmem_limit_bytes=VMEM_LIMIT),
        name=f"dil_attn_{gi}",
    )(qkv, qkv, qkv, qkv, qkv, bias)


def _swa_attn_kernel(sink_ref, qlo_ref, qhi_ref, kvc_ref, kvp_ref, bias_ref, o_ref, kvbuf, *, tq):
    i = pl.program_id(1)
    kvbuf[0:BAND] = kvp_ref[...]
    kvbuf[BAND:BAND + tq] = kvc_ref[...]
    lo = lax.broadcasted_iota(jnp.int32, (BAND, LANES), 1) < HEAD_DIM
    zero = jnp.zeros((BAND, LANES), BF16)
    for qb in range(tq // BAND):
        sel = jnp.where(i == 0, 1, 0) if qb == 0 else 0
        rq = slice(qb * BAND, (qb + 1) * BAND)
        rk = slice(qb * BAND, qb * BAND + 2 * BAND)
        for g in range(SWA_KV_HEADS):
            q_ref = qlo_ref if g == 0 else qhi_ref
            kk = kvbuf[rk, g * LANES:(g + 1) * LANES]
            vv = kvbuf[rk, (2 + g) * LANES:(3 + g) * LANES]
            masked = []
            for pp in range(4):
                q2 = q_ref[rq, pp * LANES:(pp + 1) * LANES]
                masked += [jnp.where(lo, q2, zero), jnp.where(lo, zero, q2)]
            s_all = lax.dot_general(jnp.concatenate(masked, axis=0), kk, (((1,), (1,)), ((), ())),
                                    preferred_element_type=F32)
            probs, inv = [], []
            for j in range(8):
                h = 8 * g + j
                sink = sink_ref[h] * LOG2E
                s = s_all[j * BAND:(j + 1) * BAND] + bias_ref[sel, h]
                m = jnp.maximum(jnp.max(s, axis=-1, keepdims=True), sink)
                p = jnp.exp2(s - m)
                inv.append(1.0 / (jnp.sum(p, axis=-1, keepdims=True) + jnp.exp2(sink - m)))
                probs.append(p.astype(BF16))
            o_all = jnp.dot(jnp.concatenate(probs, axis=0), vv, preferred_element_type=F32)
            for pp in range(4):
                oa = o_all[(2 * pp) * BAND:(2 * pp + 1) * BAND] * inv[2 * pp]
                ob = o_all[(2 * pp + 1) * BAND:(2 * pp + 2) * BAND] * inv[2 * pp + 1]
                o_ref[rq, (4 * g + pp) * LANES:(4 * g + pp + 1) * LANES] = jnp.where(lo, oa, ob).astype(o_ref.dtype)


def _swa_attention(swa, sinks, *, tq):
    b, t, _ = swa.shape
    tq = min(tq, t)
    nq = tq // BAND
    bias = jnp.asarray(_band_bias(_alibi_slopes(SWA_Q_HEADS), SWA_WINDOW - 1, 1))
    grid_spec = pltpu.PrefetchScalarGridSpec(
        num_scalar_prefetch=1,
        grid=(b, t // tq),
        in_specs=[
            pl.BlockSpec((None, tq, 512), lambda bi, i, s: (bi, i, 0)),
            pl.BlockSpec((None, tq, 512), lambda bi, i, s: (bi, i, 1)),
            pl.BlockSpec((None, tq, 512), lambda bi, i, s: (bi, i, 2)),
            pl.BlockSpec((None, BAND, 512), lambda bi, i, s: (bi, jnp.maximum(i * nq - 1, 0), 2)),
            pl.BlockSpec((2, SWA_Q_HEADS, BAND, 2 * BAND), lambda bi, i, s: (0, 0, 0, 0)),
        ],
        out_specs=pl.BlockSpec((None, tq, SWA_Q_W), lambda bi, i, s: (bi, i, 0)),
        scratch_shapes=[pltpu.VMEM((BAND + tq, 512), BF16)],
    )
    return pl.pallas_call(
        functools.partial(_swa_attn_kernel, tq=tq),
        out_shape=jax.ShapeDtypeStruct((b, t, SWA_Q_W), BF16),
        grid_spec=grid_spec,
        compiler_params=pltpu.CompilerParams(
            dimension_semantics=("arbitrary", "arbitrary"), vmem_limit_bytes=VMEM_LIMIT),
        name="swa_attn",
    )(sinks.astype(F32), swa, swa, swa, swa, bias)


ROUTE_CLASS, ROUTE_RANK, ROUTE_WA, ROUTE_WB = range(4)
ROUTER_LANE0 = MOE_GROUPS
PAIRS_PER_GROUP = EXPERTS_PER_GROUP * (EXPERTS_PER_GROUP - 1) // 2
N_CLASSES = MOE_GROUPS * PAIRS_PER_GROUP
_PAIRS = [(a, b) for a in range(EXPERTS_PER_GROUP) for b in range(a + 1, EXPERTS_PER_GROUP)]
CLASS_EXPERT_A = np.array([g * EXPERTS_PER_GROUP + a for g in range(MOE_GROUPS) for a, _ in _PAIRS], np.int32)
CLASS_EXPERT_B = np.array([g * EXPERTS_PER_GROUP + b for g in range(MOE_GROUPS) for _, b in _PAIRS], np.int32)


def _route(logits, carry_ref, ltri_ref, tm, active):
    lane = lax.broadcasted_iota(jnp.int32, (tm, LANES), 1)
    lanef = lane.astype(F32)

    def first_argmax(v):
        m = jnp.max(v, axis=-1, keepdims=True)
        return m, jnp.min(jnp.where(v == m, lanef, float(LANES)), axis=-1, keepdims=True)

    gl = jnp.where(lane < MOE_GROUPS, logits, NEG_INF)
    gmax, gidx = first_argmax(gl)
    g_w = 1.0 / jnp.sum(jnp.exp(gl - gmax), axis=-1, keepdims=True)
    e_lane = lane - ROUTER_LANE0
    lane_group = (e_lane >> 3).astype(F32)
    in_group = (e_lane >= 0) & (e_lane < N_EXPERTS) & (lane_group == gidx)
    el = jnp.where(in_group, logits, NEG_INF)
    m1, i1 = first_argmax(el)
    m2, i2 = first_argmax(jnp.where(lanef == i1, NEG_INF, el))
    tt = jnp.exp(m2 - m1)
    w1 = g_w / (1.0 + tt)
    w2 = g_w * tt / (1.0 + tt)
    first = float(ROUTER_LANE0) + float(EXPERTS_PER_GROUP) * gidx
    e1 = i1 - first
    e2 = i2 - first
    swap = e2 < e1
    ea = jnp.minimum(e1, e2)
    eb = jnp.maximum(e1, e2)
    wa = jnp.where(swap, w2, w1)
    wb = jnp.where(swap, w1, w2)
    pair = ea * (float(2 * EXPERTS_PER_GROUP - 1) - ea) * 0.5 + (eb - ea - 1.0)
    cls = float(PAIRS_PER_GROUP) * gidx + pair
    oh = jnp.where(lanef == cls, 1.0, 0.0)
    before = jnp.dot(ltri_ref[...], oh.astype(BF16), preferred_element_type=F32) + carry_ref[0:1, :]
    rank = jnp.sum(oh * before, axis=-1, keepdims=True)
    carry_ref[...] = carry_ref[...] + jnp.where(active, jnp.sum(oh, axis=0, keepdims=True), 0.0)
    rec = jnp.zeros((tm, LANES), F32)
    for ln, val in ((ROUTE_CLASS, cls), (ROUTE_RANK, rank), (ROUTE_WA, wa), (ROUTE_WB, wb)):
        rec = jnp.where(lane == ln, val, rec)
    return rec


def _merge_kernel(x_ref, o0_ref, l0_ref, o1_ref, l1_ref, o2_ref, l2_ref, ys_ref, gates_ref,
                  wbd_ref, wbs_ref, wo_ref, gffn_ref, wrt_ref, brt_ref, exp_ref, ltri_ref,
                  x1_ref, h2_ref, route_ref, cnt_ref,
                  uo1, ul1, uo2, ul2, carry_ref, logit_buf, *, tm):
    step = pl.program_id(0)

    @pl.when(step == 0)
    def _():
        carry_ref[...] = jnp.zeros_like(carry_ref)
        logit_buf[...] = jnp.zeros_like(logit_buf)

    route_ref[...] = _route(logit_buf[...], carry_ref, ltri_ref, tm, step > 0)
    cnt_ref[...] = carry_ref[...]

    nch = GROUP_W // LANES
    for src_o, src_l, dst_o, dst_l, d in ((o1_ref, l1_ref, uo1, ul1, 4), (o2_ref, l2_ref, uo2, ul2, 16)):
        rows = tm // d
        for r in range(d):
            dst_l[pl.ds(r, rows, stride=d), :] = src_l[r]
            for c in range(nch):
                dst_o[c, pl.ds(r, rows, stride=d), :] = src_o[r, :, c * LANES:(c + 1) * LANES].astype(F32)
    def per_head_to_lanes(w):
        hi = w.astype(BF16)
        lo = (w - hi.astype(F32)).astype(BF16)
        return jnp.dot(jnp.concatenate([hi, lo], axis=1), exp_ref[...], preferred_element_type=F32)

    def sig(v):
        return 0.5 * jnp.tanh(0.5 * v) + 0.5

    def rows_to_logits(r0, nr):
        rs = slice(r0, r0 + nr)
        la, lb, lc = l0_ref[rs, :], ul1[rs, :], ul2[rs, :]
        mx = jnp.maximum(jnp.maximum(la, lb), lc)
        ea, eb, ec = jnp.exp(la - mx), jnp.exp(lb - mx), jnp.exp(lc - mx)
        inv = 1.0 / (ea + eb + ec)
        wa, wb, wc = per_head_to_lanes(ea * inv), per_head_to_lanes(eb * inv), per_head_to_lanes(ec * inv)
        ych = []
        for c in range(nch):
            cl = slice(c * LANES, (c + 1) * LANES)
            ych.append((wa[:, cl] * o0_ref[rs, cl].astype(F32) + wb[:, cl] * uo1[c, rs, :]
                        + wc[:, cl] * uo2[c, rs, :]).astype(BF16))
        y = jnp.concatenate(ych, axis=1)
        a = jnp.dot(y, wbd_ref[...], preferred_element_type=F32)
        bsw = jnp.dot(ys_ref[rs, :], wbs_ref[...], preferred_element_type=F32)
        mixed = sig(gates_ref[rs, 0:1024]) * a.astype(BF16) + sig(gates_ref[rs, 1024:2048]) * bsw.astype(BF16)
        x1 = x_ref[rs, :] + jnp.dot(mixed, wo_ref[...], preferred_element_type=F32)
        x1_ref[rs, :] = x1
        ms = jnp.mean(x1 * x1, axis=-1, keepdims=True)
        h2 = x1 * lax.rsqrt(ms + RMS_EPS) * gffn_ref[...]
        for c in range(h2.shape[1] // LANES):
            h2_ref[pl.ds(r0 * ROW_TILE + c, nr, stride=ROW_TILE), :] = h2[:, c * LANES:(c + 1) * LANES]
        h_hi = h2.astype(BF16)
        h_lo = (h2 - h_hi.astype(F32)).astype(BF16)
        hw = jnp.dot(h_hi, wrt_ref[...], preferred_element_type=F32)
        return (hw[:, :LANES] + hw[:, LANES:]
                + jnp.dot(h_lo, wrt_ref[:, :LANES], preferred_element_type=F32) + brt_ref[...])

    logit_buf[...] = rows_to_logits(0, tm)


def _merge_route(x, o0, l0, o1, l1, o2, l2, ys, gates, w_br_dil, w_br_swa, w_out, g_ffn,
                 w_group, b_group, w_router, b_router, *, tm):
    b, t, dm = x.shape
    nt = t // tm
    wrt = jnp.zeros((dm, LANES), F32).at[:, :MOE_GROUPS].set(w_group).at[
        :, ROUTER_LANE0:ROUTER_LANE0 + N_EXPERTS].set(w_router)
    brt = jnp.zeros((1, LANES), F32).at[0, :MOE_GROUPS].set(b_group).at[
        0, ROUTER_LANE0:ROUTER_LANE0 + N_EXPERTS].set(b_router)
    wrt_hi = wrt.astype(BF16)
    wrt_lo = (wrt - wrt_hi.astype(F32)).astype(BF16)
    wrt = jnp.concatenate([wrt_hi, wrt_lo], axis=1)

    nsteps = b * nt + 1

    def tile_of(s, lag=0):
        tile = jnp.clip(s - lag, 0, nsteps - 2)
        return tile // nt, tile % nt

    def tok(width, lag=0):
        return pl.BlockSpec((None, tm, width), lambda s: (*tile_of(s, lag), 0))

    def stream(d, width):
        return pl.BlockSpec((None, d, tm // d, width), lambda s: (tile_of(s)[0], 0, tile_of(s)[1], 0))

    def const(shape):
        return pl.BlockSpec(shape, lambda s: (0,) * len(shape), pipeline_mode=pl.Buffered(1))

    head_of_lane = np.arange(GROUP_W) // HEAD_DIM
    spread = (np.arange(LANES)[:, None] == head_of_lane[None, :]).astype(np.float32)
    expand = jnp.asarray(np.concatenate([spread, spread], axis=0), dtype=BF16)

    nch = GROUP_W // LANES
    return pl.pallas_call(
        functools.partial(_merge_kernel, tm=tm),
        out_shape=(jax.ShapeDtypeStruct((b, t, dm), F32), jax.ShapeDtypeStruct((b, t * ROW_TILE, LANES), F32),
                   jax.ShapeDtypeStruct((b, t, LANES), F32), jax.ShapeDtypeStruct((8, LANES), F32)),
        grid=(nsteps,),
        in_specs=[tok(dm), tok(GROUP_W), tok(LANES), stream(4, GROUP_W), stream(4, LANES),
                  stream(16, GROUP_W), stream(16, LANES), tok(SWA_Q_W), tok(2048),
                  const((GROUP_W, dm)), const((SWA_Q_W, dm)), const((dm, dm)), const((1, dm)),
                  const((dm, 2 * LANES)), const((1, LANES)), const((2 * LANES, GROUP_W)), const((tm, tm))],
        out_specs=(tok(dm), pl.BlockSpec((None, tm * ROW_TILE, LANES), lambda s: (*tile_of(s), 0)),
                   tok(LANES, lag=1), pl.BlockSpec((8, LANES), lambda s: (0, 0))),
        scratch_shapes=[pltpu.VMEM((nch, tm, LANES), F32), pltpu.VMEM((tm, LANES), F32),
                        pltpu.VMEM((nch, tm, LANES), F32), pltpu.VMEM((tm, LANES), F32),
                        pltpu.VMEM((8, LANES), F32), pltpu.VMEM((tm, LANES), F32)],
        compiler_params=pltpu.CompilerParams(
            dimension_semantics=("arbitrary",), vmem_limit_bytes=VMEM_LIMIT),
        name="merge_route",
    )(x, o0, l0, o1, l1, o2, l2, ys, gates, w_br_dil.astype(BF16), w_br_swa.astype(BF16), w_out.astype(BF16),
      g_ffn.reshape(1, dm), wrt, brt, expand, jnp.asarray(np.tril(np.ones((tm, tm), np.float32), -1), dtype=BF16))


def _dispatch_kernel(pos_ref, zoff_ref, nt_ref, h_ref, xs_ref, zbuf, sem, zsem, *, td, tme, nt_max):
    step = pl.program_id(0)

    def _zero_copy(e):
        off = pl.multiple_of(zoff_ref[e] * ROW_TILE, tme * ROW_TILE)
        return pltpu.make_async_copy(zbuf, xs_ref.at[pl.ds(off, tme * ROW_TILE)], zsem)

    def _row_copy(src_row, dst_row):
        return pltpu.make_async_copy(h_ref.at[pl.ds(pl.multiple_of(src_row * ROW_TILE, ROW_TILE), ROW_TILE)],
                                     xs_ref.at[pl.ds(pl.multiple_of(dst_row * ROW_TILE, ROW_TILE), ROW_TILE)], sem)

    def _wait_rows():
        pltpu.make_async_copy(h_ref, xs_ref.at[pl.ds(0, td * ROW_TILE)], sem).wait()

    def _tail_copy(tile):
        off = pl.multiple_of(tile * (tme * ROW_TILE), tme * ROW_TILE)
        return pltpu.make_async_copy(zbuf, xs_ref.at[pl.ds(off, tme * ROW_TILE)], zsem)

    @pl.when(step == 0)
    def _():
        zbuf[...] = jnp.zeros_like(zbuf)
        for e in range(N_CLASSES):
            _zero_copy(e).start()
        for e in range(N_CLASSES):
            _zero_copy(e).wait()

        def tail(tile, carry):
            _tail_copy(tile).start()
            _tail_copy(tile).wait()
            return carry

        lax.fori_loop(nt_ref[0], nt_max, tail, 0)

    def issue(jb, carry):
        for u in range(ROWS_PER_ISSUE):
            j = jb * ROWS_PER_ISSUE + u
            _row_copy(j, pos_ref[0, 0, j]).start(priority=u % 2)
        return carry

    lax.fori_loop(0, td // ROWS_PER_ISSUE, issue, 0)
    _wait_rows()


def _dispatch(h2, pos, zoff, ntiles, nslot, *, td, tme):
    n = h2.shape[0] // ROW_TILE
    nb = n // td
    grid_spec = pltpu.PrefetchScalarGridSpec(
        num_scalar_prefetch=0,
        grid=(nb,),
        in_specs=[
            pl.BlockSpec((1, 1, td), lambda i: (i, 0, 0), memory_space=pltpu.SMEM),
            pl.BlockSpec(memory_space=pltpu.SMEM),
            pl.BlockSpec(memory_space=pltpu.SMEM),
            pl.BlockSpec((td * ROW_TILE, LANES), lambda i: (i, 0)),
        ],
        out_specs=pl.BlockSpec(memory_space=pl.ANY),
        scratch_shapes=[pltpu.VMEM((tme * ROW_TILE, LANES), F32), pltpu.SemaphoreType.DMA(()),
                        pltpu.SemaphoreType.DMA(())],
    )
    return pl.pallas_call(
        functools.partial(_dispatch_kernel, td=td, tme=tme, nt_max=nslot // tme),
        out_shape=jax.ShapeDtypeStruct((nslot * ROW_TILE, LANES), F32),
        grid_spec=grid_spec,
        compiler_params=pltpu.CompilerParams(dimension_semantics=("arbitrary",), has_side_effects=True,
                                             vmem_limit_bytes=VMEM_LIMIT),
        name="dispatch",
    )(pos.reshape(nb, 1, td), zoff, ntiles, h2)


PAIR_ROWS = 2 * ROW_TILE


PLAN_FIRST, PLAN_SLOT, PLAN_NEXT, PLAN_HAS_NEXT = range(4)


def _expert_kernel(ta_ref, tb_ref, tblk_ref, trows_ref, plan_ref, nt_ref, x_ref, wg_hbm, wu_hbm, wd_hbm, y_ref,
                   wg_buf, wu_buf, wd_buf, wsem, yscr, *, tme):
    i = pl.program_id(0)
    live = i < nt_ref[0]
    half_rows = tme // 2

    def weight_copies(which, expert, slot):
        return [pltpu.make_async_copy(src.at[expert], buf.at[which, slot], wsem.at[which, slot])
                for src, buf in ((wg_hbm, wg_buf), (wu_hbm, wu_buf), (wd_hbm, wd_buf))]

    slots = []
    for which, t_ref in ((0, ta_ref), (1, tb_ref)):
        base = (i * 2 + which) * 4
        first = plan_ref[base + PLAN_FIRST] == 1
        slot = plan_ref[base + PLAN_SLOT]
        slots.append(slot)

        @pl.when(first & (i == 0))
        def _():
            for cp in weight_copies(which, t_ref[i], slot):
                cp.start()

        @pl.when(first)
        def _():
            for cp in weight_copies(which, t_ref[i], slot):
                cp.wait()

        @pl.when(first & (plan_ref[base + PLAN_HAS_NEXT] == 1))
        def _():
            for cp in weight_copies(which, plan_ref[base + PLAN_NEXT], 1 - slot):
                cp.start()

    def run(nrows):
        xb = jnp.concatenate([x_ref[pl.ds(c, nrows, stride=ROW_TILE), :] for c in range(ROW_TILE)], axis=1)
        nw = 2 * LANES
        for half in range(2):
            wg_ref, wu_ref, wd_ref = (buf.at[half, slots[half]] for buf in (wg_buf, wu_buf, wd_buf))
            act = []
            for c0 in range(0, D_EXPERT, nw):
                g = jnp.dot(xb, wg_ref[:, c0:c0 + nw], preferred_element_type=F32)
                u = jnp.dot(xb, wu_ref[:, c0:c0 + nw], preferred_element_type=F32)
                act.append(g * jax.nn.sigmoid(g) * u)
            a = jnp.concatenate(act, axis=1)
            for c0 in range(0, ROW_TILE * LANES, nw):
                y = jnp.dot(a, wd_ref[:, c0:c0 + nw], preferred_element_type=F32)
                for k in range(nw // LANES):
                    c = c0 // LANES + k
                    yscr[half, pl.ds(c, nrows, stride=ROW_TILE), :] = y[:, k * LANES:(k + 1) * LANES]
            y_ref[0:nrows, half] = yscr[half, 0:nrows * ROW_TILE, :].reshape(nrows, ROW_TILE, LANES)

    @pl.when(live & (trows_ref[i] > half_rows))
    def _():
        run(tme)

    @pl.when(live & (trows_ref[i] <= half_rows))
    def _():
        run(half_rows)
        y_ref[half_rows:] = jnp.zeros((half_rows, 2, ROW_TILE, LANES), F32)

    @pl.when(jnp.logical_not(live))
    def _():
        y_ref[...] = jnp.zeros_like(y_ref)


def _weight_plan(tile_expert):
    nt = tile_expert.shape[0]
    first = jnp.concatenate([jnp.ones((1,), jnp.int32), (tile_expert[1:] != tile_expert[:-1]).astype(jnp.int32)])
    run_id = jnp.cumsum(first) - 1
    next_start = jnp.sum((run_id[None, :] <= run_id[:, None]).astype(jnp.int32), axis=1)
    has_next = (run_id < run_id[-1]).astype(jnp.int32)
    nxt = jnp.take(tile_expert, jnp.minimum(next_start, nt - 1))
    return jnp.stack([first, run_id % 2, nxt, has_next], axis=1).astype(jnp.int32)


def _experts(xs, tile_a, tile_b, tile_block, tile_rows, ntiles, w_e_gate, w_e_up, w_e_down, *, tme):
    nslot = xs.shape[0] // ROW_TILE
    dm = ROW_TILE * LANES
    nt = nslot // tme
    wg, wu, wd = w_e_gate, w_e_up, w_e_down
    plan =jnp.concatenate([_weight_plan(tile_a), _weight_plan(tile_b)], axis=1).reshape(-1)

    grid_spec = pltpu.PrefetchScalarGridSpec(
        num_scalar_prefetch=6,
        grid=(nt,),
        in_specs=[
            pl.BlockSpec((tme * ROW_TILE, LANES), lambda i, ta, tb, tk, tr, pn, n: (tk[i], 0)),
            pl.BlockSpec(memory_space=pl.ANY), pl.BlockSpec(memory_space=pl.ANY), pl.BlockSpec(memory_space=pl.ANY),
        ],
        out_specs=pl.BlockSpec((tme, 2, ROW_TILE, LANES), lambda i, ta, tb, tk, tr, pn, n: (i, 0, 0, 0)),
        scratch_shapes=[pltpu.VMEM((2, 2, dm, D_EXPERT), F32), pltpu.VMEM((2, 2, dm, D_EXPERT), F32),
                        pltpu.VMEM((2, 2, D_EXPERT, dm), F32), pltpu.SemaphoreType.DMA((2, 2)),
                        pltpu.VMEM((2, tme * ROW_TILE, LANES), F32)],
    )
    return pl.pallas_call(
        functools.partial(_expert_kernel, tme=tme),
        out_shape=jax.ShapeDtypeStruct((nslot, 2, ROW_TILE, LANES), F32),
        grid_spec=grid_spec,
        compiler_params=pltpu.CompilerParams(dimension_semantics=("arbitrary",), vmem_limit_bytes=VMEM_LIMIT),
        name="experts",
    )(tile_a, tile_b, tile_block, tile_rows, plan, ntiles, xs, wg, wu, wd)


def _combine_kernel(pos_ref, posn_ref, ys_ref, ysflat_ref, x1_ref, route_ref, gfin_ref, out_ref, ybuf, sem, *, tc):
    i = pl.program_id(0)
    nb = pl.num_programs(0)
    slot = i % 2

    def gather(p_ref, s):
        def issue(jb, carry):
            for u in range(ROWS_PER_ISSUE):
                j = jb * ROWS_PER_ISSUE + u
                dst = pl.multiple_of(j * ROW_TILE, ROW_TILE)
                pltpu.make_async_copy(ys_ref.at[p_ref[0, 0, j]], ybuf.at[s, :, pl.ds(dst, ROW_TILE), :],
                                      sem.at[s]).start(priority=1)
            return carry

        lax.fori_loop(0, tc // ROWS_PER_ISSUE, issue, 0)

    @pl.when(i == 0)
    def _():
        gather(pos_ref, 0)

    @pl.when(i + 1 < nb)
    def _():
        gather(posn_ref, 1 - slot)

    for plane in range(2):
        pltpu.make_async_copy(ysflat_ref.at[pl.ds(0, tc * ROW_TILE)], ybuf.at[slot, plane], sem.at[slot]).wait()
    rec = route_ref[...]
    wa = rec[:, ROUTE_WA:ROUTE_WA + 1]
    wb = rec[:, ROUTE_WB:ROUTE_WB + 1]
    ya = jnp.concatenate([ybuf[slot, 0, pl.ds(c, tc, stride=ROW_TILE), :] for c in range(ROW_TILE)], axis=1)
    yb = jnp.concatenate([ybuf[slot, 1, pl.ds(c, tc, stride=ROW_TILE), :] for c in range(ROW_TILE)], axis=1)
    z = x1_ref[...] + wa * ya + wb * yb
    ms = jnp.mean(z * z, axis=-1, keepdims=True)
    out_ref[...] = z * lax.rsqrt(ms + RMS_EPS) * gfin_ref[...]


def _combine(ys, pos, x1, route, g_final, *, tc):
    n, dm = x1.shape
    nb = n // tc
    grid_spec = pltpu.PrefetchScalarGridSpec(
        num_scalar_prefetch=0,
        grid=(nb,),
        in_specs=[
            pl.BlockSpec((1, 1, tc), lambda i: (i, 0, 0), memory_space=pltpu.SMEM),
            pl.BlockSpec((1, 1, tc), lambda i: (jnp.minimum(i + 1, nb - 1), 0, 0), memory_space=pltpu.SMEM),
            pl.BlockSpec(memory_space=pl.ANY),
            pl.BlockSpec(memory_space=pl.ANY),
            pl.BlockSpec((tc, dm), lambda i: (i, 0)),
            pl.BlockSpec((tc, LANES), lambda i: (i, 0)),
            pl.BlockSpec((1, dm), lambda i: (0, 0)),
        ],
        out_specs=pl.BlockSpec((tc, dm), lambda i: (i, 0)),
        scratch_shapes=[pltpu.VMEM((2, 2, tc * ROW_TILE, LANES), F32), pltpu.SemaphoreType.DMA((2,))],
    )
    nslot = ys.shape[0]
    return pl.pallas_call(
        functools.partial(_combine_kernel, tc=tc),
        out_shape=jax.ShapeDtypeStruct((n, dm), F32),
        grid_spec=grid_spec,
        compiler_params=pltpu.CompilerParams(dimension_semantics=("arbitrary",), vmem_limit_bytes=VMEM_LIMIT),
        name="combine",
    )(pos.reshape(nb, 1, tc), pos.reshape(nb, 1, tc), ys, ys.reshape(nslot * PAIR_ROWS, LANES), x1, route,
      g_final.reshape(1, dm))


def _slot_layout(route, counts, n, tme):
    cnt = counts[0, :N_CLASSES].astype(jnp.int32)
    tiles_per = (cnt + tme - 1) // tme
    tile_end = jnp.cumsum(tiles_per)
    seg_base = (tile_end - tiles_per) * tme
    ntiles = tile_end[-1]
    nt_max = n // tme + N_CLASSES
    tid = jnp.arange(nt_max, dtype=jnp.int32)
    live = jnp.minimum(tid, ntiles - 1)
    tile_class = jnp.sum((tile_end[None, :] <= live[:, None]).astype(jnp.int32), axis=1)
    tile_a = jnp.take(jnp.asarray(CLASS_EXPERT_A), tile_class)
    tile_b = jnp.take(jnp.asarray(CLASS_EXPERT_B), tile_class)
    first_tile = jnp.take(tile_end - tiles_per, tile_class)
    tile_rows = jnp.clip(jnp.take(cnt, tile_class) - (live - first_tile) * tme, 0, tme)
    cls = route[:, ROUTE_CLASS].astype(jnp.int32)
    rank = route[:, ROUTE_RANK].astype(jnp.int32)
    base = jnp.sum(jnp.where(cls[:, None] == jnp.arange(N_CLASSES, dtype=jnp.int32)[None, :], seg_base[None, :], 0),
                   axis=1)
    pos = base + rank
    zoff = jnp.maximum(tile_end - 1, 0) * tme
    return (pos, zoff.astype(jnp.int32), tile_a.astype(jnp.int32), tile_b.astype(jnp.int32),
            live.astype(jnp.int32), tile_rows.astype(jnp.int32), ntiles.reshape(1).astype(jnp.int32))


def kernel(x, g_mix, w_in, sinks, w_br_dil, w_br_swa, w_out, g_ffn, w_group, b_group, w_router, b_router,
           w_e_gate, w_e_up, w_e_down, g_final):
    b, t, dm = x.shape
    n = b * t
    tme = 256
    qkv0, swa, gates, qkv4, qkv16 = _inproj(x, g_mix[0], w_in[0], tm=512)
    o0, l0 = _dil_attention(qkv0.reshape(b, 1, t, QKV_W), 0, tq=2048)
    o1, l1 = _dil_attention(qkv4, 1, tq=2048)
    o2, l2 = _dil_attention(qkv16, 2, tq=2048)
    ys = _swa_attention(swa, sinks[0], tq=2048)
    x1, h2, route, counts = _merge_route(
        x, o0.reshape(b, t, GROUP_W), l0.reshape(b, t, LANES), o1, l1, o2, l2, ys, gates,
        w_br_dil[0], w_br_swa[0], w_out[0], g_ffn[0], w_group[0], b_group[0], w_router[0], b_router[0], tm=512)
    route = route.reshape(n, LANES)
    pos, zoff, tile_a, tile_b, tile_block, tile_rows, ntiles = _slot_layout(route, counts, n, tme)
    nslot = n + N_CLASSES * tme
    xs = _dispatch(h2.reshape(n * ROW_TILE, LANES), pos, zoff, ntiles, nslot, td=4096, tme=tme)
    yslots = _experts(xs, tile_a, tile_b, tile_block, tile_rows, ntiles, w_e_gate[0], w_e_up[0], w_e_down[0],
                      tme=tme)
    out = _combine(yslots, pos, x1.reshape(n, dm), route, g_final, tc=512)
    return out.reshape(b, t, dm)
```
